```python
import math
import jax, jax.numpy as jnp
from jax import lax
import numpy as np

D_MODEL = 1024
BATCH = 32
SEQ = 2048
DEPTH = 4

N_A_LAYERS = max(1, DEPTH // 2)
N_B_LAYERS = DEPTH - N_A_LAYERS
POOL_WINDOWS = (2, 4, 8, 16)
N_POOL_GROUPS = len(POOL_WINDOWS)
POOL_GROUP_DIM = D_MODEL // N_POOL_GROUPS
HEAD_DIM = 64
N_HEADS = D_MODEL // HEAD_DIM
DILATED_GROUPS = ((128, 1), (512, 4), (2048, 16))
N_GROUPS = len(DILATED_GROUPS)
ATTN_DIM = N_HEADS * HEAD_DIM
Q_DIM = N_GROUPS * ATTN_DIM
ROPE_THETA = 10000.0
D_FF = 2816
CONV_WIDTH = 3
DEEPNORM_ALPHA = (2.0 * DEPTH) ** 0.25
DEEPNORM_BETA = (8.0 * DEPTH) ** -0.25
LN_EPS = 1e-5

kernel_name = "yoco_pool_dilated_attn_convffn_deepnorm"


def layer_norm(x, g, b):
    xf = x.astype(jnp.float32)
    mu = xf.mean(-1, keepdims=True)
    var = jnp.square(xf - mu).mean(-1, keepdims=True)
    y = (xf - mu) * lax.rsqrt(var + LN_EPS) * g.astype(jnp.float32) + b.astype(jnp.float32)
    return y.astype(x.dtype)


def rope_tables(seq):
    inv_freq = ROPE_THETA ** (-jnp.arange(0, HEAD_DIM, 2, dtype=jnp.float32) / HEAD_DIM)
    ang = jnp.arange(seq, dtype=jnp.float32)[:, None] * inv_freq[None, :]
    return jnp.cos(ang), jnp.sin(ang)


def apply_rope(t, cos, sin):
    tf = t.astype(jnp.float32)
    x1, x2 = tf[..., : HEAD_DIM // 2], tf[..., HEAD_DIM // 2:]
    c, s = cos[None, :, None, :], sin[None, :, None, :]
    return jnp.concatenate([x1 * c - x2 * s, x2 * c + x1 * s], axis=-1).astype(t.dtype)


def pool_mixer(x, pool_w, pool_scale):
    B, S, D = x.shape
    xg = x.reshape(B, S, N_POOL_GROUPS, POOL_GROUP_DIM)
    csum = jnp.cumsum(xg.astype(jnp.float32), axis=1)
    c0 = jnp.concatenate([jnp.zeros_like(csum[:, :1]), csum], axis=1)
    pos = jnp.arange(S, dtype=jnp.float32)
    pooled = []
    for g, w in enumerate(POOL_WINDOWS):
        w_eff = min(w, S)
        lagged = jnp.concatenate([jnp.zeros_like(c0[:, : w_eff - 1, g]), c0[:, : S - w_eff + 1, g]], axis=1)
        count = jnp.minimum(pos + 1.0, float(w))[None, :, None]
        pooled.append((c0[:, 1:, g] - lagged) / count)
    pooled = jnp.stack(pooled, axis=2).astype(x.dtype) - xg
    y = jnp.einsum('bsgc,gce->bsge', pooled, pool_w).reshape(B, S, D)
    return y * pool_scale


def dilated_branch(q, k, v, dilation, span):
    B, S, H, Dh = q.shape
    L = S // dilation
    nb = -(-L // span)
    pad = nb * span - L

    def strided_blocks(t):
        t = t.reshape(B, L, dilation, H, Dh).transpose(0, 2, 3, 1, 4)
        t = jnp.pad(t, ((0, 0), (0, 0), (0, 0), (0, pad), (0, 0)))
        return t.reshape(B, dilation, H, nb, span, Dh)

    def with_prev(t):
        prev = jnp.concatenate([jnp.zeros_like(t[:, :, :, :1]), t[:, :, :, :-1]], axis=3)
        return jnp.concatenate([prev, t], axis=4)

    qb = strided_blocks(q)
    kk = with_prev(strided_blocks(k))
    vv = with_prev(strided_blocks(v))
    s = jnp.einsum('brhnqc,brhnkc->brhnqk', qb, kk, preferred_element_type=jnp.float32)
    qi = jnp.arange(span)[:, None]
    kj = jnp.arange(2 * span)[None, :]
    rel = span + qi - kj
    band = (rel >= 0) & (rel <= span)
    has_prev = (jnp.arange(nb) > 0)[:, None, None] | (kj >= span)[None]
    valid = band[None] & has_prev
    s = jnp.where(valid, s, -jnp.inf)
    m = s.max(-1, keepdims=True)
    p = jnp.exp(s - m)
    l = p.sum(-1, keepdims=True)
    o = jnp.einsum('brhnqk,brhnkc->brhnqc', (p / l).astype(v.dtype), vv)
    lse = (m + jnp.log(l))[..., 0]
    o = o.reshape(B, dilation, H, nb * span, Dh)[:, :, :, :L].transpose(0, 3, 1, 2, 4).reshape(B, S, H, Dh)
    lse = lse.reshape(B, dilation, H, nb * span)[..., :L].transpose(0, 3, 1, 2).reshape(B, S, H)
    return o, lse


def dilated_attention(x, k_shared, v_shared, w_q, w_o, cos, sin):
    B, S, _ = x.shape
    q = (x @ w_q).reshape(B, S, N_GROUPS * N_HEADS, HEAD_DIM)
    q = (apply_rope(q, cos, sin) * (HEAD_DIM ** -0.5)).reshape(B, S, N_GROUPS, N_HEADS, HEAD_DIM)
    outs, lses = [], []
    for g, (window, dilation) in enumerate(DILATED_GROUPS):
        o, lse = dilated_branch(q[:, :, g], k_shared[:, :, g], v_shared[:, :, g], dilation, window // dilation)
        outs.append(o)
        lses.append(lse)
    weights = jax.nn.softmax(jnp.stack(lses, axis=0), axis=0)
    o = jnp.sum(weights[..., None].astype(x.dtype) * jnp.stack(outs, axis=0), axis=0)
    return o.reshape(B, S, ATTN_DIM) @ w_o


def conv_ffn(x, w_gate, w_up, conv_w, conv_b, w_down):
    S = x.shape[1]
    g = x @ w_gate
    u = x @ w_up
    gp = jnp.pad(g, ((0, 0), (CONV_WIDTH - 1, 0), (0, 0)))
    conv = conv_b
    for j in range(CONV_WIDTH):
        conv = conv + conv_w[j] * gp[:, j: j + S]
    h = jax.nn.gelu(conv) * u
    return h @ w_down


def _fwd_setup_inputs(seed: int = 0) -> dict:
    key = jax.random.key(seed)
    ks = jax.random.split(key, 16)
    f32 = jnp.float32
    beta = DEEPNORM_BETA
    nrm = lambda k, shape, scale: jax.random.normal(k, shape, f32) * scale
    x = jax.random.normal(ks[0], (BATCH, SEQ, D_MODEL), f32)
    pool_w = nrm(ks[1], (N_A_LAYERS, N_POOL_GROUPS, POOL_GROUP_DIM, POOL_GROUP_DIM), beta * POOL_GROUP_DIM ** -0.5)
    pool_scale = 1.0 + nrm(ks[2], (N_A_LAYERS, D_MODEL), 0.1)
    w_q = nrm(ks[3], (N_B_LAYERS, D_MODEL, Q_DIM), D_MODEL ** -0.5)
    w_k = nrm(ks[4], (D_MODEL, Q_DIM), D_MODEL ** -0.5)
    w_v = nrm(ks[5], (D_MODEL, Q_DIM), beta * D_MODEL ** -0.5)
    w_kv = jnp.concatenate([w_k, w_v], axis=1)
    w_o = nrm(ks[6], (N_B_LAYERS, ATTN_DIM, D_MODEL), beta * ATTN_DIM ** -0.5)
    ffn_w_gate = nrm(ks[7], (DEPTH, D_MODEL, D_FF), D_MODEL ** -0.5)
    ffn_w_up = nrm(ks[8], (DEPTH, D_MODEL, D_FF), beta * D_MODEL ** -0.5)
    ffn_conv_w = nrm(ks[9], (DEPTH, CONV_WIDTH, D_FF), CONV_WIDTH ** -0.5)
    ffn_conv_b = nrm(ks[10], (DEPTH, D_FF), 0.02)
    ffn_w_down = nrm(ks[11], (DEPTH, D_FF, D_MODEL), beta * D_FF ** -0.5)
    ln1_g = 1.0 + nrm(ks[12], (DEPTH, D_MODEL), 0.05)
    ln1_b = nrm(ks[13], (DEPTH, D_MODEL), 0.02)
    ln2_g = 1.0 + nrm(ks[14], (DEPTH, D_MODEL), 0.05)
    ln2_b = nrm(ks[15], (DEPTH, D_MODEL), 0.02)
    return {"x": x, "pool_w": pool_w, "pool_scale": pool_scale, "w_q": w_q, "w_kv": w_kv,
            "w_o": w_o, "ffn_w_gate": ffn_w_gate, "ffn_w_up": ffn_w_up, "ffn_conv_w": ffn_conv_w,
            "ffn_conv_b": ffn_conv_b, "ffn_w_down": ffn_w_down, "ln1_g": ln1_g, "ln1_b": ln1_b,
            "ln2_g": ln2_g, "ln2_b": ln2_b}


def _fwd_reference(x, pool_w, pool_scale, w_q, w_kv, w_o, ffn_w_gate, ffn_w_up, ffn_conv_w,
              ffn_conv_b, ffn_w_down, ln1_g, ln1_b, ln2_g, ln2_b):
    B, S, _ = x.shape
    cos, sin = rope_tables(S)
    k_shared = None
    v_shared = None
    for i in range(DEPTH):
        if i < N_A_LAYERS:
            mix = pool_mixer(x, pool_w[i], pool_scale[i])
        else:
            j = i - N_A_LAYERS
            mix = dilated_attention(x, k_shared, v_shared, w_q[j], w_o[j], cos, sin)
        x = layer_norm(DEEPNORM_ALPHA * x + mix, ln1_g[i], ln1_b[i])
        ffn = conv_ffn(x, ffn_w_gate[i], ffn_w_up[i], ffn_conv_w[i], ffn_conv_b[i], ffn_w_down[i])
        x = layer_norm(DEEPNORM_ALPHA * x + ffn, ln2_g[i], ln2_b[i])
        if i == N_A_LAYERS - 1:
            kv = (x @ w_kv).reshape(B, S, 2, N_GROUPS * N_HEADS, HEAD_DIM)
            k_shared = apply_rope(kv[:, :, 0], cos, sin).reshape(B, S, N_GROUPS, N_HEADS, HEAD_DIM)
            v_shared = kv[:, :, 1].reshape(B, S, N_GROUPS, N_HEADS, HEAD_DIM)
    return x


import jax as _jax
import jax.numpy as _jnp

TWIN_FORMAT = 'train_step'
FWD_PARAMS = ['x', 'pool_w', 'pool_scale', 'w_q', 'w_kv', 'w_o', 'ffn_w_gate', 'ffn_w_up', 'ffn_conv_w', 'ffn_conv_b', 'ffn_w_down', 'ln1_g', 'ln1_b', 'ln2_g', 'ln2_b']
TWIN_WEIGHTS = ['pool_w', 'pool_scale', 'w_q', 'w_kv', 'w_o', 'ffn_w_gate', 'ffn_w_up', 'ffn_conv_w', 'ffn_conv_b', 'ffn_w_down', 'ln1_g', 'ln1_b', 'ln2_g', 'ln2_b']
TWIN_DIFF_INPUT = 'x'
TWIN_INPUTS = ['x', 'pool_w', 'pool_scale', 'w_q', 'w_kv', 'w_o', 'ffn_w_gate', 'ffn_w_up', 'ffn_conv_w', 'ffn_conv_b', 'ffn_w_down', 'ln1_g', 'ln1_b', 'ln2_g', 'ln2_b', 'loss_target', 'm_pool_w', 'm_pool_scale', 'm_w_q', 'm_w_kv', 'm_w_o', 'm_ffn_w_gate', 'm_ffn_w_up', 'm_ffn_conv_w', 'm_ffn_conv_b', 'm_ffn_w_down', 'm_ln1_g', 'm_ln1_b', 'm_ln2_g', 'm_ln2_b', 'v_pool_w', 'v_pool_scale', 'v_w_q', 'v_w_kv', 'v_w_o', 'v_ffn_w_gate', 'v_ffn_w_up', 'v_ffn_conv_w', 'v_ffn_conv_b', 'v_ffn_w_down', 'v_ln1_g', 'v_ln1_b', 'v_ln2_g', 'v_ln2_b']
TWIN_OUTPUTS = ['loss', 'grad_x', 'grad_pool_w', 'grad_pool_scale', 'grad_w_q', 'grad_w_kv', 'grad_w_o', 'grad_ffn_w_gate', 'grad_ffn_w_up', 'grad_ffn_conv_w', 'grad_ffn_conv_b', 'grad_ffn_w_down', 'grad_ln1_g', 'grad_ln1_b', 'grad_ln2_g', 'grad_ln2_b', 'delta_pool_w', 'delta_pool_scale', 'delta_w_q', 'delta_w_kv', 'delta_w_o', 'delta_ffn_w_gate', 'delta_ffn_w_up', 'delta_ffn_conv_w', 'delta_ffn_conv_b', 'delta_ffn_w_down', 'delta_ln1_g', 'delta_ln1_b', 'delta_ln2_g', 'delta_ln2_b', 'new_m_pool_w', 'new_m_pool_scale', 'new_m_w_q', 'new_m_w_kv', 'new_m_w_o', 'new_m_ffn_w_gate', 'new_m_ffn_w_up', 'new_m_ffn_conv_w', 'new_m_ffn_conv_b', 'new_m_ffn_w_down', 'new_m_ln1_g', 'new_m_ln1_b', 'new_m_ln2_g', 'new_m_ln2_b', 'new_v_pool_w', 'new_v_pool_scale', 'new_v_w_q', 'new_v_w_kv', 'new_v_w_o', 'new_v_ffn_w_gate', 'new_v_ffn_w_up', 'new_v_ffn_conv_w', 'new_v_ffn_conv_b', 'new_v_ffn_w_down', 'new_v_ln1_g', 'new_v_ln1_b', 'new_v_ln2_g', 'new_v_ln2_b']
TWIN_LEAF_KINDS = {'loss': 'loss', 'grad_x': 'grad_x', 'grad_pool_w': 'grad_w', 'grad_pool_scale': 'grad_w', 'grad_w_q': 'grad_w', 'grad_w_kv': 'grad_w', 'grad_w_o': 'grad_w', 'grad_ffn_w_gate': 'grad_w', 'grad_ffn_w_up': 'grad_w', 'grad_ffn_conv_w': 'grad_w', 'grad_ffn_conv_b': 'grad_w', 'grad_ffn_w_down': 'grad_w', 'grad_ln1_g': 'grad_w', 'grad_ln1_b': 'grad_w', 'grad_ln2_g': 'grad_w', 'grad_ln2_b': 'grad_w', 'delta_pool_w': 'delta_w', 'delta_pool_scale': 'delta_w', 'delta_w_q': 'delta_w', 'delta_w_kv': 'delta_w', 'delta_w_o': 'delta_w', 'delta_ffn_w_gate': 'delta_w', 'delta_ffn_w_up': 'delta_w', 'delta_ffn_conv_w': 'delta_w', 'delta_ffn_conv_b': 'delta_w', 'delta_ffn_w_down': 'delta_w', 'delta_ln1_g': 'delta_w', 'delta_ln1_b': 'delta_w', 'delta_ln2_g': 'delta_w', 'delta_ln2_b': 'delta_w', 'new_m_pool_w': 'new_m', 'new_m_pool_scale': 'new_m', 'new_m_w_q': 'new_m', 'new_m_w_kv': 'new_m', 'new_m_w_o': 'new_m', 'new_m_ffn_w_gate': 'new_m', 'new_m_ffn_w_up': 'new_m', 'new_m_ffn_conv_w': 'new_m', 'new_m_ffn_conv_b': 'new_m', 'new_m_ffn_w_down': 'new_m', 'new_m_ln1_g': 'new_m', 'new_m_ln1_b': 'new_m', 'new_m_ln2_g': 'new_m', 'new_m_ln2_b': 'new_m', 'new_v_pool_w': 'new_v', 'new_v_pool_scale': 'new_v', 'new_v_w_q': 'new_v', 'new_v_w_kv': 'new_v', 'new_v_w_o': 'new_v', 'new_v_ffn_w_gate': 'new_v', 'new_v_ffn_w_up': 'new_v', 'new_v_ffn_conv_w': 'new_v', 'new_v_ffn_conv_b': 'new_v', 'new_v_ffn_w_down': 'new_v', 'new_v_ln1_g': 'new_v', 'new_v_ln1_b': 'new_v', 'new_v_ln2_g': 'new_v', 'new_v_ln2_b': 'new_v'}


def _forward(args):
    return _fwd_reference(*[args[k] for k in FWD_PARAMS])


def _output_shape():
    out = _jax.eval_shape(lambda: _forward(_fwd_setup_inputs(0)))
    return out.shape, out.dtype

N_MICROBATCH = 1
ADAM_LR = 0.001
ADAM_B1 = 0.9
ADAM_B2 = 0.999
ADAM_EPS = 1e-08
ADAM_WD = 0.01
ADAM_STEP = 10
PER_EXAMPLE_BATCH_AXIS = {'x': 0, 'loss_target': 0}
SHARED_INPUTS = []
_WEIGHT_DTYPES = {'pool_w': _jnp.float32, 'pool_scale': _jnp.float32, 'w_q': _jnp.float32, 'w_kv': _jnp.float32, 'w_o': _jnp.float32, 'ffn_w_gate': _jnp.float32, 'ffn_w_up': _jnp.float32, 'ffn_conv_w': _jnp.float32, 'ffn_conv_b': _jnp.float32, 'ffn_w_down': _jnp.float32, 'ln1_g': _jnp.float32, 'ln1_b': _jnp.float32, 'ln2_g': _jnp.float32, 'ln2_b': _jnp.float32}
MOMENT_SCALE = {'pool_w': 1.543860e-01, 'pool_scale': 1.991745e-01, 'w_q': 2.550412e-03, 'w_kv': 7.450448e-03, 'w_o': 1.207241e-02, 'ffn_w_gate': 1.092586e-02, 'ffn_w_up': 2.545025e-02, 'ffn_conv_w': 1.111347e-02, 'ffn_conv_b': 1.105169e-02, 'ffn_w_down': 4.257117e-02, 'ln1_g': 6.020041e+00, 'ln1_b': 8.069773e-01, 'ln2_g': 3.364374e+01, 'ln2_b': 1.517190e+00}


def _to_microbatches(a, axis):
    t = _jnp.moveaxis(a, axis, 0)
    t = t.reshape((N_MICROBATCH, t.shape[0] // N_MICROBATCH) + t.shape[1:])
    return _jnp.moveaxis(t, 1, axis + 1)


def setup_inputs(seed: int = 0) -> dict:
    inp = _fwd_setup_inputs(seed)
    key = _jax.random.fold_in(_jax.random.key(seed), 7919)
    shape, _ = _output_shape()
    out = dict(inp)
    out["loss_target"] = _jax.random.normal(_jax.random.fold_in(key, 0), shape, _jnp.float32)
    for i, name in enumerate(TWIN_WEIGHTS):
        w = inp[name].astype(_jnp.float32)
        if MOMENT_SCALE is None:
            s = _jnp.sqrt(_jnp.mean(_jnp.square(w)) + 1e-30)
        else:
            s = MOMENT_SCALE[name]
        km, kv = _jax.random.split(_jax.random.fold_in(key, i + 1))
        out[name] = w
        out["m_" + name] = s * _jax.random.normal(km, w.shape, _jnp.float32)
        out["v_" + name] = (s * s) * _jax.random.uniform(kv, w.shape, _jnp.float32, 0.5, 1.5)
    if N_MICROBATCH > 1:
        for name, axis in PER_EXAMPLE_BATCH_AXIS.items():
            out[name] = _to_microbatches(out[name], axis)
    return {'x': out['x'], 'pool_w': out['pool_w'], 'pool_scale': out['pool_scale'], 'w_q': out['w_q'], 'w_kv': out['w_kv'], 'w_o': out['w_o'], 'ffn_w_gate': out['ffn_w_gate'], 'ffn_w_up': out['ffn_w_up'], 'ffn_conv_w': out['ffn_conv_w'], 'ffn_conv_b': out['ffn_conv_b'], 'ffn_w_down': out['ffn_w_down'], 'ln1_g': out['ln1_g'], 'ln1_b': out['ln1_b'], 'ln2_g': out['ln2_g'], 'ln2_b': out['ln2_b'], 'loss_target': out['loss_target'], 'm_pool_w': out['m_pool_w'], 'm_pool_scale': out['m_pool_scale'], 'm_w_q': out['m_w_q'], 'm_w_kv': out['m_w_kv'], 'm_w_o': out['m_w_o'], 'm_ffn_w_gate': out['m_ffn_w_gate'], 'm_ffn_w_up': out['m_ffn_w_up'], 'm_ffn_conv_w': out['m_ffn_conv_w'], 'm_ffn_conv_b': out['m_ffn_conv_b'], 'm_ffn_w_down': out['m_ffn_w_down'], 'm_ln1_g': out['m_ln1_g'], 'm_ln1_b': out['m_ln1_b'], 'm_ln2_g': out['m_ln2_g'], 'm_ln2_b': out['m_ln2_b'], 'v_pool_w': out['v_pool_w'], 'v_pool_scale': out['v_pool_scale'], 'v_w_q': out['v_w_q'], 'v_w_kv': out['v_w_kv'], 'v_w_o': out['v_w_o'], 'v_ffn_w_gate': out['v_ffn_w_gate'], 'v_ffn_w_up': out['v_ffn_w_up'], 'v_ffn_conv_w': out['v_ffn_conv_w'], 'v_ffn_conv_b': out['v_ffn_conv_b'], 'v_ffn_w_down': out['v_ffn_w_down'], 'v_ln1_g': out['v_ln1_g'], 'v_ln1_b': out['v_ln1_b'], 'v_ln2_g': out['v_ln2_g'], 'v_ln2_b': out['v_ln2_b']}


def _loss(weights, diff, rest, loss_target):
    with _jax.named_scope("forward"):
        args = {**rest, TWIN_DIFF_INPUT: diff, **{k: w.astype(_WEIGHT_DTYPES[k]) for k, w in weights.items()}}
        y = _forward(args)
    with _jax.named_scope("loss_head"):
        err = _jnp.square(y.astype(_jnp.float32) - loss_target)
        return 0.5 * _jnp.sum(_jnp.mean(err, axis=-1)) if err.ndim else 0.5 * err


def _adamw(w, g, m, v):
    m = ADAM_B1 * m + (1.0 - ADAM_B1) * g
    v = ADAM_B2 * v + (1.0 - ADAM_B2) * _jnp.square(g)
    m_hat = m / (1.0 - ADAM_B1 ** ADAM_STEP)
    v_hat = v / (1.0 - ADAM_B2 ** ADAM_STEP)
    delta = -ADAM_LR * (m_hat / (_jnp.sqrt(v_hat) + ADAM_EPS) + ADAM_WD * w)
    return delta, m, v


def reference(x, pool_w, pool_scale, w_q, w_kv, w_o, ffn_w_gate, ffn_w_up, ffn_conv_w, ffn_conv_b, ffn_w_down, ln1_g, ln1_b, ln2_g, ln2_b, loss_target, m_pool_w, m_pool_scale, m_w_q, m_w_kv, m_w_o, m_ffn_w_gate, m_ffn_w_up, m_ffn_conv_w, m_ffn_conv_b, m_ffn_w_down, m_ln1_g, m_ln1_b, m_ln2_g, m_ln2_b, v_pool_w, v_pool_scale, v_w_q, v_w_kv, v_w_o, v_ffn_w_gate, v_ffn_w_up, v_ffn_conv_w, v_ffn_conv_b, v_ffn_w_down, v_ln1_g, v_ln1_b, v_ln2_g, v_ln2_b):
    given = dict(x=x, pool_w=pool_w, pool_scale=pool_scale, w_q=w_q, w_kv=w_kv, w_o=w_o, ffn_w_gate=ffn_w_gate, ffn_w_up=ffn_w_up, ffn_conv_w=ffn_conv_w, ffn_conv_b=ffn_conv_b, ffn_w_down=ffn_w_down, ln1_g=ln1_g, ln1_b=ln1_b, ln2_g=ln2_g, ln2_b=ln2_b, loss_target=loss_target, m_pool_w=m_pool_w, m_pool_scale=m_pool_scale, m_w_q=m_w_q, m_w_kv=m_w_kv, m_w_o=m_w_o, m_ffn_w_gate=m_ffn_w_gate, m_ffn_w_up=m_ffn_w_up, m_ffn_conv_w=m_ffn_conv_w, m_ffn_conv_b=m_ffn_conv_b, m_ffn_w_down=m_ffn_w_down, m_ln1_g=m_ln1_g, m_ln1_b=m_ln1_b, m_ln2_g=m_ln2_g, m_ln2_b=m_ln2_b, v_pool_w=v_pool_w, v_pool_scale=v_pool_scale, v_w_q=v_w_q, v_w_kv=v_w_kv, v_w_o=v_w_o, v_ffn_w_gate=v_ffn_w_gate, v_ffn_w_up=v_ffn_w_up, v_ffn_conv_w=v_ffn_conv_w, v_ffn_conv_b=v_ffn_conv_b, v_ffn_w_down=v_ffn_w_down, v_ln1_g=v_ln1_g, v_ln1_b=v_ln1_b, v_ln2_g=v_ln2_g, v_ln2_b=v_ln2_b)
    weights = {n: given[n] for n in TWIN_WEIGHTS}
    shared = {n: given[n] for n in SHARED_INPUTS}
    per_example = {n: given[n] for n in ['x']}
    grad_fn = _jax.value_and_grad(_loss, argnums=(0, 1))

    def one_microbatch(ex, loss_target):
        ex = dict(ex)
        diff = ex.pop(TWIN_DIFF_INPUT)
        return grad_fn(weights, diff, {**shared, **ex}, loss_target)

    if N_MICROBATCH == 1:
        loss, (grad_w, grad_x) = one_microbatch(per_example, given["loss_target"])
    else:
        def body(carry, xs):
            loss_sum, grad_sum = carry
            l_k, (gw_k, gx_k) = one_microbatch(xs[0], xs[1])
            with _jax.named_scope("update"):
                return (loss_sum + l_k, _jax.tree.map(_jnp.add, grad_sum, gw_k)), gx_k

        init = (_jnp.zeros((), _jnp.float32), _jax.tree.map(_jnp.zeros_like, weights))
        (loss, grad_w), grad_x = _jax.lax.scan(body, init, (per_example, given["loss_target"]))
    with _jax.named_scope("update"):
        delta_w, new_m, new_v = {}, {}, {}
        for n in TWIN_WEIGHTS:
            delta_w[n], new_m[n], new_v[n] = _adamw(weights[n], grad_w[n], given["m_" + n], given["v_" + n])
    return (loss, grad_x, *[grad_w[n] for n in TWIN_WEIGHTS], *[delta_w[n] for n in TWIN_WEIGHTS],
            *[new_m[n] for n in TWIN_WEIGHTS], *[new_v[n] for n in TWIN_WEIGHTS])
```

```python
import functools
import math

import jax
import jax.numpy as jnp
from jax import lax
from jax.experimental import pallas as pl
from jax.experimental.pallas import tpu as pltpu

F32 = jnp.float32
BF16 = jnp.bfloat16
SDS = jax.ShapeDtypeStruct
MESH = pl.DeviceIdType.MESH

N_DEV = 8
HEAD_DIM = 64
BLK = 128
DILATIONS = (1, 4, 16)
POOL_WINDOWS = (2, 4, 8, 16)
ROPE_THETA = 10000.0
LN_EPS = 1e-5
NEG = -1e30
V7X_VMEM_LIMIT = 56 * 1024 * 1024

ADAM_LR, ADAM_B1, ADAM_B2, ADAM_EPS, ADAM_WD, ADAM_STEP = 0.001, 0.9, 0.999, 1e-08, 0.01, 10

NN = (((1,), (0,)), ((), ()))
NT = (((1,), (1,)), ((), ()))
TN = (((0,), (0,)), ((), ()))


def _cp(sem=None):
    kw = dict(vmem_limit_bytes=V7X_VMEM_LIMIT)
    if sem is not None:
        kw["dimension_semantics"] = sem
    return pltpu.CompilerParams(**kw)


def _dot(a, b, dims=NN):
    return lax.dot_general(a, b, dims, preferred_element_type=F32)


def _tile(n, target, mult):
    best = None
    for t in range(mult, min(n, target) + 1, mult):
        if n % t == 0:
            best = t
    return best if best is not None else n


def _mesh_pos():
    return lax.axis_index("x"), lax.axis_index("y"), lax.axis_index("c")


def all_gather_blocks(xl, name, in_vmem):
    R, C = xl.shape
    space = pltpu.VMEM if in_vmem else pl.ANY

    def body(x_ref, out_ref, send_sems, recv_sems, local_sem):
        x, y, c = _mesh_pos()
        me, sibling = (x, y, c), (x, y, 1 - c)
        chips = [(1 - x, y), (x, 1 - y), (1 - x, 1 - y)]

        def slot(px, py, pc):
            return out_ref.at[4 * px + 2 * py + pc]

        def copy(k, block, to, src=None):
            return pltpu.make_async_remote_copy(
                src_ref=slot(*block) if src is None else src, dst_ref=slot(*block),
                send_sem=send_sems.at[k], recv_sem=recv_sems.at[k], device_id=to, device_id_type=MESH)

        mine = pltpu.make_async_copy(x_ref, slot(*me), local_sem)
        mine.start()
        first = [copy(0, me, sibling, src=x_ref)]
        first += [copy(1 + j, me, (*chip, c), src=x_ref) for j, chip in enumerate(chips)]
        for cp in first:
            cp.start()
        passed = [copy(4 + j, (*chip, c), sibling) for j, chip in enumerate(chips)]
        for j, chip in enumerate(chips):
            copy(1 + j, (*chip, c), me).wait_recv()
            passed[j].start()
        copy(0, sibling, me).wait_recv()
        for j, chip in enumerate(chips):
            copy(4 + j, (*chip, 1 - c), me).wait_recv()
        for cp in first + passed:
            cp.wait_send()
        mine.wait()

    return pl.pallas_call(
        body, name=name,
        out_shape=SDS((N_DEV, R, C), xl.dtype),
        in_specs=[pl.BlockSpec(memory_space=space)],
        out_specs=pl.BlockSpec(memory_space=space),
        scratch_shapes=[pltpu.SemaphoreType.DMA((7,)), pltpu.SemaphoreType.DMA((7,)), pltpu.SemaphoreType.DMA],
        compiler_params=_cp(),
    )(xl)


def scatter_partials(parts, name):
    C = parts[0].shape[2]
    rows = [p.shape[1] for p in parts]
    offs = [sum(rows[:i]) for i in range(len(rows))]
    R = sum(rows)
    n = len(parts)

    def body(*refs):
        part_refs, out_ref = refs[:n], refs[n]
        send_sems, recv_sems, local_sem = refs[n + 1:]
        x, y, c = _mesh_pos()
        me_lin = 4 * x + 2 * y + c
        loc = [pltpu.make_async_copy(part_refs[i].at[me_lin], out_ref.at[me_lin, pl.ds(offs[i], rows[i])], local_sem)
               for i in range(n)]
        for cp in loc:
            cp.start()
        peers = []
        for r in range(1, N_DEV):
            px = 1 - x if (r & 4) else x
            py = 1 - y if (r & 2) else y
            pc = 1 - c if (r & 1) else c
            peers.append((px, py, pc))
        for k, (px, py, pc) in enumerate(peers):
            p_lin = 4 * px + 2 * py + pc
            for i in range(n):
                pltpu.make_async_remote_copy(
                    src_ref=part_refs[i].at[p_lin], dst_ref=out_ref.at[me_lin, pl.ds(offs[i], rows[i])],
                    send_sem=send_sems.at[k], recv_sem=recv_sems.at[k],
                    device_id=(px, py, pc), device_id_type=MESH).start()
        for k, (px, py, pc) in enumerate(peers):
            p_lin = 4 * px + 2 * py + pc
            whole = pltpu.make_async_remote_copy(
                src_ref=out_ref.at[p_lin], dst_ref=out_ref.at[p_lin],
                send_sem=send_sems.at[k], recv_sem=recv_sems.at[k],
                device_id=(px, py, pc), device_id_type=MESH)
            whole.wait_recv()
            whole.wait_send()
        pltpu.make_async_copy(out_ref.at[me_lin], out_ref.at[me_lin], local_sem).wait()

    return pl.pallas_call(
        body, name=name,
        out_shape=SDS((N_DEV, R, C), parts[0].dtype),
        in_specs=[pl.BlockSpec(memory_space=pl.ANY)] * n,
        out_specs=pl.BlockSpec(memory_space=pl.ANY),
        scratch_shapes=[pltpu.SemaphoreType.DMA((7,)), pltpu.SemaphoreType.DMA((7,)), pltpu.SemaphoreType.DMA],
        compiler_params=_cp(),
    )(*parts)


def sum_slots(slots, name, out_dtype=F32):
    _, R, C = slots.shape
    tr = _tile(R, 512, 16)

    def body(s_ref, o_ref):
        acc = s_ref[0].astype(F32)
        for s in range(1, N_DEV):
            acc = acc + s_ref[s].astype(F32)
        o_ref[...] = acc.astype(out_dtype)

    return pl.pallas_call(
        body, name=name, grid=(R // tr,),
        in_specs=[pl.BlockSpec((N_DEV, tr, C), lambda i: (0, i, 0))],
        out_specs=pl.BlockSpec((tr, C), lambda i: (i, 0)),
        out_shape=SDS((R, C), out_dtype), compiler_params=_cp(),
    )(slots)


def add_ln(x, mix, g, b, alpha):
    T, D = x.shape
    tm = _tile(T, 512, 16)

    def body(x_ref, m_ref, g_ref, b_ref, a_ref, y_ref, yb_ref):
        a = alpha * x_ref[...] + m_ref[...]
        mu = jnp.mean(a, axis=-1, keepdims=True)
        xc = a - mu
        var = jnp.mean(xc * xc, axis=-1, keepdims=True)
        y = xc * lax.rsqrt(var + LN_EPS) * g_ref[...] + b_ref[...]
        a_ref[...] = a
        y_ref[...] = y
        yb_ref[...] = y.astype(BF16)

    row = pl.BlockSpec((tm, D), lambda i: (i, 0))
    vec = pl.BlockSpec((1, D), lambda i: (0, 0))
    return pl.pallas_call(
        body, name="add_ln", grid=(T // tm,),
        in_specs=[row, row, vec, vec], out_specs=[row, row, row],
        out_shape=[SDS((T, D), F32), SDS((T, D), F32), SDS((T, D), BF16)], compiler_params=_cp(),
    )(x, mix, g.reshape(1, D), b.reshape(1, D))


def ln_bwd(dy, a, g):
    T, D = a.shape
    tm = _tile(T, 512, 16)

    def body(dy_ref, a_ref, g_ref, da_ref, dab_ref, dg_ref, db_ref):
        @pl.when(pl.program_id(0) == 0)
        def _():
            dg_ref[...] = jnp.zeros_like(dg_ref)
            db_ref[...] = jnp.zeros_like(db_ref)

        av = a_ref[...]
        mu = jnp.mean(av, axis=-1, keepdims=True)
        xc = av - mu
        var = jnp.mean(xc * xc, axis=-1, keepdims=True)
        r = lax.rsqrt(var + LN_EPS)
        xh = xc * r
        dyv = dy_ref[...]
        dxh = dyv * g_ref[...]
        m1 = jnp.mean(dxh, axis=-1, keepdims=True)
        m2 = jnp.mean(dxh * xh, axis=-1, keepdims=True)
        da = r * (dxh - m1 - xh * m2)
        da_ref[...] = da
        dab_ref[...] = da.astype(BF16)
        dg_ref[...] += jnp.sum(dyv * xh, axis=0, keepdims=True)
        db_ref[...] += jnp.sum(dyv, axis=0, keepdims=True)

    row = pl.BlockSpec((tm, D), lambda i: (i, 0))
    vec = pl.BlockSpec((1, D), lambda i: (0, 0))
    return pl.pallas_call(
        body, name="ln_bwd", grid=(T // tm,),
        in_specs=[row, row, vec], out_specs=[row, row, vec, vec],
        out_shape=[SDS((T, D), F32), SDS((T, D), BF16), SDS((1, D), F32), SDS((1, D), F32)],
        compiler_params=_cp(("arbitrary",)),
    )(dy, a, g.reshape(1, D))


def loss_grad(y, tgt):
    T, D = y.shape
    tm = _tile(T, 512, 16)

    def body(y_ref, t_ref, dy_ref, sq_ref):
        @pl.when(pl.program_id(0) == 0)
        def _():
            sq_ref[...] = jnp.zeros_like(sq_ref)

        e = y_ref[...] - t_ref[...]
        dy_ref[...] = e / float(D)
        sq_ref[...] += jnp.sum(e * e, axis=0, keepdims=True)

    row = pl.BlockSpec((tm, D), lambda i: (i, 0))
    vec = pl.BlockSpec((1, D), lambda i: (0, 0))
    return pl.pallas_call(
        body, name="loss_grad", grid=(T // tm,),
        in_specs=[row, row], out_specs=[row, vec],
        out_shape=[SDS((T, D), F32), SDS((1, D), F32)], compiler_params=_cp(("arbitrary",)),
    )(y, tgt)


def matmul_rows(a, w, dims, name, res=None, alpha=1.0, tm_target=512):
    if dims == TN:
        K, T = a.shape
    else:
        T, K = a.shape
    N = w.shape[0] if dims == NT else w.shape[1]
    tm = _tile(T, tm_target, 128 if dims == TN else 16)
    has_res = res is not None

    def body(*refs):
        if has_res:
            a_ref, w_ref, r_ref, o_ref = refs
        else:
            a_ref, w_ref, o_ref = refs
        acc = _dot(a_ref[...], w_ref[...], dims)
        if has_res:
            acc = alpha * r_ref[...] + acc
        o_ref[...] = acc

    a_spec = pl.BlockSpec((K, tm), lambda i: (0, i)) if dims == TN else pl.BlockSpec((tm, K), lambda i: (i, 0))
    in_specs = [a_spec, pl.BlockSpec(w.shape, lambda i: (0, 0))]
    args = [a, w]
    if has_res:
        in_specs.append(pl.BlockSpec((tm, N), lambda i: (i, 0)))
        args.append(res)
    return pl.pallas_call(
        body, name=name, grid=(T // tm,), in_specs=in_specs,
        out_specs=pl.BlockSpec((tm, N), lambda i: (i, 0)),
        out_shape=SDS((T, N), F32), compiler_params=_cp(),
    )(*args)


def matmul_to_T(w, a, name):
    M, K = w.shape
    T = a.shape[0]
    tt = _tile(T, 512, 128)

    def body(w_ref, a_ref, o_ref):
        o_ref[...] = _dot(w_ref[...], a_ref[...], NT).astype(BF16)

    return pl.pallas_call(
        body, name=name, grid=(T // tt,),
        in_specs=[pl.BlockSpec((M, K), lambda i: (0, 0)), pl.BlockSpec((tt, K), lambda i: (i, 0))],
        out_specs=pl.BlockSpec((M, tt), lambda i: (0, i)),
        out_shape=SDS((M, T), BF16), compiler_params=_cp(),
    )(w, a)


def wgrad_rows(a, b, name):
    T, M = a.shape
    N = b.shape[1]
    tt = _tile(T, 512, 16)
    tmm = _tile(M, 1536, 128)

    def body(a_ref, b_ref, o_ref, acc_ref):
        t = pl.program_id(1)

        @pl.when(t == 0)
        def _():
            acc_ref[...] = jnp.zeros_like(acc_ref)

        acc_ref[...] += _dot(a_ref[...], b_ref[...], TN)

        @pl.when(t == pl.num_programs(1) - 1)
        def _():
            o_ref[...] = acc_ref[...].astype(BF16)

    return pl.pallas_call(
        body, name=name, grid=(M // tmm, T // tt),
        in_specs=[pl.BlockSpec((tt, tmm), lambda i, t: (t, i)), pl.BlockSpec((tt, N), lambda i, t: (t, 0))],
        out_specs=pl.BlockSpec((tmm, N), lambda i, t: (i, 0)),
        out_shape=SDS((M, N), BF16), scratch_shapes=[pltpu.VMEM((tmm, N), F32)],
        compiler_params=_cp(("arbitrary", "arbitrary")),
    )(a, b)


def wgrad_T(aT, bT, name, b_mod=None):
    G, M, T = aT.shape
    Gb, N, _ = bT.shape
    tt = _tile(T, 1024, 128)

    def body(a_ref, b_ref, o_ref, acc_ref):
        t = pl.program_id(1)

        @pl.when(t == 0)
        def _():
            acc_ref[...] = jnp.zeros_like(acc_ref)

        acc_ref[...] += _dot(a_ref[0], b_ref[0], NT)

        @pl.when(t == pl.num_programs(1) - 1)
        def _():
            o_ref[0] = acc_ref[...].astype(BF16)

    return pl.pallas_call(
        body, name=name, grid=(G, T // tt),
        in_specs=[pl.BlockSpec((1, M, tt), lambda g, t: (g, 0, t)),
                  pl.BlockSpec((1, N, tt), lambda g, t: (g % Gb, 0, t))],
        out_specs=pl.BlockSpec((1, M, N), lambda g, t: (g, 0, 0)),
        out_shape=SDS((G, M, N), BF16), scratch_shapes=[pltpu.VMEM((M, N), F32)],
        compiler_params=_cp(("arbitrary", "arbitrary")),
    )(aT, bT)


def wgrad_mixed(aT, b, name):
    M, T = aT.shape
    N = b.shape[1]
    tt = _tile(T, 1024, 128)

    def body(a_ref, b_ref, o_ref, acc_ref):
        t = pl.program_id(0)

        @pl.when(t == 0)
        def _():
            acc_ref[...] = jnp.zeros_like(acc_ref)

        acc_ref[...] += _dot(a_ref[...], b_ref[...], NN)

        @pl.when(t == pl.num_programs(0) - 1)
        def _():
            o_ref[...] = acc_ref[...].astype(BF16)

    return pl.pallas_call(
        body, name=name, grid=(T // tt,),
        in_specs=[pl.BlockSpec((M, tt), lambda t: (0, t)), pl.BlockSpec((tt, N), lambda t: (t, 0))],
        out_specs=pl.BlockSpec((M, N), lambda t: (0, 0)),
        out_shape=SDS((M, N), BF16), scratch_shapes=[pltpu.VMEM((M, N), F32)],
        compiler_params=_cp(("arbitrary",)),
    )(aT, b)


def _shift_down(x, k, rows):
    return jnp.where(rows >= k, pltpu.roll(x, k, 0), 0.0)


def _shift_up(x, k, rows):
    n = x.shape[0]
    return jnp.where(rows < n - k, pltpu.roll(x, n - k, 0), 0.0)


def _pick(g, vals):
    out = vals[-1]
    for k in range(len(vals) - 2, -1, -1):
        out = jnp.where(g == k, vals[k], out)
    return out


def pool_fwd(x, pw, scale, B, S):
    T, D = x.shape
    G = len(POOL_WINDOWS)
    Cg = D // G

    def body(x_ref, w_ref, s_ref, mix_ref, pooled_ref):
        g = pl.program_id(1)
        xv = x_ref[...]
        rows = lax.broadcasted_iota(jnp.int32, xv.shape, 0)
        sums, cur, k = [], xv, 1
        for _ in POOL_WINDOWS:
            cur = cur + _shift_down(cur, k, rows)
            sums.append(cur)
            k *= 2
        win = 2 * lax.shift_left(jnp.int32(1), g)
        total = _pick(g, sums)
        count = jnp.minimum(rows + 1, win).astype(F32)
        pooled = total / count - xv
        pb = pooled.astype(BF16)
        pooled_ref[...] = pb
        mix_ref[...] = _dot(pb, w_ref[0]) * s_ref[...]

    blk = pl.BlockSpec((S, Cg), lambda b, g: (b, g))
    return pl.pallas_call(
        body, name="pool_fwd", grid=(B, G),
        in_specs=[blk, pl.BlockSpec((1, Cg, Cg), lambda b, g: (g, 0, 0)), pl.BlockSpec((1, Cg), lambda b, g: (0, g))],
        out_specs=[blk, blk],
        out_shape=[SDS((T, D), F32), SDS((T, D), BF16)], compiler_params=_cp(),
    )(x, pw, scale)


def pool_bwd(dmix, pooled, pw, scale, alpha, B, S):
    T, D = dmix.shape
    G = len(POOL_WINDOWS)
    Cg = D // G

    def body(d_ref, p_ref, w_ref, s_ref, dx_ref, ds_ref, dw_ref):
        g = pl.program_id(1)
        dm = d_ref[...]
        pb = p_ref[...]
        w = w_ref[0]
        ypre = _dot(pb, w)
        ds_ref[0] = jnp.sum(dm * ypre, axis=0, keepdims=True)
        dy = (dm * s_ref[...]).astype(BF16)
        dpool = _dot(dy, w, NT)
        dw_ref[0, 0] = _dot(pb, dy, TN)
        rows = lax.broadcasted_iota(jnp.int32, dm.shape, 0)
        win = 2 * lax.shift_left(jnp.int32(1), g)
        count = jnp.minimum(rows + 1, win).astype(F32)
        cur, k, sums = dpool / count, 1, []
        for _ in POOL_WINDOWS:
            cur = cur + _shift_up(cur, k, rows)
            sums.append(cur)
            k *= 2
        dx_ref[...] = alpha * dm + _pick(g, sums) - dpool

    blk = pl.BlockSpec((S, Cg), lambda b, g: (b, g))
    return pl.pallas_call(
        body, name="pool_bwd", grid=(B, G),
        in_specs=[blk, blk, pl.BlockSpec((1, Cg, Cg), lambda b, g: (g, 0, 0)), pl.BlockSpec((1, Cg), lambda b, g: (0, g))],
        out_specs=[blk, pl.BlockSpec((1, 1, Cg), lambda b, g: (b, 0, g)),
                   pl.BlockSpec((1, 1, Cg, Cg), lambda b, g: (b, g, 0, 0))],
        out_shape=[SDS((T, D), F32), SDS((B, 1, D), F32), SDS((B, G, Cg, Cg), F32)], compiler_params=_cp(),
    )(dmix, pooled, pw, scale)


_GELU_K = math.sqrt(2.0 / math.pi)
_GELU_C = 0.044715


def _conv(g, cw, cb, rows):
    return cb + cw[0:1] * _shift_down(g, 2, rows) + cw[1:2] * _shift_down(g, 1, rows) + cw[2:3] * g


def ffn_up(hb, wgT, wuT, cw, cb, B, S):
    T, D = hb.shape
    Fd = wgT.shape[0]
    fn = _tile(Fd, 256, 128)

    def body(h_ref, wg_ref, wu_ref, cw_ref, cb_ref, g_ref, u_ref, hh_ref):
        h = h_ref[...]
        g = _dot(h, wg_ref[...], NT)
        u = _dot(h, wu_ref[...], NT)
        rows = lax.broadcasted_iota(jnp.int32, g.shape, 0)
        c = _conv(g, cw_ref[...], cb_ref[...], rows)
        cdf = 0.5 * (1.0 + jnp.tanh(_GELU_K * (c + _GELU_C * (c * c * c))))
        g_ref[...] = g
        u_ref[...] = u
        hh_ref[...] = (c * cdf * u).astype(BF16)

    hspec = pl.BlockSpec((S, D), lambda b, j: (b, 0))
    wspec = pl.BlockSpec((fn, D), lambda b, j: (j, 0))
    ospec = pl.BlockSpec((S, fn), lambda b, j: (b, j))
    return pl.pallas_call(
        body, name="ffn_up", grid=(B, Fd // fn),
        in_specs=[hspec, wspec, wspec, pl.BlockSpec((3, fn), lambda b, j: (0, j)), pl.BlockSpec((1, fn), lambda b, j: (0, j))],
        out_specs=[ospec, ospec, ospec],
        out_shape=[SDS((T, Fd), F32), SDS((T, Fd), F32), SDS((T, Fd), BF16)], compiler_params=_cp(),
    )(hb, wgT, wuT, cw, cb)


def ffn_mid_bwd(dfb, wd, g, u, cw, cb, B, S):
    T, D = dfb.shape
    Fd = wd.shape[0]
    fn = _tile(Fd, 256, 128)

    def body(df_ref, wd_ref, g_ref, u_ref, cw_ref, cb_ref, dg_ref, du_ref, dcb_ref, dcw_ref):
        dhh = _dot(df_ref[...], wd_ref[...], NT)
        gv = g_ref[...]
        uv = u_ref[...]
        cw = cw_ref[...]
        rows = lax.broadcasted_iota(jnp.int32, gv.shape, 0)
        g1 = _shift_down(gv, 1, rows)
        g2 = _shift_down(gv, 2, rows)
        c = cb_ref[...] + cw[0:1] * g2 + cw[1:2] * g1 + cw[2:3] * gv
        c2 = c * c
        th = jnp.tanh(_GELU_K * (c + _GELU_C * (c2 * c)))
        cdf = 0.5 * (1.0 + th)
        du_ref[...] = (dhh * (c * cdf)).astype(BF16)
        dgelu = cdf + c * (0.5 * (1.0 - th * th) * (_GELU_K * (1.0 + 3.0 * _GELU_C * c2)))
        dc = dhh * uv * dgelu
        dcb_ref[0] = jnp.sum(dc, axis=0, keepdims=True)
        dcw_ref[0] = jnp.concatenate(
            [jnp.sum(dc * g2, axis=0, keepdims=True), jnp.sum(dc * g1, axis=0, keepdims=True),
             jnp.sum(dc * gv, axis=0, keepdims=True)], axis=0)
        dg = cw[2:3] * dc + cw[1:2] * _shift_up(dc, 1, rows) + cw[0:1] * _shift_up(dc, 2, rows)
        dg_ref[...] = dg.astype(BF16)

    tspec = pl.BlockSpec((S, fn), lambda b, j: (b, j))
    return pl.pallas_call(
        body, name="ffn_mid_bwd", grid=(B, Fd // fn),
        in_specs=[pl.BlockSpec((S, D), lambda b, j: (b, 0)), pl.BlockSpec((fn, D), lambda b, j: (j, 0)), tspec, tspec,
                  pl.BlockSpec((3, fn), lambda b, j: (0, j)), pl.BlockSpec((1, fn), lambda b, j: (0, j))],
        out_specs=[tspec, tspec, pl.BlockSpec((1, 1, fn), lambda b, j: (b, 0, j)),
                   pl.BlockSpec((1, 3, fn), lambda b, j: (b, 0, j))],
        out_shape=[SDS((T, Fd), BF16), SDS((T, Fd), BF16), SDS((B, 1, Fd), F32), SDS((B, 3, Fd), F32)],
        compiler_params=_cp(),
    )(dfb, wd, g, u, cw, cb)


def ffn_dx(dg, du, wgT, wuT, res, alpha):
    T, Fd = dg.shape
    D = wgT.shape[1]
    tm = _tile(T, 256, 16)

    def body(dg_ref, du_ref, wg_ref, wu_ref, r_ref, o_ref):
        o_ref[...] = alpha * r_ref[...] + _dot(dg_ref[...], wg_ref[...]) + _dot(du_ref[...], wu_ref[...])

    a_spec = pl.BlockSpec((tm, Fd), lambda i: (i, 0))
    w_spec = pl.BlockSpec((Fd, D), lambda i: (0, 0))
    o_spec = pl.BlockSpec((tm, D), lambda i: (i, 0))
    return pl.pallas_call(
        body, name="ffn_dx", grid=(T // tm,), in_specs=[a_spec, a_spec, w_spec, w_spec, o_spec], out_specs=o_spec,
        out_shape=SDS((T, D), F32), compiler_params=_cp(),
    )(dg, du, wgT, wuT, res)


def _partner_all(x):
    n = x.shape[0]
    r = lax.broadcasted_iota(jnp.int32, x.shape, 0)
    return jnp.where((r % HEAD_DIM) < HEAD_DIM // 2, pltpu.roll(x, n - HEAD_DIM // 2, 0), pltpu.roll(x, HEAD_DIM // 2, 0))


def proj_T(w, xT3, cosT, sinT, blk_off, rope, scale, name):
    G, K, T = xT3.shape
    S = cosT.shape[2]
    Dout = K
    tt = _tile(S, 512, 128)
    H = Dout // HEAD_DIM
    nS = S // tt

    def body(w_ref, x_ref, c_ref, s_ref, o_ref):
        acc = _dot(w_ref[...], x_ref[0])
        if rope:
            cos = jnp.tile(c_ref[0], (H, 1))
            sin = jnp.tile(s_ref[0], (H, 1))
            acc = acc * cos + _partner_all(acc) * sin
        if scale != 1.0:
            acc = acc * scale
        o_ref[0] = acc.astype(BF16)

    tab = pl.BlockSpec((1, HEAD_DIM, tt), lambda g, j: (g, 0, j % nS))
    return pl.pallas_call(
        body, name=name, grid=(G, T // tt),
        in_specs=[pl.BlockSpec((Dout, K), lambda g, j: (g + blk_off, 0)), pl.BlockSpec((1, K, tt), lambda g, j: (g, 0, j)), tab, tab],
        out_specs=pl.BlockSpec((1, Dout, tt), lambda g, j: (g, 0, j)),
        out_shape=SDS((G, Dout, T), BF16), compiler_params=_cp(),
    )(w, xT3, cosT, sinT)


def _attn_masks():
    kj = lax.broadcasted_iota(jnp.int32, (2 * BLK, BLK), 0)
    qi = lax.broadcasted_iota(jnp.int32, (2 * BLK, BLK), 1)
    cur = (kj >= BLK) & (kj - BLK <= qi)
    prev = (kj < BLK) & (kj >= qi)
    return cur, prev


def _nb_minus1(g, S):
    vals = [S // (d * BLK) - 1 for d in DILATIONS]
    return _pick(g, [jnp.int32(v) for v in vals])


def attn_fwd(qT, kT, vT, B, S):
    G, D, T = qT.shape
    H = D // HEAD_DIM
    nblk = S // BLK

    def body(q_ref, k_ref, v_ref, o_ref, l_ref):
        nbm1 = _nb_minus1(pl.program_id(0), S)
        cur, prev = _attn_masks()
        for n in range(nblk):
            p0 = max(n - 1, 0)
            has_prev = (n & nbm1) != 0
            qn = q_ref[0, :, n * BLK:(n + 1) * BLK]
            kk = jnp.concatenate([k_ref[0, :, p0 * BLK:(p0 + 1) * BLK], k_ref[0, :, n * BLK:(n + 1) * BLK]], axis=1)
            vv = jnp.concatenate([v_ref[0, :, p0 * BLK:(p0 + 1) * BLK], v_ref[0, :, n * BLK:(n + 1) * BLK]], axis=1)
            sT = _dot(kk, qn, TN)
            sT = jnp.where(cur | (prev & has_prev), sT, NEG)
            m = jnp.max(sT, axis=0, keepdims=True)
            p = jnp.exp(sT - m)
            l = jnp.sum(p, axis=0, keepdims=True)
            o_ref[0, :, n * BLK:(n + 1) * BLK] = _dot(vv, p.astype(BF16)) / l
            l_ref[0, 0, :, n * BLK:(n + 1) * BLK] = m + jnp.log(l)

    spec = pl.BlockSpec((1, HEAD_DIM, S), lambda g, b, h: (g, h, b))
    return pl.pallas_call(
        body, name="attn_fwd", grid=(G, B, H), in_specs=[spec, spec, spec],
        out_specs=[spec, pl.BlockSpec((1, 1, 1, S), lambda g, b, h: (g, h, 0, b))],
        out_shape=[SDS((G, D, T), F32), SDS((G, H, 1, T), F32)], compiler_params=_cp(),
    )(qT, kT, vT)


def attn_combine(oT3, lse3):
    G, D, T = oT3.shape
    H = D // HEAD_DIM
    tn = _tile(T, 2048, 128)

    def body(o_ref, l_ref, ob_ref, of_ref, lt_ref):
        ls = [l_ref[g, 0] for g in range(G)]
        m = functools.reduce(jnp.maximum, ls)
        es = [jnp.exp(v - m) for v in ls]
        z = functools.reduce(lambda a, b: a + b, es)
        o = (es[0] / z) * o_ref[0]
        for g in range(1, G):
            o = o + (es[g] / z) * o_ref[g]
        ob_ref[...] = o.astype(BF16)
        of_ref[...] = o
        lt_ref[0] = m + jnp.log(z)

    return pl.pallas_call(
        body, name="attn_combine", grid=(H, T // tn),
        in_specs=[pl.BlockSpec((G, HEAD_DIM, tn), lambda h, j: (0, h, j)), pl.BlockSpec((G, 1, 1, tn), lambda h, j: (0, h, 0, j))],
        out_specs=[pl.BlockSpec((HEAD_DIM, tn), lambda h, j: (h, j)), pl.BlockSpec((HEAD_DIM, tn), lambda h, j: (h, j)),
                   pl.BlockSpec((1, 1, tn), lambda h, j: (h, 0, j))],
        out_shape=[SDS((D, T), BF16), SDS((D, T), F32), SDS((H, 1, T), F32)], compiler_params=_cp(),
    )(oT3, lse3)


def attn_delta(doT, oT):
    D, T = doT.shape
    H = D // HEAD_DIM
    tn = _tile(T, 2048, 128)

    def body(d_ref, o_ref, r_ref):
        r_ref[0] = jnp.sum(d_ref[...].astype(F32) * o_ref[...], axis=0, keepdims=True)

    spec = pl.BlockSpec((HEAD_DIM, tn), lambda h, j: (h, j))
    return pl.pallas_call(
        body, name="attn_delta", grid=(H, T // tn), in_specs=[spec, spec],
        out_specs=pl.BlockSpec((1, 1, tn), lambda h, j: (h, 0, j)),
        out_shape=SDS((H, 1, T), F32), compiler_params=_cp(),
    )(doT, oT)


def attn_bwd(qT, kT, vT, doT3, lse3, delta3, cosT, sinT, q_scale, B, S):
    G, D, T = qT.shape
    H = D // HEAD_DIM
    nblk = S // BLK
    half = HEAD_DIM // 2

    def body(q_ref, k_ref, v_ref, do_ref, l_ref, d_ref, c_ref, s_ref, dq_ref, dk_ref, dv_ref):
        nbm1 = _nb_minus1(pl.program_id(0), S)
        cur, prev = _attn_masks()
        dk_ref[...] = jnp.zeros_like(dk_ref)
        dv_ref[...] = jnp.zeros_like(dv_ref)
        for n in range(nblk):
            p0 = max(n - 1, 0)
            sl_n = slice(n * BLK, (n + 1) * BLK)
            sl_p = slice(p0 * BLK, (p0 + 1) * BLK)
            has_prev = (n & nbm1) != 0
            qn = q_ref[0, :, sl_n]
            don = do_ref[0, :, sl_n]
            kk = jnp.concatenate([k_ref[0, :, sl_p], k_ref[0, :, sl_n]], axis=1)
            vv = jnp.concatenate([v_ref[0, :, sl_p], v_ref[0, :, sl_n]], axis=1)
            sT = _dot(kk, qn, TN)
            sT = jnp.where(cur | (prev & has_prev), sT, NEG)
            pT = jnp.exp(sT - l_ref[0, 0, :, sl_n])
            dpT = _dot(vv, don, TN)
            dsT = (pT * (dpT - d_ref[0, 0, :, sl_n])).astype(BF16)
            dvb = _dot(don, pT.astype(BF16), NT)
            dkb = _dot(qn, dsT, NT)
            dv_ref[0, :, sl_p] += dvb[:, :BLK]
            dv_ref[0, :, sl_n] += dvb[:, BLK:]
            dk_ref[0, :, sl_p] += dkb[:, :BLK]
            dk_ref[0, :, sl_n] += dkb[:, BLK:]
            dq = _dot(kk, dsT)
            dq = dq * c_ref[0, :, sl_n] - pltpu.roll(dq, half, 0) * s_ref[0, :, sl_n]
            dq_ref[0, :, sl_n] = (dq * q_scale).astype(BF16)
        dk = dk_ref[0]
        dk_ref[0] = dk * c_ref[0] - pltpu.roll(dk, half, 0) * s_ref[0]

    spec = pl.BlockSpec((1, HEAD_DIM, S), lambda g, b, h: (g, h, b))
    sspec = pl.BlockSpec((1, 1, 1, S), lambda g, b, h: (g, h, 0, b))
    tab = pl.BlockSpec((1, HEAD_DIM, S), lambda g, b, h: (g, 0, 0))
    return pl.pallas_call(
        body, name="attn_bwd", grid=(G, B, H),
        in_specs=[spec, spec, spec, spec, sspec, sspec, tab, tab], out_specs=[spec, spec, spec],
        out_shape=[SDS((G, D, T), BF16), SDS((G, D, T), F32), SDS((G, D, T), F32)], compiler_params=_cp(),
    )(qT, kT, vT, doT3, lse3, delta3, cosT, sinT)


def adamw(w, g, m, v, name):
    R, C = w.shape
    tr = _tile(R, 512, 8)

    def body(w_ref, g_ref, m_ref, v_ref, d_ref, nm_ref, nv_ref):
        gv = g_ref[...]
        nm = ADAM_B1 * m_ref[...] + (1.0 - ADAM_B1) * gv
        nv = ADAM_B2 * v_ref[...] + (1.0 - ADAM_B2) * (gv * gv)
        m_hat = nm / (1.0 - ADAM_B1 ** ADAM_STEP)
        v_hat = nv / (1.0 - ADAM_B2 ** ADAM_STEP)
        d_ref[...] = -ADAM_LR * (m_hat / (jnp.sqrt(v_hat) + ADAM_EPS) + ADAM_WD * w_ref[...])
        nm_ref[...] = nm
        nv_ref[...] = nv

    spec = pl.BlockSpec((tr, C), lambda i: (i, 0))
    return pl.pallas_call(
        body, name=name, grid=(R // tr,), in_specs=[spec] * 4, out_specs=[spec] * 3,
        out_shape=[SDS((R, C), F32)] * 3, compiler_params=_cp(),
    )(w, g, m, v)


def _perm(a, B, S, d):
    if d == 1:
        return a
    lead = a.shape[:-1]
    return a.reshape(*lead, B, S // d, d).swapaxes(-1, -2).reshape(*lead, B * S)


def _unperm(a, B, S, d):
    if d == 1:
        return a
    lead = a.shape[:-1]
    return a.reshape(*lead, B, d, S // d).swapaxes(-1, -2).reshape(*lead, B * S)


def _perm3(a, B, S):
    return jnp.stack([_perm(a, B, S, d) for d in DILATIONS])


def _unperm3(a3, B, S):
    return jnp.stack([_unperm(a3[i], B, S, d) for i, d in enumerate(DILATIONS)])


def _xT3(xb, B, S):
    D = xb.shape[1]
    outs = []
    for d in DILATIONS:
        outs.append(xb.reshape(B, S // d, d, D).transpose(3, 0, 2, 1).reshape(D, B * S))
    return jnp.stack(outs)


def _rope_tables(S):
    half = HEAD_DIM // 2
    inv_freq = ROPE_THETA ** (-jnp.arange(0, HEAD_DIM, 2, dtype=F32) / HEAD_DIM)
    ang = jnp.arange(S, dtype=F32)[:, None] * inv_freq[None, :]
    cos = jnp.concatenate([jnp.cos(ang), jnp.cos(ang)], axis=1).T
    sin = jnp.concatenate([-jnp.sin(ang), jnp.sin(ang)], axis=1).T
    return _perm3(cos, 1, S), _perm3(sin, 1, S)


def kernel(x, pool_w, pool_scale, w_q, w_kv, w_o, ffn_w_gate, ffn_w_up, ffn_conv_w, ffn_conv_b, ffn_w_down, ln1_g, ln1_b, ln2_g, ln2_b, loss_target, m_pool_w, m_pool_scale, m_w_q, m_w_kv, m_w_o, m_ffn_w_gate, m_ffn_w_up, m_ffn_conv_w, m_ffn_conv_b, m_ffn_w_down, m_ln1_g, m_ln1_b, m_ln2_g, m_ln2_b, v_pool_w, v_pool_scale, v_w_q, v_w_kv, v_w_o, v_ffn_w_gate, v_ffn_w_up, v_ffn_conv_w, v_ffn_conv_b, v_ffn_w_down, v_ln1_g, v_ln1_b, v_ln2_g, v_ln2_b):
    B, S, D = x.shape
    T = B * S
    depth = ln1_g.shape[0]
    nA, nB = pool_w.shape[0], w_q.shape[0]
    Fs = ffn_w_down.shape[1]
    Fd = Fs * N_DEV
    H = D // HEAD_DIM
    G = len(DILATIONS)
    PG = len(POOL_WINDOWS)
    Cg = D // PG
    alpha = (2.0 * depth) ** 0.25
    me = 4 * lax.axis_index("x") + 2 * lax.axis_index("y") + lax.axis_index("c")

    qs, kvs, os_ = w_q.shape[2], w_kv.shape[1], w_o.shape[1]
    pool_rows = pool_w.size // D
    pieces = [jnp.swapaxes(w_q, 1, 2).reshape(nB * qs, D), w_kv.T, w_o.reshape(nB * os_, D),
              jnp.swapaxes(ffn_w_gate, 1, 2).reshape(depth * Fs, D), jnp.swapaxes(ffn_w_up, 1, 2).reshape(depth * Fs, D),
              ffn_w_down.reshape(depth * Fs, D), pool_w.reshape(pool_rows, D)]
    sizes = [p.shape[0] for p in pieces]
    big = all_gather_blocks(jnp.concatenate(pieces, axis=0).astype(BF16), "gather_weights", in_vmem=False)
    offs = [sum(sizes[:i]) for i in range(len(sizes))]

    def full(i, per_layer, nl):
        blk = big[:, offs[i]:offs[i] + sizes[i]].reshape(N_DEV, nl, per_layer, D)
        return blk.transpose(1, 0, 2, 3).reshape(nl, N_DEV * per_layer, D)

    WqT = full(0, qs, nB)
    WkvT = full(1, kvs, 1)[0]
    Wo = full(2, os_, nB)
    WgT, WuT, Wd = full(3, Fs, depth), full(4, Fs, depth), full(5, Fs, depth)
    PW = big[:, offs[6]:offs[6] + sizes[6]].reshape(N_DEV, nA, PG, Cg // N_DEV, Cg)
    PW = PW.transpose(1, 2, 0, 3, 4).reshape(nA, PG, Cg, Cg)

    sm_cols = 128
    sm_local = jnp.concatenate([ffn_conv_w.reshape(-1), pool_scale.reshape(-1)])
    sm_rows = -(-sm_local.size // sm_cols)
    sm_rows_p = -(-sm_rows // 8) * 8
    sm_local = jnp.pad(sm_local, (0, sm_rows_p * sm_cols - sm_local.size)).reshape(sm_rows_p, sm_cols)
    sm = all_gather_blocks(sm_local, "gather_small", in_vmem=True).reshape(N_DEV, -1)
    ncw = ffn_conv_w.size
    conv_w_full = sm[:, :ncw].reshape(N_DEV, depth, 3, Fs).transpose(1, 2, 0, 3).reshape(depth, 3, Fd)
    pool_scale_full = sm[:, ncw:ncw + pool_scale.size].reshape(N_DEV, nA, D // N_DEV).transpose(1, 0, 2).reshape(nA, 1, D)

    cosT, sinT = _rope_tables(S)

    xs = x.reshape(T, D)
    saved = []
    cur, curb = xs, None
    kT = vT = x1T3 = None
    for i in range(depth):
        sv = {}
        if i < nA:
            mix, pooled = pool_fwd(cur, PW[i], pool_scale_full[i], B, S)
            sv["pooled"] = pooled
        else:
            j = i - nA
            xT3 = x1T3 if j == 0 else _xT3(curb, B, S)
            qT = proj_T(WqT[j], xT3, cosT, sinT, 0, True, HEAD_DIM ** -0.5, "q_proj")
            oT3p, lse3p = attn_fwd(qT, kT, vT, B, S)
            oTb, oTf, lse_tot = attn_combine(_unperm3(oT3p, B, S), _unperm3(lse3p, B, S))
            mix = matmul_rows(oTb, Wo[j], TN, "o_proj")
            sv.update(xT3=xT3, qT=qT, oTb=oTb, oTf=oTf, lse_tot=lse_tot)
        a1, h, hb = add_ln(cur, mix, ln1_g[i], ln1_b[i], alpha)
        g, u, hh = ffn_up(hb, WgT[i], WuT[i], conv_w_full[i], ffn_conv_b[i].reshape(1, Fd), B, S)
        f = matmul_rows(hh, Wd[i], NN, "ffn_down")
        a2, cur, curb = add_ln(h, f, ln2_g[i], ln2_b[i], alpha)
        sv.update(a1=a1, hb=hb, g=g, u=u, hh=hh, a2=a2)
        saved.append(sv)
        if i == nA - 1:
            x1T3 = _xT3(curb, B, S)
            kT = proj_T(WkvT, x1T3, cosT, sinT, 0, True, 1.0, "k_proj")
            vT = proj_T(WkvT, x1T3, cosT, sinT, G, False, 1.0, "v_proj")

    dy, sq = loss_grad(cur, loss_target.reshape(T, D))

    gparts = {}
    small = {k: [None] * depth for k in ("ln1_g", "ln1_b", "ln2_g", "ln2_b", "conv_b", "conv_w")}
    dscale = [None] * nA
    dpw = [None] * nA
    dWq, dWo_, dWg, dWu, dWd_ = [None] * nB, [None] * nB, [None] * depth, [None] * depth, [None] * depth
    dkT_acc = dvT_acc = None
    dcur = dy
    for i in reversed(range(depth)):
        sv = saved[i]
        db2, db2b, small["ln2_g"][i], small["ln2_b"][i] = ln_bwd(dcur, sv["a2"], ln2_g[i])
        dg_, du_, dcb, dcw = ffn_mid_bwd(db2b, Wd[i], sv["g"], sv["u"], conv_w_full[i], ffn_conv_b[i].reshape(1, Fd), B, S)
        small["conv_b"][i] = jnp.sum(dcb, axis=0)
        small["conv_w"][i] = jnp.sum(dcw, axis=0)
        dWd_[i] = wgrad_rows(sv["hh"], db2b, "wgrad_down")
        dWg[i] = wgrad_rows(dg_, sv["hb"], "wgrad_gate")
        dWu[i] = wgrad_rows(du_, sv["hb"], "wgrad_up")
        dh = ffn_dx(dg_, du_, WgT[i], WuT[i], db2, alpha)
        da1, da1b, small["ln1_g"][i], small["ln1_b"][i] = ln_bwd(dh, sv["a1"], ln1_g[i])
        if i < nA:
            dcur, dsp, dpwp = pool_bwd(da1, sv["pooled"], PW[i], pool_scale_full[i], alpha, B, S)
            dscale[i] = jnp.sum(dsp, axis=0)
            dpw[i] = jnp.sum(dpwp, axis=0)
        else:
            j = i - nA
            doT = matmul_to_T(Wo[j], da1b, "o_proj_bwd")
            dWo_[j] = wgrad_mixed(sv["oTb"], da1b, "wgrad_o")
            delta = attn_delta(doT, sv["oTf"])
            dqT, dkT, dvT = attn_bwd(sv["qT"], kT, vT, _perm3(doT, B, S), _perm3(sv["lse_tot"], B, S),
                                     _perm3(delta, B, S), cosT, sinT, HEAD_DIM ** -0.5, B, S)
            dWq[j] = wgrad_T(dqT, sv["xT3"], "wgrad_q").reshape(G * D, D)
            dcur = matmul_rows(_unperm3(dqT, B, S).reshape(G * D, T), WqT[j], TN, "q_proj_bwd", res=da1, alpha=alpha)
            dkT_acc = dkT if dkT_acc is None else dkT_acc + dkT
            dvT_acc = dvT if dvT_acc is None else dvT_acc + dvT
            if j == 0:
                dkvT = jnp.concatenate([dkT_acc, dvT_acc], axis=0).astype(BF16)
                dWkv = wgrad_T(dkvT, x1T3, "wgrad_kv").reshape(2 * G * D, D)
                dkv_tok = jnp.concatenate([_unperm3(dkvT[:G], B, S), _unperm3(dkvT[G:], B, S)], axis=0)
                dcur = matmul_rows(dkv_tok.reshape(2 * G * D, T), WkvT, TN, "kv_proj_bwd", res=dcur, alpha=1.0, tm_target=256)
    grad_x = dcur.reshape(B, S, D)

    def blocks(a, rows):
        return a.reshape(N_DEV, rows, D)

    dpw_all = jnp.stack(dpw).reshape(nA, PG, N_DEV, Cg // N_DEV, Cg).transpose(2, 0, 1, 3, 4).reshape(N_DEV, pool_rows, D)
    parts = ([blocks(dWq[j], qs) for j in range(nB)] + [blocks(dWkv, kvs)] + [blocks(dWo_[j], os_) for j in range(nB)]
             + [blocks(dWg[i], Fs) for i in range(depth)] + [blocks(dWu[i], Fs) for i in range(depth)]
             + [blocks(dWd_[i], Fs) for i in range(depth)] + [dpw_all.astype(BF16)])
    gsum = sum_slots(scatter_partials(parts, "scatter_grads"), "sum_grads")

    def take(i):
        return gsum[offs[i]:offs[i] + sizes[i]]

    g_w_q = jnp.swapaxes(take(0).reshape(nB, qs, D), 1, 2)
    g_w_kv = take(1).T
    g_w_o = take(2).reshape(nB, os_, D)
    g_gate = jnp.swapaxes(take(3).reshape(depth, Fs, D), 1, 2)
    g_up = jnp.swapaxes(take(4).reshape(depth, Fs, D), 1, 2)
    g_down = take(5).reshape(depth, Fs, D)
    g_pool_w = take(6).reshape(pool_w.shape)

    def rows_of(a):
        a = a.reshape(-1)
        n = -(-a.size // D) * D
        return jnp.pad(a, (0, n - a.size)).reshape(-1, D)

    sm_parts = [rows_of(jnp.concatenate(small[k], axis=0)) for k in ("ln1_g", "ln1_b", "ln2_g", "ln2_b")]
    sm_parts += [rows_of(jnp.stack(small["conv_b"])), rows_of(jnp.stack(small["conv_w"])), rows_of(jnp.stack(dscale)), sq]
    sm_sizes = [p.shape[0] for p in sm_parts]
    sm_all = jnp.concatenate(sm_parts, axis=0)
    pad_rows = -(-sm_all.shape[0] // 8) * 8 - sm_all.shape[0]
    sm_all = jnp.pad(sm_all, ((0, pad_rows), (0, 0)))
    sm_sum = sum_slots(all_gather_blocks(sm_all, "gather_small_grads", in_vmem=True), "sum_small_grads")
    sm_offs = [sum(sm_sizes[:i]) for i in range(len(sm_sizes))]

    def sm_take(i, shape):
        n = math.prod(shape)
        return sm_sum[sm_offs[i]:sm_offs[i] + sm_sizes[i]].reshape(-1)[:n].reshape(shape)

    g_ln1_g, g_ln1_b = sm_take(0, (depth, D)), sm_take(1, (depth, D))
    g_ln2_g, g_ln2_b = sm_take(2, (depth, D)), sm_take(3, (depth, D))
    g_conv_b = sm_take(4, (depth, Fd))
    g_conv_w = lax.dynamic_slice_in_dim(sm_take(5, (depth, 3, Fd)), me * Fs, Fs, axis=2)
    g_pool_scale = lax.dynamic_slice_in_dim(sm_take(6, (nA, D)), me * (D // N_DEV), D // N_DEV, axis=1)
    loss = (0.5 / D) * jnp.sum(sm_take(7, (D,)))

    def v2(a):
        return a.reshape(-1, a.shape[-1])

    names = ["pool_w", "pool_scale", "w_q", "w_kv", "w_o", "ffn_w_gate", "ffn_w_up", "ffn_conv_w", "ffn_conv_b",
             "ffn_w_down", "ln1_g", "ln1_b", "ln2_g", "ln2_b"]
    ws = [pool_w, pool_scale, w_q, w_kv, w_o, ffn_w_gate, ffn_w_up, ffn_conv_w, ffn_conv_b, ffn_w_down, ln1_g, ln1_b, ln2_g, ln2_b]
    ms = [m_pool_w, m_pool_scale, m_w_q, m_w_kv, m_w_o, m_ffn_w_gate, m_ffn_w_up, m_ffn_conv_w, m_ffn_conv_b, m_ffn_w_down, m_ln1_g, m_ln1_b, m_ln2_g, m_ln2_b]
    vs = [v_pool_w, v_pool_scale, v_w_q, v_w_kv, v_w_o, v_ffn_w_gate, v_ffn_w_up, v_ffn_conv_w, v_ffn_conv_b, v_ffn_w_down, v_ln1_g, v_ln1_b, v_ln2_g, v_ln2_b]
    gs = [g_pool_w, g_pool_scale, g_w_q, g_w_kv, g_w_o, g_gate, g_up, g_conv_w, g_conv_b, g_down, g_ln1_g, g_ln1_b, g_ln2_g, g_ln2_b]
    deltas, new_ms, new_vs = [], [], []
    for nm, w, gr, m_, v_ in zip(names, ws, gs, ms, vs):
        d_, nm_, nv_ = adamw(v2(w), v2(gr), v2(m_), v2(v_), "adamw_" + nm)
        deltas.append(d_.reshape(w.shape))
        new_ms.append(nm_.reshape(w.shape))
        new_vs.append(nv_.reshape(w.shape))

    return (loss, grad_x, *gs, *deltas, *new_ms, *new_vs)
```

```python
import functools
import math

import jax
import jax.numpy as jnp
from jax import lax
from jax.experimental import pallas as pl
from jax.experimental.pallas import tpu as pltpu

F32 = jnp.float32
BF16 = jnp.bfloat16
SDS = jax.ShapeDtypeStruct
MESH = pl.DeviceIdType.MESH

N_DEV = 8
HEAD_DIM = 64
BLK = 128
DILATIONS = (1, 4, 16)
POOL_WINDOWS = (2, 4, 8, 16)
ROPE_THETA = 10000.0
LN_EPS = 1e-5
NEG = -1e30
V7X_VMEM_LIMIT = 56 * 1024 * 1024

ADAM_LR, ADAM_B1, ADAM_B2, ADAM_EPS, ADAM_WD, ADAM_STEP = 0.001, 0.9, 0.999, 1e-08, 0.01, 10

NN = (((1,), (0,)), ((), ()))
NT = (((1,), (1,)), ((), ()))
TN = (((0,), (0,)), ((), ()))


def _cp(sem=None):
    kw = dict(vmem_limit_bytes=V7X_VMEM_LIMIT)
    if sem is not None:
        kw["dimension_semantics"] = sem
    return pltpu.CompilerParams(**kw)


def _dot(a, b, dims=NN):
    return lax.dot_general(a, b, dims, preferred_element_type=F32)


def _tile(n, target, mult):
    best = None
    for t in range(mult, min(n, target) + 1, mult):
        if n % t == 0:
            best = t
    return best if best is not None else n


def _mesh_pos():
    return lax.axis_index("x"), lax.axis_index("y"), lax.axis_index("c")


def all_gather_blocks(xl, name, in_vmem):
    R, C = xl.shape
    space = pltpu.VMEM if in_vmem else pl.ANY

    def body(x_ref, out_ref, send_sems, recv_sems, local_sem):
        x, y, c = _mesh_pos()
        me, sibling = (x, y, c), (x, y, 1 - c)
        chips = [(1 - x, y), (x, 1 - y), (1 - x, 1 - y)]

        def slot(px, py, pc):
            return out_ref.at[4 * px + 2 * py + pc]

        def copy(k, block, to, src=None):
            return pltpu.make_async_remote_copy(
                src_ref=slot(*block) if src is None else src, dst_ref=slot(*block),
                send_sem=send_sems.at[k], recv_sem=recv_sems.at[k], device_id=to, device_id_type=MESH)

        mine = pltpu.make_async_copy(x_ref, slot(*me), local_sem)
        mine.start()
        first = [copy(0, me, sibling, src=x_ref)]
        first += [copy(1 + j, me, (*chip, c), src=x_ref) for j, chip in enumerate(chips)]
        for cp in first:
            cp.start()
        passed = [copy(4 + j, (*chip, c), sibling) for j, chip in enumerate(chips)]
        for j, chip in enumerate(chips):
            copy(1 + j, (*chip, c), me).wait_recv()
            passed[j].start()
        copy(0, sibling, me).wait_recv()
        for j, chip in enumerate(chips):
            copy(4 + j, (*chip, 1 - c), me).wait_recv()
        for cp in first + passed:
            cp.wait_send()
        mine.wait()

    return pl.pallas_call(
        body, name=name,
        out_shape=SDS((N_DEV, R, C), xl.dtype),
        in_specs=[pl.BlockSpec(memory_space=space)],
        out_specs=pl.BlockSpec(memory_space=space),
        scratch_shapes=[pltpu.SemaphoreType.DMA((7,)), pltpu.SemaphoreType.DMA((7,)), pltpu.SemaphoreType.DMA],
        compiler_params=_cp(),
    )(xl)


def scatter_partials(parts, name):
    C = parts[0].shape[2]
    rows = [p.shape[1] for p in parts]
    offs = [sum(rows[:i]) for i in range(len(rows))]
    R = sum(rows)
    n = len(parts)

    def body(*refs):
        part_refs, out_ref = refs[:n], refs[n]
        send_sems, recv_sems, local_sem = refs[n + 1:]
        x, y, c = _mesh_pos()
        me_lin = 4 * x + 2 * y + c
        loc = [pltpu.make_async_copy(part_refs[i].at[me_lin], out_ref.at[me_lin, pl.ds(offs[i], rows[i])], local_sem)
               for i in range(n)]
        for cp in loc:
            cp.start()
        peers = []
        for r in range(1, N_DEV):
            px = 1 - x if (r & 4) else x
            py = 1 - y if (r & 2) else y
            pc = 1 - c if (r & 1) else c
            peers.append((px, py, pc))
        for k, (px, py, pc) in enumerate(peers):
            p_lin = 4 * px + 2 * py + pc
            for i in range(n):
                pltpu.make_async_remote_copy(
                    src_ref=part_refs[i].at[p_lin], dst_ref=out_ref.at[me_lin, pl.ds(offs[i], rows[i])],
                    send_sem=send_sems.at[k], recv_sem=recv_sems.at[k],
                    device_id=(px, py, pc), device_id_type=MESH).start()
        for k, (px, py, pc) in enumerate(peers):
            p_lin = 4 * px + 2 * py + pc
            whole = pltpu.make_async_remote_copy(
                src_ref=out_ref.at[p_lin], dst_ref=out_ref.at[p_lin],
                send_sem=send_sems.at[k], recv_sem=recv_sems.at[k],
                device_id=(px, py, pc), device_id_type=MESH)
            whole.wait_recv()
            whole.wait_send()
        pltpu.make_async_copy(out_ref.at[me_lin], out_ref.at[me_lin], local_sem).wait()

    return pl.pallas_call(
        body, name=name,
        out_shape=SDS((N_DEV, R, C), parts[0].dtype),
        in_specs=[pl.BlockSpec(memory_space=pl.ANY)] * n,
        out_specs=pl.BlockSpec(memory_space=pl.ANY),
        scratch_shapes=[pltpu.SemaphoreType.DMA((7,)), pltpu.SemaphoreType.DMA((7,)), pltpu.SemaphoreType.DMA],
        compiler_params=_cp(),
    )(*parts)


def sum_slots(slots, name, out_dtype=F32):
    _, R, C = slots.shape
    tr = _tile(R, 512, 16)

    def body(s_ref, o_ref):
        acc = s_ref[0].astype(F32)
        for s in range(1, N_DEV):
            acc = acc + s_ref[s].astype(F32)
        o_ref[...] = acc.astype(out_dtype)

    return pl.pallas_call(
        body, name=name, grid=(R // tr,),
        in_specs=[pl.BlockSpec((N_DEV, tr, C), lambda i: (0, i, 0))],
        out_specs=pl.BlockSpec((tr, C), lambda i: (i, 0)),
        out_shape=SDS((R, C), out_dtype), compiler_params=_cp(),
    )(slots)


def add_ln(x, mix, g, b, alpha):
    T, D = x.shape
    tm = _tile(T, 512, 16)

    def body(x_ref, m_ref, g_ref, b_ref, a_ref, y_ref, yb_ref):
        a = alpha * x_ref[...] + m_ref[...]
        mu = jnp.mean(a, axis=-1, keepdims=True)
        xc = a - mu
        var = jnp.mean(xc * xc, axis=-1, keepdims=True)
        y = xc * lax.rsqrt(var + LN_EPS) * g_ref[...] + b_ref[...]
        a_ref[...] = a
        y_ref[...] = y
        yb_ref[...] = y.astype(BF16)

    row = pl.BlockSpec((tm, D), lambda i: (i, 0))
    vec = pl.BlockSpec((1, D), lambda i: (0, 0))
    return pl.pallas_call(
        body, name="add_ln", grid=(T // tm,),
        in_specs=[row, row, vec, vec], out_specs=[row, row, row],
        out_shape=[SDS((T, D), F32), SDS((T, D), F32), SDS((T, D), BF16)], compiler_params=_cp(),
    )(x, mix, g.reshape(1, D), b.reshape(1, D))


def ln_bwd(dy, a, g):
    T, D = a.shape
    tm = _tile(T, 512, 16)

    def body(dy_ref, a_ref, g_ref, da_ref, dab_ref, dg_ref, db_ref):
        @pl.when(pl.program_id(0) == 0)
        def _():
            dg_ref[...] = jnp.zeros_like(dg_ref)
            db_ref[...] = jnp.zeros_like(db_ref)

        av = a_ref[...]
        mu = jnp.mean(av, axis=-1, keepdims=True)
        xc = av - mu
        var = jnp.mean(xc * xc, axis=-1, keepdims=True)
        r = lax.rsqrt(var + LN_EPS)
        xh = xc * r
        dyv = dy_ref[...]
        dxh = dyv * g_ref[...]
        m1 = jnp.mean(dxh, axis=-1, keepdims=True)
        m2 = jnp.mean(dxh * xh, axis=-1, keepdims=True)
        da = r * (dxh - m1 - xh * m2)
        da_ref[...] = da
        dab_ref[...] = da.astype(BF16)
        dg_ref[...] += jnp.sum(dyv * xh, axis=0, keepdims=True)
        db_ref[...] += jnp.sum(dyv, axis=0, keepdims=True)

    row = pl.BlockSpec((tm, D), lambda i: (i, 0))
    vec = pl.BlockSpec((1, D), lambda i: (0, 0))
    return pl.pallas_call(
        body, name="ln_bwd", grid=(T // tm,),
        in_specs=[row, row, vec], out_specs=[row, row, vec, vec],
        out_shape=[SDS((T, D), F32), SDS((T, D), BF16), SDS((1, D), F32), SDS((1, D), F32)],
        compiler_params=_cp(("arbitrary",)),
    )(dy, a, g.reshape(1, D))


def loss_grad(y, tgt):
    T, D = y.shape
    tm = _tile(T, 512, 16)

    def body(y_ref, t_ref, dy_ref, sq_ref):
        @pl.when(pl.program_id(0) == 0)
        def _():
            sq_ref[...] = jnp.zeros_like(sq_ref)

        e = y_ref[...] - t_ref[...]
        dy_ref[...] = e / float(D)
        sq_ref[...] += jnp.sum(e * e, axis=0, keepdims=True)

    row = pl.BlockSpec((tm, D), lambda i: (i, 0))
    vec = pl.BlockSpec((1, D), lambda i: (0, 0))
    return pl.pallas_call(
        body, name="loss_grad", grid=(T // tm,),
        in_specs=[row, row], out_specs=[row, vec],
        out_shape=[SDS((T, D), F32), SDS((1, D), F32)], compiler_params=_cp(("arbitrary",)),
    )(y, tgt)


def matmul_ln(a, w, dims, res, g, b, alpha, name):
    if dims == TN:
        K, T = a.shape
    else:
        T, K = a.shape
    D = w.shape[1]
    tm = _tile(T, 512, 128 if dims == TN else 16)

    def body(a_ref, w_ref, r_ref, g_ref, b_ref, p_ref, y_ref, yb_ref):
        pre = alpha * r_ref[...] + _dot(a_ref[...], w_ref[...], dims)
        mu = jnp.mean(pre, axis=-1, keepdims=True)
        xc = pre - mu
        var = jnp.mean(xc * xc, axis=-1, keepdims=True)
        y = xc * lax.rsqrt(var + LN_EPS) * g_ref[...] + b_ref[...]
        p_ref[...] = pre
        y_ref[...] = y
        yb_ref[...] = y.astype(BF16)

    a_spec = pl.BlockSpec((K, tm), lambda i: (0, i)) if dims == TN else pl.BlockSpec((tm, K), lambda i: (i, 0))
    row = pl.BlockSpec((tm, D), lambda i: (i, 0))
    vec = pl.BlockSpec((1, D), lambda i: (0, 0))
    return pl.pallas_call(
        body, name=name, grid=(T // tm,),
        in_specs=[a_spec, pl.BlockSpec(w.shape, lambda i: (0, 0)), row, vec, vec], out_specs=[row, row, row],
        out_shape=[SDS((T, D), F32), SDS((T, D), F32), SDS((T, D), BF16)], compiler_params=_cp(),
    )(a, w, res, g.reshape(1, D), b.reshape(1, D))


def dx_from_T(aTs, w, res, alpha, name, tm_target):
    T = aTs[0].shape[1]
    N = w.shape[1]
    ks = [a.shape[0] for a in aTs]
    n = len(aTs)
    tm = _tile(T, tm_target, 128)

    def body(*refs):
        a_refs, w_ref, r_ref, o_ref = refs[:n], refs[n], refs[n + 1], refs[n + 2]
        acc = alpha * r_ref[...]
        off = 0
        for a_ref, k in zip(a_refs, ks):
            acc = acc + _dot(a_ref[...], w_ref[off:off + k, :], TN)
            off += k
        o_ref[...] = acc

    row = pl.BlockSpec((tm, N), lambda i: (i, 0))
    return pl.pallas_call(
        body, name=name, grid=(T // tm,),
        in_specs=[pl.BlockSpec((k, tm), lambda i: (0, i)) for k in ks] + [pl.BlockSpec(w.shape, lambda i: (0, 0)), row],
        out_specs=row, out_shape=SDS((T, N), F32), compiler_params=_cp(),
    )(*aTs, w, res)


def matmul_to_T(w, a, name):
    M, K = w.shape
    T = a.shape[0]
    tt = _tile(T, 512, 128)

    def body(w_ref, a_ref, o_ref):
        o_ref[...] = _dot(w_ref[...], a_ref[...], NT).astype(BF16)

    return pl.pallas_call(
        body, name=name, grid=(T // tt,),
        in_specs=[pl.BlockSpec((M, K), lambda i: (0, 0)), pl.BlockSpec((tt, K), lambda i: (i, 0))],
        out_specs=pl.BlockSpec((M, tt), lambda i: (0, i)),
        out_shape=SDS((M, T), BF16), compiler_params=_cp(),
    )(w, a)


def wgrad_rows(a, b, name):
    T, M = a.shape
    N = b.shape[1]
    tt = _tile(T, 512, 16)
    tmm = _tile(M, 1536, 128)

    def body(a_ref, b_ref, o_ref, acc_ref):
        t = pl.program_id(1)

        @pl.when(t == 0)
        def _():
            acc_ref[...] = jnp.zeros_like(acc_ref)

        acc_ref[...] += _dot(a_ref[...], b_ref[...], TN)

        @pl.when(t == pl.num_programs(1) - 1)
        def _():
            o_ref[...] = acc_ref[...].astype(BF16)

    return pl.pallas_call(
        body, name=name, grid=(M // tmm, T // tt),
        in_specs=[pl.BlockSpec((tt, tmm), lambda i, t: (t, i)), pl.BlockSpec((tt, N), lambda i, t: (t, 0))],
        out_specs=pl.BlockSpec((tmm, N), lambda i, t: (i, 0)),
        out_shape=SDS((M, N), BF16), scratch_shapes=[pltpu.VMEM((tmm, N), F32)],
        compiler_params=_cp(("arbitrary", "arbitrary")),
    )(a, b)


def wgrad_T(aT, bT3, g, name):
    M, T = aT.shape
    N = bT3.shape[1]
    tt = _tile(T, 1024, 128)
    nt = T // tt

    def body(a_ref, b_ref, o_ref, acc_ref):
        t = pl.program_id(0)

        @pl.when(t == 0)
        def _():
            acc_ref[...] = jnp.zeros_like(acc_ref)

        acc_ref[...] += _dot(a_ref[...], b_ref[0], NT)

        @pl.when(t == nt - 1)
        def _():
            o_ref[...] = acc_ref[...].astype(BF16)

    return pl.pallas_call(
        body, name=name, grid=(nt,),
        in_specs=[pl.BlockSpec((M, tt), lambda t: (0, t)), pl.BlockSpec((1, N, tt), lambda t: (g, 0, t))],
        out_specs=pl.BlockSpec((M, N), lambda t: (0, 0)),
        out_shape=SDS((M, N), BF16), scratch_shapes=[pltpu.VMEM((M, N), F32)],
        compiler_params=_cp(("arbitrary",)),
    )(aT, bT3)


def wgrad_mixed(aT, b, name):
    M, T = aT.shape
    N = b.shape[1]
    tt = _tile(T, 1024, 128)

    def body(a_ref, b_ref, o_ref, acc_ref):
        t = pl.program_id(0)

        @pl.when(t == 0)
        def _():
            acc_ref[...] = jnp.zeros_like(acc_ref)

        acc_ref[...] += _dot(a_ref[...], b_ref[...], NN)

        @pl.when(t == pl.num_programs(0) - 1)
        def _():
            o_ref[...] = acc_ref[...].astype(BF16)

    return pl.pallas_call(
        body, name=name, grid=(T // tt,),
        in_specs=[pl.BlockSpec((M, tt), lambda t: (0, t)), pl.BlockSpec((tt, N), lambda t: (t, 0))],
        out_specs=pl.BlockSpec((M, N), lambda t: (0, 0)),
        out_shape=SDS((M, N), BF16), scratch_shapes=[pltpu.VMEM((M, N), F32)],
        compiler_params=_cp(("arbitrary",)),
    )(aT, b)


def _shift_down(x, k, rows):
    return jnp.where(rows >= k, pltpu.roll(x, k, 0), 0.0)


def _shift_up(x, k, rows):
    n = x.shape[0]
    return jnp.where(rows < n - k, pltpu.roll(x, n - k, 0), 0.0)


def _pick(g, vals):
    out = vals[-1]
    for k in range(len(vals) - 2, -1, -1):
        out = jnp.where(g == k, vals[k], out)
    return out


def pool_fwd(x, pw, scale, B, S):
    T, D = x.shape
    G = len(POOL_WINDOWS)
    Cg = D // G

    def body(x_ref, w_ref, s_ref, mix_ref, pooled_ref):
        g = pl.program_id(1)
        xv = x_ref[...]
        rows = lax.broadcasted_iota(jnp.int32, xv.shape, 0)
        sums, cur, k = [], xv, 1
        for _ in POOL_WINDOWS:
            cur = cur + _shift_down(cur, k, rows)
            sums.append(cur)
            k *= 2
        win = 2 * lax.shift_left(jnp.int32(1), g)
        total = _pick(g, sums)
        count = jnp.minimum(rows + 1, win).astype(F32)
        pooled = total / count - xv
        pb = pooled.astype(BF16)
        pooled_ref[...] = pb
        mix_ref[...] = _dot(pb, w_ref[0]) * s_ref[...]

    blk = pl.BlockSpec((S, Cg), lambda b, g: (b, g))
    return pl.pallas_call(
        body, name="pool_fwd", grid=(B, G),
        in_specs=[blk, pl.BlockSpec((1, Cg, Cg), lambda b, g: (g, 0, 0)), pl.BlockSpec((1, Cg), lambda b, g: (0, g))],
        out_specs=[blk, blk],
        out_shape=[SDS((T, D), F32), SDS((T, D), BF16)], compiler_params=_cp(),
    )(x, pw, scale)


def pool_bwd(dmix, pooled, pw, scale, alpha, B, S):
    T, D = dmix.shape
    G = len(POOL_WINDOWS)
    Cg = D // G

    def body(d_ref, p_ref, w_ref, s_ref, dx_ref, ds_ref, dw_ref):
        g = pl.program_id(1)
        dm = d_ref[...]
        pb = p_ref[...]
        w = w_ref[0]
        ypre = _dot(pb, w)
        ds_ref[0] = jnp.sum(dm * ypre, axis=0, keepdims=True)
        dy = (dm * s_ref[...]).astype(BF16)
        dpool = _dot(dy, w, NT)
        dw_ref[0, 0] = _dot(pb, dy, TN)
        rows = lax.broadcasted_iota(jnp.int32, dm.shape, 0)
        win = 2 * lax.shift_left(jnp.int32(1), g)
        count = jnp.minimum(rows + 1, win).astype(F32)
        cur, k, sums = dpool / count, 1, []
        for _ in POOL_WINDOWS:
            cur = cur + _shift_up(cur, k, rows)
            sums.append(cur)
            k *= 2
        dx_ref[...] = alpha * dm + _pick(g, sums) - dpool

    blk = pl.BlockSpec((S, Cg), lambda b, g: (b, g))
    return pl.pallas_call(
        body, name="pool_bwd", grid=(B, G),
        in_specs=[blk, blk, pl.BlockSpec((1, Cg, Cg), lambda b, g: (g, 0, 0)), pl.BlockSpec((1, Cg), lambda b, g: (0, g))],
        out_specs=[blk, pl.BlockSpec((1, 1, Cg), lambda b, g: (b, 0, g)),
                   pl.BlockSpec((1, 1, Cg, Cg), lambda b, g: (b, g, 0, 0))],
        out_shape=[SDS((T, D), F32), SDS((B, 1, D), F32), SDS((B, G, Cg, Cg), F32)], compiler_params=_cp(),
    )(dmix, pooled, pw, scale)


_GELU_K = math.sqrt(2.0 / math.pi)
_GELU_C = 0.044715


def _conv(g, cw, cb, rows):
    return cb + cw[0:1] * _shift_down(g, 2, rows) + cw[1:2] * _shift_down(g, 1, rows) + cw[2:3] * g


def ffn_up(hb, wgT, wuT, cw, cb, B, S):
    T, D = hb.shape
    Fd = wgT.shape[0]
    fn = _tile(Fd, 256, 128)

    def body(h_ref, wg_ref, wu_ref, cw_ref, cb_ref, g_ref, ge_ref, ud_ref, hh_ref):
        h = h_ref[...]
        g = _dot(h, wg_ref[...], NT)
        u = _dot(h, wu_ref[...], NT)
        rows = lax.broadcasted_iota(jnp.int32, g.shape, 0)
        c = _conv(g, cw_ref[...], cb_ref[...], rows)
        c2 = c * c
        th = jnp.tanh(_GELU_K * (c + _GELU_C * (c2 * c)))
        cdf = 0.5 * (1.0 + th)
        ge = c * cdf
        dgelu = cdf + c * (0.5 * (1.0 - th * th) * (_GELU_K * (1.0 + 3.0 * _GELU_C * c2)))
        g_ref[...] = g.astype(BF16)
        ge_ref[...] = ge.astype(BF16)
        ud_ref[...] = (u * dgelu).astype(BF16)
        hh_ref[...] = (ge * u).astype(BF16)

    hspec = pl.BlockSpec((S, D), lambda b, j: (b, 0))
    wspec = pl.BlockSpec((fn, D), lambda b, j: (j, 0))
    ospec = pl.BlockSpec((S, fn), lambda b, j: (b, j))
    return pl.pallas_call(
        body, name="ffn_up", grid=(B, Fd // fn),
        in_specs=[hspec, wspec, wspec, pl.BlockSpec((3, fn), lambda b, j: (0, j)), pl.BlockSpec((1, fn), lambda b, j: (0, j))],
        out_specs=[ospec] * 4,
        out_shape=[SDS((T, Fd), BF16)] * 4, compiler_params=_cp(),
    )(hb, wgT, wuT, cw, cb)


def ffn_mid_bwd(dfb, wd, g, ge, ud, cw, B, S):
    T, D = dfb.shape
    Fd = wd.shape[0]
    fn = _tile(Fd, 256, 128)

    def body(df_ref, wd_ref, g_ref, ge_ref, ud_ref, cw_ref, dg_ref, du_ref, dcb_ref, dcw_ref):
        dhh = _dot(df_ref[...], wd_ref[...], NT)
        gv = g_ref[...].astype(F32)
        cw = cw_ref[...]
        rows = lax.broadcasted_iota(jnp.int32, gv.shape, 0)
        g1 = _shift_down(gv, 1, rows)
        g2 = _shift_down(gv, 2, rows)
        du_ref[...] = (dhh * ge_ref[...].astype(F32)).astype(BF16)
        dc = dhh * ud_ref[...].astype(F32)
        dcb_ref[0] = jnp.sum(dc, axis=0, keepdims=True)
        dcw_ref[0] = jnp.concatenate(
            [jnp.sum(dc * g2, axis=0, keepdims=True), jnp.sum(dc * g1, axis=0, keepdims=True),
             jnp.sum(dc * gv, axis=0, keepdims=True)], axis=0)
        dg = cw[2:3] * dc + cw[1:2] * _shift_up(dc, 1, rows) + cw[0:1] * _shift_up(dc, 2, rows)
        dg_ref[...] = dg.astype(BF16)

    tspec = pl.BlockSpec((S, fn), lambda b, j: (b, j))
    return pl.pallas_call(
        body, name="ffn_mid_bwd", grid=(B, Fd // fn),
        in_specs=[pl.BlockSpec((S, D), lambda b, j: (b, 0)), pl.BlockSpec((fn, D), lambda b, j: (j, 0)), tspec, tspec, tspec,
                  pl.BlockSpec((3, fn), lambda b, j: (0, j))],
        out_specs=[tspec, tspec, pl.BlockSpec((1, 1, fn), lambda b, j: (b, 0, j)),
                   pl.BlockSpec((1, 3, fn), lambda b, j: (b, 0, j))],
        out_shape=[SDS((T, Fd), BF16), SDS((T, Fd), BF16), SDS((B, 1, Fd), F32), SDS((B, 3, Fd), F32)],
        compiler_params=_cp(),
    )(dfb, wd, g, ge, ud, cw)


def ffn_dx(dg, du, wgT, wuT, res, alpha):
    T, Fd = dg.shape
    D = wgT.shape[1]
    tm = _tile(T, 256, 16)

    def body(dg_ref, du_ref, wg_ref, wu_ref, r_ref, o_ref):
        o_ref[...] = alpha * r_ref[...] + _dot(dg_ref[...], wg_ref[...]) + _dot(du_ref[...], wu_ref[...])

    a_spec = pl.BlockSpec((tm, Fd), lambda i: (i, 0))
    w_spec = pl.BlockSpec((Fd, D), lambda i: (0, 0))
    o_spec = pl.BlockSpec((tm, D), lambda i: (i, 0))
    return pl.pallas_call(
        body, name="ffn_dx", grid=(T // tm,), in_specs=[a_spec, a_spec, w_spec, w_spec, o_spec], out_specs=o_spec,
        out_shape=SDS((T, D), F32), compiler_params=_cp(),
    )(dg, du, wgT, wuT, res)


def _partner_all(x):
    n = x.shape[0]
    r = lax.broadcasted_iota(jnp.int32, x.shape, 0)
    return jnp.where((r % HEAD_DIM) < HEAD_DIM // 2, pltpu.roll(x, n - HEAD_DIM // 2, 0), pltpu.roll(x, HEAD_DIM // 2, 0))


def proj_T(w, xT3, cosT, sinT, blk_off, rope, scale, name):
    G, K, T = xT3.shape
    S = cosT.shape[2]
    Dout = K
    tt = _tile(S, 512, 128)
    H = Dout // HEAD_DIM
    nS = S // tt

    def body(w_ref, x_ref, c_ref, s_ref, o_ref):
        acc = _dot(w_ref[...], x_ref[0])
        if rope:
            cos = jnp.tile(c_ref[0], (H, 1))
            sin = jnp.tile(s_ref[0], (H, 1))
            acc = acc * cos + _partner_all(acc) * sin
        if scale != 1.0:
            acc = acc * scale
        o_ref[0] = acc.astype(BF16)

    tab = pl.BlockSpec((1, HEAD_DIM, tt), lambda g, j: (g, 0, j % nS))
    return pl.pallas_call(
        body, name=name, grid=(G, T // tt),
        in_specs=[pl.BlockSpec((Dout, K), lambda g, j: (g + blk_off, 0)), pl.BlockSpec((1, K, tt), lambda g, j: (g, 0, j)), tab, tab],
        out_specs=pl.BlockSpec((1, Dout, tt), lambda g, j: (g, 0, j)),
        out_shape=SDS((G, Dout, T), BF16), compiler_params=_cp(),
    )(w, xT3, cosT, sinT)


def _attn_bias():
    kj = lax.broadcasted_iota(jnp.int32, (2 * BLK, BLK), 0)
    qi = lax.broadcasted_iota(jnp.int32, (2 * BLK, BLK), 1)
    ok = ((kj >= BLK) & (kj - BLK <= qi)) | ((kj < BLK) & (kj >= qi))
    return jnp.where(ok, 0.0, NEG).astype(F32)


def _has_prev(g, S):
    nb = S // (DILATIONS[g] * BLK)
    return [(n % nb) != 0 for n in range(S // BLK)]


def _win(ref, n, hp):
    lo = (n - 1) * BLK if hp else n * BLK
    return ref[0, :, lo:(n + 1) * BLK]


def attn_fwd(qT3, kT3, vT3, bias, g, B, S):
    _, D, T = qT3.shape
    H = D // HEAD_DIM
    nblk = S // BLK
    hp = _has_prev(g, S)

    def body(q_ref, k_ref, v_ref, b_ref, o_ref, l_ref, s_scr, p_scr, rl_scr):
        for n in range(nblk):
            lo = 0 if hp[n] else BLK
            s_scr[n, lo:, :] = _dot(_win(k_ref, n, hp[n]), q_ref[0, :, n * BLK:(n + 1) * BLK], TN)
        for n in range(nblk):
            lo = 0 if hp[n] else BLK
            sT = s_scr[n, lo:, :] + b_ref[lo:, :]
            m = jnp.max(sT, axis=0, keepdims=True)
            p = jnp.exp(sT - m)
            l = jnp.sum(p, axis=0, keepdims=True)
            p_scr[n, lo:, :] = p.astype(BF16)
            rl_scr[n:n + 1, :] = 1.0 / l
            l_ref[0, :, n * BLK:(n + 1) * BLK] = m + jnp.log(l)
        for n in range(nblk):
            lo = 0 if hp[n] else BLK
            o_ref[:, n * BLK:(n + 1) * BLK] = _dot(_win(v_ref, n, hp[n]), p_scr[n, lo:, :]) * rl_scr[n:n + 1, :]

    spec = pl.BlockSpec((1, HEAD_DIM, S), lambda b, h: (g, h, b))
    return pl.pallas_call(
        body, name=f"attn_fwd_g{g}", grid=(B, H),
        in_specs=[spec, spec, spec, pl.BlockSpec((2 * BLK, BLK), lambda b, h: (0, 0))],
        out_specs=[pl.BlockSpec((HEAD_DIM, S), lambda b, h: (h, b)), pl.BlockSpec((1, 1, S), lambda b, h: (h, 0, b))],
        out_shape=[SDS((D, T), F32), SDS((H, 1, T), F32)],
        scratch_shapes=[pltpu.VMEM((nblk, 2 * BLK, BLK), F32), pltpu.VMEM((nblk, 2 * BLK, BLK), BF16),
                        pltpu.VMEM((nblk, BLK), F32)],
        compiler_params=_cp(),
    )(qT3, kT3, vT3, bias)


def attn_combine(oTs, lses):
    G = len(oTs)
    D, T = oTs[0].shape
    H = D // HEAD_DIM
    tn = _tile(T, 2048, 128)

    def body(*refs):
        o_refs, l_refs = refs[:G], refs[G:2 * G]
        ob_ref, of_ref, lt_ref = refs[2 * G:]
        ls = [r[0] for r in l_refs]
        m = functools.reduce(jnp.maximum, ls)
        es = [jnp.exp(v - m) for v in ls]
        z = functools.reduce(lambda a, b: a + b, es)
        o = (es[0] / z) * o_refs[0][...]
        for i in range(1, G):
            o = o + (es[i] / z) * o_refs[i][...]
        ob_ref[...] = o.astype(BF16)
        of_ref[...] = o
        lt_ref[0] = m + jnp.log(z)

    ospec = pl.BlockSpec((HEAD_DIM, tn), lambda h, j: (h, j))
    lspec = pl.BlockSpec((1, 1, tn), lambda h, j: (h, 0, j))
    return pl.pallas_call(
        body, name="attn_combine", grid=(H, T // tn),
        in_specs=[ospec] * G + [lspec] * G, out_specs=[ospec, ospec, lspec],
        out_shape=[SDS((D, T), BF16), SDS((D, T), F32), SDS((H, 1, T), F32)], compiler_params=_cp(),
    )(*oTs, *lses)


def attn_delta(doT, oT):
    D, T = doT.shape
    H = D // HEAD_DIM
    tn = _tile(T, 2048, 128)

    def body(d_ref, o_ref, r_ref):
        r_ref[0] = jnp.sum(d_ref[...].astype(F32) * o_ref[...], axis=0, keepdims=True)

    spec = pl.BlockSpec((HEAD_DIM, tn), lambda h, j: (h, j))
    return pl.pallas_call(
        body, name="attn_delta", grid=(H, T // tn), in_specs=[spec, spec],
        out_specs=pl.BlockSpec((1, 1, tn), lambda h, j: (h, 0, j)),
        out_shape=SDS((H, 1, T), F32), compiler_params=_cp(),
    )(doT, oT)


def attn_bwd(qT3, kT3, vT3, doT, lse, delta, cosT, sinT, bias, g, q_scale, B, S, dk_prev=None, dv_prev=None):
    _, D, T = qT3.shape
    H = D // HEAD_DIM
    nblk = S // BLK
    half = HEAD_DIM // 2
    hp = _has_prev(g, S)
    acc_in = dk_prev is not None
    kv_dtype = BF16 if acc_in else F32

    def body(*refs):
        q_ref, k_ref, v_ref, do_ref, l_ref, d_ref, c_ref, s_ref, b_ref = refs[:9]
        rest = refs[9:]
        if acc_in:
            dkp_ref, dvp_ref = rest[:2]
            rest = rest[2:]
        dq_ref, dk_ref, dv_ref, s_scr, dp_scr, p_scr, ds_scr = rest
        for n in range(nblk):
            lo = 0 if hp[n] else BLK
            blk = slice(n * BLK, (n + 1) * BLK)
            s_scr[n, lo:, :] = _dot(_win(k_ref, n, hp[n]), q_ref[0, :, blk], TN)
            dp_scr[n, lo:, :] = _dot(_win(v_ref, n, hp[n]), do_ref[:, blk], TN)
        for n in range(nblk):
            lo = 0 if hp[n] else BLK
            blk = slice(n * BLK, (n + 1) * BLK)
            pT = jnp.exp(s_scr[n, lo:, :] + b_ref[lo:, :] - l_ref[0, :, blk])
            p_scr[n, lo:, :] = pT.astype(BF16)
            ds_scr[n, lo:, :] = (pT * (dp_scr[n, lo:, :] - d_ref[0, :, blk])).astype(BF16)
        for j in range(nblk):
            blk = slice(j * BLK, (j + 1) * BLK)
            if j + 1 < nblk and hp[j + 1]:
                two = slice(j * BLK, (j + 2) * BLK)
                pj = jnp.concatenate([p_scr[j, BLK:, :], p_scr[j + 1, :BLK, :]], axis=1)
                dsj = jnp.concatenate([ds_scr[j, BLK:, :], ds_scr[j + 1, :BLK, :]], axis=1)
                dv = _dot(do_ref[:, two], pj, NT)
                dk = _dot(q_ref[0, :, two], dsj, NT)
            else:
                dv = _dot(do_ref[:, blk], p_scr[j, BLK:, :], NT)
                dk = _dot(q_ref[0, :, blk], ds_scr[j, BLK:, :], NT)
            dk = dk * c_ref[0, :, blk] - pltpu.roll(dk, half, 0) * s_ref[0, :, blk]
            if acc_in:
                dk = dk + dkp_ref[:, blk]
                dv = dv + dvp_ref[:, blk]
            dk_ref[:, blk] = dk.astype(kv_dtype)
            dv_ref[:, blk] = dv.astype(kv_dtype)
            lo = 0 if hp[j] else BLK
            dq = _dot(_win(k_ref, j, hp[j]), ds_scr[j, lo:, :])
            dq = dq * c_ref[0, :, blk] - pltpu.roll(dq, half, 0) * s_ref[0, :, blk]
            dq_ref[:, blk] = (dq * q_scale).astype(BF16)

    spec3 = pl.BlockSpec((1, HEAD_DIM, S), lambda b, h: (g, h, b))
    spec = pl.BlockSpec((HEAD_DIM, S), lambda b, h: (h, b))
    sspec = pl.BlockSpec((1, 1, S), lambda b, h: (h, 0, b))
    tab = pl.BlockSpec((1, HEAD_DIM, S), lambda b, h: (g, 0, 0))
    in_specs = [spec3, spec3, spec3, spec, sspec, sspec, tab, tab, pl.BlockSpec((2 * BLK, BLK), lambda b, h: (0, 0))]
    args = [qT3, kT3, vT3, doT, lse, delta, cosT, sinT, bias]
    if acc_in:
        in_specs += [spec, spec]
        args += [dk_prev, dv_prev]
    return pl.pallas_call(
        body, name=f"attn_bwd_g{g}" + ("_acc" if acc_in else ""), grid=(B, H),
        in_specs=in_specs, out_specs=[spec, spec, spec],
        out_shape=[SDS((D, T), BF16), SDS((D, T), kv_dtype), SDS((D, T), kv_dtype)],
        scratch_shapes=[pltpu.VMEM((nblk, 2 * BLK, BLK), F32), pltpu.VMEM((nblk, 2 * BLK, BLK), F32),
                        pltpu.VMEM((nblk, 2 * BLK, BLK), BF16), pltpu.VMEM((nblk, 2 * BLK, BLK), BF16)],
        compiler_params=_cp(),
    )(*args)


def adamw(w, g, m, v, name):
    R, C = w.shape
    tr = _tile(R, 512, 8)

    def body(w_ref, g_ref, m_ref, v_ref, d_ref, nm_ref, nv_ref):
        gv = g_ref[...]
        nm = ADAM_B1 * m_ref[...] + (1.0 - ADAM_B1) * gv
        nv = ADAM_B2 * v_ref[...] + (1.0 - ADAM_B2) * (gv * gv)
        m_hat = nm / (1.0 - ADAM_B1 ** ADAM_STEP)
        v_hat = nv / (1.0 - ADAM_B2 ** ADAM_STEP)
        d_ref[...] = -ADAM_LR * (m_hat / (jnp.sqrt(v_hat) + ADAM_EPS) + ADAM_WD * w_ref[...])
        nm_ref[...] = nm
        nv_ref[...] = nv

    spec = pl.BlockSpec((tr, C), lambda i: (i, 0))
    return pl.pallas_call(
        body, name=name, grid=(R // tr,), in_specs=[spec] * 4, out_specs=[spec] * 3,
        out_shape=[SDS((R, C), F32)] * 3, compiler_params=_cp(),
    )(w, g, m, v)


def _perm(a, B, S, d):
    if d == 1:
        return a
    lead = a.shape[:-1]
    return a.reshape(*lead, B, S // d, d).swapaxes(-1, -2).reshape(*lead, B * S)


def _unperm(a, B, S, d):
    if d == 1:
        return a
    lead = a.shape[:-1]
    return a.reshape(*lead, B, d, S // d).swapaxes(-1, -2).reshape(*lead, B * S)


def _perm3(a, B, S):
    return jnp.stack([_perm(a, B, S, d) for d in DILATIONS])


def _xT3(xb, B, S):
    D = xb.shape[1]
    outs = []
    for d in DILATIONS:
        outs.append(xb.reshape(B, S // d, d, D).transpose(3, 0, 2, 1).reshape(D, B * S))
    return jnp.stack(outs)


def _rope_tables(S):
    half = HEAD_DIM // 2
    inv_freq = ROPE_THETA ** (-jnp.arange(0, HEAD_DIM, 2, dtype=F32) / HEAD_DIM)
    ang = jnp.arange(S, dtype=F32)[:, None] * inv_freq[None, :]
    cos = jnp.concatenate([jnp.cos(ang), jnp.cos(ang)], axis=1).T
    sin = jnp.concatenate([-jnp.sin(ang), jnp.sin(ang)], axis=1).T
    return _perm3(cos, 1, S), _perm3(sin, 1, S)


def kernel(x, pool_w, pool_scale, w_q, w_kv, w_o, ffn_w_gate, ffn_w_up, ffn_conv_w, ffn_conv_b, ffn_w_down, ln1_g, ln1_b, ln2_g, ln2_b, loss_target, m_pool_w, m_pool_scale, m_w_q, m_w_kv, m_w_o, m_ffn_w_gate, m_ffn_w_up, m_ffn_conv_w, m_ffn_conv_b, m_ffn_w_down, m_ln1_g, m_ln1_b, m_ln2_g, m_ln2_b, v_pool_w, v_pool_scale, v_w_q, v_w_kv, v_w_o, v_ffn_w_gate, v_ffn_w_up, v_ffn_conv_w, v_ffn_conv_b, v_ffn_w_down, v_ln1_g, v_ln1_b, v_ln2_g, v_ln2_b):
    B, S, D = x.shape
    T = B * S
    depth = ln1_g.shape[0]
    nA, nB = pool_w.shape[0], w_q.shape[0]
    Fs = ffn_w_down.shape[1]
    Fd = Fs * N_DEV
    H = D // HEAD_DIM
    G = len(DILATIONS)
    PG = len(POOL_WINDOWS)
    Cg = D // PG
    alpha = (2.0 * depth) ** 0.25
    me = 4 * lax.axis_index("x") + 2 * lax.axis_index("y") + lax.axis_index("c")

    qs, kvs, os_ = w_q.shape[2], w_kv.shape[1], w_o.shape[1]
    pool_rows = pool_w.size // D
    pieces = [jnp.swapaxes(w_q, 1, 2).reshape(nB * qs, D), w_kv.T, w_o.reshape(nB * os_, D),
              jnp.swapaxes(ffn_w_gate, 1, 2).reshape(depth * Fs, D), jnp.swapaxes(ffn_w_up, 1, 2).reshape(depth * Fs, D),
              ffn_w_down.reshape(depth * Fs, D), pool_w.reshape(pool_rows, D)]
    sizes = [p.shape[0] for p in pieces]
    big = all_gather_blocks(jnp.concatenate(pieces, axis=0).astype(BF16), "gather_weights", in_vmem=False)
    offs = [sum(sizes[:i]) for i in range(len(sizes))]

    def full(i, per_layer, nl):
        blk = big[:, offs[i]:offs[i] + sizes[i]].reshape(N_DEV, nl, per_layer, D)
        return blk.transpose(1, 0, 2, 3).reshape(nl, N_DEV * per_layer, D)

    WqT = full(0, qs, nB)
    WkvT = full(1, kvs, 1)[0]
    Wo = full(2, os_, nB)
    WgT, WuT, Wd = full(3, Fs, depth), full(4, Fs, depth), full(5, Fs, depth)
    PW = big[:, offs[6]:offs[6] + sizes[6]].reshape(N_DEV, nA, PG, Cg // N_DEV, Cg)
    PW = PW.transpose(1, 2, 0, 3, 4).reshape(nA, PG, Cg, Cg)

    sm_cols = 128
    sm_local = jnp.concatenate([ffn_conv_w.reshape(-1), pool_scale.reshape(-1)])
    sm_rows = -(-sm_local.size // sm_cols)
    sm_rows_p = -(-sm_rows // 8) * 8
    sm_local = jnp.pad(sm_local, (0, sm_rows_p * sm_cols - sm_local.size)).reshape(sm_rows_p, sm_cols)
    sm = all_gather_blocks(sm_local, "gather_small", in_vmem=True).reshape(N_DEV, -1)
    ncw = ffn_conv_w.size
    conv_w_full = sm[:, :ncw].reshape(N_DEV, depth, 3, Fs).transpose(1, 2, 0, 3).reshape(depth, 3, Fd)
    pool_scale_full = sm[:, ncw:ncw + pool_scale.size].reshape(N_DEV, nA, D // N_DEV).transpose(1, 0, 2).reshape(nA, 1, D)

    cosT, sinT = _rope_tables(S)
    bias = _attn_bias()

    xs = x.reshape(T, D)
    saved = []
    cur, curb = xs, None
    kT = vT = x1T3 = None
    for i in range(depth):
        sv = {}
        if i < nA:
            mix, pooled = pool_fwd(cur, PW[i], pool_scale_full[i], B, S)
            sv["pooled"] = pooled
            a1, h, hb = add_ln(cur, mix, ln1_g[i], ln1_b[i], alpha)
        else:
            j = i - nA
            xT3 = x1T3 if j == 0 else _xT3(curb, B, S)
            qT = proj_T(WqT[j], xT3, cosT, sinT, 0, True, HEAD_DIM ** -0.5, "q_proj")
            oTs, lses = [], []
            for gi, d in enumerate(DILATIONS):
                o_g, lse_g = attn_fwd(qT, kT, vT, bias, gi, B, S)
                oTs.append(_unperm(o_g, B, S, d))
                lses.append(_unperm(lse_g, B, S, d))
            oTb, oTf, lse_tot = attn_combine(oTs, lses)
            a1, h, hb = matmul_ln(oTb, Wo[j], TN, cur, ln1_g[i], ln1_b[i], alpha, "o_proj_ln")
            sv.update(xT3=xT3, qT=qT, oTb=oTb, oTf=oTf, lse_tot=lse_tot)
        g, ge, ud, hh = ffn_up(hb, WgT[i], WuT[i], conv_w_full[i], ffn_conv_b[i].reshape(1, Fd), B, S)
        a2, cur, curb = matmul_ln(hh, Wd[i], NN, h, ln2_g[i], ln2_b[i], alpha, "ffn_down_ln")
        sv.update(a1=a1, hb=hb, g=g, ge=ge, ud=ud, hh=hh, a2=a2)
        saved.append(sv)
        if i == nA - 1:
            x1T3 = _xT3(curb, B, S)
            kT = proj_T(WkvT, x1T3, cosT, sinT, 0, True, 1.0, "k_proj")
            vT = proj_T(WkvT, x1T3, cosT, sinT, G, False, 1.0, "v_proj")

    dy, sq = loss_grad(cur, loss_target.reshape(T, D))

    small = {k: [None] * depth for k in ("ln1_g", "ln1_b", "ln2_g", "ln2_b", "conv_b", "conv_w")}
    dscale = [None] * nA
    dpw = [None] * nA
    dWq, dWo_, dWg, dWu, dWd_ = [None] * nB, [None] * nB, [None] * depth, [None] * depth, [None] * depth
    dk_acc, dv_acc = [None] * G, [None] * G
    dcur = dy
    for i in reversed(range(depth)):
        sv = saved[i]
        db2, db2b, small["ln2_g"][i], small["ln2_b"][i] = ln_bwd(dcur, sv["a2"], ln2_g[i])
        dg_, du_, dcb, dcw = ffn_mid_bwd(db2b, Wd[i], sv["g"], sv["ge"], sv["ud"], conv_w_full[i], B, S)
        small["conv_b"][i] = jnp.sum(dcb, axis=0)
        small["conv_w"][i] = jnp.sum(dcw, axis=0)
        dWd_[i] = wgrad_rows(sv["hh"], db2b, "wgrad_down")
        dWg[i] = wgrad_rows(dg_, sv["hb"], "wgrad_gate")
        dWu[i] = wgrad_rows(du_, sv["hb"], "wgrad_up")
        dh = ffn_dx(dg_, du_, WgT[i], WuT[i], db2, alpha)
        da1, da1b, small["ln1_g"][i], small["ln1_b"][i] = ln_bwd(dh, sv["a1"], ln1_g[i])
        if i < nA:
            dcur, dsp, dpwp = pool_bwd(da1, sv["pooled"], PW[i], pool_scale_full[i], alpha, B, S)
            dscale[i] = jnp.sum(dsp, axis=0)
            dpw[i] = jnp.sum(dpwp, axis=0)
        else:
            j = i - nA
            doT = matmul_to_T(Wo[j], da1b, "o_proj_bwd")
            dWo_[j] = wgrad_mixed(sv["oTb"], da1b, "wgrad_o")
            delta = attn_delta(doT, sv["oTf"])
            dq_tok, dwq = [], []
            for gi, d in enumerate(DILATIONS):
                dq_g, dk_acc[gi], dv_acc[gi] = attn_bwd(
                    sv["qT"], kT, vT, _perm(doT, B, S, d), _perm(sv["lse_tot"], B, S, d), _perm(delta, B, S, d),
                    cosT, sinT, bias, gi, HEAD_DIM ** -0.5, B, S, dk_prev=dk_acc[gi], dv_prev=dv_acc[gi])
                dwq.append(wgrad_T(dq_g, sv["xT3"], gi, "wgrad_q"))
                dq_tok.append(_unperm(dq_g, B, S, d))
            dWq[j] = jnp.concatenate(dwq, axis=0)
            dcur = dx_from_T(dq_tok, WqT[j], da1, alpha, "q_proj_bwd", 512)
            if j == 0:
                dkv = [a.astype(BF16) for a in dk_acc + dv_acc]
                dWkv = jnp.concatenate([wgrad_T(a, x1T3, gi % G, "wgrad_kv") for gi, a in enumerate(dkv)], axis=0)
                dkv_tok = [_unperm(a, B, S, DILATIONS[gi % G]) for gi, a in enumerate(dkv)]
                dcur = dx_from_T(dkv_tok, WkvT, dcur, 1.0, "kv_proj_bwd", 256)
    grad_x = dcur.reshape(B, S, D)

    def blocks(a, rows):
        return a.reshape(N_DEV, rows, D)

    dpw_all = jnp.stack(dpw).reshape(nA, PG, N_DEV, Cg // N_DEV, Cg).transpose(2, 0, 1, 3, 4).reshape(N_DEV, pool_rows, D)
    parts = ([blocks(dWq[j], qs) for j in range(nB)] + [blocks(dWkv, kvs)] + [blocks(dWo_[j], os_) for j in range(nB)]
             + [blocks(dWg[i], Fs) for i in range(depth)] + [blocks(dWu[i], Fs) for i in range(depth)]
             + [blocks(dWd_[i], Fs) for i in range(depth)] + [dpw_all.astype(BF16)])
    gsum = sum_slots(scatter_partials(parts, "scatter_grads"), "sum_grads")

    def take(i):
        return gsum[offs[i]:offs[i] + sizes[i]]

    g_w_q = jnp.swapaxes(take(0).reshape(nB, qs, D), 1, 2)
    g_w_kv = take(1).T
    g_w_o = take(2).reshape(nB, os_, D)
    g_gate = jnp.swapaxes(take(3).reshape(depth, Fs, D), 1, 2)
    g_up = jnp.swapaxes(take(4).reshape(depth, Fs, D), 1, 2)
    g_down = take(5).reshape(depth, Fs, D)
    g_pool_w = take(6).reshape(pool_w.shape)

    def rows_of(a):
        a = a.reshape(-1)
        n = -(-a.size // D) * D
        return jnp.pad(a, (0, n - a.size)).reshape(-1, D)

    sm_parts = [rows_of(jnp.concatenate(small[k], axis=0)) for k in ("ln1_g", "ln1_b", "ln2_g", "ln2_b")]
    sm_parts += [rows_of(jnp.stack(small["conv_b"])), rows_of(jnp.stack(small["conv_w"])), rows_of(jnp.stack(dscale)), sq]
    sm_sizes = [p.shape[0] for p in sm_parts]
    sm_all = jnp.concatenate(sm_parts, axis=0)
    pad_rows = -(-sm_all.shape[0] // 8) * 8 - sm_all.shape[0]
    sm_all = jnp.pad(sm_all, ((0, pad_rows), (0, 0)))
    sm_sum = sum_slots(all_gather_blocks(sm_all, "gather_small_grads", in_vmem=True), "sum_small_grads")
    sm_offs = [sum(sm_sizes[:i]) for i in range(len(sm_sizes))]

    def sm_take(i, shape):
        n = math.prod(shape)
        return sm_sum[sm_offs[i]:sm_offs[i] + sm_sizes[i]].reshape(-1)[:n].reshape(shape)

    g_ln1_g, g_ln1_b = sm_take(0, (depth, D)), sm_take(1, (depth, D))
    g_ln2_g, g_ln2_b = sm_take(2, (depth, D)), sm_take(3, (depth, D))
    g_conv_b = sm_take(4, (depth, Fd))
    g_conv_w = lax.dynamic_slice_in_dim(sm_take(5, (depth, 3, Fd)), me * Fs, Fs, axis=2)
    g_pool_scale = lax.dynamic_slice_in_dim(sm_take(6, (nA, D)), me * (D // N_DEV), D // N_DEV, axis=1)
    loss = (0.5 / D) * jnp.sum(sm_take(7, (D,)))

    def v2(a):
        return a.reshape(-1, a.shape[-1])

    names = ["pool_w", "pool_scale", "w_q", "w_kv", "w_o", "ffn_w_gate", "ffn_w_up", "ffn_conv_w", "ffn_conv_b",
             "ffn_w_down", "ln1_g", "ln1_b", "ln2_g", "ln2_b"]
    ws = [pool_w, pool_scale, w_q, w_kv, w_o, ffn_w_gate, ffn_w_up, ffn_conv_w, ffn_conv_b, ffn_w_down, ln1_g, ln1_b, ln2_g, ln2_b]
    ms = [m_pool_w, m_pool_scale, m_w_q, m_w_kv, m_w_o, m_ffn_w_gate, m_ffn_w_up, m_ffn_conv_w, m_ffn_conv_b, m_ffn_w_down, m_ln1_g, m_ln1_b, m_ln2_g, m_ln2_b]
    vs = [v_pool_w, v_pool_scale, v_w_q, v_w_kv, v_w_o, v_ffn_w_gate, v_ffn_w_up, v_ffn_conv_w, v_ffn_conv_b, v_ffn_w_down, v_ln1_g, v_ln1_b, v_ln2_g, v_ln2_b]
    gs = [g_pool_w, g_pool_scale, g_w_q, g_w_kv, g_w_o, g_gate, g_up, g_conv_w, g_conv_b, g_down, g_ln1_g, g_ln1_b, g_ln2_g, g_ln2_b]
    deltas, new_ms, new_vs = [], [], []
    for nm, w, gr, m_, v_ in zip(names, ws, gs, ms, vs):
        d_, nm_, nv_ = adamw(v2(w), v2(gr), v2(m_), v2(v_), "adamw_" + nm)
        deltas.append(d_.reshape(w.shape))
        new_ms.append(nm_.reshape(w.shape))
        new_vs.append(nv_.reshape(w.shape))

    return (loss, grad_x, *gs, *deltas, *new_ms, *new_vs)
```

```python
import functools
import math

import jax
import jax.numpy as jnp
from jax import lax
from jax.experimental import pallas as pl
from jax.experimental.pallas import tpu as pltpu

F32 = jnp.float32
BF16 = jnp.bfloat16
SDS = jax.ShapeDtypeStruct
MESH = pl.DeviceIdType.MESH

N_DEV = 8
HEAD_DIM = 64
BLK = 128
DILATIONS = (1, 4, 16)
POOL_WINDOWS = (2, 4, 8, 16)
ROPE_THETA = 10000.0
LN_EPS = 1e-5
NEG = -1e30
V7X_VMEM_LIMIT = 56 * 1024 * 1024

ADAM_LR, ADAM_B1, ADAM_B2, ADAM_EPS, ADAM_WD, ADAM_STEP = 0.001, 0.9, 0.999, 1e-08, 0.01, 10

NN = (((1,), (0,)), ((), ()))
NT = (((1,), (1,)), ((), ()))
TN = (((0,), (0,)), ((), ()))


def _cp(sem=None):
    kw = dict(vmem_limit_bytes=V7X_VMEM_LIMIT)
    if sem is not None:
        kw["dimension_semantics"] = sem
    return pltpu.CompilerParams(**kw)


def _dot(a, b, dims=NN):
    return lax.dot_general(a, b, dims, preferred_element_type=F32)


def _tile(n, target, mult):
    best = None
    for t in range(mult, min(n, target) + 1, mult):
        if n % t == 0:
            best = t
    return best if best is not None else n


def _mesh_pos():
    return lax.axis_index("x"), lax.axis_index("y"), lax.axis_index("c")


def all_gather_blocks(xl, name, in_vmem):
    R, C = xl.shape
    space = pltpu.VMEM if in_vmem else pl.ANY

    def body(x_ref, out_ref, send_sems, recv_sems, local_sem):
        x, y, c = _mesh_pos()
        me, sibling = (x, y, c), (x, y, 1 - c)
        chips = [(1 - x, y), (x, 1 - y), (1 - x, 1 - y)]

        def slot(px, py, pc):
            return out_ref.at[4 * px + 2 * py + pc]

        def copy(k, block, to, src=None):
            return pltpu.make_async_remote_copy(
                src_ref=slot(*block) if src is None else src, dst_ref=slot(*block),
                send_sem=send_sems.at[k], recv_sem=recv_sems.at[k], device_id=to, device_id_type=MESH)

        mine = pltpu.make_async_copy(x_ref, slot(*me), local_sem)
        mine.start()
        first = [copy(0, me, sibling, src=x_ref)]
        first += [copy(1 + j, me, (*chip, c), src=x_ref) for j, chip in enumerate(chips)]
        for cp in first:
            cp.start()
        passed = [copy(4 + j, (*chip, c), sibling) for j, chip in enumerate(chips)]
        for j, chip in enumerate(chips):
            copy(1 + j, (*chip, c), me).wait_recv()
            passed[j].start()
        copy(0, sibling, me).wait_recv()
        for j, chip in enumerate(chips):
            copy(4 + j, (*chip, 1 - c), me).wait_recv()
        for cp in first + passed:
            cp.wait_send()
        mine.wait()

    return pl.pallas_call(
        body, name=name,
        out_shape=SDS((N_DEV, R, C), xl.dtype),
        in_specs=[pl.BlockSpec(memory_space=space)],
        out_specs=pl.BlockSpec(memory_space=space),
        scratch_shapes=[pltpu.SemaphoreType.DMA((7,)), pltpu.SemaphoreType.DMA((7,)), pltpu.SemaphoreType.DMA],
        compiler_params=_cp(),
    )(xl)


class Scatter:
    def __init__(self, parts):
        self.parts = list(parts)
        self.rows = [p.shape[1] for p in parts]
        self.offs = [sum(self.rows[:i]) for i in range(len(self.rows))]
        self.out_shape = SDS((N_DEV, sum(self.rows), parts[0].shape[2]), parts[0].dtype)
        self.scratch = [pltpu.SemaphoreType.DMA((7,)), pltpu.SemaphoreType.DMA((7,)), pltpu.SemaphoreType.DMA]

    @staticmethod
    def _peers():
        x, y, c = _mesh_pos()
        peers = []
        for r in range(1, N_DEV):
            peers.append((1 - x if (r & 4) else x, 1 - y if (r & 2) else y, 1 - c if (r & 1) else c))
        return 4 * x + 2 * y + c, peers

    def start(self, part_refs, out_ref, send_sems, recv_sems, local_sem):
        me_lin, peers = self._peers()
        for i, (off, r) in enumerate(zip(self.offs, self.rows)):
            pltpu.make_async_copy(part_refs[i].at[me_lin], out_ref.at[me_lin, pl.ds(off, r)], local_sem).start()
        for k, (px, py, pc) in enumerate(peers):
            p_lin = 4 * px + 2 * py + pc
            for i, (off, r) in enumerate(zip(self.offs, self.rows)):
                pltpu.make_async_remote_copy(
                    src_ref=part_refs[i].at[p_lin], dst_ref=out_ref.at[me_lin, pl.ds(off, r)],
                    send_sem=send_sems.at[k], recv_sem=recv_sems.at[k],
                    device_id=(px, py, pc), device_id_type=MESH).start()

    def wait(self, out_ref, send_sems, recv_sems, local_sem):
        me_lin, peers = self._peers()
        for k, (px, py, pc) in enumerate(peers):
            p_lin = 4 * px + 2 * py + pc
            whole = pltpu.make_async_remote_copy(
                src_ref=out_ref.at[p_lin], dst_ref=out_ref.at[p_lin],
                send_sem=send_sems.at[k], recv_sem=recv_sems.at[k],
                device_id=(px, py, pc), device_id_type=MESH)
            whole.wait_recv()
            whole.wait_send()
        pltpu.make_async_copy(out_ref.at[me_lin], out_ref.at[me_lin], local_sem).wait()


def scatter_partials(parts, name):
    sc = Scatter(parts)
    n = len(parts)

    def body(*refs):
        sc.start(refs[:n], refs[n], *refs[n + 1:])
        sc.wait(refs[n], *refs[n + 1:])

    return pl.pallas_call(
        body, name=name, out_shape=sc.out_shape,
        in_specs=[pl.BlockSpec(memory_space=pl.ANY)] * n, out_specs=pl.BlockSpec(memory_space=pl.ANY),
        scratch_shapes=sc.scratch, compiler_params=_cp(),
    )(*parts)


def _call(body, name, grid, in_specs, out_specs, out_shape, args, scratch=(), sem=None, carry=None):
    in_specs, out_specs, out_shape, scratch = list(in_specs), list(out_specs), list(out_shape), list(scratch)
    if carry is None:
        outs = pl.pallas_call(body, name=name, grid=grid, in_specs=in_specs, out_specs=out_specs, out_shape=out_shape,
                              scratch_shapes=scratch, compiler_params=_cp(sem))(*args)
        return list(outs), None
    n_in, n_out, n_scr, n_c = len(in_specs), len(out_specs), len(scratch), len(carry.parts)
    last = [g - 1 for g in grid]

    def carried(*refs):
        ins, c_ins = refs[:n_in], refs[n_in:n_in + n_c]
        o0 = n_in + n_c
        outs, c_out = refs[o0:o0 + n_out], refs[o0 + n_out]
        s0 = o0 + n_out + 1
        scr, c_scr = refs[s0:s0 + n_scr], refs[s0 + n_scr:]
        ids = [pl.program_id(a) for a in range(len(grid))]
        is_first = functools.reduce(jnp.logical_and, [i == 0 for i in ids])
        is_last = functools.reduce(jnp.logical_and, [i == l for i, l in zip(ids, last)])

        @pl.when(is_first)
        def _():
            carry.start(c_ins, c_out, *c_scr)

        body(*ins, *outs, *scr)

        @pl.when(is_last)
        def _():
            carry.wait(c_out, *c_scr)

    hbm = pl.BlockSpec(memory_space=pl.ANY)
    outs = pl.pallas_call(
        carried, name=name + "_carry", grid=grid, in_specs=in_specs + [hbm] * n_c, out_specs=out_specs + [hbm],
        out_shape=out_shape + [carry.out_shape], scratch_shapes=scratch + carry.scratch,
        compiler_params=_cp(sem if sem is not None else ("arbitrary",) * len(grid)),
    )(*args, *carry.parts)
    return list(outs[:-1]), outs[-1]


def sum_slots(slots, name, out_dtype=F32):
    _, R, C = slots.shape
    tr = _tile(R, 512, 16)

    def body(s_ref, o_ref):
        acc = s_ref[0].astype(F32)
        for s in range(1, N_DEV):
            acc = acc + s_ref[s].astype(F32)
        o_ref[...] = acc.astype(out_dtype)

    return pl.pallas_call(
        body, name=name, grid=(R // tr,),
        in_specs=[pl.BlockSpec((N_DEV, tr, C), lambda i: (0, i, 0))],
        out_specs=pl.BlockSpec((tr, C), lambda i: (i, 0)),
        out_shape=SDS((R, C), out_dtype), compiler_params=_cp(),
    )(slots)


def add_ln(x, mix, g, b, alpha):
    T, D = x.shape
    tm = _tile(T, 512, 16)

    def body(x_ref, m_ref, g_ref, b_ref, a_ref, y_ref, yb_ref):
        a = alpha * x_ref[...] + m_ref[...]
        mu = jnp.mean(a, axis=-1, keepdims=True)
        xc = a - mu
        var = jnp.mean(xc * xc, axis=-1, keepdims=True)
        y = xc * lax.rsqrt(var + LN_EPS) * g_ref[...] + b_ref[...]
        a_ref[...] = a
        y_ref[...] = y
        yb_ref[...] = y.astype(BF16)

    row = pl.BlockSpec((tm, D), lambda i: (i, 0))
    vec = pl.BlockSpec((1, D), lambda i: (0, 0))
    return pl.pallas_call(
        body, name="add_ln", grid=(T // tm,),
        in_specs=[row, row, vec, vec], out_specs=[row, row, row],
        out_shape=[SDS((T, D), F32), SDS((T, D), F32), SDS((T, D), BF16)], compiler_params=_cp(),
    )(x, mix, g.reshape(1, D), b.reshape(1, D))


def ln_bwd(dy, a, g):
    T, D = a.shape
    tm = _tile(T, 512, 16)

    def body(dy_ref, a_ref, g_ref, da_ref, dab_ref, dg_ref, db_ref):
        @pl.when(pl.program_id(0) == 0)
        def _():
            dg_ref[...] = jnp.zeros_like(dg_ref)
            db_ref[...] = jnp.zeros_like(db_ref)

        av = a_ref[...]
        mu = jnp.mean(av, axis=-1, keepdims=True)
        xc = av - mu
        var = jnp.mean(xc * xc, axis=-1, keepdims=True)
        r = lax.rsqrt(var + LN_EPS)
        xh = xc * r
        dyv = dy_ref[...]
        dxh = dyv * g_ref[...]
        m1 = jnp.mean(dxh, axis=-1, keepdims=True)
        m2 = jnp.mean(dxh * xh, axis=-1, keepdims=True)
        da = r * (dxh - m1 - xh * m2)
        da_ref[...] = da
        dab_ref[...] = da.astype(BF16)
        dg_ref[...] += jnp.sum(dyv * xh, axis=0, keepdims=True)
        db_ref[...] += jnp.sum(dyv, axis=0, keepdims=True)

    row = pl.BlockSpec((tm, D), lambda i: (i, 0))
    vec = pl.BlockSpec((1, D), lambda i: (0, 0))
    return pl.pallas_call(
        body, name="ln_bwd", grid=(T // tm,),
        in_specs=[row, row, vec], out_specs=[row, row, vec, vec],
        out_shape=[SDS((T, D), F32), SDS((T, D), BF16), SDS((1, D), F32), SDS((1, D), F32)],
        compiler_params=_cp(("arbitrary",)),
    )(dy, a, g.reshape(1, D))


def loss_grad(y, tgt):
    T, D = y.shape
    tm = _tile(T, 512, 16)

    def body(y_ref, t_ref, dy_ref, sq_ref):
        @pl.when(pl.program_id(0) == 0)
        def _():
            sq_ref[...] = jnp.zeros_like(sq_ref)

        e = y_ref[...] - t_ref[...]
        dy_ref[...] = e / float(D)
        sq_ref[...] += jnp.sum(e * e, axis=0, keepdims=True)

    row = pl.BlockSpec((tm, D), lambda i: (i, 0))
    vec = pl.BlockSpec((1, D), lambda i: (0, 0))
    return pl.pallas_call(
        body, name="loss_grad", grid=(T // tm,),
        in_specs=[row, row], out_specs=[row, vec],
        out_shape=[SDS((T, D), F32), SDS((1, D), F32)], compiler_params=_cp(("arbitrary",)),
    )(y, tgt)


def matmul_ln(a, w, dims, res, g, b, alpha, name):
    if dims == TN:
        K, T = a.shape
    else:
        T, K = a.shape
    D = w.shape[1]
    tm = _tile(T, 512, 128 if dims == TN else 16)

    def body(a_ref, w_ref, r_ref, g_ref, b_ref, p_ref, y_ref, yb_ref):
        pre = alpha * r_ref[...] + _dot(a_ref[...], w_ref[...], dims)
        mu = jnp.mean(pre, axis=-1, keepdims=True)
        xc = pre - mu
        var = jnp.mean(xc * xc, axis=-1, keepdims=True)
        y = xc * lax.rsqrt(var + LN_EPS) * g_ref[...] + b_ref[...]
        p_ref[...] = pre
        y_ref[...] = y
        yb_ref[...] = y.astype(BF16)

    a_spec = pl.BlockSpec((K, tm), lambda i: (0, i)) if dims == TN else pl.BlockSpec((tm, K), lambda i: (i, 0))
    row = pl.BlockSpec((tm, D), lambda i: (i, 0))
    vec = pl.BlockSpec((1, D), lambda i: (0, 0))
    return pl.pallas_call(
        body, name=name, grid=(T // tm,),
        in_specs=[a_spec, pl.BlockSpec(w.shape, lambda i: (0, 0)), row, vec, vec], out_specs=[row, row, row],
        out_shape=[SDS((T, D), F32), SDS((T, D), F32), SDS((T, D), BF16)], compiler_params=_cp(),
    )(a, w, res, g.reshape(1, D), b.reshape(1, D))


def dx_from_T(aTs, w, res, alpha, name, tm_target):
    T = aTs[0].shape[1]
    N = w.shape[1]
    ks = [a.shape[0] for a in aTs]
    n = len(aTs)
    tm = _tile(T, tm_target, 128)

    def body(*refs):
        a_refs, w_ref, r_ref, o_ref = refs[:n], refs[n], refs[n + 1], refs[n + 2]
        acc = alpha * r_ref[...]
        off = 0
        for a_ref, k in zip(a_refs, ks):
            acc = acc + _dot(a_ref[...], w_ref[off:off + k, :], TN)
            off += k
        o_ref[...] = acc

    row = pl.BlockSpec((tm, N), lambda i: (i, 0))
    return pl.pallas_call(
        body, name=name, grid=(T // tm,),
        in_specs=[pl.BlockSpec((k, tm), lambda i: (0, i)) for k in ks] + [pl.BlockSpec(w.shape, lambda i: (0, 0)), row],
        out_specs=row, out_shape=SDS((T, N), F32), compiler_params=_cp(),
    )(*aTs, w, res)


def matmul_to_T(w, a, name):
    M, K = w.shape
    T = a.shape[0]
    tt = _tile(T, 512, 128)

    def body(w_ref, a_ref, o_ref):
        o_ref[...] = _dot(w_ref[...], a_ref[...], NT).astype(BF16)

    return pl.pallas_call(
        body, name=name, grid=(T // tt,),
        in_specs=[pl.BlockSpec((M, K), lambda i: (0, 0)), pl.BlockSpec((tt, K), lambda i: (i, 0))],
        out_specs=pl.BlockSpec((M, tt), lambda i: (0, i)),
        out_shape=SDS((M, T), BF16), compiler_params=_cp(),
    )(w, a)


def wgrad_rows(a, b, name, carry=None):
    T, M = a.shape
    N = b.shape[1]
    tt = _tile(T, 512, 16)
    tmm = _tile(M, 1536, 128)
    nt = T // tt

    def body(a_ref, b_ref, o_ref, acc_ref):
        t = pl.program_id(1)

        @pl.when(t == 0)
        def _():
            acc_ref[...] = jnp.zeros_like(acc_ref)

        acc_ref[...] += _dot(a_ref[...], b_ref[...], TN)

        @pl.when(t == nt - 1)
        def _():
            o_ref[...] = acc_ref[...].astype(BF16)

    outs, landed = _call(
        body, name, (M // tmm, nt),
        [pl.BlockSpec((tt, tmm), lambda i, t: (t, i)), pl.BlockSpec((tt, N), lambda i, t: (t, 0))],
        [pl.BlockSpec((tmm, N), lambda i, t: (i, 0))], [SDS((M, N), BF16)], (a, b),
        scratch=[pltpu.VMEM((tmm, N), F32)], sem=("arbitrary", "arbitrary"), carry=carry)
    return outs[0], landed


def wgrad_T(aT, bT3, g, name):
    M, T = aT.shape
    N = bT3.shape[1]
    tt = _tile(T, 1024, 128)
    nt = T // tt

    def body(a_ref, b_ref, o_ref, acc_ref):
        t = pl.program_id(0)

        @pl.when(t == 0)
        def _():
            acc_ref[...] = jnp.zeros_like(acc_ref)

        acc_ref[...] += _dot(a_ref[...], b_ref[0], NT)

        @pl.when(t == nt - 1)
        def _():
            o_ref[...] = acc_ref[...].astype(BF16)

    return pl.pallas_call(
        body, name=name, grid=(nt,),
        in_specs=[pl.BlockSpec((M, tt), lambda t: (0, t)), pl.BlockSpec((1, N, tt), lambda t: (g, 0, t))],
        out_specs=pl.BlockSpec((M, N), lambda t: (0, 0)),
        out_shape=SDS((M, N), BF16), scratch_shapes=[pltpu.VMEM((M, N), F32)],
        compiler_params=_cp(("arbitrary",)),
    )(aT, bT3)


def wgrad_mixed(aT, b, name):
    M, T = aT.shape
    N = b.shape[1]
    tt = _tile(T, 1024, 128)

    def body(a_ref, b_ref, o_ref, acc_ref):
        t = pl.program_id(0)

        @pl.when(t == 0)
        def _():
            acc_ref[...] = jnp.zeros_like(acc_ref)

        acc_ref[...] += _dot(a_ref[...], b_ref[...], NN)

        @pl.when(t == pl.num_programs(0) - 1)
        def _():
            o_ref[...] = acc_ref[...].astype(BF16)

    return pl.pallas_call(
        body, name=name, grid=(T // tt,),
        in_specs=[pl.BlockSpec((M, tt), lambda t: (0, t)), pl.BlockSpec((tt, N), lambda t: (t, 0))],
        out_specs=pl.BlockSpec((M, N), lambda t: (0, 0)),
        out_shape=SDS((M, N), BF16), scratch_shapes=[pltpu.VMEM((M, N), F32)],
        compiler_params=_cp(("arbitrary",)),
    )(aT, b)


def _shift_down(x, k, rows):
    return jnp.where(rows >= k, pltpu.roll(x, k, 0), 0.0)


def _shift_up(x, k, rows):
    n = x.shape[0]
    return jnp.where(rows < n - k, pltpu.roll(x, n - k, 0), 0.0)


def _pick(g, vals):
    out = vals[-1]
    for k in range(len(vals) - 2, -1, -1):
        out = jnp.where(g == k, vals[k], out)
    return out


def pool_fwd(x, pw, scale, B, S):
    T, D = x.shape
    G = len(POOL_WINDOWS)
    Cg = D // G

    def body(x_ref, w_ref, s_ref, mix_ref, pooled_ref):
        g = pl.program_id(1)
        xv = x_ref[...]
        rows = lax.broadcasted_iota(jnp.int32, xv.shape, 0)
        sums, cur, k = [], xv, 1
        for _ in POOL_WINDOWS:
            cur = cur + _shift_down(cur, k, rows)
            sums.append(cur)
            k *= 2
        win = 2 * lax.shift_left(jnp.int32(1), g)
        total = _pick(g, sums)
        count = jnp.minimum(rows + 1, win).astype(F32)
        pooled = total / count - xv
        pb = pooled.astype(BF16)
        pooled_ref[...] = pb
        mix_ref[...] = _dot(pb, w_ref[0]) * s_ref[...]

    blk = pl.BlockSpec((S, Cg), lambda b, g: (b, g))
    return pl.pallas_call(
        body, name="pool_fwd", grid=(B, G),
        in_specs=[blk, pl.BlockSpec((1, Cg, Cg), lambda b, g: (g, 0, 0)), pl.BlockSpec((1, Cg), lambda b, g: (0, g))],
        out_specs=[blk, blk],
        out_shape=[SDS((T, D), F32), SDS((T, D), BF16)], compiler_params=_cp(),
    )(x, pw, scale)


def pool_bwd(dmix, pooled, pw, scale, alpha, B, S):
    T, D = dmix.shape
    G = len(POOL_WINDOWS)
    Cg = D // G

    def body(d_ref, p_ref, w_ref, s_ref, dx_ref, ds_ref, dw_ref):
        g = pl.program_id(1)
        dm = d_ref[...]
        pb = p_ref[...]
        w = w_ref[0]
        ypre = _dot(pb, w)
        ds_ref[0] = jnp.sum(dm * ypre, axis=0, keepdims=True)
        dy = (dm * s_ref[...]).astype(BF16)
        dpool = _dot(dy, w, NT)
        dw_ref[0, 0] = _dot(pb, dy, TN)
        rows = lax.broadcasted_iota(jnp.int32, dm.shape, 0)
        win = 2 * lax.shift_left(jnp.int32(1), g)
        count = jnp.minimum(rows + 1, win).astype(F32)
        cur, k, sums = dpool / count, 1, []
        for _ in POOL_WINDOWS:
            cur = cur + _shift_up(cur, k, rows)
            sums.append(cur)
            k *= 2
        dx_ref[...] = alpha * dm + _pick(g, sums) - dpool

    blk = pl.BlockSpec((S, Cg), lambda b, g: (b, g))
    return pl.pallas_call(
        body, name="pool_bwd", grid=(B, G),
        in_specs=[blk, blk, pl.BlockSpec((1, Cg, Cg), lambda b, g: (g, 0, 0)), pl.BlockSpec((1, Cg), lambda b, g: (0, g))],
        out_specs=[blk, pl.BlockSpec((1, 1, Cg), lambda b, g: (b, 0, g)),
                   pl.BlockSpec((1, 1, Cg, Cg), lambda b, g: (b, g, 0, 0))],
        out_shape=[SDS((T, D), F32), SDS((B, 1, D), F32), SDS((B, G, Cg, Cg), F32)], compiler_params=_cp(),
    )(dmix, pooled, pw, scale)


_GELU_K = math.sqrt(2.0 / math.pi)
_GELU_C = 0.044715


def _conv(g, cw, cb, rows):
    return cb + cw[0:1] * _shift_down(g, 2, rows) + cw[1:2] * _shift_down(g, 1, rows) + cw[2:3] * g


def ffn_up(hb, wgT, wuT, cw, cb, B, S):
    T, D = hb.shape
    Fd = wgT.shape[0]
    fn = _tile(Fd, 256, 128)

    def body(h_ref, wg_ref, wu_ref, cw_ref, cb_ref, g_ref, ge_ref, ud_ref, hh_ref):
        h = h_ref[...]
        g = _dot(h, wg_ref[...], NT)
        u = _dot(h, wu_ref[...], NT)
        rows = lax.broadcasted_iota(jnp.int32, g.shape, 0)
        c = _conv(g, cw_ref[...], cb_ref[...], rows)
        c2 = c * c
        th = jnp.tanh(_GELU_K * (c + _GELU_C * (c2 * c)))
        cdf = 0.5 * (1.0 + th)
        ge = c * cdf
        dgelu = cdf + c * (0.5 * (1.0 - th * th) * (_GELU_K * (1.0 + 3.0 * _GELU_C * c2)))
        g_ref[...] = g.astype(BF16)
        ge_ref[...] = ge.astype(BF16)
        ud_ref[...] = (u * dgelu).astype(BF16)
        hh_ref[...] = (ge * u).astype(BF16)

    hspec = pl.BlockSpec((S, D), lambda b, j: (b, 0))
    wspec = pl.BlockSpec((fn, D), lambda b, j: (j, 0))
    ospec = pl.BlockSpec((S, fn), lambda b, j: (b, j))
    return pl.pallas_call(
        body, name="ffn_up", grid=(B, Fd // fn),
        in_specs=[hspec, wspec, wspec, pl.BlockSpec((3, fn), lambda b, j: (0, j)), pl.BlockSpec((1, fn), lambda b, j: (0, j))],
        out_specs=[ospec] * 4,
        out_shape=[SDS((T, Fd), BF16)] * 4, compiler_params=_cp(),
    )(hb, wgT, wuT, cw, cb)


def ffn_mid_bwd(dfb, wd, g, ge, ud, cw, B, S, carry=None):
    T, D = dfb.shape
    Fd = wd.shape[0]
    fn = _tile(Fd, 256, 128)

    def body(df_ref, wd_ref, g_ref, ge_ref, ud_ref, cw_ref, dg_ref, du_ref, dcb_ref, dcw_ref):
        dhh = _dot(df_ref[...], wd_ref[...], NT)
        gv = g_ref[...].astype(F32)
        cw = cw_ref[...]
        rows = lax.broadcasted_iota(jnp.int32, gv.shape, 0)
        g1 = _shift_down(gv, 1, rows)
        g2 = _shift_down(gv, 2, rows)
        du_ref[...] = (dhh * ge_ref[...].astype(F32)).astype(BF16)
        dc = dhh * ud_ref[...].astype(F32)
        dcb_ref[0] = jnp.sum(dc, axis=0, keepdims=True)
        dcw_ref[0] = jnp.concatenate(
            [jnp.sum(dc * g2, axis=0, keepdims=True), jnp.sum(dc * g1, axis=0, keepdims=True),
             jnp.sum(dc * gv, axis=0, keepdims=True)], axis=0)
        dg = cw[2:3] * dc + cw[1:2] * _shift_up(dc, 1, rows) + cw[0:1] * _shift_up(dc, 2, rows)
        dg_ref[...] = dg.astype(BF16)

    tspec = pl.BlockSpec((S, fn), lambda b, j: (b, j))
    outs, landed = _call(
        body, "ffn_mid_bwd", (B, Fd // fn),
        [pl.BlockSpec((S, D), lambda b, j: (b, 0)), pl.BlockSpec((fn, D), lambda b, j: (j, 0)), tspec, tspec, tspec,
         pl.BlockSpec((3, fn), lambda b, j: (0, j))],
        [tspec, tspec, pl.BlockSpec((1, 1, fn), lambda b, j: (b, 0, j)), pl.BlockSpec((1, 3, fn), lambda b, j: (b, 0, j))],
        [SDS((T, Fd), BF16), SDS((T, Fd), BF16), SDS((B, 1, Fd), F32), SDS((B, 3, Fd), F32)],
        (dfb, wd, g, ge, ud, cw), carry=carry)
    return (*outs, landed)


def ffn_dx(dg, du, wgT, wuT, res, alpha, carry=None):
    T, Fd = dg.shape
    D = wgT.shape[1]
    tm = _tile(T, 256, 16)

    def body(dg_ref, du_ref, wg_ref, wu_ref, r_ref, o_ref):
        o_ref[...] = alpha * r_ref[...] + _dot(dg_ref[...], wg_ref[...]) + _dot(du_ref[...], wu_ref[...])

    a_spec = pl.BlockSpec((tm, Fd), lambda i: (i, 0))
    w_spec = pl.BlockSpec((Fd, D), lambda i: (0, 0))
    o_spec = pl.BlockSpec((tm, D), lambda i: (i, 0))
    outs, landed = _call(body, "ffn_dx", (T // tm,), [a_spec, a_spec, w_spec, w_spec, o_spec], [o_spec],
                         [SDS((T, D), F32)], (dg, du, wgT, wuT, res), carry=carry)
    return outs[0], landed


def _partner_all(x):
    n = x.shape[0]
    r = lax.broadcasted_iota(jnp.int32, x.shape, 0)
    return jnp.where((r % HEAD_DIM) < HEAD_DIM // 2, pltpu.roll(x, n - HEAD_DIM // 2, 0), pltpu.roll(x, HEAD_DIM // 2, 0))


def proj_T(w, xT3, cosT, sinT, blk_off, rope, scale, name):
    G, K, T = xT3.shape
    S = cosT.shape[2]
    Dout = K
    tt = _tile(S, 512, 128)
    H = Dout // HEAD_DIM
    nS = S // tt

    def body(w_ref, x_ref, c_ref, s_ref, o_ref):
        acc = _dot(w_ref[...], x_ref[0])
        if rope:
            cos = jnp.tile(c_ref[0], (H, 1))
            sin = jnp.tile(s_ref[0], (H, 1))
            acc = acc * cos + _partner_all(acc) * sin
        if scale != 1.0:
            acc = acc * scale
        o_ref[0] = acc.astype(BF16)

    tab = pl.BlockSpec((1, HEAD_DIM, tt), lambda g, j: (g, 0, j % nS))
    return pl.pallas_call(
        body, name=name, grid=(G, T // tt),
        in_specs=[pl.BlockSpec((Dout, K), lambda g, j: (g + blk_off, 0)), pl.BlockSpec((1, K, tt), lambda g, j: (g, 0, j)), tab, tab],
        out_specs=pl.BlockSpec((1, Dout, tt), lambda g, j: (g, 0, j)),
        out_shape=SDS((G, Dout, T), BF16), compiler_params=_cp(),
    )(w, xT3, cosT, sinT)


def _attn_bias():
    kj = lax.broadcasted_iota(jnp.int32, (2 * BLK, BLK), 0)
    qi = lax.broadcasted_iota(jnp.int32, (2 * BLK, BLK), 1)
    ok = ((kj >= BLK) & (kj - BLK <= qi)) | ((kj < BLK) & (kj >= qi))
    return jnp.where(ok, 0.0, NEG).astype(F32)


def _has_prev(g, S):
    nb = S // (DILATIONS[g] * BLK)
    return [(n % nb) != 0 for n in range(S // BLK)]


def _win(ref, n, hp):
    lo = (n - 1) * BLK if hp else n * BLK
    return ref[0, :, lo:(n + 1) * BLK]


def attn_fwd(qT3, kT3, vT3, bias, g, B, S):
    _, D, T = qT3.shape
    H = D // HEAD_DIM
    nblk = S // BLK
    hp = _has_prev(g, S)

    def body(q_ref, k_ref, v_ref, b_ref, o_ref, l_ref, s_scr, p_scr, rl_scr):
        for n in range(nblk):
            lo = 0 if hp[n] else BLK
            s_scr[n, lo:, :] = _dot(_win(k_ref, n, hp[n]), q_ref[0, :, n * BLK:(n + 1) * BLK], TN)
        for n in range(nblk):
            lo = 0 if hp[n] else BLK
            sT = s_scr[n, lo:, :] + b_ref[lo:, :]
            m = jnp.max(sT, axis=0, keepdims=True)
            p = jnp.exp(sT - m)
            l = jnp.sum(p, axis=0, keepdims=True)
            p_scr[n, lo:, :] = p.astype(BF16)
            rl_scr[n:n + 1, :] = 1.0 / l
            l_ref[0, :, n * BLK:(n + 1) * BLK] = m + jnp.log(l)
        for n in range(nblk):
            lo = 0 if hp[n] else BLK
            o_ref[:, n * BLK:(n + 1) * BLK] = _dot(_win(v_ref, n, hp[n]), p_scr[n, lo:, :]) * rl_scr[n:n + 1, :]

    spec = pl.BlockSpec((1, HEAD_DIM, S), lambda b, h: (g, h, b))
    return pl.pallas_call(
        body, name=f"attn_fwd_g{g}", grid=(B, H),
        in_specs=[spec, spec, spec, pl.BlockSpec((2 * BLK, BLK), lambda b, h: (0, 0))],
        out_specs=[pl.BlockSpec((HEAD_DIM, S), lambda b, h: (h, b)), pl.BlockSpec((1, 1, S), lambda b, h: (h, 0, b))],
        out_shape=[SDS((D, T), F32), SDS((H, 1, T), F32)],
        scratch_shapes=[pltpu.VMEM((nblk, 2 * BLK, BLK), F32), pltpu.VMEM((nblk, 2 * BLK, BLK), BF16),
                        pltpu.VMEM((nblk, BLK), F32)],
        compiler_params=_cp(),
    )(qT3, kT3, vT3, bias)


def attn_combine(oTs, lses):
    G = len(oTs)
    D, T = oTs[0].shape
    H = D // HEAD_DIM
    tn = _tile(T, 2048, 128)

    def body(*refs):
        o_refs, l_refs = refs[:G], refs[G:2 * G]
        ob_ref, of_ref, lt_ref = refs[2 * G:]
        ls = [r[0] for r in l_refs]
        m = functools.reduce(jnp.maximum, ls)
        es = [jnp.exp(v - m) for v in ls]
        z = functools.reduce(lambda a, b: a + b, es)
        o = (es[0] / z) * o_refs[0][...]
        for i in range(1, G):
            o = o + (es[i] / z) * o_refs[i][...]
        ob_ref[...] = o.astype(BF16)
        of_ref[...] = o
        lt_ref[0] = m + jnp.log(z)

    ospec = pl.BlockSpec((HEAD_DIM, tn), lambda h, j: (h, j))
    lspec = pl.BlockSpec((1, 1, tn), lambda h, j: (h, 0, j))
    return pl.pallas_call(
        body, name="attn_combine", grid=(H, T // tn),
        in_specs=[ospec] * G + [lspec] * G, out_specs=[ospec, ospec, lspec],
        out_shape=[SDS((D, T), BF16), SDS((D, T), F32), SDS((H, 1, T), F32)], compiler_params=_cp(),
    )(*oTs, *lses)


def attn_delta(doT, oT):
    D, T = doT.shape
    H = D // HEAD_DIM
    tn = _tile(T, 2048, 128)

    def body(d_ref, o_ref, r_ref):
        r_ref[0] = jnp.sum(d_ref[...].astype(F32) * o_ref[...], axis=0, keepdims=True)

    spec = pl.BlockSpec((HEAD_DIM, tn), lambda h, j: (h, j))
    return pl.pallas_call(
        body, name="attn_delta", grid=(H, T // tn), in_specs=[spec, spec],
        out_specs=pl.BlockSpec((1, 1, tn), lambda h, j: (h, 0, j)),
        out_shape=SDS((H, 1, T), F32), compiler_params=_cp(),
    )(doT, oT)


def attn_bwd(qT3, kT3, vT3, doT, lse, delta, cosT, sinT, bias, g, q_scale, B, S, dk_prev=None, dv_prev=None):
    _, D, T = qT3.shape
    H = D // HEAD_DIM
    nblk = S // BLK
    half = HEAD_DIM // 2
    hp = _has_prev(g, S)
    acc_in = dk_prev is not None
    kv_dtype = BF16 if acc_in else F32

    def body(*refs):
        q_ref, k_ref, v_ref, do_ref, l_ref, d_ref, c_ref, s_ref, b_ref = refs[:9]
        rest = refs[9:]
        if acc_in:
            dkp_ref, dvp_ref = rest[:2]
            rest = rest[2:]
        dq_ref, dk_ref, dv_ref, s_scr, dp_scr, p_scr, ds_scr = rest
        for n in range(nblk):
            lo = 0 if hp[n] else BLK
            blk = slice(n * BLK, (n + 1) * BLK)
            s_scr[n, lo:, :] = _dot(_win(k_ref, n, hp[n]), q_ref[0, :, blk], TN)
            dp_scr[n, lo:, :] = _dot(_win(v_ref, n, hp[n]), do_ref[:, blk], TN)
        for n in range(nblk):
            lo = 0 if hp[n] else BLK
            blk = slice(n * BLK, (n + 1) * BLK)
            pT = jnp.exp(s_scr[n, lo:, :] + b_ref[lo:, :] - l_ref[0, :, blk])
            p_scr[n, lo:, :] = pT.astype(BF16)
            ds_scr[n, lo:, :] = (pT * (dp_scr[n, lo:, :] - d_ref[0, :, blk])).astype(BF16)
        for j in range(nblk):
            blk = slice(j * BLK, (j + 1) * BLK)
            if j + 1 < nblk and hp[j + 1]:
                two = slice(j * BLK, (j + 2) * BLK)
                pj = jnp.concatenate([p_scr[j, BLK:, :], p_scr[j + 1, :BLK, :]], axis=1)
                dsj = jnp.concatenate([ds_scr[j, BLK:, :], ds_scr[j + 1, :BLK, :]], axis=1)
                dv = _dot(do_ref[:, two], pj, NT)
                dk = _dot(q_ref[0, :, two], dsj, NT)
            else:
                dv = _dot(do_ref[:, blk], p_scr[j, BLK:, :], NT)
                dk = _dot(q_ref[0, :, blk], ds_scr[j, BLK:, :], NT)
            dk = dk * c_ref[0, :, blk] - pltpu.roll(dk, half, 0) * s_ref[0, :, blk]
            if acc_in:
                dk = dk + dkp_ref[:, blk]
                dv = dv + dvp_ref[:, blk]
            dk_ref[:, blk] = dk.astype(kv_dtype)
            dv_ref[:, blk] = dv.astype(kv_dtype)
            lo = 0 if hp[j] else BLK
            dq = _dot(_win(k_ref, j, hp[j]), ds_scr[j, lo:, :])
            dq = dq * c_ref[0, :, blk] - pltpu.roll(dq, half, 0) * s_ref[0, :, blk]
            dq_ref[:, blk] = (dq * q_scale).astype(BF16)

    spec3 = pl.BlockSpec((1, HEAD_DIM, S), lambda b, h: (g, h, b))
    spec = pl.BlockSpec((HEAD_DIM, S), lambda b, h: (h, b))
    sspec = pl.BlockSpec((1, 1, S), lambda b, h: (h, 0, b))
    tab = pl.BlockSpec((1, HEAD_DIM, S), lambda b, h: (g, 0, 0))
    in_specs = [spec3, spec3, spec3, spec, sspec, sspec, tab, tab, pl.BlockSpec((2 * BLK, BLK), lambda b, h: (0, 0))]
    args = [qT3, kT3, vT3, doT, lse, delta, cosT, sinT, bias]
    if acc_in:
        in_specs += [spec, spec]
        args += [dk_prev, dv_prev]
    return pl.pallas_call(
        body, name=f"attn_bwd_g{g}" + ("_acc" if acc_in else ""), grid=(B, H),
        in_specs=in_specs, out_specs=[spec, spec, spec],
        out_shape=[SDS((D, T), BF16), SDS((D, T), kv_dtype), SDS((D, T), kv_dtype)],
        scratch_shapes=[pltpu.VMEM((nblk, 2 * BLK, BLK), F32), pltpu.VMEM((nblk, 2 * BLK, BLK), F32),
                        pltpu.VMEM((nblk, 2 * BLK, BLK), BF16), pltpu.VMEM((nblk, 2 * BLK, BLK), BF16)],
        compiler_params=_cp(),
    )(*args)


def adamw(w, g, m, v, name):
    R, C = w.shape
    tr = _tile(R, 512, 8)

    def body(w_ref, g_ref, m_ref, v_ref, d_ref, nm_ref, nv_ref):
        gv = g_ref[...]
        nm = ADAM_B1 * m_ref[...] + (1.0 - ADAM_B1) * gv
        nv = ADAM_B2 * v_ref[...] + (1.0 - ADAM_B2) * (gv * gv)
        m_hat = nm / (1.0 - ADAM_B1 ** ADAM_STEP)
        v_hat = nv / (1.0 - ADAM_B2 ** ADAM_STEP)
        d_ref[...] = -ADAM_LR * (m_hat / (jnp.sqrt(v_hat) + ADAM_EPS) + ADAM_WD * w_ref[...])
        nm_ref[...] = nm
        nv_ref[...] = nv

    spec = pl.BlockSpec((tr, C), lambda i: (i, 0))
    return pl.pallas_call(
        body, name=name, grid=(R // tr,), in_specs=[spec] * 4, out_specs=[spec] * 3,
        out_shape=[SDS((R, C), F32)] * 3, compiler_params=_cp(),
    )(w, g, m, v)


def _perm(a, B, S, d):
    if d == 1:
        return a
    lead = a.shape[:-1]
    return a.reshape(*lead, B, S // d, d).swapaxes(-1, -2).reshape(*lead, B * S)


def _unperm(a, B, S, d):
    if d == 1:
        return a
    lead = a.shape[:-1]
    return a.reshape(*lead, B, d, S // d).swapaxes(-1, -2).reshape(*lead, B * S)


def _perm3(a, B, S):
    return jnp.stack([_perm(a, B, S, d) for d in DILATIONS])


def _xT3(xb, B, S):
    D = xb.shape[1]
    outs = []
    for d in DILATIONS:
        outs.append(xb.reshape(B, S // d, d, D).transpose(3, 0, 2, 1).reshape(D, B * S))
    return jnp.stack(outs)


def _rope_tables(S):
    half = HEAD_DIM // 2
    inv_freq = ROPE_THETA ** (-jnp.arange(0, HEAD_DIM, 2, dtype=F32) / HEAD_DIM)
    ang = jnp.arange(S, dtype=F32)[:, None] * inv_freq[None, :]
    cos = jnp.concatenate([jnp.cos(ang), jnp.cos(ang)], axis=1).T
    sin = jnp.concatenate([-jnp.sin(ang), jnp.sin(ang)], axis=1).T
    return _perm3(cos, 1, S), _perm3(sin, 1, S)


def kernel(x, pool_w, pool_scale, w_q, w_kv, w_o, ffn_w_gate, ffn_w_up, ffn_conv_w, ffn_conv_b, ffn_w_down, ln1_g, ln1_b, ln2_g, ln2_b, loss_target, m_pool_w, m_pool_scale, m_w_q, m_w_kv, m_w_o, m_ffn_w_gate, m_ffn_w_up, m_ffn_conv_w, m_ffn_conv_b, m_ffn_w_down, m_ln1_g, m_ln1_b, m_ln2_g, m_ln2_b, v_pool_w, v_pool_scale, v_w_q, v_w_kv, v_w_o, v_ffn_w_gate, v_ffn_w_up, v_ffn_conv_w, v_ffn_conv_b, v_ffn_w_down, v_ln1_g, v_ln1_b, v_ln2_g, v_ln2_b):
    B, S, D = x.shape
    T = B * S
    depth = ln1_g.shape[0]
    nA, nB = pool_w.shape[0], w_q.shape[0]
    Fs = ffn_w_down.shape[1]
    Fd = Fs * N_DEV
    H = D // HEAD_DIM
    G = len(DILATIONS)
    PG = len(POOL_WINDOWS)
    Cg = D // PG
    alpha = (2.0 * depth) ** 0.25
    me = 4 * lax.axis_index("x") + 2 * lax.axis_index("y") + lax.axis_index("c")

    qs, kvs, os_ = w_q.shape[2], w_kv.shape[1], w_o.shape[1]
    pool_rows = pool_w.size // D
    pieces = [jnp.swapaxes(w_q, 1, 2).reshape(nB * qs, D), w_kv.T, w_o.reshape(nB * os_, D),
              jnp.swapaxes(ffn_w_gate, 1, 2).reshape(depth * Fs, D), jnp.swapaxes(ffn_w_up, 1, 2).reshape(depth * Fs, D),
              ffn_w_down.reshape(depth * Fs, D), pool_w.reshape(pool_rows, D)]
    sizes = [p.shape[0] for p in pieces]
    big = all_gather_blocks(jnp.concatenate(pieces, axis=0).astype(BF16), "gather_weights", in_vmem=False)
    offs = [sum(sizes[:i]) for i in range(len(sizes))]

    def full(i, per_layer, nl):
        blk = big[:, offs[i]:offs[i] + sizes[i]].reshape(N_DEV, nl, per_layer, D)
        return blk.transpose(1, 0, 2, 3).reshape(nl, N_DEV * per_layer, D)

    WqT = full(0, qs, nB)
    WkvT = full(1, kvs, 1)[0]
    Wo = full(2, os_, nB)
    WgT, WuT, Wd = full(3, Fs, depth), full(4, Fs, depth), full(5, Fs, depth)
    PW = big[:, offs[6]:offs[6] + sizes[6]].reshape(N_DEV, nA, PG, Cg // N_DEV, Cg)
    PW = PW.transpose(1, 2, 0, 3, 4).reshape(nA, PG, Cg, Cg)

    sm_cols = 128
    sm_local = jnp.concatenate([ffn_conv_w.reshape(-1), pool_scale.reshape(-1)])
    sm_rows = -(-sm_local.size // sm_cols)
    sm_rows_p = -(-sm_rows // 8) * 8
    sm_local = jnp.pad(sm_local, (0, sm_rows_p * sm_cols - sm_local.size)).reshape(sm_rows_p, sm_cols)
    sm = all_gather_blocks(sm_local, "gather_small", in_vmem=True).reshape(N_DEV, -1)
    ncw = ffn_conv_w.size
    conv_w_full = sm[:, :ncw].reshape(N_DEV, depth, 3, Fs).transpose(1, 2, 0, 3).reshape(depth, 3, Fd)
    pool_scale_full = sm[:, ncw:ncw + pool_scale.size].reshape(N_DEV, nA, D // N_DEV).transpose(1, 0, 2).reshape(nA, 1, D)

    cosT, sinT = _rope_tables(S)
    bias = _attn_bias()

    xs = x.reshape(T, D)
    saved = []
    cur, curb = xs, None
    kT = vT = x1T3 = None
    for i in range(depth):
        sv = {}
        if i < nA:
            mix, pooled = pool_fwd(cur, PW[i], pool_scale_full[i], B, S)
            sv["pooled"] = pooled
            a1, h, hb = add_ln(cur, mix, ln1_g[i], ln1_b[i], alpha)
        else:
            j = i - nA
            xT3 = x1T3 if j == 0 else _xT3(curb, B, S)
            qT = proj_T(WqT[j], xT3, cosT, sinT, 0, True, HEAD_DIM ** -0.5, "q_proj")
            oTs, lses = [], []
            for gi, d in enumerate(DILATIONS):
                o_g, lse_g = attn_fwd(qT, kT, vT, bias, gi, B, S)
                oTs.append(_unperm(o_g, B, S, d))
                lses.append(_unperm(lse_g, B, S, d))
            oTb, oTf, lse_tot = attn_combine(oTs, lses)
            a1, h, hb = matmul_ln(oTb, Wo[j], TN, cur, ln1_g[i], ln1_b[i], alpha, "o_proj_ln")
            sv.update(xT3=xT3, qT=qT, oTb=oTb, oTf=oTf, lse_tot=lse_tot)
        g, ge, ud, hh = ffn_up(hb, WgT[i], WuT[i], conv_w_full[i], ffn_conv_b[i].reshape(1, Fd), B, S)
        a2, cur, curb = matmul_ln(hh, Wd[i], NN, h, ln2_g[i], ln2_b[i], alpha, "ffn_down_ln")
        sv.update(a1=a1, hb=hb, g=g, ge=ge, ud=ud, hh=hh, a2=a2)
        saved.append(sv)
        if i == nA - 1:
            x1T3 = _xT3(curb, B, S)
            kT = proj_T(WkvT, x1T3, cosT, sinT, 0, True, 1.0, "k_proj")
            vT = proj_T(WkvT, x1T3, cosT, sinT, G, False, 1.0, "v_proj")

    dy, sq = loss_grad(cur, loss_target.reshape(T, D))

    small = {k: [None] * depth for k in ("ln1_g", "ln1_b", "ln2_g", "ln2_b", "conv_b", "conv_w")}
    dscale = [None] * nA
    dpw = [None] * nA
    dk_acc, dv_acc = [None] * G, [None] * G

    def blocks(a, rows):
        return a.reshape(N_DEV, rows, D)

    pending, landed = [], {}

    def next_carry():
        if not pending:
            return None, None
        key, parts = pending.pop(0)
        return key, Scatter(parts)

    dcur = dy
    for i in reversed(range(depth)):
        sv = saved[i]
        db2, db2b, small["ln2_g"][i], small["ln2_b"][i] = ln_bwd(dcur, sv["a2"], ln2_g[i])
        key, cr = next_carry()
        dg_, du_, dcb, dcw, got = ffn_mid_bwd(db2b, Wd[i], sv["g"], sv["ge"], sv["ud"], conv_w_full[i], B, S, carry=cr)
        if cr is not None:
            landed[key] = got
        small["conv_b"][i] = jnp.sum(dcb, axis=0)
        small["conv_w"][i] = jnp.sum(dcw, axis=0)
        key, cr = next_carry()
        dwd, got = wgrad_rows(sv["hh"], db2b, "wgrad_down", carry=cr)
        if cr is not None:
            landed[key] = got
        dwg, landed[("down", i)] = wgrad_rows(dg_, sv["hb"], "wgrad_gate", carry=Scatter([blocks(dwd, Fs)]))
        dwu, landed[("gate", i)] = wgrad_rows(du_, sv["hb"], "wgrad_up", carry=Scatter([blocks(dwg, Fs)]))
        dh, landed[("up", i)] = ffn_dx(dg_, du_, WgT[i], WuT[i], db2, alpha, carry=Scatter([blocks(dwu, Fs)]))
        da1, da1b, small["ln1_g"][i], small["ln1_b"][i] = ln_bwd(dh, sv["a1"], ln1_g[i])
        if i < nA:
            dcur, dsp, dpwp = pool_bwd(da1, sv["pooled"], PW[i], pool_scale_full[i], alpha, B, S)
            dscale[i] = jnp.sum(dsp, axis=0)
            dpw[i] = jnp.sum(dpwp, axis=0)
        else:
            j = i - nA
            doT = matmul_to_T(Wo[j], da1b, "o_proj_bwd")
            dwo = wgrad_mixed(sv["oTb"], da1b, "wgrad_o")
            delta = attn_delta(doT, sv["oTf"])
            dq_tok, dwq = [], []
            for gi, d in enumerate(DILATIONS):
                dq_g, dk_acc[gi], dv_acc[gi] = attn_bwd(
                    sv["qT"], kT, vT, _perm(doT, B, S, d), _perm(sv["lse_tot"], B, S, d), _perm(delta, B, S, d),
                    cosT, sinT, bias, gi, HEAD_DIM ** -0.5, B, S, dk_prev=dk_acc[gi], dv_prev=dv_acc[gi])
                dwq.append(wgrad_T(dq_g, sv["xT3"], gi, "wgrad_q"))
                dq_tok.append(_unperm(dq_g, B, S, d))
            dwq = jnp.concatenate(dwq, axis=0)
            dcur = dx_from_T(dq_tok, WqT[j], da1, alpha, "q_proj_bwd", 512)
            if j == 0:
                dkv = [a.astype(BF16) for a in dk_acc + dv_acc]
                dwkv = jnp.concatenate([wgrad_T(a, x1T3, gi % G, "wgrad_kv") for gi, a in enumerate(dkv)], axis=0)
                dkv_tok = [_unperm(a, B, S, DILATIONS[gi % G]) for gi, a in enumerate(dkv)]
                dcur = dx_from_T(dkv_tok, WkvT, dcur, 1.0, "kv_proj_bwd", 256)
                pending.append((("kv",), [blocks(dwkv, kvs)]))
            pending.append((("attn", j), [blocks(dwq, qs), blocks(dwo, os_)]))
    grad_x = dcur.reshape(B, S, D)

    dpw_all = jnp.stack(dpw).reshape(nA, PG, N_DEV, Cg // N_DEV, Cg).transpose(2, 0, 1, 3, 4).reshape(N_DEV, pool_rows, D)
    tail_keys = [k for k, _ in pending] + [("pool",)]
    tail_parts = [parts for _, parts in pending] + [[dpw_all.astype(BF16)]]
    tail_rows = [sum(p.shape[1] for p in parts) for parts in tail_parts]
    tail = scatter_partials([p for parts in tail_parts for p in parts], "scatter_tail")
    for t, key in enumerate(tail_keys):
        lo = sum(tail_rows[:t])
        landed[key] = tail[:, lo:lo + tail_rows[t]]

    def reduced(key):
        return sum_slots(landed[key], "sum_" + "_".join(str(k) for k in key))

    g_attn = [reduced(("attn", j)) for j in range(nB)]
    g_w_q = jnp.swapaxes(jnp.stack([a[:qs] for a in g_attn]), 1, 2)
    g_w_o = jnp.stack([a[qs:] for a in g_attn])
    g_w_kv = reduced(("kv",)).T
    g_gate = jnp.swapaxes(jnp.stack([reduced(("gate", i)) for i in range(depth)]), 1, 2)
    g_up = jnp.swapaxes(jnp.stack([reduced(("up", i)) for i in range(depth)]), 1, 2)
    g_down = jnp.stack([reduced(("down", i)) for i in range(depth)])
    g_pool_w = reduced(("pool",)).reshape(pool_w.shape)

    def rows_of(a):
        a = a.reshape(-1)
        n = -(-a.size // D) * D
        return jnp.pad(a, (0, n - a.size)).reshape(-1, D)

    sm_parts = [rows_of(jnp.concatenate(small[k], axis=0)) for k in ("ln1_g", "ln1_b", "ln2_g", "ln2_b")]
    sm_parts += [rows_of(jnp.stack(small["conv_b"])), rows_of(jnp.stack(small["conv_w"])), rows_of(jnp.stack(dscale)), sq]
    sm_sizes = [p.shape[0] for p in sm_parts]
    sm_all = jnp.concatenate(sm_parts, axis=0)
    pad_rows = -(-sm_all.shape[0] // 8) * 8 - sm_all.shape[0]
    sm_all = jnp.pad(sm_all, ((0, pad_rows), (0, 0)))
    sm_sum = sum_slots(all_gather_blocks(sm_all, "gather_small_grads", in_vmem=True), "sum_small_grads")
    sm_offs = [sum(sm_sizes[:i]) for i in range(len(sm_sizes))]

    def sm_take(i, shape):
        n = math.prod(shape)
        return sm_sum[sm_offs[i]:sm_offs[i] + sm_sizes[i]].reshape(-1)[:n].reshape(shape)

    g_ln1_g, g_ln1_b = sm_take(0, (depth, D)), sm_take(1, (depth, D))
    g_ln2_g, g_ln2_b = sm_take(2, (depth, D)), sm_take(3, (depth, D))
    g_conv_b = sm_take(4, (depth, Fd))
    g_conv_w = lax.dynamic_slice_in_dim(sm_take(5, (depth, 3, Fd)), me * Fs, Fs, axis=2)
    g_pool_scale = lax.dynamic_slice_in_dim(sm_take(6, (nA, D)), me * (D // N_DEV), D // N_DEV, axis=1)
    loss = (0.5 / D) * jnp.sum(sm_take(7, (D,)))

    def v2(a):
        return a.reshape(-1, a.shape[-1])

    names = ["pool_w", "pool_scale", "w_q", "w_kv", "w_o", "ffn_w_gate", "ffn_w_up", "ffn_conv_w", "ffn_conv_b",
             "ffn_w_down", "ln1_g", "ln1_b", "ln2_g", "ln2_b"]
    ws = [pool_w, pool_scale, w_q, w_kv, w_o, ffn_w_gate, ffn_w_up, ffn_conv_w, ffn_conv_b, ffn_w_down, ln1_g, ln1_b, ln2_g, ln2_b]
    ms = [m_pool_w, m_pool_scale, m_w_q, m_w_kv, m_w_o, m_ffn_w_gate, m_ffn_w_up, m_ffn_conv_w, m_ffn_conv_b, m_ffn_w_down, m_ln1_g, m_ln1_b, m_ln2_g, m_ln2_b]
    vs = [v_pool_w, v_pool_scale, v_w_q, v_w_kv, v_w_o, v_ffn_w_gate, v_ffn_w_up, v_ffn_conv_w, v_ffn_conv_b, v_ffn_w_down, v_ln1_g, v_ln1_b, v_ln2_g, v_ln2_b]
    gs = [g_pool_w, g_pool_scale, g_w_q, g_w_kv, g_w_o, g_gate, g_up, g_conv_w, g_conv_b, g_down, g_ln1_g, g_ln1_b, g_ln2_g, g_ln2_b]
    deltas, new_ms, new_vs = [], [], []
    for nm, w, gr, m_, v_ in zip(names, ws, gs, ms, vs):
        d_, nm_, nv_ = adamw(v2(w), v2(gr), v2(m_), v2(v_), "adamw_" + nm)
        deltas.append(d_.reshape(w.shape))
        new_ms.append(nm_.reshape(w.shape))
        new_vs.append(nv_.reshape(w.shape))

    return (loss, grad_x, *gs, *deltas, *new_ms, *new_vs)
```

```python
import functools
import math

import jax
import jax.numpy as jnp
from jax import lax
from jax.experimental import pallas as pl
from jax.experimental.pallas import tpu as pltpu

F32 = jnp.float32
BF16 = jnp.bfloat16
SDS = jax.ShapeDtypeStruct
MESH = pl.DeviceIdType.MESH

N_DEV = 8
HEAD_DIM = 64
BLK = 128
DILATIONS = (1, 4, 16)
POOL_WINDOWS = (2, 4, 8, 16)
ROPE_THETA = 10000.0
LN_EPS = 1e-5
NEG = -1e30
V7X_VMEM_LIMIT = 56 * 1024 * 1024

ADAM_LR, ADAM_B1, ADAM_B2, ADAM_EPS, ADAM_WD, ADAM_STEP = 0.001, 0.9, 0.999, 1e-08, 0.01, 10

NN = (((1,), (0,)), ((), ()))
NT = (((1,), (1,)), ((), ()))
TN = (((0,), (0,)), ((), ()))


def _cp(sem=None):
    kw = dict(vmem_limit_bytes=V7X_VMEM_LIMIT)
    if sem is not None:
        kw["dimension_semantics"] = sem
    return pltpu.CompilerParams(**kw)


def _dot(a, b, dims=NN):
    return lax.dot_general(a, b, dims, preferred_element_type=F32)


def _tile(n, target, mult):
    best = None
    for t in range(mult, min(n, target) + 1, mult):
        if n % t == 0:
            best = t
    return best if best is not None else n


def _mesh_pos():
    return lax.axis_index("x"), lax.axis_index("y"), lax.axis_index("c")


def all_gather_blocks(xl, name, in_vmem):
    R, C = xl.shape
    space = pltpu.VMEM if in_vmem else pl.ANY

    def body(x_ref, out_ref, send_sems, recv_sems, local_sem):
        x, y, c = _mesh_pos()
        me, sibling = (x, y, c), (x, y, 1 - c)
        chips = [(1 - x, y), (x, 1 - y), (1 - x, 1 - y)]

        def slot(px, py, pc):
            return out_ref.at[4 * px + 2 * py + pc]

        def copy(k, block, to, src=None):
            return pltpu.make_async_remote_copy(
                src_ref=slot(*block) if src is None else src, dst_ref=slot(*block),
                send_sem=send_sems.at[k], recv_sem=recv_sems.at[k], device_id=to, device_id_type=MESH)

        mine = pltpu.make_async_copy(x_ref, slot(*me), local_sem)
        mine.start()
        first = [copy(0, me, sibling, src=x_ref)]
        first += [copy(1 + j, me, (*chip, c), src=x_ref) for j, chip in enumerate(chips)]
        for cp in first:
            cp.start()
        passed = [copy(4 + j, (*chip, c), sibling) for j, chip in enumerate(chips)]
        for j, chip in enumerate(chips):
            copy(1 + j, (*chip, c), me).wait_recv()
            passed[j].start()
        copy(0, sibling, me).wait_recv()
        for j, chip in enumerate(chips):
            copy(4 + j, (*chip, 1 - c), me).wait_recv()
        for cp in first + passed:
            cp.wait_send()
        mine.wait()

    return pl.pallas_call(
        body, name=name,
        out_shape=SDS((N_DEV, R, C), xl.dtype),
        in_specs=[pl.BlockSpec(memory_space=space)],
        out_specs=pl.BlockSpec(memory_space=space),
        scratch_shapes=[pltpu.SemaphoreType.DMA((7,)), pltpu.SemaphoreType.DMA((7,)), pltpu.SemaphoreType.DMA],
        compiler_params=_cp(),
    )(xl)


def _peers():
    x, y, c = _mesh_pos()
    peers = []
    for r in range(1, N_DEV):
        peers.append((1 - x if (r & 4) else x, 1 - y if (r & 2) else y, 1 - c if (r & 1) else c))
    return 4 * x + 2 * y + c, peers


class Gather:
    def __init__(self, parts):
        self.parts = list(parts)
        n = len(self.parts)
        self.out_shapes = [SDS((N_DEV,) + p.shape, p.dtype) for p in self.parts]
        self.scratch = [pltpu.SemaphoreType.DMA((7 * n,)), pltpu.SemaphoreType.DMA((7 * n,)), pltpu.SemaphoreType.DMA((n,))]

    def start(self, part_refs, out_refs, send_sems, recv_sems, local_sems):
        me_lin, peers = _peers()
        n = len(self.parts)
        for i in range(n):
            pltpu.make_async_copy(part_refs[i], out_refs[i].at[me_lin], local_sems.at[i]).start()
        for k, peer in enumerate(peers):
            for i in range(n):
                pltpu.make_async_remote_copy(
                    src_ref=part_refs[i], dst_ref=out_refs[i].at[me_lin],
                    send_sem=send_sems.at[k * n + i], recv_sem=recv_sems.at[k * n + i],
                    device_id=peer, device_id_type=MESH).start()

    def wait(self, out_refs, send_sems, recv_sems, local_sems):
        me_lin, peers = _peers()
        n = len(self.parts)
        for k, (px, py, pc) in enumerate(peers):
            p_lin = 4 * px + 2 * py + pc
            for i in range(n):
                arrival = pltpu.make_async_remote_copy(
                    src_ref=out_refs[i].at[p_lin], dst_ref=out_refs[i].at[p_lin],
                    send_sem=send_sems.at[k * n + i], recv_sem=recv_sems.at[k * n + i],
                    device_id=(px, py, pc), device_id_type=MESH)
                arrival.wait_recv()
                arrival.wait_send()
        for i in range(n):
            pltpu.make_async_copy(out_refs[i].at[me_lin], out_refs[i].at[me_lin], local_sems.at[i]).wait()


class Scatter:
    def __init__(self, parts):
        self.parts = list(parts)
        self.rows = [p.shape[1] for p in parts]
        self.offs = [sum(self.rows[:i]) for i in range(len(self.rows))]
        self.out_shapes = [SDS((N_DEV, sum(self.rows), parts[0].shape[2]), parts[0].dtype)]
        self.scratch = [pltpu.SemaphoreType.DMA((7,)), pltpu.SemaphoreType.DMA((7,)), pltpu.SemaphoreType.DMA]

    def start(self, part_refs, out_refs, send_sems, recv_sems, local_sem):
        out_ref = out_refs[0]
        me_lin, peers = _peers()
        for i, (off, r) in enumerate(zip(self.offs, self.rows)):
            pltpu.make_async_copy(part_refs[i].at[me_lin], out_ref.at[me_lin, pl.ds(off, r)], local_sem).start()
        for k, (px, py, pc) in enumerate(peers):
            p_lin = 4 * px + 2 * py + pc
            for i, (off, r) in enumerate(zip(self.offs, self.rows)):
                pltpu.make_async_remote_copy(
                    src_ref=part_refs[i].at[p_lin], dst_ref=out_ref.at[me_lin, pl.ds(off, r)],
                    send_sem=send_sems.at[k], recv_sem=recv_sems.at[k],
                    device_id=(px, py, pc), device_id_type=MESH).start()

    def wait(self, out_refs, send_sems, recv_sems, local_sem):
        out_ref = out_refs[0]
        me_lin, peers = _peers()
        for k, (px, py, pc) in enumerate(peers):
            p_lin = 4 * px + 2 * py + pc
            whole = pltpu.make_async_remote_copy(
                src_ref=out_ref.at[p_lin], dst_ref=out_ref.at[p_lin],
                send_sem=send_sems.at[k], recv_sem=recv_sems.at[k],
                device_id=(px, py, pc), device_id_type=MESH)
            whole.wait_recv()
            whole.wait_send()
        pltpu.make_async_copy(out_ref.at[me_lin], out_ref.at[me_lin], local_sem).wait()


def scatter_partials(parts, name):
    sc = Scatter(parts)
    n = len(parts)

    def body(*refs):
        sc.start(refs[:n], refs[n:n + 1], *refs[n + 1:])
        sc.wait(refs[n:n + 1], *refs[n + 1:])

    return pl.pallas_call(
        body, name=name, out_shape=sc.out_shapes[0],
        in_specs=[pl.BlockSpec(memory_space=pl.ANY)] * n, out_specs=pl.BlockSpec(memory_space=pl.ANY),
        scratch_shapes=sc.scratch, compiler_params=_cp(),
    )(*parts)


def _call(body, name, grid, in_specs, out_specs, out_shape, args, scratch=(), sem=None, carry=None):
    in_specs, out_specs, out_shape, scratch = list(in_specs), list(out_specs), list(out_shape), list(scratch)
    if carry is None:
        outs = pl.pallas_call(body, name=name, grid=grid, in_specs=in_specs, out_specs=out_specs, out_shape=out_shape,
                              scratch_shapes=scratch, compiler_params=_cp(sem))(*args)
        return list(outs), None
    n_in, n_out, n_scr, n_c, n_co = len(in_specs), len(out_specs), len(scratch), len(carry.parts), len(carry.out_shapes)
    last = [g - 1 for g in grid]

    def carried(*refs):
        ins, c_ins = refs[:n_in], refs[n_in:n_in + n_c]
        o0 = n_in + n_c
        outs, c_out = refs[o0:o0 + n_out], refs[o0 + n_out:o0 + n_out + n_co]
        s0 = o0 + n_out + n_co
        scr, c_scr = refs[s0:s0 + n_scr], refs[s0 + n_scr:]
        ids = [pl.program_id(a) for a in range(len(grid))]
        is_first = functools.reduce(jnp.logical_and, [i == 0 for i in ids])
        is_last = functools.reduce(jnp.logical_and, [i == l for i, l in zip(ids, last)])

        @pl.when(is_first)
        def _():
            carry.start(c_ins, c_out, *c_scr)

        body(*ins, *outs, *scr)

        @pl.when(is_last)
        def _():
            carry.wait(c_out, *c_scr)

    hbm = pl.BlockSpec(memory_space=pl.ANY)
    outs = pl.pallas_call(
        carried, name=name + "_carry", grid=grid, in_specs=in_specs + [hbm] * n_c, out_specs=out_specs + [hbm] * n_co,
        out_shape=out_shape + carry.out_shapes, scratch_shapes=scratch + carry.scratch,
        compiler_params=_cp(sem if sem is not None else ("arbitrary",) * len(grid)),
    )(*args, *carry.parts)
    return list(outs[:n_out]), list(outs[n_out:])


def sum_slots(slots, name, out_dtype=F32):
    _, R, C = slots.shape
    tr = _tile(R, 512, 16)

    def body(s_ref, o_ref):
        acc = s_ref[0].astype(F32)
        for s in range(1, N_DEV):
            acc = acc + s_ref[s].astype(F32)
        o_ref[...] = acc.astype(out_dtype)

    return pl.pallas_call(
        body, name=name, grid=(R // tr,),
        in_specs=[pl.BlockSpec((N_DEV, tr, C), lambda i: (0, i, 0))],
        out_specs=pl.BlockSpec((tr, C), lambda i: (i, 0)),
        out_shape=SDS((R, C), out_dtype), compiler_params=_cp(),
    )(slots)


def add_ln(x, mix, g, b, alpha):
    T, D = x.shape
    tm = _tile(T, 512, 16)

    def body(x_ref, m_ref, g_ref, b_ref, a_ref, y_ref, yb_ref):
        a = alpha * x_ref[...] + m_ref[...]
        mu = jnp.mean(a, axis=-1, keepdims=True)
        xc = a - mu
        var = jnp.mean(xc * xc, axis=-1, keepdims=True)
        y = xc * lax.rsqrt(var + LN_EPS) * g_ref[...] + b_ref[...]
        a_ref[...] = a
        y_ref[...] = y
        yb_ref[...] = y.astype(BF16)

    row = pl.BlockSpec((tm, D), lambda i: (i, 0))
    vec = pl.BlockSpec((1, D), lambda i: (0, 0))
    return pl.pallas_call(
        body, name="add_ln", grid=(T // tm,),
        in_specs=[row, row, vec, vec], out_specs=[row, row, row],
        out_shape=[SDS((T, D), F32), SDS((T, D), F32), SDS((T, D), BF16)], compiler_params=_cp(),
    )(x, mix, g.reshape(1, D), b.reshape(1, D))


def ln_bwd(dy, a, g):
    T, D = a.shape
    tm = _tile(T, 512, 16)

    def body(dy_ref, a_ref, g_ref, da_ref, dab_ref, dg_ref, db_ref):
        @pl.when(pl.program_id(0) == 0)
        def _():
            dg_ref[...] = jnp.zeros_like(dg_ref)
            db_ref[...] = jnp.zeros_like(db_ref)

        av = a_ref[...]
        mu = jnp.mean(av, axis=-1, keepdims=True)
        xc = av - mu
        var = jnp.mean(xc * xc, axis=-1, keepdims=True)
        r = lax.rsqrt(var + LN_EPS)
        xh = xc * r
        dyv = dy_ref[...]
        dxh = dyv * g_ref[...]
        m1 = jnp.mean(dxh, axis=-1, keepdims=True)
        m2 = jnp.mean(dxh * xh, axis=-1, keepdims=True)
        da = r * (dxh - m1 - xh * m2)
        da_ref[...] = da
        dab_ref[...] = da.astype(BF16)
        dg_ref[...] += jnp.sum(dyv * xh, axis=0, keepdims=True)
        db_ref[...] += jnp.sum(dyv, axis=0, keepdims=True)

    row = pl.BlockSpec((tm, D), lambda i: (i, 0))
    vec = pl.BlockSpec((1, D), lambda i: (0, 0))
    return pl.pallas_call(
        body, name="ln_bwd", grid=(T // tm,),
        in_specs=[row, row, vec], out_specs=[row, row, vec, vec],
        out_shape=[SDS((T, D), F32), SDS((T, D), BF16), SDS((1, D), F32), SDS((1, D), F32)],
        compiler_params=_cp(("arbitrary",)),
    )(dy, a, g.reshape(1, D))


def loss_grad(y, tgt):
    T, D = y.shape
    tm = _tile(T, 512, 16)

    def body(y_ref, t_ref, dy_ref, sq_ref):
        @pl.when(pl.program_id(0) == 0)
        def _():
            sq_ref[...] = jnp.zeros_like(sq_ref)

        e = y_ref[...] - t_ref[...]
        dy_ref[...] = e / float(D)
        sq_ref[...] += jnp.sum(e * e, axis=0, keepdims=True)

    row = pl.BlockSpec((tm, D), lambda i: (i, 0))
    vec = pl.BlockSpec((1, D), lambda i: (0, 0))
    return pl.pallas_call(
        body, name="loss_grad", grid=(T // tm,),
        in_specs=[row, row], out_specs=[row, vec],
        out_shape=[SDS((T, D), F32), SDS((1, D), F32)], compiler_params=_cp(("arbitrary",)),
    )(y, tgt)


def matmul_ln(a, w, dims, res, g, b, alpha, name, carry=None):
    if dims == TN:
        K, T = a.shape
    else:
        T, K = a.shape
    D = w.shape[1]
    tm = _tile(T, 512, 128 if dims == TN else 16)

    def body(a_ref, w_ref, r_ref, g_ref, b_ref, p_ref, y_ref, yb_ref):
        pre = alpha * r_ref[...] + _dot(a_ref[...], w_ref[...], dims)
        mu = jnp.mean(pre, axis=-1, keepdims=True)
        xc = pre - mu
        var = jnp.mean(xc * xc, axis=-1, keepdims=True)
        y = xc * lax.rsqrt(var + LN_EPS) * g_ref[...] + b_ref[...]
        p_ref[...] = pre
        y_ref[...] = y
        yb_ref[...] = y.astype(BF16)

    a_spec = pl.BlockSpec((K, tm), lambda i: (0, i)) if dims == TN else pl.BlockSpec((tm, K), lambda i: (i, 0))
    row = pl.BlockSpec((tm, D), lambda i: (i, 0))
    vec = pl.BlockSpec((1, D), lambda i: (0, 0))
    outs, landed = _call(
        body, name, (T // tm,), [a_spec, pl.BlockSpec(w.shape, lambda i: (0, 0)), row, vec, vec], [row, row, row],
        [SDS((T, D), F32), SDS((T, D), F32), SDS((T, D), BF16)], (a, w, res, g.reshape(1, D), b.reshape(1, D)), carry=carry)
    return (*outs, landed)


def dx_from_T(aTs, w, res, alpha, name, tm_target):
    T = aTs[0].shape[1]
    N = w.shape[1]
    ks = [a.shape[0] for a in aTs]
    n = len(aTs)
    tm = _tile(T, tm_target, 128)

    def body(*refs):
        a_refs, w_ref, r_ref, o_ref = refs[:n], refs[n], refs[n + 1], refs[n + 2]
        acc = alpha * r_ref[...]
        off = 0
        for a_ref, k in zip(a_refs, ks):
            acc = acc + _dot(a_ref[...], w_ref[off:off + k, :], TN)
            off += k
        o_ref[...] = acc

    row = pl.BlockSpec((tm, N), lambda i: (i, 0))
    return pl.pallas_call(
        body, name=name, grid=(T // tm,),
        in_specs=[pl.BlockSpec((k, tm), lambda i: (0, i)) for k in ks] + [pl.BlockSpec(w.shape, lambda i: (0, 0)), row],
        out_specs=row, out_shape=SDS((T, N), F32), compiler_params=_cp(),
    )(*aTs, w, res)


def matmul_to_T(w, a, name):
    M, K = w.shape
    T = a.shape[0]
    tt = _tile(T, 512, 128)

    def body(w_ref, a_ref, o_ref):
        o_ref[...] = _dot(w_ref[...], a_ref[...], NT).astype(BF16)

    return pl.pallas_call(
        body, name=name, grid=(T // tt,),
        in_specs=[pl.BlockSpec((M, K), lambda i: (0, 0)), pl.BlockSpec((tt, K), lambda i: (i, 0))],
        out_specs=pl.BlockSpec((M, tt), lambda i: (0, i)),
        out_shape=SDS((M, T), BF16), compiler_params=_cp(),
    )(w, a)


def wgrad_rows(a, b, name, carry=None):
    T, M = a.shape
    N = b.shape[1]
    tt = _tile(T, 512, 16)
    tmm = _tile(M, 1536, 128)
    nt = T // tt

    def body(a_ref, b_ref, o_ref, acc_ref):
        t = pl.program_id(1)

        @pl.when(t == 0)
        def _():
            acc_ref[...] = jnp.zeros_like(acc_ref)

        acc_ref[...] += _dot(a_ref[...], b_ref[...], TN)

        @pl.when(t == nt - 1)
        def _():
            o_ref[...] = acc_ref[...].astype(BF16)

    outs, landed = _call(
        body, name, (M // tmm, nt),
        [pl.BlockSpec((tt, tmm), lambda i, t: (t, i)), pl.BlockSpec((tt, N), lambda i, t: (t, 0))],
        [pl.BlockSpec((tmm, N), lambda i, t: (i, 0))], [SDS((M, N), BF16)], (a, b),
        scratch=[pltpu.VMEM((tmm, N), F32)], sem=("arbitrary", "arbitrary"), carry=carry)
    return outs[0], (landed[0] if landed else None)


def wgrad_T(aT, bT3, g, name):
    M, T = aT.shape
    N = bT3.shape[1]
    tt = _tile(T, 1024, 128)
    nt = T // tt

    def body(a_ref, b_ref, o_ref, acc_ref):
        t = pl.program_id(0)

        @pl.when(t == 0)
        def _():
            acc_ref[...] = jnp.zeros_like(acc_ref)

        acc_ref[...] += _dot(a_ref[...], b_ref[0], NT)

        @pl.when(t == nt - 1)
        def _():
            o_ref[...] = acc_ref[...].astype(BF16)

    return pl.pallas_call(
        body, name=name, grid=(nt,),
        in_specs=[pl.BlockSpec((M, tt), lambda t: (0, t)), pl.BlockSpec((1, N, tt), lambda t: (g, 0, t))],
        out_specs=pl.BlockSpec((M, N), lambda t: (0, 0)),
        out_shape=SDS((M, N), BF16), scratch_shapes=[pltpu.VMEM((M, N), F32)],
        compiler_params=_cp(("arbitrary",)),
    )(aT, bT3)


def wgrad_mixed(aT, b, name):
    M, T = aT.shape
    N = b.shape[1]
    tt = _tile(T, 1024, 128)

    def body(a_ref, b_ref, o_ref, acc_ref):
        t = pl.program_id(0)

        @pl.when(t == 0)
        def _():
            acc_ref[...] = jnp.zeros_like(acc_ref)

        acc_ref[...] += _dot(a_ref[...], b_ref[...], NN)

        @pl.when(t == pl.num_programs(0) - 1)
        def _():
            o_ref[...] = acc_ref[...].astype(BF16)

    return pl.pallas_call(
        body, name=name, grid=(T // tt,),
        in_specs=[pl.BlockSpec((M, tt), lambda t: (0, t)), pl.BlockSpec((tt, N), lambda t: (t, 0))],
        out_specs=pl.BlockSpec((M, N), lambda t: (0, 0)),
        out_shape=SDS((M, N), BF16), scratch_shapes=[pltpu.VMEM((M, N), F32)],
        compiler_params=_cp(("arbitrary",)),
    )(aT, b)


def _shift_down(x, k, rows):
    return jnp.where(rows >= k, pltpu.roll(x, k, 0), 0.0)


def _shift_up(x, k, rows):
    n = x.shape[0]
    return jnp.where(rows < n - k, pltpu.roll(x, n - k, 0), 0.0)


def _pick(g, vals):
    out = vals[-1]
    for k in range(len(vals) - 2, -1, -1):
        out = jnp.where(g == k, vals[k], out)
    return out


def pool_fwd(x, pw, scale, B, S):
    T, D = x.shape
    G = len(POOL_WINDOWS)
    Cg = D // G

    def body(x_ref, w_ref, s_ref, mix_ref, pooled_ref):
        g = pl.program_id(1)
        xv = x_ref[...]
        rows = lax.broadcasted_iota(jnp.int32, xv.shape, 0)
        sums, cur, k = [], xv, 1
        for _ in POOL_WINDOWS:
            cur = cur + _shift_down(cur, k, rows)
            sums.append(cur)
            k *= 2
        win = 2 * lax.shift_left(jnp.int32(1), g)
        total = _pick(g, sums)
        count = jnp.minimum(rows + 1, win).astype(F32)
        pooled = total / count - xv
        pb = pooled.astype(BF16)
        pooled_ref[...] = pb
        mix_ref[...] = _dot(pb, w_ref[0]) * s_ref[...]

    blk = pl.BlockSpec((S, Cg), lambda b, g: (b, g))
    return pl.pallas_call(
        body, name="pool_fwd", grid=(B, G),
        in_specs=[blk, pl.BlockSpec((1, Cg, Cg), lambda b, g: (g, 0, 0)), pl.BlockSpec((1, Cg), lambda b, g: (0, g))],
        out_specs=[blk, blk],
        out_shape=[SDS((T, D), F32), SDS((T, D), BF16)], compiler_params=_cp(),
    )(x, pw, scale)


def pool_bwd(dmix, pooled, pw, scale, alpha, B, S):
    T, D = dmix.shape
    G = len(POOL_WINDOWS)
    Cg = D // G

    def body(d_ref, p_ref, w_ref, s_ref, dx_ref, ds_ref, dw_ref):
        g = pl.program_id(1)
        dm = d_ref[...]
        pb = p_ref[...]
        w = w_ref[0]
        ypre = _dot(pb, w)
        ds_ref[0] = jnp.sum(dm * ypre, axis=0, keepdims=True)
        dy = (dm * s_ref[...]).astype(BF16)
        dpool = _dot(dy, w, NT)
        dw_ref[0, 0] = _dot(pb, dy, TN)
        rows = lax.broadcasted_iota(jnp.int32, dm.shape, 0)
        win = 2 * lax.shift_left(jnp.int32(1), g)
        count = jnp.minimum(rows + 1, win).astype(F32)
        cur, k, sums = dpool / count, 1, []
        for _ in POOL_WINDOWS:
            cur = cur + _shift_up(cur, k, rows)
            sums.append(cur)
            k *= 2
        dx_ref[...] = alpha * dm + _pick(g, sums) - dpool

    blk = pl.BlockSpec((S, Cg), lambda b, g: (b, g))
    return pl.pallas_call(
        body, name="pool_bwd", grid=(B, G),
        in_specs=[blk, blk, pl.BlockSpec((1, Cg, Cg), lambda b, g: (g, 0, 0)), pl.BlockSpec((1, Cg), lambda b, g: (0, g))],
        out_specs=[blk, pl.BlockSpec((1, 1, Cg), lambda b, g: (b, 0, g)),
                   pl.BlockSpec((1, 1, Cg, Cg), lambda b, g: (b, g, 0, 0))],
        out_shape=[SDS((T, D), F32), SDS((B, 1, D), F32), SDS((B, G, Cg, Cg), F32)], compiler_params=_cp(),
    )(dmix, pooled, pw, scale)


_GELU_K = math.sqrt(2.0 / math.pi)
_GELU_C = 0.044715


def _conv(g, cw, cb, rows):
    return cb + cw[0:1] * _shift_down(g, 2, rows) + cw[1:2] * _shift_down(g, 1, rows) + cw[2:3] * g


def ffn_up(hb, wgT, wuT, cw, cb, B, S, carry=None):
    T, D = hb.shape
    Fd = wgT.shape[0]
    fn = _tile(Fd, 256, 128)

    def body(h_ref, wg_ref, wu_ref, cw_ref, cb_ref, g_ref, ge_ref, ud_ref, hh_ref):
        h = h_ref[...]
        g = _dot(h, wg_ref[...], NT)
        u = _dot(h, wu_ref[...], NT)
        rows = lax.broadcasted_iota(jnp.int32, g.shape, 0)
        c = _conv(g, cw_ref[...], cb_ref[...], rows)
        c2 = c * c
        th = jnp.tanh(_GELU_K * (c + _GELU_C * (c2 * c)))
        cdf = 0.5 * (1.0 + th)
        ge = c * cdf
        dgelu = cdf + c * (0.5 * (1.0 - th * th) * (_GELU_K * (1.0 + 3.0 * _GELU_C * c2)))
        g_ref[...] = g.astype(BF16)
        ge_ref[...] = ge.astype(BF16)
        ud_ref[...] = (u * dgelu).astype(BF16)
        hh_ref[...] = (ge * u).astype(BF16)

    hspec = pl.BlockSpec((S, D), lambda b, j: (b, 0))
    wspec = pl.BlockSpec((fn, D), lambda b, j: (j, 0))
    ospec = pl.BlockSpec((S, fn), lambda b, j: (b, j))
    outs, landed = _call(
        body, "ffn_up", (B, Fd // fn),
        [hspec, wspec, wspec, pl.BlockSpec((3, fn), lambda b, j: (0, j)), pl.BlockSpec((1, fn), lambda b, j: (0, j))],
        [ospec] * 4, [SDS((T, Fd), BF16)] * 4, (hb, wgT, wuT, cw, cb), carry=carry)
    return (*outs, landed)


def ffn_mid_bwd(dfb, wd, g, ge, ud, cw, B, S, carry=None):
    T, D = dfb.shape
    Fd = wd.shape[0]
    fn = _tile(Fd, 256, 128)

    def body(df_ref, wd_ref, g_ref, ge_ref, ud_ref, cw_ref, dg_ref, du_ref, dcb_ref, dcw_ref):
        dhh = _dot(df_ref[...], wd_ref[...], NT)
        gv = g_ref[...].astype(F32)
        cw = cw_ref[...]
        rows = lax.broadcasted_iota(jnp.int32, gv.shape, 0)
        g1 = _shift_down(gv, 1, rows)
        g2 = _shift_down(gv, 2, rows)
        du_ref[...] = (dhh * ge_ref[...].astype(F32)).astype(BF16)
        dc = dhh * ud_ref[...].astype(F32)
        dcb_ref[0] = jnp.sum(dc, axis=0, keepdims=True)
        dcw_ref[0] = jnp.concatenate(
            [jnp.sum(dc * g2, axis=0, keepdims=True), jnp.sum(dc * g1, axis=0, keepdims=True),
             jnp.sum(dc * gv, axis=0, keepdims=True)], axis=0)
        dg = cw[2:3] * dc + cw[1:2] * _shift_up(dc, 1, rows) + cw[0:1] * _shift_up(dc, 2, rows)
        dg_ref[...] = dg.astype(BF16)

    tspec = pl.BlockSpec((S, fn), lambda b, j: (b, j))
    outs, landed = _call(
        body, "ffn_mid_bwd", (B, Fd // fn),
        [pl.BlockSpec((S, D), lambda b, j: (b, 0)), pl.BlockSpec((fn, D), lambda b, j: (j, 0)), tspec, tspec, tspec,
         pl.BlockSpec((3, fn), lambda b, j: (0, j))],
        [tspec, tspec, pl.BlockSpec((1, 1, fn), lambda b, j: (b, 0, j)), pl.BlockSpec((1, 3, fn), lambda b, j: (b, 0, j))],
        [SDS((T, Fd), BF16), SDS((T, Fd), BF16), SDS((B, 1, Fd), F32), SDS((B, 3, Fd), F32)],
        (dfb, wd, g, ge, ud, cw), carry=carry)
    return (*outs, landed[0] if landed else None)


def ffn_dx(dg, du, wgT, wuT, res, alpha, carry=None):
    T, Fd = dg.shape
    D = wgT.shape[1]
    tm = _tile(T, 256, 16)

    def body(dg_ref, du_ref, wg_ref, wu_ref, r_ref, o_ref):
        o_ref[...] = alpha * r_ref[...] + _dot(dg_ref[...], wg_ref[...]) + _dot(du_ref[...], wu_ref[...])

    a_spec = pl.BlockSpec((tm, Fd), lambda i: (i, 0))
    w_spec = pl.BlockSpec((Fd, D), lambda i: (0, 0))
    o_spec = pl.BlockSpec((tm, D), lambda i: (i, 0))
    outs, landed = _call(body, "ffn_dx", (T // tm,), [a_spec, a_spec, w_spec, w_spec, o_spec], [o_spec],
                         [SDS((T, D), F32)], (dg, du, wgT, wuT, res), carry=carry)
    return outs[0], (landed[0] if landed else None)


def _partner_all(x):
    n = x.shape[0]
    r = lax.broadcasted_iota(jnp.int32, x.shape, 0)
    return jnp.where((r % HEAD_DIM) < HEAD_DIM // 2, pltpu.roll(x, n - HEAD_DIM // 2, 0), pltpu.roll(x, HEAD_DIM // 2, 0))


def proj_T(w, xT3, cosT, sinT, blk_off, rope, scale, name, carry=None):
    G, K, T = xT3.shape
    S = cosT.shape[2]
    Dout = K
    tt = _tile(S, 512, 128)
    H = Dout // HEAD_DIM
    nS = S // tt

    def body(w_ref, x_ref, c_ref, s_ref, o_ref):
        acc = _dot(w_ref[...], x_ref[0])
        if rope:
            cos = jnp.tile(c_ref[0], (H, 1))
            sin = jnp.tile(s_ref[0], (H, 1))
            acc = acc * cos + _partner_all(acc) * sin
        if scale != 1.0:
            acc = acc * scale
        o_ref[0] = acc.astype(BF16)

    tab = pl.BlockSpec((1, HEAD_DIM, tt), lambda g, j: (g, 0, j % nS))
    outs, landed = _call(
        body, name, (G, T // tt),
        [pl.BlockSpec((Dout, K), lambda g, j: (g + blk_off, 0)), pl.BlockSpec((1, K, tt), lambda g, j: (g, 0, j)), tab, tab],
        [pl.BlockSpec((1, Dout, tt), lambda g, j: (g, 0, j))], [SDS((G, Dout, T), BF16)], (w, xT3, cosT, sinT), carry=carry)
    return outs[0], landed


def _attn_bias():
    kj = lax.broadcasted_iota(jnp.int32, (2 * BLK, BLK), 0)
    qi = lax.broadcasted_iota(jnp.int32, (2 * BLK, BLK), 1)
    ok = ((kj >= BLK) & (kj - BLK <= qi)) | ((kj < BLK) & (kj >= qi))
    return jnp.where(ok, 0.0, NEG).astype(F32)


def _has_prev(g, S):
    nb = S // (DILATIONS[g] * BLK)
    return [(n % nb) != 0 for n in range(S // BLK)]


def _win(ref, n, hp):
    lo = (n - 1) * BLK if hp else n * BLK
    return ref[0, :, lo:(n + 1) * BLK]


def attn_fwd(qT3, kT3, vT3, bias, g, B, S):
    _, D, T = qT3.shape
    H = D // HEAD_DIM
    nblk = S // BLK
    hp = _has_prev(g, S)

    def body(q_ref, k_ref, v_ref, b_ref, o_ref, l_ref, s_scr, p_scr, rl_scr):
        for n in range(nblk):
            lo = 0 if hp[n] else BLK
            s_scr[n, lo:, :] = _dot(_win(k_ref, n, hp[n]), q_ref[0, :, n * BLK:(n + 1) * BLK], TN)
        for n in range(nblk):
            lo = 0 if hp[n] else BLK
            sT = s_scr[n, lo:, :] + b_ref[lo:, :]
            m = jnp.max(sT, axis=0, keepdims=True)
            p = jnp.exp(sT - m)
            l = jnp.sum(p, axis=0, keepdims=True)
            p_scr[n, lo:, :] = p.astype(BF16)
            rl_scr[n:n + 1, :] = 1.0 / l
            l_ref[0, :, n * BLK:(n + 1) * BLK] = m + jnp.log(l)
        for n in range(nblk):
            lo = 0 if hp[n] else BLK
            o_ref[:, n * BLK:(n + 1) * BLK] = _dot(_win(v_ref, n, hp[n]), p_scr[n, lo:, :]) * rl_scr[n:n + 1, :]

    spec = pl.BlockSpec((1, HEAD_DIM, S), lambda b, h: (g, h, b))
    return pl.pallas_call(
        body, name=f"attn_fwd_g{g}", grid=(B, H),
        in_specs=[spec, spec, spec, pl.BlockSpec((2 * BLK, BLK), lambda b, h: (0, 0))],
        out_specs=[pl.BlockSpec((HEAD_DIM, S), lambda b, h: (h, b)), pl.BlockSpec((1, 1, S), lambda b, h: (h, 0, b))],
        out_shape=[SDS((D, T), F32), SDS((H, 1, T), F32)],
        scratch_shapes=[pltpu.VMEM((nblk, 2 * BLK, BLK), F32), pltpu.VMEM((nblk, 2 * BLK, BLK), BF16),
                        pltpu.VMEM((nblk, BLK), F32)],
        compiler_params=_cp(),
    )(qT3, kT3, vT3, bias)


def attn_combine(oTs, lses):
    G = len(oTs)
    D, T = oTs[0].shape
    H = D // HEAD_DIM
    tn = _tile(T, 2048, 128)

    def body(*refs):
        o_refs, l_refs = refs[:G], refs[G:2 * G]
        ob_ref, of_ref, lt_ref = refs[2 * G:]
        ls = [r[0] for r in l_refs]
        m = functools.reduce(jnp.maximum, ls)
        es = [jnp.exp(v - m) for v in ls]
        z = functools.reduce(lambda a, b: a + b, es)
        o = (es[0] / z) * o_refs[0][...]
        for i in range(1, G):
            o = o + (es[i] / z) * o_refs[i][...]
        ob_ref[...] = o.astype(BF16)
        of_ref[...] = o
        lt_ref[0] = m + jnp.log(z)

    ospec = pl.BlockSpec((HEAD_DIM, tn), lambda h, j: (h, j))
    lspec = pl.BlockSpec((1, 1, tn), lambda h, j: (h, 0, j))
    return pl.pallas_call(
        body, name="attn_combine", grid=(H, T // tn),
        in_specs=[ospec] * G + [lspec] * G, out_specs=[ospec, ospec, lspec],
        out_shape=[SDS((D, T), BF16), SDS((D, T), F32), SDS((H, 1, T), F32)], compiler_params=_cp(),
    )(*oTs, *lses)


def attn_delta(doT, oT):
    D, T = doT.shape
    H = D // HEAD_DIM
    tn = _tile(T, 2048, 128)

    def body(d_ref, o_ref, r_ref):
        r_ref[0] = jnp.sum(d_ref[...].astype(F32) * o_ref[...], axis=0, keepdims=True)

    spec = pl.BlockSpec((HEAD_DIM, tn), lambda h, j: (h, j))
    return pl.pallas_call(
        body, name="attn_delta", grid=(H, T // tn), in_specs=[spec, spec],
        out_specs=pl.BlockSpec((1, 1, tn), lambda h, j: (h, 0, j)),
        out_shape=SDS((H, 1, T), F32), compiler_params=_cp(),
    )(doT, oT)


def attn_bwd(qT3, kT3, vT3, doT, lse, delta, cosT, sinT, bias, g, q_scale, B, S, dk_prev=None, dv_prev=None):
    _, D, T = qT3.shape
    H = D // HEAD_DIM
    nblk = S // BLK
    half = HEAD_DIM // 2
    hp = _has_prev(g, S)
    acc_in = dk_prev is not None
    kv_dtype = BF16 if acc_in else F32

    def body(*refs):
        q_ref, k_ref, v_ref, do_ref, l_ref, d_ref, c_ref, s_ref, b_ref = refs[:9]
        rest = refs[9:]
        if acc_in:
            dkp_ref, dvp_ref = rest[:2]
            rest = rest[2:]
        dq_ref, dk_ref, dv_ref, s_scr, dp_scr, p_scr, ds_scr = rest
        for n in range(nblk):
            lo = 0 if hp[n] else BLK
            blk = slice(n * BLK, (n + 1) * BLK)
            s_scr[n, lo:, :] = _dot(_win(k_ref, n, hp[n]), q_ref[0, :, blk], TN)
            dp_scr[n, lo:, :] = _dot(_win(v_ref, n, hp[n]), do_ref[:, blk], TN)
        for n in range(nblk):
            lo = 0 if hp[n] else BLK
            blk = slice(n * BLK, (n + 1) * BLK)
            pT = jnp.exp(s_scr[n, lo:, :] + b_ref[lo:, :] - l_ref[0, :, blk])
            p_scr[n, lo:, :] = pT.astype(BF16)
            ds_scr[n, lo:, :] = (pT * (dp_scr[n, lo:, :] - d_ref[0, :, blk])).astype(BF16)
        for j in range(nblk):
            blk = slice(j * BLK, (j + 1) * BLK)
            if j + 1 < nblk and hp[j + 1]:
                two = slice(j * BLK, (j + 2) * BLK)
                pj = jnp.concatenate([p_scr[j, BLK:, :], p_scr[j + 1, :BLK, :]], axis=1)
                dsj = jnp.concatenate([ds_scr[j, BLK:, :], ds_scr[j + 1, :BLK, :]], axis=1)
                dv = _dot(do_ref[:, two], pj, NT)
                dk = _dot(q_ref[0, :, two], dsj, NT)
            else:
                dv = _dot(do_ref[:, blk], p_scr[j, BLK:, :], NT)
                dk = _dot(q_ref[0, :, blk], ds_scr[j, BLK:, :], NT)
            dk = dk * c_ref[0, :, blk] - pltpu.roll(dk, half, 0) * s_ref[0, :, blk]
            if acc_in:
                dk = dk + dkp_ref[:, blk]
                dv = dv + dvp_ref[:, blk]
            dk_ref[:, blk] = dk.astype(kv_dtype)
            dv_ref[:, blk] = dv.astype(kv_dtype)
            lo = 0 if hp[j] else BLK
            dq = _dot(_win(k_ref, j, hp[j]), ds_scr[j, lo:, :])
            dq = dq * c_ref[0, :, blk] - pltpu.roll(dq, half, 0) * s_ref[0, :, blk]
            dq_ref[:, blk] = (dq * q_scale).astype(BF16)

    spec3 = pl.BlockSpec((1, HEAD_DIM, S), lambda b, h: (g, h, b))
    spec = pl.BlockSpec((HEAD_DIM, S), lambda b, h: (h, b))
    sspec = pl.BlockSpec((1, 1, S), lambda b, h: (h, 0, b))
    tab = pl.BlockSpec((1, HEAD_DIM, S), lambda b, h: (g, 0, 0))
    in_specs = [spec3, spec3, spec3, spec, sspec, sspec, tab, tab, pl.BlockSpec((2 * BLK, BLK), lambda b, h: (0, 0))]
    args = [qT3, kT3, vT3, doT, lse, delta, cosT, sinT, bias]
    if acc_in:
        in_specs += [spec, spec]
        args += [dk_prev, dv_prev]
    return pl.pallas_call(
        body, name=f"attn_bwd_g{g}" + ("_acc" if acc_in else ""), grid=(B, H),
        in_specs=in_specs, out_specs=[spec, spec, spec],
        out_shape=[SDS((D, T), BF16), SDS((D, T), kv_dtype), SDS((D, T), kv_dtype)],
        scratch_shapes=[pltpu.VMEM((nblk, 2 * BLK, BLK), F32), pltpu.VMEM((nblk, 2 * BLK, BLK), F32),
                        pltpu.VMEM((nblk, 2 * BLK, BLK), BF16), pltpu.VMEM((nblk, 2 * BLK, BLK), BF16)],
        compiler_params=_cp(),
    )(*args)


def adamw(w, g, m, v, name):
    R, C = w.shape
    tr = _tile(R, 512, 8)

    def body(w_ref, g_ref, m_ref, v_ref, d_ref, nm_ref, nv_ref):
        gv = g_ref[...]
        nm = ADAM_B1 * m_ref[...] + (1.0 - ADAM_B1) * gv
        nv = ADAM_B2 * v_ref[...] + (1.0 - ADAM_B2) * (gv * gv)
        m_hat = nm / (1.0 - ADAM_B1 ** ADAM_STEP)
        v_hat = nv / (1.0 - ADAM_B2 ** ADAM_STEP)
        d_ref[...] = -ADAM_LR * (m_hat / (jnp.sqrt(v_hat) + ADAM_EPS) + ADAM_WD * w_ref[...])
        nm_ref[...] = nm
        nv_ref[...] = nv

    spec = pl.BlockSpec((tr, C), lambda i: (i, 0))
    return pl.pallas_call(
        body, name=name, grid=(R // tr,), in_specs=[spec] * 4, out_specs=[spec] * 3,
        out_shape=[SDS((R, C), F32)] * 3, compiler_params=_cp(),
    )(w, g, m, v)


def _perm(a, B, S, d):
    if d == 1:
        return a
    lead = a.shape[:-1]
    return a.reshape(*lead, B, S // d, d).swapaxes(-1, -2).reshape(*lead, B * S)


def _unperm(a, B, S, d):
    if d == 1:
        return a
    lead = a.shape[:-1]
    return a.reshape(*lead, B, d, S // d).swapaxes(-1, -2).reshape(*lead, B * S)


def _perm3(a, B, S):
    return jnp.stack([_perm(a, B, S, d) for d in DILATIONS])


def _xT3(xb, B, S):
    D = xb.shape[1]
    outs = []
    for d in DILATIONS:
        outs.append(xb.reshape(B, S // d, d, D).transpose(3, 0, 2, 1).reshape(D, B * S))
    return jnp.stack(outs)


def _rope_tables(S):
    half = HEAD_DIM // 2
    inv_freq = ROPE_THETA ** (-jnp.arange(0, HEAD_DIM, 2, dtype=F32) / HEAD_DIM)
    ang = jnp.arange(S, dtype=F32)[:, None] * inv_freq[None, :]
    cos = jnp.concatenate([jnp.cos(ang), jnp.cos(ang)], axis=1).T
    sin = jnp.concatenate([-jnp.sin(ang), jnp.sin(ang)], axis=1).T
    return _perm3(cos, 1, S), _perm3(sin, 1, S)


def kernel(x, pool_w, pool_scale, w_q, w_kv, w_o, ffn_w_gate, ffn_w_up, ffn_conv_w, ffn_conv_b, ffn_w_down, ln1_g, ln1_b, ln2_g, ln2_b, loss_target, m_pool_w, m_pool_scale, m_w_q, m_w_kv, m_w_o, m_ffn_w_gate, m_ffn_w_up, m_ffn_conv_w, m_ffn_conv_b, m_ffn_w_down, m_ln1_g, m_ln1_b, m_ln2_g, m_ln2_b, v_pool_w, v_pool_scale, v_w_q, v_w_kv, v_w_o, v_ffn_w_gate, v_ffn_w_up, v_ffn_conv_w, v_ffn_conv_b, v_ffn_w_down, v_ln1_g, v_ln1_b, v_ln2_g, v_ln2_b):
    B, S, D = x.shape
    T = B * S
    depth = ln1_g.shape[0]
    nA, nB = pool_w.shape[0], w_q.shape[0]
    Fs = ffn_w_down.shape[1]
    Fd = Fs * N_DEV
    H = D // HEAD_DIM
    G = len(DILATIONS)
    PG = len(POOL_WINDOWS)
    Cg = D // PG
    alpha = (2.0 * depth) ** 0.25
    me = 4 * lax.axis_index("x") + 2 * lax.axis_index("y") + lax.axis_index("c")

    qs, kvs, os_ = w_q.shape[2], w_kv.shape[1], w_o.shape[1]
    pool_rows = pool_w.size // D
    local = {("pool",): pool_w.reshape(pool_rows, D).astype(BF16), ("wkv",): w_kv.T.astype(BF16)}
    for j in range(nB):
        local[("wq", j)] = w_q[j].T.astype(BF16)
        local[("wo", j)] = w_o[j].astype(BF16)
    for i in range(depth):
        local[("wg", i)] = ffn_w_gate[i].T.astype(BF16)
        local[("wu", i)] = ffn_w_up[i].T.astype(BF16)
        local[("wd", i)] = ffn_w_down[i].astype(BF16)
    ffn = lambda i: [("wg", i), ("wu", i), ("wd", i)]
    queue = [[("pool",)] + ffn(0)]
    if depth == 4 and nA == 2 and nB == 2:
        queue += [ffn(1), [("wo", 0), ("wo", 1)], [("wkv",), ("wq", 0)], [("wg", 2)], [("wu", 2)], [("wd", 2)], [("wq", 1)], ffn(3)]
    gathered = {}

    def land(keys, arrs):
        for k, a in zip(keys or (), arrs or ()):
            gathered[k] = a.reshape(-1, D)

    def next_gather():
        if not queue:
            return None, None
        keys = queue.pop(0)
        return keys, Gather([local[k] for k in keys])

    def weight(key):
        if key not in gathered:
            keys = [key]
            for bi, batch in enumerate(queue):
                if key in batch:
                    keys = queue.pop(bi)
                    break
            blk = all_gather_blocks(jnp.concatenate([local[k] for k in keys], axis=0), "gather_" + "_".join(map(str, key)), in_vmem=False)
            off = 0
            for k in keys:
                r = local[k].shape[0]
                gathered[k] = blk[:, off:off + r].reshape(-1, D)
                off += r
        return gathered[key]

    PW = weight(("pool",)).reshape(N_DEV, nA, PG, Cg // N_DEV, Cg).transpose(1, 2, 0, 3, 4).reshape(nA, PG, Cg, Cg)

    sm_cols = 128
    sm_local = jnp.concatenate([ffn_conv_w.reshape(-1), pool_scale.reshape(-1)])
    sm_rows = -(-sm_local.size // sm_cols)
    sm_rows_p = -(-sm_rows // 8) * 8
    sm_local = jnp.pad(sm_local, (0, sm_rows_p * sm_cols - sm_local.size)).reshape(sm_rows_p, sm_cols)
    sm = all_gather_blocks(sm_local, "gather_small", in_vmem=True).reshape(N_DEV, -1)
    ncw = ffn_conv_w.size
    conv_w_full = sm[:, :ncw].reshape(N_DEV, depth, 3, Fs).transpose(1, 2, 0, 3).reshape(depth, 3, Fd)
    pool_scale_full = sm[:, ncw:ncw + pool_scale.size].reshape(N_DEV, nA, D // N_DEV).transpose(1, 0, 2).reshape(nA, 1, D)

    cosT, sinT = _rope_tables(S)
    bias = _attn_bias()

    xs = x.reshape(T, D)
    saved = []
    cur, curb = xs, None
    kT = vT = x1T3 = None
    for i in range(depth):
        sv = {}
        if i < nA:
            mix, pooled = pool_fwd(cur, PW[i], pool_scale_full[i], B, S)
            sv["pooled"] = pooled
            a1, h, hb = add_ln(cur, mix, ln1_g[i], ln1_b[i], alpha)
        else:
            j = i - nA
            xT3 = x1T3 if j == 0 else _xT3(curb, B, S)
            keys, cr = next_gather()
            qT, got = proj_T(weight(("wq", j)), xT3, cosT, sinT, 0, True, HEAD_DIM ** -0.5, "q_proj", carry=cr)
            land(keys, got)
            oTs, lses = [], []
            for gi, d in enumerate(DILATIONS):
                o_g, lse_g = attn_fwd(qT, kT, vT, bias, gi, B, S)
                oTs.append(_unperm(o_g, B, S, d))
                lses.append(_unperm(lse_g, B, S, d))
            oTb, oTf, lse_tot = attn_combine(oTs, lses)
            a1, h, hb, _ = matmul_ln(oTb, weight(("wo", j)), TN, cur, ln1_g[i], ln1_b[i], alpha, "o_proj_ln")
            sv.update(xT3=xT3, qT=qT, oTb=oTb, oTf=oTf, lse_tot=lse_tot)
        wg_i, wu_i, wd_i = weight(("wg", i)), weight(("wu", i)), weight(("wd", i))
        keys, cr = next_gather()
        g, ge, ud, hh, got = ffn_up(hb, wg_i, wu_i, conv_w_full[i], ffn_conv_b[i].reshape(1, Fd), B, S, carry=cr)
        land(keys, got)
        keys, cr = next_gather()
        a2, cur, curb, got = matmul_ln(hh, wd_i, NN, h, ln2_g[i], ln2_b[i], alpha, "ffn_down_ln", carry=cr)
        land(keys, got)
        sv.update(a1=a1, hb=hb, g=g, ge=ge, ud=ud, hh=hh, a2=a2)
        saved.append(sv)
        if i == nA - 1:
            x1T3 = _xT3(curb, B, S)
            wkv = weight(("wkv",))
            keys, cr = next_gather()
            kT, got = proj_T(wkv, x1T3, cosT, sinT, 0, True, 1.0, "k_proj", carry=cr)
            land(keys, got)
            keys, cr = next_gather()
            vT, got = proj_T(wkv, x1T3, cosT, sinT, G, False, 1.0, "v_proj", carry=cr)
            land(keys, got)

    dy, sq = loss_grad(cur, loss_target.reshape(T, D))

    small = {k: [None] * depth for k in ("ln1_g", "ln1_b", "ln2_g", "ln2_b", "conv_b", "conv_w")}
    dscale = [None] * nA
    dpw = [None] * nA
    dk_acc, dv_acc = [None] * G, [None] * G

    def blocks(a, rows):
        return a.reshape(N_DEV, rows, D)

    pending, landed = [], {}

    def next_carry():
        if not pending:
            return None, None
        key, parts = pending.pop(0)
        return key, Scatter(parts)

    dcur = dy
    for i in reversed(range(depth)):
        sv = saved[i]
        db2, db2b, small["ln2_g"][i], small["ln2_b"][i] = ln_bwd(dcur, sv["a2"], ln2_g[i])
        key, cr = next_carry()
        dg_, du_, dcb, dcw, got = ffn_mid_bwd(db2b, gathered[("wd", i)], sv["g"], sv["ge"], sv["ud"], conv_w_full[i], B, S, carry=cr)
        if cr is not None:
            landed[key] = got
        small["conv_b"][i] = jnp.sum(dcb, axis=0)
        small["conv_w"][i] = jnp.sum(dcw, axis=0)
        key, cr = next_carry()
        dwd, got = wgrad_rows(sv["hh"], db2b, "wgrad_down", carry=cr)
        if cr is not None:
            landed[key] = got
        dwg, landed[("down", i)] = wgrad_rows(dg_, sv["hb"], "wgrad_gate", carry=Scatter([blocks(dwd, Fs)]))
        dwu, landed[("gate", i)] = wgrad_rows(du_, sv["hb"], "wgrad_up", carry=Scatter([blocks(dwg, Fs)]))
        dh, landed[("up", i)] = ffn_dx(dg_, du_, gathered[("wg", i)], gathered[("wu", i)], db2, alpha, carry=Scatter([blocks(dwu, Fs)]))
        da1, da1b, small["ln1_g"][i], small["ln1_b"][i] = ln_bwd(dh, sv["a1"], ln1_g[i])
        if i < nA:
            dcur, dsp, dpwp = pool_bwd(da1, sv["pooled"], PW[i], pool_scale_full[i], alpha, B, S)
            dscale[i] = jnp.sum(dsp, axis=0)
            dpw[i] = jnp.sum(dpwp, axis=0)
        else:
            j = i - nA
            doT = matmul_to_T(gathered[("wo", j)], da1b, "o_proj_bwd")
            dwo = wgrad_mixed(sv["oTb"], da1b, "wgrad_o")
            delta = attn_delta(doT, sv["oTf"])
            dq_tok, dwq = [], []
            for gi, d in enumerate(DILATIONS):
                dq_g, dk_acc[gi], dv_acc[gi] = attn_bwd(
                    sv["qT"], kT, vT, _perm(doT, B, S, d), _perm(sv["lse_tot"], B, S, d), _perm(delta, B, S, d),
                    cosT, sinT, bias, gi, HEAD_DIM ** -0.5, B, S, dk_prev=dk_acc[gi], dv_prev=dv_acc[gi])
                dwq.append(wgrad_T(dq_g, sv["xT3"], gi, "wgrad_q"))
                dq_tok.append(_unperm(dq_g, B, S, d))
            dwq = jnp.concatenate(dwq, axis=0)
            dcur = dx_from_T(dq_tok, gathered[("wq", j)], da1, alpha, "q_proj_bwd", 512)
            if j == 0:
                dkv = [a.astype(BF16) for a in dk_acc + dv_acc]
                dwkv = jnp.concatenate([wgrad_T(a, x1T3, gi % G, "wgrad_kv") for gi, a in enumerate(dkv)], axis=0)
                dkv_tok = [_unperm(a, B, S, DILATIONS[gi % G]) for gi, a in enumerate(dkv)]
                dcur = dx_from_T(dkv_tok, gathered[("wkv",)], dcur, 1.0, "kv_proj_bwd", 256)
                pending.append((("kv",), [blocks(dwkv, kvs)]))
            pending.append((("attn", j), [blocks(dwq, qs), blocks(dwo, os_)]))
    grad_x = dcur.reshape(B, S, D)

    dpw_all = jnp.stack(dpw).reshape(nA, PG, N_DEV, Cg // N_DEV, Cg).transpose(2, 0, 1, 3, 4).reshape(N_DEV, pool_rows, D)
    tail_keys = [k for k, _ in pending] + [("pool",)]
    tail_parts = [parts for _, parts in pending] + [[dpw_all.astype(BF16)]]
    tail_rows = [sum(p.shape[1] for p in parts) for parts in tail_parts]
    tail = scatter_partials([p for parts in tail_parts for p in parts], "scatter_tail")
    for t, key in enumerate(tail_keys):
        lo = sum(tail_rows[:t])
        landed[key] = tail[:, lo:lo + tail_rows[t]]

    def reduced(key):
        return sum_slots(landed[key], "sum_" + "_".join(str(k) for k in key))

    g_attn = [reduced(("attn", j)) for j in range(nB)]
    g_w_q = jnp.swapaxes(jnp.stack([a[:qs] for a in g_attn]), 1, 2)
    g_w_o = jnp.stack([a[qs:] for a in g_attn])
    g_w_kv = reduced(("kv",)).T
    g_gate = jnp.swapaxes(jnp.stack([reduced(("gate", i)) for i in range(depth)]), 1, 2)
    g_up = jnp.swapaxes(jnp.stack([reduced(("up", i)) for i in range(depth)]), 1, 2)
    g_down = jnp.stack([reduced(("down", i)) for i in range(depth)])
    g_pool_w = reduced(("pool",)).reshape(pool_w.shape)

    def rows_of(a):
        a = a.reshape(-1)
        n = -(-a.size // D) * D
        return jnp.pad(a, (0, n - a.size)).reshape(-1, D)

    sm_parts = [rows_of(jnp.concatenate(small[k], axis=0)) for k in ("ln1_g", "ln1_b", "ln2_g", "ln2_b")]
    sm_parts += [rows_of(jnp.stack(small["conv_b"])), rows_of(jnp.stack(small["conv_w"])), rows_of(jnp.stack(dscale)), sq]
    sm_sizes = [p.shape[0] for p in sm_parts]
    sm_all = jnp.concatenate(sm_parts, axis=0)
    pad_rows = -(-sm_all.shape[0] // 8) * 8 - sm_all.shape[0]
    sm_all = jnp.pad(sm_all, ((0, pad_rows), (0, 0)))
    sm_sum = sum_slots(all_gather_blocks(sm_all, "gather_small_grads", in_vmem=True), "sum_small_grads")
    sm_offs = [sum(sm_sizes[:i]) for i in range(len(sm_sizes))]

    def sm_take(i, shape):
        n = math.prod(shape)
        return sm_sum[sm_offs[i]:sm_offs[i] + sm_sizes[i]].reshape(-1)[:n].reshape(shape)

    g_ln1_g, g_ln1_b = sm_take(0, (depth, D)), sm_take(1, (depth, D))
    g_ln2_g, g_ln2_b = sm_take(2, (depth, D)), sm_take(3, (depth, D))
    g_conv_b = sm_take(4, (depth, Fd))
    g_conv_w = lax.dynamic_slice_in_dim(sm_take(5, (depth, 3, Fd)), me * Fs, Fs, axis=2)
    g_pool_scale = lax.dynamic_slice_in_dim(sm_take(6, (nA, D)), me * (D // N_DEV), D // N_DEV, axis=1)
    loss = (0.5 / D) * jnp.sum(sm_take(7, (D,)))

    def v2(a):
        return a.reshape(-1, a.shape[-1])

    names = ["pool_w", "pool_scale", "w_q", "w_kv", "w_o", "ffn_w_gate", "ffn_w_up", "ffn_conv_w", "ffn_conv_b",
             "ffn_w_down", "ln1_g", "ln1_b", "ln2_g", "ln2_b"]
    ws = [pool_w, pool_scale, w_q, w_kv, w_o, ffn_w_gate, ffn_w_up, ffn_conv_w, ffn_conv_b, ffn_w_down, ln1_g, ln1_b, ln2_g, ln2_b]
    ms = [m_pool_w, m_pool_scale, m_w_q, m_w_kv, m_w_o, m_ffn_w_gate, m_ffn_w_up, m_ffn_conv_w, m_ffn_conv_b, m_ffn_w_down, m_ln1_g, m_ln1_b, m_ln2_g, m_ln2_b]
    vs = [v_pool_w, v_pool_scale, v_w_q, v_w_kv, v_w_o, v_ffn_w_gate, v_ffn_w_up, v_ffn_conv_w, v_ffn_conv_b, v_ffn_w_down, v_ln1_g, v_ln1_b, v_ln2_g, v_ln2_b]
    gs = [g_pool_w, g_pool_scale, g_w_q, g_w_kv, g_w_o, g_gate, g_up, g_conv_w, g_conv_b, g_down, g_ln1_g, g_ln1_b, g_ln2_g, g_ln2_b]
    deltas, new_ms, new_vs = [], [], []
    for nm, w, gr, m_, v_ in zip(names, ws, gs, ms, vs):
        d_, nm_, nv_ = adamw(v2(w), v2(gr), v2(m_), v2(v_), "adamw_" + nm)
        deltas.append(d_.reshape(w.shape))
        new_ms.append(nm_.reshape(w.shape))
        new_vs.append(nv_.reshape(w.shape))

    return (loss, grad_x, *gs, *deltas, *new_ms, *new_vs)
```

```python
import functools
import math

import jax
import jax.numpy as jnp
from jax import lax
from jax.experimental import pallas as pl
from jax.experimental.pallas import tpu as pltpu

F32 = jnp.float32
BF16 = jnp.bfloat16
SDS = jax.ShapeDtypeStruct
MESH = pl.DeviceIdType.MESH

N_DEV = 8
HEAD_DIM = 64
BLK = 128
DILATIONS = (1, 4, 16)
POOL_WINDOWS = (2, 4, 8, 16)
ROPE_THETA = 10000.0
LN_EPS = 1e-5
NEG = -1e30
V7X_VMEM_LIMIT = 56 * 1024 * 1024

ADAM_LR, ADAM_B1, ADAM_B2, ADAM_EPS, ADAM_WD, ADAM_STEP = 0.001, 0.9, 0.999, 1e-08, 0.01, 10

NN = (((1,), (0,)), ((), ()))
NT = (((1,), (1,)), ((), ()))
TN = (((0,), (0,)), ((), ()))


def _cp(sem=None):
    kw = dict(vmem_limit_bytes=V7X_VMEM_LIMIT)
    if sem is not None:
        kw["dimension_semantics"] = sem
    return pltpu.CompilerParams(**kw)


def _dot(a, b, dims=NN):
    return lax.dot_general(a, b, dims, preferred_element_type=F32)


def _tile(n, target, mult):
    best = None
    for t in range(mult, min(n, target) + 1, mult):
        if n % t == 0:
            best = t
    return best if best is not None else n


def _mesh_pos():
    return lax.axis_index("x"), lax.axis_index("y"), lax.axis_index("c")


def all_gather_blocks(xl, name, in_vmem):
    R, C = xl.shape
    space = pltpu.VMEM if in_vmem else pl.ANY

    def body(x_ref, out_ref, send_sems, recv_sems, local_sem):
        x, y, c = _mesh_pos()
        me, sibling = (x, y, c), (x, y, 1 - c)
        chips = [(1 - x, y), (x, 1 - y), (1 - x, 1 - y)]

        def slot(px, py, pc):
            return out_ref.at[4 * px + 2 * py + pc]

        def copy(k, block, to, src=None):
            return pltpu.make_async_remote_copy(
                src_ref=slot(*block) if src is None else src, dst_ref=slot(*block),
                send_sem=send_sems.at[k], recv_sem=recv_sems.at[k], device_id=to, device_id_type=MESH)

        mine = pltpu.make_async_copy(x_ref, slot(*me), local_sem)
        mine.start()
        first = [copy(0, me, sibling, src=x_ref)]
        first += [copy(1 + j, me, (*chip, c), src=x_ref) for j, chip in enumerate(chips)]
        for cp in first:
            cp.start()
        passed = [copy(4 + j, (*chip, c), sibling) for j, chip in enumerate(chips)]
        for j, chip in enumerate(chips):
            copy(1 + j, (*chip, c), me).wait_recv()
            passed[j].start()
        copy(0, sibling, me).wait_recv()
        for j, chip in enumerate(chips):
            copy(4 + j, (*chip, 1 - c), me).wait_recv()
        for cp in first + passed:
            cp.wait_send()
        mine.wait()

    return pl.pallas_call(
        body, name=name,
        out_shape=SDS((N_DEV, R, C), xl.dtype),
        in_specs=[pl.BlockSpec(memory_space=space)],
        out_specs=pl.BlockSpec(memory_space=space),
        scratch_shapes=[pltpu.SemaphoreType.DMA((7,)), pltpu.SemaphoreType.DMA((7,)), pltpu.SemaphoreType.DMA],
        compiler_params=_cp(),
    )(xl)


def _peers():
    x, y, c = _mesh_pos()
    peers = []
    for r in range(1, N_DEV):
        peers.append((1 - x if (r & 4) else x, 1 - y if (r & 2) else y, 1 - c if (r & 1) else c))
    return 4 * x + 2 * y + c, peers


class Gather:
    def __init__(self, parts):
        self.parts = list(parts)
        n = len(self.parts)
        self.out_shapes = [SDS((N_DEV,) + p.shape, p.dtype) for p in self.parts]
        self.scratch = [pltpu.SemaphoreType.DMA((7 * n,)), pltpu.SemaphoreType.DMA((7 * n,)), pltpu.SemaphoreType.DMA((n,))]

    def start(self, part_refs, out_refs, send_sems, recv_sems, local_sems):
        me_lin, peers = _peers()
        n = len(self.parts)
        for i in range(n):
            pltpu.make_async_copy(part_refs[i], out_refs[i].at[me_lin], local_sems.at[i]).start()
        for k, peer in enumerate(peers):
            for i in range(n):
                pltpu.make_async_remote_copy(
                    src_ref=part_refs[i], dst_ref=out_refs[i].at[me_lin],
                    send_sem=send_sems.at[k * n + i], recv_sem=recv_sems.at[k * n + i],
                    device_id=peer, device_id_type=MESH).start()

    def wait(self, out_refs, send_sems, recv_sems, local_sems):
        me_lin, peers = _peers()
        n = len(self.parts)
        for k, (px, py, pc) in enumerate(peers):
            p_lin = 4 * px + 2 * py + pc
            for i in range(n):
                arrival = pltpu.make_async_remote_copy(
                    src_ref=out_refs[i].at[p_lin], dst_ref=out_refs[i].at[p_lin],
                    send_sem=send_sems.at[k * n + i], recv_sem=recv_sems.at[k * n + i],
                    device_id=(px, py, pc), device_id_type=MESH)
                arrival.wait_recv()
                arrival.wait_send()
        for i in range(n):
            pltpu.make_async_copy(out_refs[i].at[me_lin], out_refs[i].at[me_lin], local_sems.at[i]).wait()


class Scatter:
    def __init__(self, parts):
        self.parts = list(parts)
        self.rows = [p.shape[1] for p in parts]
        self.offs = [sum(self.rows[:i]) for i in range(len(self.rows))]
        self.out_shapes = [SDS((N_DEV, sum(self.rows), parts[0].shape[2]), parts[0].dtype)]
        self.scratch = [pltpu.SemaphoreType.DMA((7,)), pltpu.SemaphoreType.DMA((7,)), pltpu.SemaphoreType.DMA]

    def start(self, part_refs, out_refs, send_sems, recv_sems, local_sem):
        out_ref = out_refs[0]
        me_lin, peers = _peers()
        for i, (off, r) in enumerate(zip(self.offs, self.rows)):
            pltpu.make_async_copy(part_refs[i].at[me_lin], out_ref.at[me_lin, pl.ds(off, r)], local_sem).start()
        for k, (px, py, pc) in enumerate(peers):
            p_lin = 4 * px + 2 * py + pc
            for i, (off, r) in enumerate(zip(self.offs, self.rows)):
                pltpu.make_async_remote_copy(
                    src_ref=part_refs[i].at[p_lin], dst_ref=out_ref.at[me_lin, pl.ds(off, r)],
                    send_sem=send_sems.at[k], recv_sem=recv_sems.at[k],
                    device_id=(px, py, pc), device_id_type=MESH).start()

    def wait(self, out_refs, send_sems, recv_sems, local_sem):
        out_ref = out_refs[0]
        me_lin, peers = _peers()
        for k, (px, py, pc) in enumerate(peers):
            p_lin = 4 * px + 2 * py + pc
            whole = pltpu.make_async_remote_copy(
                src_ref=out_ref.at[p_lin], dst_ref=out_ref.at[p_lin],
                send_sem=send_sems.at[k], recv_sem=recv_sems.at[k],
                device_id=(px, py, pc), device_id_type=MESH)
            whole.wait_recv()
            whole.wait_send()
        pltpu.make_async_copy(out_ref.at[me_lin], out_ref.at[me_lin], local_sem).wait()


def scatter_partials(parts, name):
    sc = Scatter(parts)
    n = len(parts)

    def body(*refs):
        sc.start(refs[:n], refs[n:n + 1], *refs[n + 1:])
        sc.wait(refs[n:n + 1], *refs[n + 1:])

    return pl.pallas_call(
        body, name=name, out_shape=sc.out_shapes[0],
        in_specs=[pl.BlockSpec(memory_space=pl.ANY)] * n, out_specs=pl.BlockSpec(memory_space=pl.ANY),
        scratch_shapes=sc.scratch, compiler_params=_cp(),
    )(*parts)


def _call(body, name, grid, in_specs, out_specs, out_shape, args, scratch=(), sem=None, carry=None):
    in_specs, out_specs, out_shape, scratch = list(in_specs), list(out_specs), list(out_shape), list(scratch)
    if carry is None:
        outs = pl.pallas_call(body, name=name, grid=grid, in_specs=in_specs, out_specs=out_specs, out_shape=out_shape,
                              scratch_shapes=scratch, compiler_params=_cp(sem))(*args)
        return list(outs), None
    n_in, n_out, n_scr, n_c, n_co = len(in_specs), len(out_specs), len(scratch), len(carry.parts), len(carry.out_shapes)
    last = [g - 1 for g in grid]

    def carried(*refs):
        ins, c_ins = refs[:n_in], refs[n_in:n_in + n_c]
        o0 = n_in + n_c
        outs, c_out = refs[o0:o0 + n_out], refs[o0 + n_out:o0 + n_out + n_co]
        s0 = o0 + n_out + n_co
        scr, c_scr = refs[s0:s0 + n_scr], refs[s0 + n_scr:]
        ids = [pl.program_id(a) for a in range(len(grid))]
        is_first = functools.reduce(jnp.logical_and, [i == 0 for i in ids])
        is_last = functools.reduce(jnp.logical_and, [i == l for i, l in zip(ids, last)])

        @pl.when(is_first)
        def _():
            carry.start(c_ins, c_out, *c_scr)

        body(*ins, *outs, *scr)

        @pl.when(is_last)
        def _():
            carry.wait(c_out, *c_scr)

    hbm = pl.BlockSpec(memory_space=pl.ANY)
    outs = pl.pallas_call(
        carried, name=name + "_carry", grid=grid, in_specs=in_specs + [hbm] * n_c, out_specs=out_specs + [hbm] * n_co,
        out_shape=out_shape + carry.out_shapes, scratch_shapes=scratch + carry.scratch,
        compiler_params=_cp(sem if sem is not None else ("arbitrary",) * len(grid)),
    )(*args, *carry.parts)
    return list(outs[:n_out]), list(outs[n_out:])


def sum_slots(slots, name, out_dtype=F32):
    _, R, C = slots.shape
    tr = _tile(R, 512, 16)

    def body(s_ref, o_ref):
        acc = s_ref[0].astype(F32)
        for s in range(1, N_DEV):
            acc = acc + s_ref[s].astype(F32)
        o_ref[...] = acc.astype(out_dtype)

    return pl.pallas_call(
        body, name=name, grid=(R // tr,),
        in_specs=[pl.BlockSpec((N_DEV, tr, C), lambda i: (0, i, 0))],
        out_specs=pl.BlockSpec((tr, C), lambda i: (i, 0)),
        out_shape=SDS((R, C), out_dtype), compiler_params=_cp(),
    )(slots)


def add_ln(x, mix, g, b, alpha):
    T, D = x.shape
    tm = _tile(T, 512, 16)

    def body(x_ref, m_ref, g_ref, b_ref, a_ref, y_ref, yb_ref):
        a = alpha * x_ref[...] + m_ref[...]
        mu = jnp.mean(a, axis=-1, keepdims=True)
        xc = a - mu
        var = jnp.mean(xc * xc, axis=-1, keepdims=True)
        y = xc * lax.rsqrt(var + LN_EPS) * g_ref[...] + b_ref[...]
        a_ref[...] = a
        y_ref[...] = y
        yb_ref[...] = y.astype(BF16)

    row = pl.BlockSpec((tm, D), lambda i: (i, 0))
    vec = pl.BlockSpec((1, D), lambda i: (0, 0))
    return pl.pallas_call(
        body, name="add_ln", grid=(T // tm,),
        in_specs=[row, row, vec, vec], out_specs=[row, row, row],
        out_shape=[SDS((T, D), F32), SDS((T, D), F32), SDS((T, D), BF16)], compiler_params=_cp(),
    )(x, mix, g.reshape(1, D), b.reshape(1, D))


def ln_bwd(dy, a, g):
    T, D = a.shape
    tm = _tile(T, 512, 16)

    def body(dy_ref, a_ref, g_ref, da_ref, dab_ref, dg_ref, db_ref):
        @pl.when(pl.program_id(0) == 0)
        def _():
            dg_ref[...] = jnp.zeros_like(dg_ref)
            db_ref[...] = jnp.zeros_like(db_ref)

        av = a_ref[...]
        mu = jnp.mean(av, axis=-1, keepdims=True)
        xc = av - mu
        var = jnp.mean(xc * xc, axis=-1, keepdims=True)
        r = lax.rsqrt(var + LN_EPS)
        xh = xc * r
        dyv = dy_ref[...]
        dxh = dyv * g_ref[...]
        m1 = jnp.mean(dxh, axis=-1, keepdims=True)
        m2 = jnp.mean(dxh * xh, axis=-1, keepdims=True)
        da = r * (dxh - m1 - xh * m2)
        da_ref[...] = da
        dab_ref[...] = da.astype(BF16)
        dg_ref[...] += jnp.sum(dyv * xh, axis=0, keepdims=True)
        db_ref[...] += jnp.sum(dyv, axis=0, keepdims=True)

    row = pl.BlockSpec((tm, D), lambda i: (i, 0))
    vec = pl.BlockSpec((1, D), lambda i: (0, 0))
    return pl.pallas_call(
        body, name="ln_bwd", grid=(T // tm,),
        in_specs=[row, row, vec], out_specs=[row, row, vec, vec],
        out_shape=[SDS((T, D), F32), SDS((T, D), BF16), SDS((1, D), F32), SDS((1, D), F32)],
        compiler_params=_cp(("arbitrary",)),
    )(dy, a, g.reshape(1, D))


def loss_grad(y, tgt):
    T, D = y.shape
    tm = _tile(T, 512, 16)

    def body(y_ref, t_ref, dy_ref, sq_ref):
        @pl.when(pl.program_id(0) == 0)
        def _():
            sq_ref[...] = jnp.zeros_like(sq_ref)

        e = y_ref[...] - t_ref[...]
        dy_ref[...] = e / float(D)
        sq_ref[...] += jnp.sum(e * e, axis=0, keepdims=True)

    row = pl.BlockSpec((tm, D), lambda i: (i, 0))
    vec = pl.BlockSpec((1, D), lambda i: (0, 0))
    return pl.pallas_call(
        body, name="loss_grad", grid=(T // tm,),
        in_specs=[row, row], out_specs=[row, vec],
        out_shape=[SDS((T, D), F32), SDS((1, D), F32)], compiler_params=_cp(("arbitrary",)),
    )(y, tgt)


def matmul_ln(a, w, dims, res, g, b, alpha, name, carry=None):
    if dims == TN:
        K, T = a.shape
    else:
        T, K = a.shape
    D = w.shape[1]
    tm = _tile(T, 512, 128 if dims == TN else 16)

    def body(a_ref, w_ref, r_ref, g_ref, b_ref, p_ref, y_ref, yb_ref):
        pre = alpha * r_ref[...] + _dot(a_ref[...], w_ref[...], dims)
        mu = jnp.mean(pre, axis=-1, keepdims=True)
        xc = pre - mu
        var = jnp.mean(xc * xc, axis=-1, keepdims=True)
        y = xc * lax.rsqrt(var + LN_EPS) * g_ref[...] + b_ref[...]
        p_ref[...] = pre
        y_ref[...] = y
        yb_ref[...] = y.astype(BF16)

    a_spec = pl.BlockSpec((K, tm), lambda i: (0, i)) if dims == TN else pl.BlockSpec((tm, K), lambda i: (i, 0))
    row = pl.BlockSpec((tm, D), lambda i: (i, 0))
    vec = pl.BlockSpec((1, D), lambda i: (0, 0))
    outs, landed = _call(
        body, name, (T // tm,), [a_spec, pl.BlockSpec(w.shape, lambda i: (0, 0)), row, vec, vec], [row, row, row],
        [SDS((T, D), F32), SDS((T, D), F32), SDS((T, D), BF16)], (a, w, res, g.reshape(1, D), b.reshape(1, D)), carry=carry)
    return (*outs, landed)


def wgrad_rows(a, b, name, carry=None):
    T, M = a.shape
    N = b.shape[1]
    tt = _tile(T, 512, 16)
    tmm = _tile(M, 1536, 128)
    nt = T // tt

    def body(a_ref, b_ref, o_ref, acc_ref):
        t = pl.program_id(1)

        @pl.when(t == 0)
        def _():
            acc_ref[...] = jnp.zeros_like(acc_ref)

        acc_ref[...] += _dot(a_ref[...], b_ref[...], TN)

        @pl.when(t == nt - 1)
        def _():
            o_ref[...] = acc_ref[...].astype(BF16)

    outs, landed = _call(
        body, name, (M // tmm, nt),
        [pl.BlockSpec((tt, tmm), lambda i, t: (t, i)), pl.BlockSpec((tt, N), lambda i, t: (t, 0))],
        [pl.BlockSpec((tmm, N), lambda i, t: (i, 0))], [SDS((M, N), BF16)], (a, b),
        scratch=[pltpu.VMEM((tmm, N), F32)], sem=("arbitrary", "arbitrary"), carry=carry)
    return outs[0], (landed[0] if landed else None)


def _shift_down(x, k, rows):
    return jnp.where(rows >= k, pltpu.roll(x, k, 0), 0.0)


def _shift_up(x, k, rows):
    n = x.shape[0]
    return jnp.where(rows < n - k, pltpu.roll(x, n - k, 0), 0.0)


def _pick(g, vals):
    out = vals[-1]
    for k in range(len(vals) - 2, -1, -1):
        out = jnp.where(g == k, vals[k], out)
    return out


def pool_fwd(x, pw, scale, B, S):
    T, D = x.shape
    G = len(POOL_WINDOWS)
    Cg = D // G

    def body(x_ref, w_ref, s_ref, mix_ref, pooled_ref):
        g = pl.program_id(1)
        xv = x_ref[...]
        rows = lax.broadcasted_iota(jnp.int32, xv.shape, 0)
        sums, cur, k = [], xv, 1
        for _ in POOL_WINDOWS:
            cur = cur + _shift_down(cur, k, rows)
            sums.append(cur)
            k *= 2
        win = 2 * lax.shift_left(jnp.int32(1), g)
        total = _pick(g, sums)
        count = jnp.minimum(rows + 1, win).astype(F32)
        pooled = total / count - xv
        pb = pooled.astype(BF16)
        pooled_ref[...] = pb
        mix_ref[...] = _dot(pb, w_ref[0]) * s_ref[...]

    blk = pl.BlockSpec((S, Cg), lambda b, g: (b, g))
    return pl.pallas_call(
        body, name="pool_fwd", grid=(B, G),
        in_specs=[blk, pl.BlockSpec((1, Cg, Cg), lambda b, g: (g, 0, 0)), pl.BlockSpec((1, Cg), lambda b, g: (0, g))],
        out_specs=[blk, blk],
        out_shape=[SDS((T, D), F32), SDS((T, D), BF16)], compiler_params=_cp(),
    )(x, pw, scale)


def pool_bwd(dmix, pooled, pw, scale, alpha, B, S):
    T, D = dmix.shape
    G = len(POOL_WINDOWS)
    Cg = D // G

    def body(d_ref, p_ref, w_ref, s_ref, dx_ref, ds_ref, dw_ref):
        g = pl.program_id(1)
        dm = d_ref[...]
        pb = p_ref[...]
        w = w_ref[0]
        ypre = _dot(pb, w)
        ds_ref[0] = jnp.sum(dm * ypre, axis=0, keepdims=True)
        dy = (dm * s_ref[...]).astype(BF16)
        dpool = _dot(dy, w, NT)
        dw_ref[0, 0] = _dot(pb, dy, TN)
        rows = lax.broadcasted_iota(jnp.int32, dm.shape, 0)
        win = 2 * lax.shift_left(jnp.int32(1), g)
        count = jnp.minimum(rows + 1, win).astype(F32)
        cur, k, sums = dpool / count, 1, []
        for _ in POOL_WINDOWS:
            cur = cur + _shift_up(cur, k, rows)
            sums.append(cur)
            k *= 2
        dx_ref[...] = alpha * dm + _pick(g, sums) - dpool

    blk = pl.BlockSpec((S, Cg), lambda b, g: (b, g))
    return pl.pallas_call(
        body, name="pool_bwd", grid=(B, G),
        in_specs=[blk, blk, pl.BlockSpec((1, Cg, Cg), lambda b, g: (g, 0, 0)), pl.BlockSpec((1, Cg), lambda b, g: (0, g))],
        out_specs=[blk, pl.BlockSpec((1, 1, Cg), lambda b, g: (b, 0, g)),
                   pl.BlockSpec((1, 1, Cg, Cg), lambda b, g: (b, g, 0, 0))],
        out_shape=[SDS((T, D), F32), SDS((B, 1, D), F32), SDS((B, G, Cg, Cg), F32)], compiler_params=_cp(),
    )(dmix, pooled, pw, scale)


_GELU_K = math.sqrt(2.0 / math.pi)
_GELU_C = 0.044715


def _conv(g, cw, cb, rows):
    return cb + cw[0:1] * _shift_down(g, 2, rows) + cw[1:2] * _shift_down(g, 1, rows) + cw[2:3] * g


def ffn_up(hb, wgT, wuT, cw, cb, B, S, carry=None):
    T, D = hb.shape
    Fd = wgT.shape[0]
    fn = _tile(Fd, 256, 128)

    def body(h_ref, wg_ref, wu_ref, cw_ref, cb_ref, g_ref, ge_ref, ud_ref, hh_ref):
        h = h_ref[...]
        g = _dot(h, wg_ref[...], NT)
        u = _dot(h, wu_ref[...], NT)
        rows = lax.broadcasted_iota(jnp.int32, g.shape, 0)
        c = _conv(g, cw_ref[...], cb_ref[...], rows)
        c2 = c * c
        th = jnp.tanh(_GELU_K * (c + _GELU_C * (c2 * c)))
        cdf = 0.5 * (1.0 + th)
        ge = c * cdf
        dgelu = cdf + c * (0.5 * (1.0 - th * th) * (_GELU_K * (1.0 + 3.0 * _GELU_C * c2)))
        g_ref[...] = g.astype(BF16)
        ge_ref[...] = ge.astype(BF16)
        ud_ref[...] = (u * dgelu).astype(BF16)
        hh_ref[...] = (ge * u).astype(BF16)

    hspec = pl.BlockSpec((S, D), lambda b, j: (b, 0))
    wspec = pl.BlockSpec((fn, D), lambda b, j: (j, 0))
    ospec = pl.BlockSpec((S, fn), lambda b, j: (b, j))
    outs, landed = _call(
        body, "ffn_up", (B, Fd // fn),
        [hspec, wspec, wspec, pl.BlockSpec((3, fn), lambda b, j: (0, j)), pl.BlockSpec((1, fn), lambda b, j: (0, j))],
        [ospec] * 4, [SDS((T, Fd), BF16)] * 4, (hb, wgT, wuT, cw, cb), carry=carry)
    return (*outs, landed)


def ffn_mid_bwd(dfb, wd, g, ge, ud, cw, B, S, carry=None):
    T, D = dfb.shape
    Fd = wd.shape[0]
    fn = _tile(Fd, 256, 128)

    def body(df_ref, wd_ref, g_ref, ge_ref, ud_ref, cw_ref, dg_ref, du_ref, dcb_ref, dcw_ref):
        dhh = _dot(df_ref[...], wd_ref[...], NT)
        gv = g_ref[...].astype(F32)
        cw = cw_ref[...]
        rows = lax.broadcasted_iota(jnp.int32, gv.shape, 0)
        g1 = _shift_down(gv, 1, rows)
        g2 = _shift_down(gv, 2, rows)
        du_ref[...] = (dhh * ge_ref[...].astype(F32)).astype(BF16)
        dc = dhh * ud_ref[...].astype(F32)
        dcb_ref[0] = jnp.sum(dc, axis=0, keepdims=True)
        dcw_ref[0] = jnp.concatenate(
            [jnp.sum(dc * g2, axis=0, keepdims=True), jnp.sum(dc * g1, axis=0, keepdims=True),
             jnp.sum(dc * gv, axis=0, keepdims=True)], axis=0)
        dg = cw[2:3] * dc + cw[1:2] * _shift_up(dc, 1, rows) + cw[0:1] * _shift_up(dc, 2, rows)
        dg_ref[...] = dg.astype(BF16)

    tspec = pl.BlockSpec((S, fn), lambda b, j: (b, j))
    outs, landed = _call(
        body, "ffn_mid_bwd", (B, Fd // fn),
        [pl.BlockSpec((S, D), lambda b, j: (b, 0)), pl.BlockSpec((fn, D), lambda b, j: (j, 0)), tspec, tspec, tspec,
         pl.BlockSpec((3, fn), lambda b, j: (0, j))],
        [tspec, tspec, pl.BlockSpec((1, 1, fn), lambda b, j: (b, 0, j)), pl.BlockSpec((1, 3, fn), lambda b, j: (b, 0, j))],
        [SDS((T, Fd), BF16), SDS((T, Fd), BF16), SDS((B, 1, Fd), F32), SDS((B, 3, Fd), F32)],
        (dfb, wd, g, ge, ud, cw), carry=carry)
    return (*outs, landed[0] if landed else None)


def ffn_dx(dg, du, wgT, wuT, res, alpha, carry=None):
    T, Fd = dg.shape
    D = wgT.shape[1]
    tm = _tile(T, 256, 16)

    def body(dg_ref, du_ref, wg_ref, wu_ref, r_ref, o_ref):
        o_ref[...] = alpha * r_ref[...] + _dot(dg_ref[...], wg_ref[...]) + _dot(du_ref[...], wu_ref[...])

    a_spec = pl.BlockSpec((tm, Fd), lambda i: (i, 0))
    w_spec = pl.BlockSpec((Fd, D), lambda i: (0, 0))
    o_spec = pl.BlockSpec((tm, D), lambda i: (i, 0))
    outs, landed = _call(body, "ffn_dx", (T // tm,), [a_spec, a_spec, w_spec, w_spec, o_spec], [o_spec],
                         [SDS((T, D), F32)], (dg, du, wgT, wuT, res), carry=carry)
    return outs[0], (landed[0] if landed else None)


def _partner_all(x):
    n = x.shape[0]
    r = lax.broadcasted_iota(jnp.int32, x.shape, 0)
    return jnp.where((r % HEAD_DIM) < HEAD_DIM // 2, pltpu.roll(x, n - HEAD_DIM // 2, 0), pltpu.roll(x, HEAD_DIM // 2, 0))


def _attn_bias():
    kj = lax.broadcasted_iota(jnp.int32, (2 * BLK, BLK), 0)
    qi = lax.broadcasted_iota(jnp.int32, (2 * BLK, BLK), 1)
    ok = ((kj >= BLK) & (kj - BLK <= qi)) | ((kj < BLK) & (kj >= qi))
    return jnp.where(ok, 0.0, NEG).astype(F32)


def _has_prev(g, S):
    nb = S // (DILATIONS[g] * BLK)
    return [(n % nb) != 0 for n in range(S // BLK)]


def _win(ref, n, hp):
    lo = (n - 1) * BLK if hp else n * BLK
    return ref[0, :, lo:(n + 1) * BLK]


def attn_fwd(qT3, kT3, vT3, bias, g, B, S):
    _, D, T = qT3.shape
    H = D // HEAD_DIM
    nblk = S // BLK
    hp = _has_prev(g, S)

    def body(q_ref, k_ref, v_ref, b_ref, o_ref, l_ref, s_scr, p_scr, rl_scr):
        for n in range(nblk):
            lo = 0 if hp[n] else BLK
            s_scr[n, lo:, :] = _dot(_win(k_ref, n, hp[n]), q_ref[0, :, n * BLK:(n + 1) * BLK], TN)
        for n in range(nblk):
            lo = 0 if hp[n] else BLK
            sT = s_scr[n, lo:, :] + b_ref[lo:, :]
            m = jnp.max(sT, axis=0, keepdims=True)
            p = jnp.exp(sT - m)
            l = jnp.sum(p, axis=0, keepdims=True)
            p_scr[n, lo:, :] = p.astype(BF16)
            rl_scr[n:n + 1, :] = 1.0 / l
            l_ref[0, :, n * BLK:(n + 1) * BLK] = m + jnp.log(l)
        for n in range(nblk):
            lo = 0 if hp[n] else BLK
            o_ref[:, n * BLK:(n + 1) * BLK] = _dot(_win(v_ref, n, hp[n]), p_scr[n, lo:, :]) * rl_scr[n:n + 1, :]

    spec = pl.BlockSpec((1, HEAD_DIM, S), lambda b, h: (g, h, b))
    return pl.pallas_call(
        body, name=f"attn_fwd_g{g}", grid=(B, H),
        in_specs=[spec, spec, spec, pl.BlockSpec((2 * BLK, BLK), lambda b, h: (0, 0))],
        out_specs=[pl.BlockSpec((HEAD_DIM, S), lambda b, h: (h, b)), pl.BlockSpec((1, 1, S), lambda b, h: (h, 0, b))],
        out_shape=[SDS((D, T), F32), SDS((H, 1, T), F32)],
        scratch_shapes=[pltpu.VMEM((nblk, 2 * BLK, BLK), F32), pltpu.VMEM((nblk, 2 * BLK, BLK), BF16),
                        pltpu.VMEM((nblk, BLK), F32)],
        compiler_params=_cp(),
    )(qT3, kT3, vT3, bias)


def attn_bwd(qT3, kT3, vT3, doT3, lse, delta, cosT, sinT, bias, g, q_scale, B, S, dk_prev=None, dv_prev=None):
    _, D, T = qT3.shape
    H = D // HEAD_DIM
    nblk = S // BLK
    half = HEAD_DIM // 2
    hp = _has_prev(g, S)
    acc_in = dk_prev is not None
    kv_dtype = BF16 if acc_in else F32

    def body(*refs):
        q_ref, k_ref, v_ref, do_ref, l_ref, d_ref, c_ref, s_ref, b_ref = refs[:9]
        rest = refs[9:]
        if acc_in:
            dkp_ref, dvp_ref = rest[:2]
            rest = rest[2:]
        dq_ref, dk_ref, dv_ref, s_scr, dp_scr, p_scr, ds_scr = rest
        for n in range(nblk):
            lo = 0 if hp[n] else BLK
            blk = slice(n * BLK, (n + 1) * BLK)
            s_scr[n, lo:, :] = _dot(_win(k_ref, n, hp[n]), q_ref[0, :, blk], TN)
            dp_scr[n, lo:, :] = _dot(_win(v_ref, n, hp[n]), do_ref[0, :, blk], TN)
        for n in range(nblk):
            lo = 0 if hp[n] else BLK
            blk = slice(n * BLK, (n + 1) * BLK)
            pT = jnp.exp(s_scr[n, lo:, :] + b_ref[lo:, :] - l_ref[0, :, blk])
            p_scr[n, lo:, :] = pT.astype(BF16)
            ds_scr[n, lo:, :] = (pT * (dp_scr[n, lo:, :] - d_ref[0, :, blk])).astype(BF16)
        for j in range(nblk):
            blk = slice(j * BLK, (j + 1) * BLK)
            if j + 1 < nblk and hp[j + 1]:
                two = slice(j * BLK, (j + 2) * BLK)
                pj = jnp.concatenate([p_scr[j, BLK:, :], p_scr[j + 1, :BLK, :]], axis=1)
                dsj = jnp.concatenate([ds_scr[j, BLK:, :], ds_scr[j + 1, :BLK, :]], axis=1)
                dv = _dot(do_ref[0, :, two], pj, NT)
                dk = _dot(q_ref[0, :, two], dsj, NT)
            else:
                dv = _dot(do_ref[0, :, blk], p_scr[j, BLK:, :], NT)
                dk = _dot(q_ref[0, :, blk], ds_scr[j, BLK:, :], NT)
            dk = dk * c_ref[0, :, blk] - pltpu.roll(dk, half, 0) * s_ref[0, :, blk]
            if acc_in:
                dk = dk + dkp_ref[:, blk]
                dv = dv + dvp_ref[:, blk]
            dk_ref[:, blk] = dk.astype(kv_dtype)
            dv_ref[:, blk] = dv.astype(kv_dtype)
            lo = 0 if hp[j] else BLK
            dq = _dot(_win(k_ref, j, hp[j]), ds_scr[j, lo:, :])
            dq = dq * c_ref[0, :, blk] - pltpu.roll(dq, half, 0) * s_ref[0, :, blk]
            dq_ref[:, blk] = (dq * q_scale).astype(BF16)

    spec3 = pl.BlockSpec((1, HEAD_DIM, S), lambda b, h: (g, h, b))
    spec = pl.BlockSpec((HEAD_DIM, S), lambda b, h: (h, b))
    sspec = pl.BlockSpec((1, 1, S), lambda b, h: (h, 0, b))
    tab = pl.BlockSpec((1, HEAD_DIM, S), lambda b, h: (g, 0, 0))
    in_specs = [spec3, spec3, spec3, spec3, sspec, sspec, tab, tab, pl.BlockSpec((2 * BLK, BLK), lambda b, h: (0, 0))]
    args = [qT3, kT3, vT3, doT3, lse, delta, cosT, sinT, bias]
    if acc_in:
        in_specs += [spec, spec]
        args += [dk_prev, dv_prev]
    return pl.pallas_call(
        body, name=f"attn_bwd_g{g}" + ("_acc" if acc_in else ""), grid=(B, H),
        in_specs=in_specs, out_specs=[spec, spec, spec],
        out_shape=[SDS((D, T), BF16), SDS((D, T), kv_dtype), SDS((D, T), kv_dtype)],
        scratch_shapes=[pltpu.VMEM((nblk, 2 * BLK, BLK), F32), pltpu.VMEM((nblk, 2 * BLK, BLK), F32),
                        pltpu.VMEM((nblk, 2 * BLK, BLK), BF16), pltpu.VMEM((nblk, 2 * BLK, BLK), BF16)],
        compiler_params=_cp(),
    )(*args)


RT = 512


def _rows_view(a, g, B, S):
    d = DILATIONS[g]
    L = S // d
    Dx = a.shape[1]
    view = a.reshape(B, L, d * Dx)
    if L >= RT:
        n = L // RT
        return view, (1, RT, Dx), (lambda b, s: (b, s % n, s // n)), 1, RT
    return view, (1, L, (RT // L) * Dx), (lambda b, s: (b, 0, s)), RT // L, L


def proj_rows_T(w, blk_off, blk_step, x_rows, cosT, sinT, rope, scale, name, B, S, carry=None):
    T, K = x_rows.shape
    Dw = K
    H = Dw // HEAD_DIM
    nS = S // RT
    G = len(DILATIONS)
    views = [_rows_view(x_rows, g, B, S) for g in range(G)]

    def body(w_ref, x0, x1, x2, c_ref, s_ref, o_ref):
        gid = pl.program_id(0)
        for g, x_ref in enumerate((x0, x1, x2)):
            k, lt = views[g][3], views[g][4]

            @pl.when(gid == g)
            def _(x_ref=x_ref, k=k, lt=lt):
                for q in range(k):
                    acc = _dot(w_ref[...], x_ref[0, :, q * K:(q + 1) * K], NT)
                    if rope:
                        cos = jnp.tile(c_ref[0, :, q * lt:(q + 1) * lt], (H, 1))
                        sin = jnp.tile(s_ref[0, :, q * lt:(q + 1) * lt], (H, 1))
                        acc = acc * cos + _partner_all(acc) * sin
                    if scale != 1.0:
                        acc = acc * scale
                    o_ref[0, :, q * lt:(q + 1) * lt] = acc.astype(BF16)

    def x_spec(g):
        _, blk, imap, _, _ = views[g]
        return pl.BlockSpec(blk, lambda gg, b, s: tuple(jnp.where(gg == g, i, 0) for i in imap(b, s)))

    tab = pl.BlockSpec((1, HEAD_DIM, RT), lambda gg, b, s: (gg, 0, s))
    outs, landed = _call(
        body, name, (G, B, nS),
        [pl.BlockSpec((Dw, K), lambda gg, b, s: (blk_off + gg * blk_step, 0)), x_spec(0), x_spec(1), x_spec(2), tab, tab],
        [pl.BlockSpec((1, Dw, RT), lambda gg, b, s: (gg, 0, b * nS + s))], [SDS((G, Dw, T), BF16)],
        (w, views[0][0], views[1][0], views[2][0], cosT, sinT), carry=carry)
    return outs[0], landed


def rows_from_T(aTs, w, w_blks, res, alpha, g, B, S, name, ln=None):
    Dq, T = aTs[0].shape
    D = w.shape[1]
    n = len(aTs)
    nS = S // RT
    rview, blk, imap, k, lt = _rows_view(res, g, B, S)
    n_out = 3 if ln is not None else 1

    def body(*refs):
        a_refs, w_ref, r_ref = refs[:n], refs[n], refs[n + 1]
        rest = refs[n + 2:]
        if ln is not None:
            g_ref, b_ref = rest[:2]
            rest = rest[2:]
        for q in range(k):
            cols = slice(q * lt, (q + 1) * lt)
            dsl = slice(q * D, (q + 1) * D)
            acc = alpha * r_ref[0, :, dsl]
            for a_ref, wb in zip(a_refs, w_blks):
                acc = acc + _dot(a_ref[:, cols], w_ref[wb * Dq:(wb + 1) * Dq, :], TN)
            rest[0][0, :, dsl] = acc
            if ln is not None:
                mu = jnp.mean(acc, axis=-1, keepdims=True)
                xc = acc - mu
                var = jnp.mean(xc * xc, axis=-1, keepdims=True)
                y = xc * lax.rsqrt(var + LN_EPS) * g_ref[...] + b_ref[...]
                rest[1][0, :, dsl] = y
                rest[2][0, :, dsl] = y.astype(BF16)

    a_spec = pl.BlockSpec((Dq, RT), lambda b, s: (0, b * nS + s))
    row = pl.BlockSpec(blk, imap)
    vec = pl.BlockSpec((1, D), lambda b, s: (0, 0))
    in_specs = [a_spec] * n + [pl.BlockSpec(w.shape, lambda b, s: (0, 0)), row]
    args = list(aTs) + [w, rview]
    if ln is not None:
        in_specs += [vec, vec]
        args += [ln[0].reshape(1, D), ln[1].reshape(1, D)]
    outs = pl.pallas_call(
        body, name=name, grid=(B, nS), in_specs=in_specs, out_specs=[row] * n_out,
        out_shape=[SDS(rview.shape, F32)] * min(n_out, 2) + ([SDS(rview.shape, BF16)] if ln is not None else []),
        compiler_params=_cp(),
    )(*args)
    outs = [o.reshape(T, D) for o in outs]
    return outs if ln is not None else outs[0]


def wgrad_T_rows(aT, x_rows, g, B, S, name):
    M, T = aT.shape
    D = x_rows.shape[1]
    nS = S // RT
    xview, blk, imap, k, lt = _rows_view(x_rows, g, B, S)

    def body(a_ref, x_ref, o_ref, acc_ref):
        first = (pl.program_id(0) == 0) & (pl.program_id(1) == 0)
        last = (pl.program_id(0) == B - 1) & (pl.program_id(1) == nS - 1)

        @pl.when(first)
        def _():
            acc_ref[...] = jnp.zeros_like(acc_ref)

        acc = acc_ref[...]
        for q in range(k):
            acc = acc + _dot(a_ref[:, q * lt:(q + 1) * lt], x_ref[0, :, q * D:(q + 1) * D])
        acc_ref[...] = acc

        @pl.when(last)
        def _():
            o_ref[...] = acc_ref[...].astype(BF16)

    return pl.pallas_call(
        body, name=name, grid=(B, nS),
        in_specs=[pl.BlockSpec((M, RT), lambda b, s: (0, b * nS + s)), pl.BlockSpec(blk, imap)],
        out_specs=pl.BlockSpec((M, D), lambda b, s: (0, 0)),
        out_shape=SDS((M, D), BF16), scratch_shapes=[pltpu.VMEM((M, D), F32)],
        compiler_params=_cp(("arbitrary", "arbitrary")),
    )(aT, xview)


def attn_weights(lses):
    G = len(lses)
    shape = lses[0].shape

    def body(*refs):
        ls = [r[...] for r in refs[:G]]
        m = functools.reduce(jnp.maximum, ls)
        es = [jnp.exp(v - m) for v in ls]
        z = functools.reduce(lambda a, b: a + b, es)
        for i in range(G):
            refs[G + i][...] = es[i] / z
        refs[2 * G][...] = m + jnp.log(z)

    outs = pl.pallas_call(body, name="attn_weights", out_shape=[SDS(shape, F32)] * (G + 1), compiler_params=_cp())(*lses)
    return list(outs[:G]), outs[G]


def attn_scale(oT, w):
    D, T = oT.shape
    H = D // HEAD_DIM
    tn = _tile(T, 2048, 128)

    def body(o_ref, w_ref, r_ref):
        r_ref[...] = (o_ref[...] * w_ref[0]).astype(BF16)

    spec = pl.BlockSpec((HEAD_DIM, tn), lambda h, j: (h, j))
    return pl.pallas_call(
        body, name="attn_scale", grid=(H, T // tn), in_specs=[spec, pl.BlockSpec((1, 1, tn), lambda h, j: (h, 0, j))],
        out_specs=spec, out_shape=SDS((D, T), BF16), compiler_params=_cp(),
    )(oT, w)


def attn_delta_part(doT3, g, oT, w):
    D, T = oT.shape
    H = D // HEAD_DIM
    tn = _tile(T, 2048, 128)

    def body(d_ref, o_ref, w_ref, r_ref):
        r_ref[0] = w_ref[0] * jnp.sum(d_ref[0].astype(F32) * o_ref[...], axis=0, keepdims=True)

    sspec = pl.BlockSpec((1, 1, tn), lambda h, j: (h, 0, j))
    return pl.pallas_call(
        body, name="attn_delta_part", grid=(H, T // tn),
        in_specs=[pl.BlockSpec((1, HEAD_DIM, tn), lambda h, j: (g, h, j)), pl.BlockSpec((HEAD_DIM, tn), lambda h, j: (h, j)), sspec],
        out_specs=sspec, out_shape=SDS((H, 1, T), F32), compiler_params=_cp(),
    )(doT3, oT, w)


def adamw(w, g, m, v, name):
    R, C = w.shape
    tr = _tile(R, 512, 8)

    def body(w_ref, g_ref, m_ref, v_ref, d_ref, nm_ref, nv_ref):
        gv = g_ref[...]
        nm = ADAM_B1 * m_ref[...] + (1.0 - ADAM_B1) * gv
        nv = ADAM_B2 * v_ref[...] + (1.0 - ADAM_B2) * (gv * gv)
        m_hat = nm / (1.0 - ADAM_B1 ** ADAM_STEP)
        v_hat = nv / (1.0 - ADAM_B2 ** ADAM_STEP)
        d_ref[...] = -ADAM_LR * (m_hat / (jnp.sqrt(v_hat) + ADAM_EPS) + ADAM_WD * w_ref[...])
        nm_ref[...] = nm
        nv_ref[...] = nv

    spec = pl.BlockSpec((tr, C), lambda i: (i, 0))
    return pl.pallas_call(
        body, name=name, grid=(R // tr,), in_specs=[spec] * 4, out_specs=[spec] * 3,
        out_shape=[SDS((R, C), F32)] * 3, compiler_params=_cp(),
    )(w, g, m, v)


def _perm(a, B, S, d):
    if d == 1:
        return a
    lead = a.shape[:-1]
    return a.reshape(*lead, B, S // d, d).swapaxes(-1, -2).reshape(*lead, B * S)


def _unperm(a, B, S, d):
    if d == 1:
        return a
    lead = a.shape[:-1]
    return a.reshape(*lead, B, d, S // d).swapaxes(-1, -2).reshape(*lead, B * S)


def _perm3(a, B, S):
    return jnp.stack([_perm(a, B, S, d) for d in DILATIONS])


def _rope_tables(S):
    half = HEAD_DIM // 2
    inv_freq = ROPE_THETA ** (-jnp.arange(0, HEAD_DIM, 2, dtype=F32) / HEAD_DIM)
    ang = jnp.arange(S, dtype=F32)[:, None] * inv_freq[None, :]
    cos = jnp.concatenate([jnp.cos(ang), jnp.cos(ang)], axis=1).T
    sin = jnp.concatenate([-jnp.sin(ang), jnp.sin(ang)], axis=1).T
    return _perm3(cos, 1, S), _perm3(sin, 1, S)


def kernel(x, pool_w, pool_scale, w_q, w_kv, w_o, ffn_w_gate, ffn_w_up, ffn_conv_w, ffn_conv_b, ffn_w_down, ln1_g, ln1_b, ln2_g, ln2_b, loss_target, m_pool_w, m_pool_scale, m_w_q, m_w_kv, m_w_o, m_ffn_w_gate, m_ffn_w_up, m_ffn_conv_w, m_ffn_conv_b, m_ffn_w_down, m_ln1_g, m_ln1_b, m_ln2_g, m_ln2_b, v_pool_w, v_pool_scale, v_w_q, v_w_kv, v_w_o, v_ffn_w_gate, v_ffn_w_up, v_ffn_conv_w, v_ffn_conv_b, v_ffn_w_down, v_ln1_g, v_ln1_b, v_ln2_g, v_ln2_b):
    B, S, D = x.shape
    T = B * S
    depth = ln1_g.shape[0]
    nA, nB = pool_w.shape[0], w_q.shape[0]
    Fs = ffn_w_down.shape[1]
    Fd = Fs * N_DEV
    H = D // HEAD_DIM
    G = len(DILATIONS)
    PG = len(POOL_WINDOWS)
    Cg = D // PG
    alpha = (2.0 * depth) ** 0.25
    me = 4 * lax.axis_index("x") + 2 * lax.axis_index("y") + lax.axis_index("c")

    qs, kvs, os_ = w_q.shape[2], w_kv.shape[1], w_o.shape[1]
    pool_rows = pool_w.size // D
    local = {("pool",): pool_w.reshape(pool_rows, D).astype(BF16), ("wkv",): w_kv.T.astype(BF16)}
    for j in range(nB):
        local[("wq", j)] = w_q[j].T.astype(BF16)
        local[("wo", j)] = w_o[j].astype(BF16)
    for i in range(depth):
        local[("wg", i)] = ffn_w_gate[i].T.astype(BF16)
        local[("wu", i)] = ffn_w_up[i].T.astype(BF16)
        local[("wd", i)] = ffn_w_down[i].astype(BF16)
    ffn = lambda i: [("wg", i), ("wu", i), ("wd", i)]
    queue = [[("pool",)] + ffn(0)]
    if depth == 4 and nA == 2 and nB == 2:
        queue += [ffn(1), [("wo", 0), ("wo", 1)], [("wkv",), ("wq", 0)], [("wg", 2)], [("wu", 2)], [("wd", 2)], [("wq", 1)], ffn(3)]
    gathered = {}

    def land(keys, arrs):
        for k, a in zip(keys or (), arrs or ()):
            gathered[k] = a.reshape(-1, D)

    def next_gather():
        if not queue:
            return None, None
        keys = queue.pop(0)
        return keys, Gather([local[k] for k in keys])

    def weight(key):
        if key not in gathered:
            keys = [key]
            for bi, batch in enumerate(queue):
                if key in batch:
                    keys = queue.pop(bi)
                    break
            blk = all_gather_blocks(jnp.concatenate([local[k] for k in keys], axis=0), "gather_" + "_".join(map(str, key)), in_vmem=False)
            off = 0
            for k in keys:
                r = local[k].shape[0]
                gathered[k] = blk[:, off:off + r].reshape(-1, D)
                off += r
        return gathered[key]

    PW = weight(("pool",)).reshape(N_DEV, nA, PG, Cg // N_DEV, Cg).transpose(1, 2, 0, 3, 4).reshape(nA, PG, Cg, Cg)

    sm_cols = 128
    sm_local = jnp.concatenate([ffn_conv_w.reshape(-1), pool_scale.reshape(-1)])
    sm_rows = -(-sm_local.size // sm_cols)
    sm_rows_p = -(-sm_rows // 8) * 8
    sm_local = jnp.pad(sm_local, (0, sm_rows_p * sm_cols - sm_local.size)).reshape(sm_rows_p, sm_cols)
    sm = all_gather_blocks(sm_local, "gather_small", in_vmem=True).reshape(N_DEV, -1)
    ncw = ffn_conv_w.size
    conv_w_full = sm[:, :ncw].reshape(N_DEV, depth, 3, Fs).transpose(1, 2, 0, 3).reshape(depth, 3, Fd)
    pool_scale_full = sm[:, ncw:ncw + pool_scale.size].reshape(N_DEV, nA, D // N_DEV).transpose(1, 0, 2).reshape(nA, 1, D)

    cosT, sinT = _rope_tables(S)
    bias = _attn_bias()

    xs = x.reshape(T, D)
    saved = []
    cur, curb = xs, None
    kT = vT = x1b = None
    for i in range(depth):
        sv = {}
        if i < nA:
            mix, pooled = pool_fwd(cur, PW[i], pool_scale_full[i], B, S)
            sv["pooled"] = pooled
            a1, h, hb = add_ln(cur, mix, ln1_g[i], ln1_b[i], alpha)
        else:
            j = i - nA
            keys, cr = next_gather()
            qT, got = proj_rows_T(weight(("wq", j)), 0, 1, curb, cosT, sinT, True, HEAD_DIM ** -0.5, "q_proj", B, S, carry=cr)
            land(keys, got)
            o_gs, lse_gs = zip(*[attn_fwd(qT, kT, vT, bias, gi, B, S) for gi in range(G)])
            w_tok, lse_tot = attn_weights([_unperm(l, B, S, d) for l, d in zip(lse_gs, DILATIONS)])
            w_gs = [_perm(w, B, S, d) for w, d in zip(w_tok, DILATIONS)]
            os_gs = [attn_scale(o, w) for o, w in zip(o_gs, w_gs)]
            wo = weight(("wo", j))
            pre = cur
            for gi in range(G - 1):
                pre = rows_from_T([os_gs[gi]], wo, [0], pre, alpha if gi == 0 else 1.0, gi, B, S, f"o_proj_g{gi}")
            a1, h, hb = rows_from_T([os_gs[G - 1]], wo, [0], pre, 1.0, G - 1, B, S, "o_proj_ln", ln=(ln1_g[i], ln1_b[i]))
            sv.update(xb=curb, qT=qT, o_gs=o_gs, os_gs=os_gs, w_gs=w_gs, lse_tot=lse_tot)
        wg_i, wu_i, wd_i = weight(("wg", i)), weight(("wu", i)), weight(("wd", i))
        keys, cr = next_gather()
        g, ge, ud, hh, got = ffn_up(hb, wg_i, wu_i, conv_w_full[i], ffn_conv_b[i].reshape(1, Fd), B, S, carry=cr)
        land(keys, got)
        keys, cr = next_gather()
        a2, cur, curb, got = matmul_ln(hh, wd_i, NN, h, ln2_g[i], ln2_b[i], alpha, "ffn_down_ln", carry=cr)
        land(keys, got)
        sv.update(a1=a1, hb=hb, g=g, ge=ge, ud=ud, hh=hh, a2=a2)
        saved.append(sv)
        if i == nA - 1:
            x1b = curb
            wkv = weight(("wkv",))
            keys, cr = next_gather()
            kT, got = proj_rows_T(wkv, 0, 1, x1b, cosT, sinT, True, 1.0, "k_proj", B, S, carry=cr)
            land(keys, got)
            keys, cr = next_gather()
            vT, got = proj_rows_T(wkv, G, 1, x1b, cosT, sinT, False, 1.0, "v_proj", B, S, carry=cr)
            land(keys, got)

    dy, sq = loss_grad(cur, loss_target.reshape(T, D))

    small = {k: [None] * depth for k in ("ln1_g", "ln1_b", "ln2_g", "ln2_b", "conv_b", "conv_w")}
    dscale = [None] * nA
    dpw = [None] * nA
    dk_acc, dv_acc = [None] * G, [None] * G

    def blocks(a, rows):
        return a.reshape(N_DEV, rows, D)

    pending, landed = [], {}

    def next_carry():
        if not pending:
            return None, None
        key, parts = pending.pop(0)
        return key, Scatter(parts)

    dcur = dy
    for i in reversed(range(depth)):
        sv = saved[i]
        db2, db2b, small["ln2_g"][i], small["ln2_b"][i] = ln_bwd(dcur, sv["a2"], ln2_g[i])
        key, cr = next_carry()
        dg_, du_, dcb, dcw, got = ffn_mid_bwd(db2b, gathered[("wd", i)], sv["g"], sv["ge"], sv["ud"], conv_w_full[i], B, S, carry=cr)
        if cr is not None:
            landed[key] = got
        small["conv_b"][i] = jnp.sum(dcb, axis=0)
        small["conv_w"][i] = jnp.sum(dcw, axis=0)
        key, cr = next_carry()
        dwd, got = wgrad_rows(sv["hh"], db2b, "wgrad_down", carry=cr)
        if cr is not None:
            landed[key] = got
        dwg, landed[("down", i)] = wgrad_rows(dg_, sv["hb"], "wgrad_gate", carry=Scatter([blocks(dwd, Fs)]))
        dwu, landed[("gate", i)] = wgrad_rows(du_, sv["hb"], "wgrad_up", carry=Scatter([blocks(dwg, Fs)]))
        dh, landed[("up", i)] = ffn_dx(dg_, du_, gathered[("wg", i)], gathered[("wu", i)], db2, alpha, carry=Scatter([blocks(dwu, Fs)]))
        da1, da1b, small["ln1_g"][i], small["ln1_b"][i] = ln_bwd(dh, sv["a1"], ln1_g[i])
        if i < nA:
            dcur, dsp, dpwp = pool_bwd(da1, sv["pooled"], PW[i], pool_scale_full[i], alpha, B, S)
            dscale[i] = jnp.sum(dsp, axis=0)
            dpw[i] = jnp.sum(dpwp, axis=0)
        else:
            j = i - nA
            doT3 = proj_rows_T(gathered[("wo", j)], 0, 0, da1b, cosT, sinT, False, 1.0, "o_proj_bwd", B, S)[0]
            dwo_g = [wgrad_T_rows(sv["os_gs"][gi], da1b, gi, B, S, "wgrad_o") for gi in range(G)]
            dwo = functools.reduce(lambda a, b: a + b, [a.astype(F32) for a in dwo_g]).astype(BF16)
            delta_tok = functools.reduce(lambda a, b: a + b, [
                _unperm(attn_delta_part(doT3, gi, sv["o_gs"][gi], sv["w_gs"][gi]), B, S, d) for gi, d in enumerate(DILATIONS)])
            dwq = []
            dcur = da1
            for gi, d in enumerate(DILATIONS):
                dq_g, dk_acc[gi], dv_acc[gi] = attn_bwd(
                    sv["qT"], kT, vT, doT3, _perm(sv["lse_tot"], B, S, d), _perm(delta_tok, B, S, d),
                    cosT, sinT, bias, gi, HEAD_DIM ** -0.5, B, S, dk_prev=dk_acc[gi], dv_prev=dv_acc[gi])
                dwq.append(wgrad_T_rows(dq_g, sv["xb"], gi, B, S, "wgrad_q"))
                dcur = rows_from_T([dq_g], gathered[("wq", j)], [gi], dcur, alpha if gi == 0 else 1.0, gi, B, S, f"q_proj_bwd_g{gi}")
            dwq = jnp.concatenate(dwq, axis=0)
            if j == 0:
                dkv = [a.astype(BF16) for a in dk_acc + dv_acc]
                dwkv = jnp.concatenate([wgrad_T_rows(a, x1b, gi % G, B, S, "wgrad_kv") for gi, a in enumerate(dkv)], axis=0)
                for gi in range(G):
                    dcur = rows_from_T([dkv[gi], dkv[G + gi]], gathered[("wkv",)], [gi, G + gi], dcur, 1.0, gi, B, S, f"kv_proj_bwd_g{gi}")
                pending.append((("kv",), [blocks(dwkv, kvs)]))
            pending.append((("attn", j), [blocks(dwq, qs), blocks(dwo, os_)]))
    grad_x = dcur.reshape(B, S, D)

    dpw_all = jnp.stack(dpw).reshape(nA, PG, N_DEV, Cg // N_DEV, Cg).transpose(2, 0, 1, 3, 4).reshape(N_DEV, pool_rows, D)
    tail_keys = [k for k, _ in pending] + [("pool",)]
    tail_parts = [parts for _, parts in pending] + [[dpw_all.astype(BF16)]]
    tail_rows = [sum(p.shape[1] for p in parts) for parts in tail_parts]
    tail = scatter_partials([p for parts in tail_parts for p in parts], "scatter_tail")
    for t, key in enumerate(tail_keys):
        lo = sum(tail_rows[:t])
        landed[key] = tail[:, lo:lo + tail_rows[t]]

    def reduced(key):
        return sum_slots(landed[key], "sum_" + "_".join(str(k) for k in key))

    g_attn = [reduced(("attn", j)) for j in range(nB)]
    g_w_q = jnp.swapaxes(jnp.stack([a[:qs] for a in g_attn]), 1, 2)
    g_w_o = jnp.stack([a[qs:] for a in g_attn])
    g_w_kv = reduced(("kv",)).T
    g_gate = jnp.swapaxes(jnp.stack([reduced(("gate", i)) for i in range(depth)]), 1, 2)
    g_up = jnp.swapaxes(jnp.stack([reduced(("up", i)) for i in range(depth)]), 1, 2)
    g_down = jnp.stack([reduced(("down", i)) for i in range(depth)])
    g_pool_w = reduced(("pool",)).reshape(pool_w.shape)

    def rows_of(a):
        a = a.reshape(-1)
        n = -(-a.size // D) * D
        return jnp.pad(a, (0, n - a.size)).reshape(-1, D)

    sm_parts = [rows_of(jnp.concatenate(small[k], axis=0)) for k in ("ln1_g", "ln1_b", "ln2_g", "ln2_b")]
    sm_parts += [rows_of(jnp.stack(small["conv_b"])), rows_of(jnp.stack(small["conv_w"])), rows_of(jnp.stack(dscale)), sq]
    sm_sizes = [p.shape[0] for p in sm_parts]
    sm_all = jnp.concatenate(sm_parts, axis=0)
    pad_rows = -(-sm_all.shape[0] // 8) * 8 - sm_all.shape[0]
    sm_all = jnp.pad(sm_all, ((0, pad_rows), (0, 0)))
    sm_sum = sum_slots(all_gather_blocks(sm_all, "gather_small_grads", in_vmem=True), "sum_small_grads")
    sm_offs = [sum(sm_sizes[:i]) for i in range(len(sm_sizes))]

    def sm_take(i, shape):
        n = math.prod(shape)
        return sm_sum[sm_offs[i]:sm_offs[i] + sm_sizes[i]].reshape(-1)[:n].reshape(shape)

    g_ln1_g, g_ln1_b = sm_take(0, (depth, D)), sm_take(1, (depth, D))
    g_ln2_g, g_ln2_b = sm_take(2, (depth, D)), sm_take(3, (depth, D))
    g_conv_b = sm_take(4, (depth, Fd))
    g_conv_w = lax.dynamic_slice_in_dim(sm_take(5, (depth, 3, Fd)), me * Fs, Fs, axis=2)
    g_pool_scale = lax.dynamic_slice_in_dim(sm_take(6, (nA, D)), me * (D // N_DEV), D // N_DEV, axis=1)
    loss = (0.5 / D) * jnp.sum(sm_take(7, (D,)))

    def v2(a):
        return a.reshape(-1, a.shape[-1])

    names = ["pool_w", "pool_scale", "w_q", "w_kv", "w_o", "ffn_w_gate", "ffn_w_up", "ffn_conv_w", "ffn_conv_b",
             "ffn_w_down", "ln1_g", "ln1_b", "ln2_g", "ln2_b"]
    ws = [pool_w, pool_scale, w_q, w_kv, w_o, ffn_w_gate, ffn_w_up, ffn_conv_w, ffn_conv_b, ffn_w_down, ln1_g, ln1_b, ln2_g, ln2_b]
    ms = [m_pool_w, m_pool_scale, m_w_q, m_w_kv, m_w_o, m_ffn_w_gate, m_ffn_w_up, m_ffn_conv_w, m_ffn_conv_b, m_ffn_w_down, m_ln1_g, m_ln1_b, m_ln2_g, m_ln2_b]
    vs = [v_pool_w, v_pool_scale, v_w_q, v_w_kv, v_w_o, v_ffn_w_gate, v_ffn_w_up, v_ffn_conv_w, v_ffn_conv_b, v_ffn_w_down, v_ln1_g, v_ln1_b, v_ln2_g, v_ln2_b]
    gs = [g_pool_w, g_pool_scale, g_w_q, g_w_kv, g_w_o, g_gate, g_up, g_conv_w, g_conv_b, g_down, g_ln1_g, g_ln1_b, g_ln2_g, g_ln2_b]
    deltas, new_ms, new_vs = [], [], []
    for nm, w, gr, m_, v_ in zip(names, ws, gs, ms, vs):
        d_, nm_, nv_ = adamw(v2(w), v2(gr), v2(m_), v2(v_), "adamw_" + nm)
        deltas.append(d_.reshape(w.shape))
        new_ms.append(nm_.reshape(w.shape))
        new_vs.append(nv_.reshape(w.shape))

    return (loss, grad_x, *gs, *deltas, *new_ms, *new_vs)
```

```python
import functools
import math

import jax
import jax.numpy as jnp
from jax import lax
from jax.experimental import pallas as pl
from jax.experimental.pallas import tpu as pltpu

F32 = jnp.float32
BF16 = jnp.bfloat16
SDS = jax.ShapeDtypeStruct
MESH = pl.DeviceIdType.MESH

N_DEV = 8
HEAD_DIM = 64
BLK = 128
DILATIONS = (1, 4, 16)
POOL_WINDOWS = (2, 4, 8, 16)
ROPE_THETA = 10000.0
LN_EPS = 1e-5
NEG = -1e30
V7X_VMEM_LIMIT = 56 * 1024 * 1024

ADAM_LR, ADAM_B1, ADAM_B2, ADAM_EPS, ADAM_WD, ADAM_STEP = 0.001, 0.9, 0.999, 1e-08, 0.01, 10

NN = (((1,), (0,)), ((), ()))
NT = (((1,), (1,)), ((), ()))
TN = (((0,), (0,)), ((), ()))


def _cp(sem=None):
    kw = dict(vmem_limit_bytes=V7X_VMEM_LIMIT)
    if sem is not None:
        kw["dimension_semantics"] = sem
    return pltpu.CompilerParams(**kw)


def _dot(a, b, dims=NN):
    return lax.dot_general(a, b, dims, preferred_element_type=F32)


def _tile(n, target, mult):
    best = None
    for t in range(mult, min(n, target) + 1, mult):
        if n % t == 0:
            best = t
    return best if best is not None else n


def _mesh_pos():
    return lax.axis_index("x"), lax.axis_index("y"), lax.axis_index("c")


def all_gather_blocks(xl, name, in_vmem):
    R, C = xl.shape
    space = pltpu.VMEM if in_vmem else pl.ANY

    def body(x_ref, out_ref, send_sems, recv_sems, local_sem):
        x, y, c = _mesh_pos()
        me, sibling = (x, y, c), (x, y, 1 - c)
        chips = [(1 - x, y), (x, 1 - y), (1 - x, 1 - y)]

        def slot(px, py, pc):
            return out_ref.at[4 * px + 2 * py + pc]

        def copy(k, block, to, src=None):
            return pltpu.make_async_remote_copy(
                src_ref=slot(*block) if src is None else src, dst_ref=slot(*block),
                send_sem=send_sems.at[k], recv_sem=recv_sems.at[k], device_id=to, device_id_type=MESH)

        mine = pltpu.make_async_copy(x_ref, slot(*me), local_sem)
        mine.start()
        first = [copy(0, me, sibling, src=x_ref)]
        first += [copy(1 + j, me, (*chip, c), src=x_ref) for j, chip in enumerate(chips)]
        for cp in first:
            cp.start()
        passed = [copy(4 + j, (*chip, c), sibling) for j, chip in enumerate(chips)]
        for j, chip in enumerate(chips):
            copy(1 + j, (*chip, c), me).wait_recv()
            passed[j].start()
        copy(0, sibling, me).wait_recv()
        for j, chip in enumerate(chips):
            copy(4 + j, (*chip, 1 - c), me).wait_recv()
        for cp in first + passed:
            cp.wait_send()
        mine.wait()

    return pl.pallas_call(
        body, name=name,
        out_shape=SDS((N_DEV, R, C), xl.dtype),
        in_specs=[pl.BlockSpec(memory_space=space)],
        out_specs=pl.BlockSpec(memory_space=space),
        scratch_shapes=[pltpu.SemaphoreType.DMA((7,)), pltpu.SemaphoreType.DMA((7,)), pltpu.SemaphoreType.DMA],
        compiler_params=_cp(),
    )(xl)


def _peers():
    x, y, c = _mesh_pos()
    peers = []
    for r in range(1, N_DEV):
        peers.append((1 - x if (r & 4) else x, 1 - y if (r & 2) else y, 1 - c if (r & 1) else c))
    return 4 * x + 2 * y + c, peers


class Gather:
    def __init__(self, parts):
        self.parts = list(parts)
        n = len(self.parts)
        self.out_shapes = [SDS((N_DEV,) + p.shape, p.dtype) for p in self.parts]
        self.scratch = [pltpu.SemaphoreType.DMA((7 * n,)), pltpu.SemaphoreType.DMA((7 * n,)), pltpu.SemaphoreType.DMA((n,))]

    def start(self, part_refs, out_refs, send_sems, recv_sems, local_sems):
        me_lin, peers = _peers()
        n = len(self.parts)
        for i in range(n):
            pltpu.make_async_copy(part_refs[i], out_refs[i].at[me_lin], local_sems.at[i]).start()
        for k, peer in enumerate(peers):
            for i in range(n):
                pltpu.make_async_remote_copy(
                    src_ref=part_refs[i], dst_ref=out_refs[i].at[me_lin],
                    send_sem=send_sems.at[k * n + i], recv_sem=recv_sems.at[k * n + i],
                    device_id=peer, device_id_type=MESH).start()

    def wait(self, out_refs, send_sems, recv_sems, local_sems):
        me_lin, peers = _peers()
        n = len(self.parts)
        for k, (px, py, pc) in enumerate(peers):
            p_lin = 4 * px + 2 * py + pc
            for i in range(n):
                arrival = pltpu.make_async_remote_copy(
                    src_ref=out_refs[i].at[p_lin], dst_ref=out_refs[i].at[p_lin],
                    send_sem=send_sems.at[k * n + i], recv_sem=recv_sems.at[k * n + i],
                    device_id=(px, py, pc), device_id_type=MESH)
                arrival.wait_recv()
                arrival.wait_send()
        for i in range(n):
            pltpu.make_async_copy(out_refs[i].at[me_lin], out_refs[i].at[me_lin], local_sems.at[i]).wait()


class Scatter:
    def __init__(self, parts):
        self.parts = list(parts)
        self.rows = [p.shape[1] for p in parts]
        self.offs = [sum(self.rows[:i]) for i in range(len(self.rows))]
        self.out_shapes = [SDS((N_DEV, sum(self.rows), parts[0].shape[2]), parts[0].dtype)]
        self.scratch = [pltpu.SemaphoreType.DMA((7,)), pltpu.SemaphoreType.DMA((7,)), pltpu.SemaphoreType.DMA]

    def start(self, part_refs, out_refs, send_sems, recv_sems, local_sem):
        out_ref = out_refs[0]
        me_lin, peers = _peers()
        for i, (off, r) in enumerate(zip(self.offs, self.rows)):
            pltpu.make_async_copy(part_refs[i].at[me_lin], out_ref.at[me_lin, pl.ds(off, r)], local_sem).start()
        for k, (px, py, pc) in enumerate(peers):
            p_lin = 4 * px + 2 * py + pc
            for i, (off, r) in enumerate(zip(self.offs, self.rows)):
                pltpu.make_async_remote_copy(
                    src_ref=part_refs[i].at[p_lin], dst_ref=out_ref.at[me_lin, pl.ds(off, r)],
                    send_sem=send_sems.at[k], recv_sem=recv_sems.at[k],
                    device_id=(px, py, pc), device_id_type=MESH).start()

    def wait(self, out_refs, send_sems, recv_sems, local_sem):
        out_ref = out_refs[0]
        me_lin, peers = _peers()
        for k, (px, py, pc) in enumerate(peers):
            p_lin = 4 * px + 2 * py + pc
            whole = pltpu.make_async_remote_copy(
                src_ref=out_ref.at[p_lin], dst_ref=out_ref.at[p_lin],
                send_sem=send_sems.at[k], recv_sem=recv_sems.at[k],
                device_id=(px, py, pc), device_id_type=MESH)
            whole.wait_recv()
            whole.wait_send()
        pltpu.make_async_copy(out_ref.at[me_lin], out_ref.at[me_lin], local_sem).wait()


def scatter_partials(parts, name):
    sc = Scatter(parts)
    n = len(parts)

    def body(*refs):
        sc.start(refs[:n], refs[n:n + 1], *refs[n + 1:])
        sc.wait(refs[n:n + 1], *refs[n + 1:])

    return pl.pallas_call(
        body, name=name, out_shape=sc.out_shapes[0],
        in_specs=[pl.BlockSpec(memory_space=pl.ANY)] * n, out_specs=pl.BlockSpec(memory_space=pl.ANY),
        scratch_shapes=sc.scratch, compiler_params=_cp(),
    )(*parts)


def _call(body, name, grid, in_specs, out_specs, out_shape, args, scratch=(), sem=None, carry=None):
    in_specs, out_specs, out_shape, scratch = list(in_specs), list(out_specs), list(out_shape), list(scratch)
    if carry is None:
        outs = pl.pallas_call(body, name=name, grid=grid, in_specs=in_specs, out_specs=out_specs, out_shape=out_shape,
                              scratch_shapes=scratch, compiler_params=_cp(sem))(*args)
        return list(outs), None
    n_in, n_out, n_scr, n_c, n_co = len(in_specs), len(out_specs), len(scratch), len(carry.parts), len(carry.out_shapes)
    last = [g - 1 for g in grid]

    def carried(*refs):
        ins, c_ins = refs[:n_in], refs[n_in:n_in + n_c]
        o0 = n_in + n_c
        outs, c_out = refs[o0:o0 + n_out], refs[o0 + n_out:o0 + n_out + n_co]
        s0 = o0 + n_out + n_co
        scr, c_scr = refs[s0:s0 + n_scr], refs[s0 + n_scr:]
        ids = [pl.program_id(a) for a in range(len(grid))]
        is_first = functools.reduce(jnp.logical_and, [i == 0 for i in ids])
        is_last = functools.reduce(jnp.logical_and, [i == l for i, l in zip(ids, last)])

        @pl.when(is_first)
        def _():
            carry.start(c_ins, c_out, *c_scr)

        body(*ins, *outs, *scr)

        @pl.when(is_last)
        def _():
            carry.wait(c_out, *c_scr)

    hbm = pl.BlockSpec(memory_space=pl.ANY)
    outs = pl.pallas_call(
        carried, name=name + "_carry", grid=grid, in_specs=in_specs + [hbm] * n_c, out_specs=out_specs + [hbm] * n_co,
        out_shape=out_shape + carry.out_shapes, scratch_shapes=scratch + carry.scratch,
        compiler_params=_cp(sem if sem is not None else ("arbitrary",) * len(grid)),
    )(*args, *carry.parts)
    return list(outs[:n_out]), list(outs[n_out:])


def sum_slots(slots, name, out_dtype=F32):
    _, R, C = slots.shape
    tr = _tile(R, 512, 16)

    def body(s_ref, o_ref):
        acc = s_ref[0].astype(F32)
        for s in range(1, N_DEV):
            acc = acc + s_ref[s].astype(F32)
        o_ref[...] = acc.astype(out_dtype)

    return pl.pallas_call(
        body, name=name, grid=(R // tr,),
        in_specs=[pl.BlockSpec((N_DEV, tr, C), lambda i: (0, i, 0))],
        out_specs=pl.BlockSpec((tr, C), lambda i: (i, 0)),
        out_shape=SDS((R, C), out_dtype), compiler_params=_cp(),
    )(slots)


def add_ln(x, mix, g, b, alpha, carry=None):
    T, D = x.shape
    tm = _tile(T, 512, 16)

    def body(x_ref, m_ref, g_ref, b_ref, a_ref, y_ref, yb_ref):
        a = alpha * x_ref[...] + m_ref[...]
        mu = jnp.mean(a, axis=-1, keepdims=True)
        xc = a - mu
        var = jnp.mean(xc * xc, axis=-1, keepdims=True)
        y = xc * lax.rsqrt(var + LN_EPS) * g_ref[...] + b_ref[...]
        a_ref[...] = a
        y_ref[...] = y
        yb_ref[...] = y.astype(BF16)

    row = pl.BlockSpec((tm, D), lambda i: (i, 0))
    vec = pl.BlockSpec((1, D), lambda i: (0, 0))
    outs, landed = _call(body, "add_ln", (T // tm,), [row, row, vec, vec], [row, row, row],
                         [SDS((T, D), F32), SDS((T, D), F32), SDS((T, D), BF16)],
                         (x, mix, g.reshape(1, D), b.reshape(1, D)), carry=carry)
    return (*outs, landed)


def _ln_bwd_tile(dy, a, gamma):
    mu = jnp.mean(a, axis=-1, keepdims=True)
    xc = a - mu
    var = jnp.mean(xc * xc, axis=-1, keepdims=True)
    r = lax.rsqrt(var + LN_EPS)
    xh = xc * r
    dxh = dy * gamma
    m1 = jnp.mean(dxh, axis=-1, keepdims=True)
    m2 = jnp.mean(dxh * xh, axis=-1, keepdims=True)
    da = r * (dxh - m1 - xh * m2)
    return da, jnp.sum(dy * xh, axis=0, keepdims=True), jnp.sum(dy, axis=0, keepdims=True)


def ln_bwd(dy, a, g):
    T, D = a.shape
    tm = _tile(T, 512, 16)

    def body(dy_ref, a_ref, g_ref, da_ref, dab_ref, dg_ref, db_ref):
        @pl.when(pl.program_id(0) == 0)
        def _():
            dg_ref[...] = jnp.zeros_like(dg_ref)
            db_ref[...] = jnp.zeros_like(db_ref)

        da, sg, sb = _ln_bwd_tile(dy_ref[...], a_ref[...], g_ref[...])
        da_ref[...] = da
        dab_ref[...] = da.astype(BF16)
        dg_ref[...] += sg
        db_ref[...] += sb

    row = pl.BlockSpec((tm, D), lambda i: (i, 0))
    vec = pl.BlockSpec((1, D), lambda i: (0, 0))
    return pl.pallas_call(
        body, name="ln_bwd", grid=(T // tm,),
        in_specs=[row, row, vec], out_specs=[row, row, vec, vec],
        out_shape=[SDS((T, D), F32), SDS((T, D), BF16), SDS((1, D), F32), SDS((1, D), F32)],
        compiler_params=_cp(("arbitrary",)),
    )(dy, a, g.reshape(1, D))


def loss_ln_bwd(y, tgt, a, g):
    T, D = y.shape
    tm = _tile(T, 512, 16)

    def body(y_ref, t_ref, a_ref, g_ref, da_ref, dab_ref, dg_ref, db_ref, sq_ref):
        @pl.when(pl.program_id(0) == 0)
        def _():
            dg_ref[...] = jnp.zeros_like(dg_ref)
            db_ref[...] = jnp.zeros_like(db_ref)
            sq_ref[...] = jnp.zeros_like(sq_ref)

        e = y_ref[...] - t_ref[...]
        da, sg, sb = _ln_bwd_tile(e / float(D), a_ref[...], g_ref[...])
        da_ref[...] = da
        dab_ref[...] = da.astype(BF16)
        dg_ref[...] += sg
        db_ref[...] += sb
        sq_ref[...] += jnp.sum(e * e, axis=0, keepdims=True)

    row = pl.BlockSpec((tm, D), lambda i: (i, 0))
    vec = pl.BlockSpec((1, D), lambda i: (0, 0))
    return pl.pallas_call(
        body, name="loss_ln_bwd", grid=(T // tm,),
        in_specs=[row, row, row, vec], out_specs=[row, row, vec, vec, vec],
        out_shape=[SDS((T, D), F32), SDS((T, D), BF16), SDS((1, D), F32), SDS((1, D), F32), SDS((1, D), F32)],
        compiler_params=_cp(("arbitrary",)),
    )(y, tgt, a, g.reshape(1, D))


def matmul_ln(a, w, dims, res, g, b, alpha, name, carry=None):
    if dims == TN:
        K, T = a.shape
    else:
        T, K = a.shape
    D = w.shape[1]
    tm = _tile(T, 512, 128 if dims == TN else 16)

    def body(a_ref, w_ref, r_ref, g_ref, b_ref, p_ref, y_ref, yb_ref):
        pre = alpha * r_ref[...] + _dot(a_ref[...], w_ref[...], dims)
        mu = jnp.mean(pre, axis=-1, keepdims=True)
        xc = pre - mu
        var = jnp.mean(xc * xc, axis=-1, keepdims=True)
        y = xc * lax.rsqrt(var + LN_EPS) * g_ref[...] + b_ref[...]
        p_ref[...] = pre
        y_ref[...] = y
        yb_ref[...] = y.astype(BF16)

    a_spec = pl.BlockSpec((K, tm), lambda i: (0, i)) if dims == TN else pl.BlockSpec((tm, K), lambda i: (i, 0))
    row = pl.BlockSpec((tm, D), lambda i: (i, 0))
    vec = pl.BlockSpec((1, D), lambda i: (0, 0))
    outs, landed = _call(
        body, name, (T // tm,), [a_spec, pl.BlockSpec(w.shape, lambda i: (0, 0)), row, vec, vec], [row, row, row],
        [SDS((T, D), F32), SDS((T, D), F32), SDS((T, D), BF16)], (a, w, res, g.reshape(1, D), b.reshape(1, D)), carry=carry)
    return (*outs, landed)


def dx_from_T(aTs, w, res, alpha, name, tm_target):
    T = aTs[0].shape[1]
    N = w.shape[1]
    ks = [a.shape[0] for a in aTs]
    n = len(aTs)
    tm = _tile(T, tm_target, 128)

    def body(*refs):
        a_refs, w_ref, r_ref, o_ref = refs[:n], refs[n], refs[n + 1], refs[n + 2]
        acc = alpha * r_ref[...]
        off = 0
        for a_ref, k in zip(a_refs, ks):
            acc = acc + _dot(a_ref[...], w_ref[off:off + k, :], TN)
            off += k
        o_ref[...] = acc

    row = pl.BlockSpec((tm, N), lambda i: (i, 0))
    return pl.pallas_call(
        body, name=name, grid=(T // tm,),
        in_specs=[pl.BlockSpec((k, tm), lambda i: (0, i)) for k in ks] + [pl.BlockSpec(w.shape, lambda i: (0, 0)), row],
        out_specs=row, out_shape=SDS((T, N), F32), compiler_params=_cp(),
    )(*aTs, w, res)


def matmul_to_T(w, a, name):
    M, K = w.shape
    T = a.shape[0]
    tt = _tile(T, 512, 128)

    def body(w_ref, a_ref, o_ref):
        o_ref[...] = _dot(w_ref[...], a_ref[...], NT).astype(BF16)

    return pl.pallas_call(
        body, name=name, grid=(T // tt,),
        in_specs=[pl.BlockSpec((M, K), lambda i: (0, 0)), pl.BlockSpec((tt, K), lambda i: (i, 0))],
        out_specs=pl.BlockSpec((M, tt), lambda i: (0, i)),
        out_shape=SDS((M, T), BF16), compiler_params=_cp(),
    )(w, a)


def wgrad_rows(a, b, name, carry=None):
    T, M = a.shape
    N = b.shape[1]
    tt = _tile(T, 512, 16)
    tmm = _tile(M, 1536, 128)
    nt = T // tt

    def body(a_ref, b_ref, o_ref, acc_ref):
        t = pl.program_id(1)

        @pl.when(t == 0)
        def _():
            acc_ref[...] = jnp.zeros_like(acc_ref)

        acc_ref[...] += _dot(a_ref[...], b_ref[...], TN)

        @pl.when(t == nt - 1)
        def _():
            o_ref[...] = acc_ref[...].astype(BF16)

    outs, landed = _call(
        body, name, (M // tmm, nt),
        [pl.BlockSpec((tt, tmm), lambda i, t: (t, i)), pl.BlockSpec((tt, N), lambda i, t: (t, 0))],
        [pl.BlockSpec((tmm, N), lambda i, t: (i, 0))], [SDS((M, N), BF16)], (a, b),
        scratch=[pltpu.VMEM((tmm, N), F32)], sem=("arbitrary", "arbitrary"), carry=carry)
    return outs[0], (landed[0] if landed else None)


def wgrad_T(aT, bT3, g, name):
    M, T = aT.shape
    N = bT3.shape[1]
    tt = _tile(T, 1024, 128)
    nt = T // tt

    def body(a_ref, b_ref, o_ref, acc_ref):
        t = pl.program_id(0)

        @pl.when(t == 0)
        def _():
            acc_ref[...] = jnp.zeros_like(acc_ref)

        acc_ref[...] += _dot(a_ref[...], b_ref[0], NT)

        @pl.when(t == nt - 1)
        def _():
            o_ref[...] = acc_ref[...].astype(BF16)

    return pl.pallas_call(
        body, name=name, grid=(nt,),
        in_specs=[pl.BlockSpec((M, tt), lambda t: (0, t)), pl.BlockSpec((1, N, tt), lambda t: (g, 0, t))],
        out_specs=pl.BlockSpec((M, N), lambda t: (0, 0)),
        out_shape=SDS((M, N), BF16), scratch_shapes=[pltpu.VMEM((M, N), F32)],
        compiler_params=_cp(("arbitrary",)),
    )(aT, bT3)


def wgrad_mixed(aT, b, name):
    M, T = aT.shape
    N = b.shape[1]
    tt = _tile(T, 1024, 128)
    nt = T // tt

    def body(a_ref, b_ref, o_ref, acc_ref):
        t = pl.program_id(0)

        @pl.when(t == 0)
        def _():
            acc_ref[...] = jnp.zeros_like(acc_ref)

        acc_ref[...] += _dot(a_ref[...], b_ref[...], NN)

        @pl.when(t == nt - 1)
        def _():
            o_ref[...] = acc_ref[...].astype(BF16)

    return pl.pallas_call(
        body, name=name, grid=(nt,),
        in_specs=[pl.BlockSpec((M, tt), lambda t: (0, t)), pl.BlockSpec((tt, N), lambda t: (t, 0))],
        out_specs=pl.BlockSpec((M, N), lambda t: (0, 0)),
        out_shape=SDS((M, N), BF16), scratch_shapes=[pltpu.VMEM((M, N), F32)],
        compiler_params=_cp(("arbitrary",)),
    )(aT, b)


def _shift_down(x, k, rows):
    return jnp.where(rows >= k, pltpu.roll(x, k, 0), 0.0)


def _shift_up(x, k, rows):
    n = x.shape[0]
    return jnp.where(rows < n - k, pltpu.roll(x, n - k, 0), 0.0)


def _pick(g, vals):
    out = vals[-1]
    for k in range(len(vals) - 2, -1, -1):
        out = jnp.where(g == k, vals[k], out)
    return out


def pool_fwd(x, pw, scale, B, S, carry=None):
    T, D = x.shape
    G = len(POOL_WINDOWS)
    Cg = D // G

    def body(x_ref, w_ref, s_ref, mix_ref, pooled_ref):
        g = pl.program_id(1)
        xv = x_ref[...]
        rows = lax.broadcasted_iota(jnp.int32, xv.shape, 0)
        sums, cur, k = [], xv, 1
        for _ in POOL_WINDOWS:
            cur = cur + _shift_down(cur, k, rows)
            sums.append(cur)
            k *= 2
        win = 2 * lax.shift_left(jnp.int32(1), g)
        total = _pick(g, sums)
        count = jnp.minimum(rows + 1, win).astype(F32)
        pooled = total / count - xv
        pb = pooled.astype(BF16)
        pooled_ref[...] = pb
        mix_ref[...] = _dot(pb, w_ref[0]) * s_ref[...]

    blk = pl.BlockSpec((S, Cg), lambda b, g: (b, g))
    outs, landed = _call(
        body, "pool_fwd", (B, G),
        [blk, pl.BlockSpec((1, Cg, Cg), lambda b, g: (g, 0, 0)), pl.BlockSpec((1, Cg), lambda b, g: (0, g))],
        [blk, blk], [SDS((T, D), F32), SDS((T, D), BF16)], (x, pw, scale), carry=carry)
    return (*outs, landed)


def pool_bwd(dmix, pooled, pw, scale, alpha, B, S):
    T, D = dmix.shape
    G = len(POOL_WINDOWS)
    Cg = D // G

    def body(d_ref, p_ref, w_ref, s_ref, dx_ref, ds_ref, dw_ref):
        g = pl.program_id(1)
        dm = d_ref[...]
        pb = p_ref[...]
        w = w_ref[0]
        ypre = _dot(pb, w)
        ds_ref[0] = jnp.sum(dm * ypre, axis=0, keepdims=True)
        dy = (dm * s_ref[...]).astype(BF16)
        dpool = _dot(dy, w, NT)
        dw_ref[0, 0] = _dot(pb, dy, TN)
        rows = lax.broadcasted_iota(jnp.int32, dm.shape, 0)
        win = 2 * lax.shift_left(jnp.int32(1), g)
        count = jnp.minimum(rows + 1, win).astype(F32)
        cur, k, sums = dpool / count, 1, []
        for _ in POOL_WINDOWS:
            cur = cur + _shift_up(cur, k, rows)
            sums.append(cur)
            k *= 2
        dx_ref[...] = alpha * dm + _pick(g, sums) - dpool

    blk = pl.BlockSpec((S, Cg), lambda b, g: (b, g))
    return pl.pallas_call(
        body, name="pool_bwd", grid=(B, G),
        in_specs=[blk, blk, pl.BlockSpec((1, Cg, Cg), lambda b, g: (g, 0, 0)), pl.BlockSpec((1, Cg), lambda b, g: (0, g))],
        out_specs=[blk, pl.BlockSpec((1, 1, Cg), lambda b, g: (b, 0, g)),
                   pl.BlockSpec((1, 1, Cg, Cg), lambda b, g: (b, g, 0, 0))],
        out_shape=[SDS((T, D), F32), SDS((B, 1, D), F32), SDS((B, G, Cg, Cg), F32)], compiler_params=_cp(),
    )(dmix, pooled, pw, scale)


_GELU_K = math.sqrt(2.0 / math.pi)
_GELU_C = 0.044715
FFN_ROWS = 512
FFN_HALO = 16


def ffn_up(hb, wgT, wuT, cw, cb, B, S, carry=None):
    T, D = hb.shape
    Fd = wgT.shape[0]
    fn = _tile(Fd, 256, 128)

    nc = S // _tile(S, FFN_ROWS, FFN_HALO)
    rc = S // nc

    def body(h_ref, wg_ref, wu_ref, cw_ref, cb_ref, g_ref, ge_ref, ud_ref, hh_ref):
        wg, wu, cw, cb = wg_ref[...], wu_ref[...], cw_ref[...], cb_ref[...]
        halo = jnp.zeros((FFN_HALO, fn), F32)
        for ci in range(nc):
            rows = slice(ci * rc, (ci + 1) * rc)
            h = h_ref[rows, :]
            g = _dot(h, wg, NT)
            u = _dot(h, wu, NT)
            gext = jnp.concatenate([halo, g], axis=0)
            halo = g[rc - FFN_HALO:, :]
            c = cb + cw[0:1] * pltpu.roll(gext, 2, 0)[FFN_HALO:, :] + cw[1:2] * pltpu.roll(gext, 1, 0)[FFN_HALO:, :] + cw[2:3] * g
            c2 = c * c
            th = jnp.tanh(c * (_GELU_K + (_GELU_K * _GELU_C) * c2))
            cdf = 0.5 * th + 0.5
            ge = c * cdf
            dgelu = cdf + (c * (1.0 - th * th)) * (0.5 * _GELU_K + (1.5 * _GELU_K * _GELU_C) * c2)
            g_ref[rows, :] = g.astype(BF16)
            ge_ref[rows, :] = ge.astype(BF16)
            ud_ref[rows, :] = (u * dgelu).astype(BF16)
            hh_ref[rows, :] = (ge * u).astype(BF16)

    hspec = pl.BlockSpec((S, D), lambda b, j: (b, 0))
    wspec = pl.BlockSpec((fn, D), lambda b, j: (j, 0))
    ospec = pl.BlockSpec((S, fn), lambda b, j: (b, j))
    outs, landed = _call(
        body, "ffn_up", (B, Fd // fn),
        [hspec, wspec, wspec, pl.BlockSpec((3, fn), lambda b, j: (0, j)), pl.BlockSpec((1, fn), lambda b, j: (0, j))],
        [ospec] * 4, [SDS((T, Fd), BF16)] * 4, (hb, wgT, wuT, cw, cb), carry=carry)
    return (*outs, landed)


def ffn_mid_bwd(dfb, wd, g, ge, ud, cw, B, S, carry=None):
    T, D = dfb.shape
    Fd = wd.shape[0]
    fn = _tile(Fd, 256, 128)

    nc = S // _tile(S, FFN_ROWS, FFN_HALO)
    rc = S // nc

    def body(df_ref, wd_ref, g_ref, ge_ref, ud_ref, cw_ref, dg_ref, du_ref, dcb_ref, dcw_ref):
        wd, cw = wd_ref[...], cw_ref[...]
        zero_halo = jnp.zeros((FFN_HALO, fn), F32)

        def dg_of(dc, nxt):
            ext = jnp.concatenate([dc, nxt], axis=0)
            n = rc + FFN_HALO
            return cw[2:3] * dc + cw[1:2] * pltpu.roll(ext, n - 1, 0)[:rc, :] + cw[0:1] * pltpu.roll(ext, n - 2, 0)[:rc, :]

        sums = [jnp.zeros((1, fn), F32) for _ in range(4)]
        dc_prev = None
        for ci in range(nc):
            rows = slice(ci * rc, (ci + 1) * rc)
            dhh = _dot(df_ref[rows, :], wd, NT)
            du_ref[rows, :] = (dhh * ge_ref[rows, :].astype(F32)).astype(BF16)
            dc = dhh * ud_ref[rows, :].astype(F32)
            if ci == 0:
                gext = jnp.concatenate([zero_halo, g_ref[rows, :].astype(F32)], axis=0)
            else:
                gext = g_ref[ci * rc - FFN_HALO:(ci + 1) * rc, :].astype(F32)
            parts = [dc, dc * pltpu.roll(gext, 2, 0)[FFN_HALO:, :], dc * pltpu.roll(gext, 1, 0)[FFN_HALO:, :], dc * gext[FFN_HALO:, :]]
            sums = [s + jnp.sum(p, axis=0, keepdims=True) for s, p in zip(sums, parts)]
            if dc_prev is not None:
                dg_ref[(ci - 1) * rc:ci * rc, :] = dg_of(dc_prev, dc[:FFN_HALO, :]).astype(BF16)
            dc_prev = dc
        dg_ref[(nc - 1) * rc:, :] = dg_of(dc_prev, zero_halo).astype(BF16)
        dcb_ref[0] = sums[0]
        dcw_ref[0] = jnp.concatenate(sums[1:], axis=0)

    tspec = pl.BlockSpec((S, fn), lambda b, j: (b, j))
    outs, landed = _call(
        body, "ffn_mid_bwd", (B, Fd // fn),
        [pl.BlockSpec((S, D), lambda b, j: (b, 0)), pl.BlockSpec((fn, D), lambda b, j: (j, 0)), tspec, tspec, tspec,
         pl.BlockSpec((3, fn), lambda b, j: (0, j))],
        [tspec, tspec, pl.BlockSpec((1, 1, fn), lambda b, j: (b, 0, j)), pl.BlockSpec((1, 3, fn), lambda b, j: (b, 0, j))],
        [SDS((T, Fd), BF16), SDS((T, Fd), BF16), SDS((B, 1, Fd), F32), SDS((B, 3, Fd), F32)],
        (dfb, wd, g, ge, ud, cw), carry=carry)
    return (*outs, landed[0] if landed else None)


def ffn_dx_ln(dg, du, wgT, wuT, res, alpha, a, gamma, carry=None):
    T, Fd = dg.shape
    D = wgT.shape[1]
    tm = _tile(T, 256, 16)

    def body(dg_ref, du_ref, wg_ref, wu_ref, r_ref, a_ref, g_ref, da_ref, dab_ref, dgm_ref, dbt_ref):
        @pl.when(pl.program_id(0) == 0)
        def _():
            dgm_ref[...] = jnp.zeros_like(dgm_ref)
            dbt_ref[...] = jnp.zeros_like(dbt_ref)

        dh = alpha * r_ref[...] + _dot(dg_ref[...], wg_ref[...]) + _dot(du_ref[...], wu_ref[...])
        da, sg, sb = _ln_bwd_tile(dh, a_ref[...], g_ref[...])
        da_ref[...] = da
        dab_ref[...] = da.astype(BF16)
        dgm_ref[...] += sg
        dbt_ref[...] += sb

    a_spec = pl.BlockSpec((tm, Fd), lambda i: (i, 0))
    w_spec = pl.BlockSpec((Fd, D), lambda i: (0, 0))
    row = pl.BlockSpec((tm, D), lambda i: (i, 0))
    vec = pl.BlockSpec((1, D), lambda i: (0, 0))
    outs, landed = _call(body, "ffn_dx_ln", (T // tm,), [a_spec, a_spec, w_spec, w_spec, row, row, vec], [row, row, vec, vec],
                         [SDS((T, D), F32), SDS((T, D), BF16), SDS((1, D), F32), SDS((1, D), F32)],
                         (dg, du, wgT, wuT, res, a, gamma.reshape(1, D)), sem=("arbitrary",), carry=carry)
    return (*outs, landed[0] if landed else None)


def _partner_all(x):
    n = x.shape[0]
    r = lax.broadcasted_iota(jnp.int32, x.shape, 0)
    return jnp.where((r % HEAD_DIM) < HEAD_DIM // 2, pltpu.roll(x, n - HEAD_DIM // 2, 0), pltpu.roll(x, HEAD_DIM // 2, 0))


def proj_T(w, xT3, cosT, sinT, blk_off, rope, scale, name, carry=None):
    G, K, T = xT3.shape
    S = cosT.shape[2]
    Dout = K
    tt = _tile(S, 512, 128)
    H = Dout // HEAD_DIM
    nS = S // tt

    def body(w_ref, x_ref, c_ref, s_ref, o_ref):
        acc = _dot(w_ref[...], x_ref[0])
        if rope:
            cos = jnp.tile(c_ref[0], (H, 1))
            sin = jnp.tile(s_ref[0], (H, 1))
            acc = acc * cos + _partner_all(acc) * sin
        if scale != 1.0:
            acc = acc * scale
        o_ref[0] = acc.astype(BF16)

    tab = pl.BlockSpec((1, HEAD_DIM, tt), lambda g, j: (g, 0, j % nS))
    outs, landed = _call(
        body, name, (G, T // tt),
        [pl.BlockSpec((Dout, K), lambda g, j: (g + blk_off, 0)), pl.BlockSpec((1, K, tt), lambda g, j: (g, 0, j)), tab, tab],
        [pl.BlockSpec((1, Dout, tt), lambda g, j: (g, 0, j))], [SDS((G, Dout, T), BF16)], (w, xT3, cosT, sinT), carry=carry)
    return outs[0], landed


def _attn_bias():
    kj = lax.broadcasted_iota(jnp.int32, (2 * BLK, BLK), 0)
    qi = lax.broadcasted_iota(jnp.int32, (2 * BLK, BLK), 1)
    ok = ((kj >= BLK) & (kj - BLK <= qi)) | ((kj < BLK) & (kj >= qi))
    return jnp.where(ok, 0.0, NEG).astype(F32)


def _has_prev(g, S):
    nb = S // (DILATIONS[g] * BLK)
    return [(n % nb) != 0 for n in range(S // BLK)]


def _win(ref, n, hp):
    lo = (n - 1) * BLK if hp else n * BLK
    return ref[0, :, lo:(n + 1) * BLK]


def attn_fwd(qT3, kT3, vT3, bias, g, B, S):
    _, D, T = qT3.shape
    H = D // HEAD_DIM
    nblk = S // BLK
    hp = _has_prev(g, S)

    def body(q_ref, k_ref, v_ref, b_ref, o_ref, l_ref, s_scr, p_scr, rl_scr):
        for n in range(nblk):
            lo = 0 if hp[n] else BLK
            s_scr[n, lo:, :] = _dot(_win(k_ref, n, hp[n]), q_ref[0, :, n * BLK:(n + 1) * BLK], TN)
        for n in range(nblk):
            lo = 0 if hp[n] else BLK
            sT = s_scr[n, lo:, :] + b_ref[lo:, :]
            m = jnp.max(sT, axis=0, keepdims=True)
            p = jnp.exp(sT - m)
            l = jnp.sum(p, axis=0, keepdims=True)
            p_scr[n, lo:, :] = p.astype(BF16)
            rl_scr[n:n + 1, :] = 1.0 / l
            l_ref[0, :, n * BLK:(n + 1) * BLK] = m + jnp.log(l)
        for n in range(nblk):
            lo = 0 if hp[n] else BLK
            o_ref[:, n * BLK:(n + 1) * BLK] = _dot(_win(v_ref, n, hp[n]), p_scr[n, lo:, :]) * rl_scr[n:n + 1, :]

    spec = pl.BlockSpec((1, HEAD_DIM, S), lambda b, h: (g, h, b))
    return pl.pallas_call(
        body, name=f"attn_fwd_g{g}", grid=(B, H),
        in_specs=[spec, spec, spec, pl.BlockSpec((2 * BLK, BLK), lambda b, h: (0, 0))],
        out_specs=[pl.BlockSpec((HEAD_DIM, S), lambda b, h: (h, b)), pl.BlockSpec((1, 1, S), lambda b, h: (h, 0, b))],
        out_shape=[SDS((D, T), F32), SDS((H, 1, T), F32)],
        scratch_shapes=[pltpu.VMEM((nblk, 2 * BLK, BLK), F32), pltpu.VMEM((nblk, 2 * BLK, BLK), BF16),
                        pltpu.VMEM((nblk, BLK), F32)],
        compiler_params=_cp(),
    )(qT3, kT3, vT3, bias)


def attn_combine(oTs, lses):
    G = len(oTs)
    D, T = oTs[0].shape
    H = D // HEAD_DIM
    tn = _tile(T, 2048, 128)

    def body(*refs):
        o_refs, l_refs = refs[:G], refs[G:2 * G]
        ob_ref, of_ref, lt_ref = refs[2 * G:]
        ls = [r[0] for r in l_refs]
        m = functools.reduce(jnp.maximum, ls)
        es = [jnp.exp(v - m) for v in ls]
        z = functools.reduce(lambda a, b: a + b, es)
        o = (es[0] / z) * o_refs[0][...]
        for i in range(1, G):
            o = o + (es[i] / z) * o_refs[i][...]
        ob_ref[...] = o.astype(BF16)
        of_ref[...] = o
        lt_ref[0] = m + jnp.log(z)

    ospec = pl.BlockSpec((HEAD_DIM, tn), lambda h, j: (h, j))
    lspec = pl.BlockSpec((1, 1, tn), lambda h, j: (h, 0, j))
    return pl.pallas_call(
        body, name="attn_combine", grid=(H, T // tn),
        in_specs=[ospec] * G + [lspec] * G, out_specs=[ospec, ospec, lspec],
        out_shape=[SDS((D, T), BF16), SDS((D, T), F32), SDS((H, 1, T), F32)], compiler_params=_cp(),
    )(*oTs, *lses)


def attn_delta(doT, oT):
    D, T = doT.shape
    H = D // HEAD_DIM
    tn = _tile(T, 2048, 128)

    def body(d_ref, o_ref, r_ref):
        r_ref[0] = jnp.sum(d_ref[...].astype(F32) * o_ref[...], axis=0, keepdims=True)

    spec = pl.BlockSpec((HEAD_DIM, tn), lambda h, j: (h, j))
    return pl.pallas_call(
        body, name="attn_delta", grid=(H, T // tn), in_specs=[spec, spec],
        out_specs=pl.BlockSpec((1, 1, tn), lambda h, j: (h, 0, j)),
        out_shape=SDS((H, 1, T), F32), compiler_params=_cp(),
    )(doT, oT)


def attn_bwd(qT3, kT3, vT3, doT, lse, delta, cosT, sinT, bias, g, q_scale, B, S, dk_prev=None, dv_prev=None):
    _, D, T = qT3.shape
    H = D // HEAD_DIM
    nblk = S // BLK
    half = HEAD_DIM // 2
    hp = _has_prev(g, S)
    acc_in = dk_prev is not None
    kv_dtype = BF16 if acc_in else F32

    def body(*refs):
        q_ref, k_ref, v_ref, do_ref, l_ref, d_ref, c_ref, s_ref, b_ref = refs[:9]
        rest = refs[9:]
        if acc_in:
            dkp_ref, dvp_ref = rest[:2]
            rest = rest[2:]
        dq_ref, dk_ref, dv_ref, s_scr, dp_scr, p_scr, ds_scr = rest
        for n in range(nblk):
            lo = 0 if hp[n] else BLK
            blk = slice(n * BLK, (n + 1) * BLK)
            s_scr[n, lo:, :] = _dot(_win(k_ref, n, hp[n]), q_ref[0, :, blk], TN)
            dp_scr[n, lo:, :] = _dot(_win(v_ref, n, hp[n]), do_ref[:, blk], TN)
        for n in range(nblk):
            lo = 0 if hp[n] else BLK
            blk = slice(n * BLK, (n + 1) * BLK)
            pT = jnp.exp(s_scr[n, lo:, :] + b_ref[lo:, :] - l_ref[0, :, blk])
            p_scr[n, lo:, :] = pT.astype(BF16)
            ds_scr[n, lo:, :] = (pT * (dp_scr[n, lo:, :] - d_ref[0, :, blk])).astype(BF16)
        for j in range(nblk):
            blk = slice(j * BLK, (j + 1) * BLK)
            if j + 1 < nblk and hp[j + 1]:
                two = slice(j * BLK, (j + 2) * BLK)
                pj = jnp.concatenate([p_scr[j, BLK:, :], p_scr[j + 1, :BLK, :]], axis=1)
                dsj = jnp.concatenate([ds_scr[j, BLK:, :], ds_scr[j + 1, :BLK, :]], axis=1)
                dv = _dot(do_ref[:, two], pj, NT)
                dk = _dot(q_ref[0, :, two], dsj, NT)
            else:
                dv = _dot(do_ref[:, blk], p_scr[j, BLK:, :], NT)
                dk = _dot(q_ref[0, :, blk], ds_scr[j, BLK:, :], NT)
            dk = dk * c_ref[0, :, blk] - pltpu.roll(dk, half, 0) * s_ref[0, :, blk]
            if acc_in:
                dk = dk + dkp_ref[:, blk]
                dv = dv + dvp_ref[:, blk]
            dk_ref[:, blk] = dk.astype(kv_dtype)
            dv_ref[:, blk] = dv.astype(kv_dtype)
            lo = 0 if hp[j] else BLK
            dq = _dot(_win(k_ref, j, hp[j]), ds_scr[j, lo:, :])
            dq = dq * c_ref[0, :, blk] - pltpu.roll(dq, half, 0) * s_ref[0, :, blk]
            dq_ref[:, blk] = (dq * q_scale).astype(BF16)

    spec3 = pl.BlockSpec((1, HEAD_DIM, S), lambda b, h: (g, h, b))
    spec = pl.BlockSpec((HEAD_DIM, S), lambda b, h: (h, b))
    sspec = pl.BlockSpec((1, 1, S), lambda b, h: (h, 0, b))
    tab = pl.BlockSpec((1, HEAD_DIM, S), lambda b, h: (g, 0, 0))
    in_specs = [spec3, spec3, spec3, spec, sspec, sspec, tab, tab, pl.BlockSpec((2 * BLK, BLK), lambda b, h: (0, 0))]
    args = [qT3, kT3, vT3, doT, lse, delta, cosT, sinT, bias]
    if acc_in:
        in_specs += [spec, spec]
        args += [dk_prev, dv_prev]
    return pl.pallas_call(
        body, name=f"attn_bwd_g{g}" + ("_acc" if acc_in else ""), grid=(B, H),
        in_specs=in_specs, out_specs=[spec, spec, spec],
        out_shape=[SDS((D, T), BF16), SDS((D, T), kv_dtype), SDS((D, T), kv_dtype)],
        scratch_shapes=[pltpu.VMEM((nblk, 2 * BLK, BLK), F32), pltpu.VMEM((nblk, 2 * BLK, BLK), F32),
                        pltpu.VMEM((nblk, 2 * BLK, BLK), BF16), pltpu.VMEM((nblk, 2 * BLK, BLK), BF16)],
        compiler_params=_cp(),
    )(*args)


def adamw(w, g, m, v, name):
    R, C = w.shape
    tr = _tile(R, 512, 8)

    def body(w_ref, g_ref, m_ref, v_ref, d_ref, nm_ref, nv_ref):
        gv = g_ref[...]
        nm = ADAM_B1 * m_ref[...] + (1.0 - ADAM_B1) * gv
        nv = ADAM_B2 * v_ref[...] + (1.0 - ADAM_B2) * (gv * gv)
        m_hat = nm / (1.0 - ADAM_B1 ** ADAM_STEP)
        v_hat = nv / (1.0 - ADAM_B2 ** ADAM_STEP)
        d_ref[...] = -ADAM_LR * (m_hat / (jnp.sqrt(v_hat) + ADAM_EPS) + ADAM_WD * w_ref[...])
        nm_ref[...] = nm
        nv_ref[...] = nv

    spec = pl.BlockSpec((tr, C), lambda i: (i, 0))
    return pl.pallas_call(
        body, name=name, grid=(R // tr,), in_specs=[spec] * 4, out_specs=[spec] * 3,
        out_shape=[SDS((R, C), F32)] * 3, compiler_params=_cp(),
    )(w, g, m, v)


def _perm(a, B, S, d):
    if d == 1:
        return a
    lead = a.shape[:-1]
    return a.reshape(*lead, B, S // d, d).swapaxes(-1, -2).reshape(*lead, B * S)


def _unperm(a, B, S, d):
    if d == 1:
        return a
    lead = a.shape[:-1]
    return a.reshape(*lead, B, d, S // d).swapaxes(-1, -2).reshape(*lead, B * S)


def _perm3(a, B, S):
    return jnp.stack([_perm(a, B, S, d) for d in DILATIONS])


def _xT3(xb, B, S):
    D = xb.shape[1]
    outs = []
    for d in DILATIONS:
        outs.append(xb.reshape(B, S // d, d, D).transpose(3, 0, 2, 1).reshape(D, B * S))
    return jnp.stack(outs)


def _rope_tables(S):
    half = HEAD_DIM // 2
    inv_freq = ROPE_THETA ** (-jnp.arange(0, HEAD_DIM, 2, dtype=F32) / HEAD_DIM)
    ang = jnp.arange(S, dtype=F32)[:, None] * inv_freq[None, :]
    cos = jnp.concatenate([jnp.cos(ang), jnp.cos(ang)], axis=1).T
    sin = jnp.concatenate([-jnp.sin(ang), jnp.sin(ang)], axis=1).T
    return _perm3(cos, 1, S), _perm3(sin, 1, S)


def kernel(x, pool_w, pool_scale, w_q, w_kv, w_o, ffn_w_gate, ffn_w_up, ffn_conv_w, ffn_conv_b, ffn_w_down, ln1_g, ln1_b, ln2_g, ln2_b, loss_target, m_pool_w, m_pool_scale, m_w_q, m_w_kv, m_w_o, m_ffn_w_gate, m_ffn_w_up, m_ffn_conv_w, m_ffn_conv_b, m_ffn_w_down, m_ln1_g, m_ln1_b, m_ln2_g, m_ln2_b, v_pool_w, v_pool_scale, v_w_q, v_w_kv, v_w_o, v_ffn_w_gate, v_ffn_w_up, v_ffn_conv_w, v_ffn_conv_b, v_ffn_w_down, v_ln1_g, v_ln1_b, v_ln2_g, v_ln2_b):
    B, S, D = x.shape
    T = B * S
    depth = ln1_g.shape[0]
    nA, nB = pool_w.shape[0], w_q.shape[0]
    Fs = ffn_w_down.shape[1]
    Fd = Fs * N_DEV
    H = D // HEAD_DIM
    G = len(DILATIONS)
    PG = len(POOL_WINDOWS)
    Cg = D // PG
    alpha = (2.0 * depth) ** 0.25
    me = 4 * lax.axis_index("x") + 2 * lax.axis_index("y") + lax.axis_index("c")

    qs, kvs, os_ = w_q.shape[2], w_kv.shape[1], w_o.shape[1]
    pool_rows = pool_w.size // D
    local = {("pool",): pool_w.reshape(pool_rows, D).astype(BF16), ("wkv",): w_kv.T.astype(BF16)}
    for j in range(nB):
        local[("wq", j)] = w_q[j].T.astype(BF16)
        local[("wo", j)] = w_o[j].astype(BF16)
    for i in range(depth):
        local[("wg", i)] = ffn_w_gate[i].T.astype(BF16)
        local[("wu", i)] = ffn_w_up[i].T.astype(BF16)
        local[("wd", i)] = ffn_w_down[i].astype(BF16)
    ffn = lambda i: [("wg", i), ("wu", i), ("wd", i)]
    queue = [[("pool",)]]
    if depth == 4 and nA == 2 and nB == 2:
        queue += [[("wg", 0)], [("wu", 0)], [("wd", 0), ("wg", 1), ("wu", 1)], [("wd", 1)],
                  [("wo", 0), ("wo", 1)], [("wg", 2)], [("wkv",), ("wq", 0)], [("wu", 2)], [("wd", 2)], [("wq", 1)],
                  [], ffn(3)]
    gathered = {}

    def land(keys, arrs):
        for k, a in zip(keys or (), arrs or ()):
            gathered[k] = a.reshape(-1, D)

    def next_gather():
        if not queue:
            return None, None
        keys = queue.pop(0)
        if not keys:
            return None, None
        return keys, Gather([local[k] for k in keys])

    def weight(key):
        if key not in gathered:
            keys = [key]
            for bi, batch in enumerate(queue):
                if key in batch:
                    keys = queue.pop(bi)
                    break
            blk = all_gather_blocks(jnp.concatenate([local[k] for k in keys], axis=0), "gather_" + "_".join(map(str, key)), in_vmem=False)
            off = 0
            for k in keys:
                r = local[k].shape[0]
                gathered[k] = blk[:, off:off + r].reshape(-1, D)
                off += r
        return gathered[key]

    PW = weight(("pool",)).reshape(N_DEV, nA, PG, Cg // N_DEV, Cg).transpose(1, 2, 0, 3, 4).reshape(nA, PG, Cg, Cg)

    sm_cols = 128
    sm_local = jnp.concatenate([ffn_conv_w.reshape(-1), pool_scale.reshape(-1)])
    sm_rows = -(-sm_local.size // sm_cols)
    sm_rows_p = -(-sm_rows // 8) * 8
    sm_local = jnp.pad(sm_local, (0, sm_rows_p * sm_cols - sm_local.size)).reshape(sm_rows_p, sm_cols)
    sm = all_gather_blocks(sm_local, "gather_small", in_vmem=True).reshape(N_DEV, -1)
    ncw = ffn_conv_w.size
    conv_w_full = sm[:, :ncw].reshape(N_DEV, depth, 3, Fs).transpose(1, 2, 0, 3).reshape(depth, 3, Fd)
    pool_scale_full = sm[:, ncw:ncw + pool_scale.size].reshape(N_DEV, nA, D // N_DEV).transpose(1, 0, 2).reshape(nA, 1, D)

    cosT, sinT = _rope_tables(S)
    bias = _attn_bias()

    xs = x.reshape(T, D)
    saved = []
    cur, curb = xs, None
    kT = vT = x1T3 = None
    for i in range(depth):
        sv = {}
        if i < nA:
            keys, cr = next_gather()
            mix, pooled, got = pool_fwd(cur, PW[i], pool_scale_full[i], B, S, carry=cr)
            land(keys, got)
            sv["pooled"] = pooled
            keys, cr = next_gather()
            a1, h, hb, got = add_ln(cur, mix, ln1_g[i], ln1_b[i], alpha, carry=cr)
            land(keys, got)
        else:
            j = i - nA
            xT3 = x1T3 if j == 0 else _xT3(curb, B, S)
            keys, cr = next_gather()
            qT, got = proj_T(weight(("wq", j)), xT3, cosT, sinT, 0, True, HEAD_DIM ** -0.5, "q_proj", carry=cr)
            land(keys, got)
            oTs, lses = [], []
            for gi, d in enumerate(DILATIONS):
                o_g, lse_g = attn_fwd(qT, kT, vT, bias, gi, B, S)
                oTs.append(_unperm(o_g, B, S, d))
                lses.append(_unperm(lse_g, B, S, d))
            oTb, oTf, lse_tot = attn_combine(oTs, lses)
            a1, h, hb, _ = matmul_ln(oTb, weight(("wo", j)), TN, cur, ln1_g[i], ln1_b[i], alpha, "o_proj_ln")
            sv.update(xT3=xT3, qT=qT, oTb=oTb, oTf=oTf, lse_tot=lse_tot)
        wg_i, wu_i = weight(("wg", i)), weight(("wu", i))
        keys, cr = next_gather()
        g, ge, ud, hh, got = ffn_up(hb, wg_i, wu_i, conv_w_full[i], ffn_conv_b[i].reshape(1, Fd), B, S, carry=cr)
        land(keys, got)
        wd_i = weight(("wd", i))
        keys, cr = next_gather()
        a2, cur, curb, got = matmul_ln(hh, wd_i, NN, h, ln2_g[i], ln2_b[i], alpha, "ffn_down_ln", carry=cr)
        land(keys, got)
        sv.update(a1=a1, hb=hb, g=g, ge=ge, ud=ud, hh=hh, a2=a2)
        saved.append(sv)
        if i == nA - 1:
            x1T3 = _xT3(curb, B, S)
            wkv = weight(("wkv",))
            keys, cr = next_gather()
            kT, got = proj_T(wkv, x1T3, cosT, sinT, 0, True, 1.0, "k_proj", carry=cr)
            land(keys, got)
            keys, cr = next_gather()
            vT, got = proj_T(wkv, x1T3, cosT, sinT, G, False, 1.0, "v_proj", carry=cr)
            land(keys, got)


    small = {k: [None] * depth for k in ("ln1_g", "ln1_b", "ln2_g", "ln2_b", "conv_b", "conv_w")}
    dscale = [None] * nA
    dpw = [None] * nA
    dk_acc, dv_acc = [None] * G, [None] * G

    def blocks(a, rows):
        return a.reshape(N_DEV, rows, D)

    pending, landed = [], {}

    def next_carry():
        if not pending:
            return None, None
        key, parts = pending.pop(0)
        return key, Scatter(parts)

    dcur = sq = None
    for i in reversed(range(depth)):
        sv = saved[i]
        if i == depth - 1:
            db2, db2b, small["ln2_g"][i], small["ln2_b"][i], sq = loss_ln_bwd(cur, loss_target.reshape(T, D), sv["a2"], ln2_g[i])
        else:
            db2, db2b, small["ln2_g"][i], small["ln2_b"][i] = ln_bwd(dcur, sv["a2"], ln2_g[i])
        key, cr = next_carry()
        dg_, du_, dcb, dcw, got = ffn_mid_bwd(db2b, gathered[("wd", i)], sv["g"], sv["ge"], sv["ud"], conv_w_full[i], B, S, carry=cr)
        if cr is not None:
            landed[key] = got
        small["conv_b"][i] = jnp.sum(dcb, axis=0)
        small["conv_w"][i] = jnp.sum(dcw, axis=0)
        key, cr = next_carry()
        dwd, got = wgrad_rows(sv["hh"], db2b, "wgrad_down", carry=cr)
        if cr is not None:
            landed[key] = got
        dwg, landed[("down", i)] = wgrad_rows(dg_, sv["hb"], "wgrad_gate", carry=Scatter([blocks(dwd, Fs)]))
        dwu, landed[("gate", i)] = wgrad_rows(du_, sv["hb"], "wgrad_up", carry=Scatter([blocks(dwg, Fs)]))
        da1, da1b, small["ln1_g"][i], small["ln1_b"][i], landed[("up", i)] = ffn_dx_ln(
            dg_, du_, gathered[("wg", i)], gathered[("wu", i)], db2, alpha, sv["a1"], ln1_g[i], carry=Scatter([blocks(dwu, Fs)]))
        if i < nA:
            dcur, dsp, dpwp = pool_bwd(da1, sv["pooled"], PW[i], pool_scale_full[i], alpha, B, S)
            dscale[i] = jnp.sum(dsp, axis=0)
            dpw[i] = jnp.sum(dpwp, axis=0)
        else:
            j = i - nA
            doT = matmul_to_T(gathered[("wo", j)], da1b, "o_proj_bwd")
            dwo = wgrad_mixed(sv["oTb"], da1b, "wgrad_o")
            delta = attn_delta(doT, sv["oTf"])
            dq_tok, dwq = [], []
            for gi, d in enumerate(DILATIONS):
                dq_g, dk_acc[gi], dv_acc[gi] = attn_bwd(
                    sv["qT"], kT, vT, _perm(doT, B, S, d), _perm(sv["lse_tot"], B, S, d), _perm(delta, B, S, d),
                    cosT, sinT, bias, gi, HEAD_DIM ** -0.5, B, S, dk_prev=dk_acc[gi], dv_prev=dv_acc[gi])
                dwq.append(wgrad_T(dq_g, sv["xT3"], gi, "wgrad_q"))
                dq_tok.append(_unperm(dq_g, B, S, d))
            dwq = jnp.concatenate(dwq, axis=0)
            dcur = dx_from_T(dq_tok, gathered[("wq", j)], da1, alpha, "q_proj_bwd", 512)
            if j == 0:
                dkv = [a.astype(BF16) for a in dk_acc + dv_acc]
                dwkv = jnp.concatenate([wgrad_T(a, x1T3, gi % G, "wgrad_kv") for gi, a in enumerate(dkv)], axis=0)
                dkv_tok = [_unperm(a, B, S, DILATIONS[gi % G]) for gi, a in enumerate(dkv)]
                dcur = dx_from_T(dkv_tok, gathered[("wkv",)], dcur, 1.0, "kv_proj_bwd", 256)
                pending.append((("kv",), [blocks(dwkv, kvs)]))
            pending.append((("attn", j), [blocks(dwq, qs), blocks(dwo, os_)]))
    grad_x = dcur.reshape(B, S, D)

    dpw_all = jnp.stack(dpw).reshape(nA, PG, N_DEV, Cg // N_DEV, Cg).transpose(2, 0, 1, 3, 4).reshape(N_DEV, pool_rows, D)
    tail_keys = [k for k, _ in pending] + [("pool",)]
    tail_parts = [parts for _, parts in pending] + [[dpw_all.astype(BF16)]]
    tail_rows = [sum(p.shape[1] for p in parts) for parts in tail_parts]
    tail = scatter_partials([p for parts in tail_parts for p in parts], "scatter_tail")
    for t, key in enumerate(tail_keys):
        lo = sum(tail_rows[:t])
        landed[key] = tail[:, lo:lo + tail_rows[t]]

    def reduced(key):
        return sum_slots(landed[key], "sum_" + "_".join(str(k) for k in key))

    g_attn = [reduced(("attn", j)) for j in range(nB)]
    g_w_q = jnp.swapaxes(jnp.stack([a[:qs] for a in g_attn]), 1, 2)
    g_w_o = jnp.stack([a[qs:] for a in g_attn])
    g_w_kv = reduced(("kv",)).T
    g_gate = jnp.swapaxes(jnp.stack([reduced(("gate", i)) for i in range(depth)]), 1, 2)
    g_up = jnp.swapaxes(jnp.stack([reduced(("up", i)) for i in range(depth)]), 1, 2)
    g_down = jnp.stack([reduced(("down", i)) for i in range(depth)])
    g_pool_w = reduced(("pool",)).reshape(pool_w.shape)

    def rows_of(a):
        a = a.reshape(-1)
        n = -(-a.size // D) * D
        return jnp.pad(a, (0, n - a.size)).reshape(-1, D)

    sm_parts = [rows_of(jnp.concatenate(small[k], axis=0)) for k in ("ln1_g", "ln1_b", "ln2_g", "ln2_b")]
    sm_parts += [rows_of(jnp.stack(small["conv_b"])), rows_of(jnp.stack(small["conv_w"])), rows_of(jnp.stack(dscale)), sq]
    sm_sizes = [p.shape[0] for p in sm_parts]
    sm_all = jnp.concatenate(sm_parts, axis=0)
    pad_rows = -(-sm_all.shape[0] // 8) * 8 - sm_all.shape[0]
    sm_all = jnp.pad(sm_all, ((0, pad_rows), (0, 0)))
    sm_sum = sum_slots(all_gather_blocks(sm_all, "gather_small_grads", in_vmem=True), "sum_small_grads")
    sm_offs = [sum(sm_sizes[:i]) for i in range(len(sm_sizes))]

    def sm_take(i, shape):
        n = math.prod(shape)
        return sm_sum[sm_offs[i]:sm_offs[i] + sm_sizes[i]].reshape(-1)[:n].reshape(shape)

    g_ln1_g, g_ln1_b = sm_take(0, (depth, D)), sm_take(1, (depth, D))
    g_ln2_g, g_ln2_b = sm_take(2, (depth, D)), sm_take(3, (depth, D))
    g_conv_b = sm_take(4, (depth, Fd))
    g_conv_w = lax.dynamic_slice_in_dim(sm_take(5, (depth, 3, Fd)), me * Fs, Fs, axis=2)
    g_pool_scale = lax.dynamic_slice_in_dim(sm_take(6, (nA, D)), me * (D // N_DEV), D // N_DEV, axis=1)
    loss = (0.5 / D) * jnp.sum(sm_take(7, (D,)))

    def v2(a):
        return a.reshape(-1, a.shape[-1])

    names = ["pool_w", "pool_scale", "w_q", "w_kv", "w_o", "ffn_w_gate", "ffn_w_up", "ffn_conv_w", "ffn_conv_b",
             "ffn_w_down", "ln1_g", "ln1_b", "ln2_g", "ln2_b"]
    ws = [pool_w, pool_scale, w_q, w_kv, w_o, ffn_w_gate, ffn_w_up, ffn_conv_w, ffn_conv_b, ffn_w_down, ln1_g, ln1_b, ln2_g, ln2_b]
    ms = [m_pool_w, m_pool_scale, m_w_q, m_w_kv, m_w_o, m_ffn_w_gate, m_ffn_w_up, m_ffn_conv_w, m_ffn_conv_b, m_ffn_w_down, m_ln1_g, m_ln1_b, m_ln2_g, m_ln2_b]
    vs = [v_pool_w, v_pool_scale, v_w_q, v_w_kv, v_w_o, v_ffn_w_gate, v_ffn_w_up, v_ffn_conv_w, v_ffn_conv_b, v_ffn_w_down, v_ln1_g, v_ln1_b, v_ln2_g, v_ln2_b]
    gs = [g_pool_w, g_pool_scale, g_w_q, g_w_kv, g_w_o, g_gate, g_up, g_conv_w, g_conv_b, g_down, g_ln1_g, g_ln1_b, g_ln2_g, g_ln2_b]
    deltas, new_ms, new_vs = [], [], []
    for nm, w, gr, m_, v_ in zip(names, ws, gs, ms, vs):
        d_, nm_, nv_ = adamw(v2(w), v2(gr), v2(m_), v2(v_), "adamw_" + nm)
        deltas.append(d_.reshape(w.shape))
        new_ms.append(nm_.reshape(w.shape))
        new_vs.append(nv_.reshape(w.shape))

    return (loss, grad_x, *gs, *deltas, *new_ms, *new_vs)
```

```python
import functools
import math

import jax
import jax.numpy as jnp
from jax import lax
from jax.experimental import pallas as pl
from jax.experimental.pallas import tpu as pltpu

F32 = jnp.float32
BF16 = jnp.bfloat16
SDS = jax.ShapeDtypeStruct
MESH = pl.DeviceIdType.MESH

N_DEV = 8
HEAD_DIM = 64
BLK = 128
DILATIONS = (1, 4, 16)
POOL_WINDOWS = (2, 4, 8, 16)
ROPE_THETA = 10000.0
LN_EPS = 1e-5
NEG = -1e30
V7X_VMEM_LIMIT = 56 * 1024 * 1024

ADAM_LR, ADAM_B1, ADAM_B2, ADAM_EPS, ADAM_WD, ADAM_STEP = 0.001, 0.9, 0.999, 1e-08, 0.01, 10

NN = (((1,), (0,)), ((), ()))
NT = (((1,), (1,)), ((), ()))
TN = (((0,), (0,)), ((), ()))


def _cp(sem=None):
    kw = dict(vmem_limit_bytes=V7X_VMEM_LIMIT)
    if sem is not None:
        kw["dimension_semantics"] = sem
    return pltpu.CompilerParams(**kw)


def _dot(a, b, dims=NN):
    return lax.dot_general(a, b, dims, preferred_element_type=F32)


def _tile(n, target, mult):
    best = None
    for t in range(mult, min(n, target) + 1, mult):
        if n % t == 0:
            best = t
    return best if best is not None else n


def _mesh_pos():
    return lax.axis_index("x"), lax.axis_index("y"), lax.axis_index("c")


def all_gather_blocks(xl, name, in_vmem):
    R, C = xl.shape
    space = pltpu.VMEM if in_vmem else pl.ANY

    def body(x_ref, out_ref, send_sems, recv_sems, local_sem):
        x, y, c = _mesh_pos()
        me, sibling = (x, y, c), (x, y, 1 - c)
        chips = [(1 - x, y), (x, 1 - y), (1 - x, 1 - y)]

        def slot(px, py, pc):
            return out_ref.at[4 * px + 2 * py + pc]

        def copy(k, block, to, src=None):
            return pltpu.make_async_remote_copy(
                src_ref=slot(*block) if src is None else src, dst_ref=slot(*block),
                send_sem=send_sems.at[k], recv_sem=recv_sems.at[k], device_id=to, device_id_type=MESH)

        mine = pltpu.make_async_copy(x_ref, slot(*me), local_sem)
        mine.start()
        first = [copy(0, me, sibling, src=x_ref)]
        first += [copy(1 + j, me, (*chip, c), src=x_ref) for j, chip in enumerate(chips)]
        for cp in first:
            cp.start()
        passed = [copy(4 + j, (*chip, c), sibling) for j, chip in enumerate(chips)]
        for j, chip in enumerate(chips):
            copy(1 + j, (*chip, c), me).wait_recv()
            passed[j].start()
        copy(0, sibling, me).wait_recv()
        for j, chip in enumerate(chips):
            copy(4 + j, (*chip, 1 - c), me).wait_recv()
        for cp in first + passed:
            cp.wait_send()
        mine.wait()

    return pl.pallas_call(
        body, name=name,
        out_shape=SDS((N_DEV, R, C), xl.dtype),
        in_specs=[pl.BlockSpec(memory_space=space)],
        out_specs=pl.BlockSpec(memory_space=space),
        scratch_shapes=[pltpu.SemaphoreType.DMA((7,)), pltpu.SemaphoreType.DMA((7,)), pltpu.SemaphoreType.DMA],
        compiler_params=_cp(),
    )(xl)


def _peers():
    x, y, c = _mesh_pos()
    peers = []
    for r in range(1, N_DEV):
        peers.append((1 - x if (r & 4) else x, 1 - y if (r & 2) else y, 1 - c if (r & 1) else c))
    return 4 * x + 2 * y + c, peers


class Gather:
    def __init__(self, parts):
        self.parts = list(parts)
        n = len(self.parts)
        self.out_shapes = [SDS((N_DEV,) + p.shape, p.dtype) for p in self.parts]
        self.scratch = [pltpu.SemaphoreType.DMA((7 * n,)), pltpu.SemaphoreType.DMA((7 * n,)), pltpu.SemaphoreType.DMA((n,))]

    def start(self, part_refs, out_refs, send_sems, recv_sems, local_sems):
        me_lin, peers = _peers()
        n = len(self.parts)
        for i in range(n):
            pltpu.make_async_copy(part_refs[i], out_refs[i].at[me_lin], local_sems.at[i]).start()
        for k, peer in enumerate(peers):
            for i in range(n):
                pltpu.make_async_remote_copy(
                    src_ref=part_refs[i], dst_ref=out_refs[i].at[me_lin],
                    send_sem=send_sems.at[k * n + i], recv_sem=recv_sems.at[k * n + i],
                    device_id=peer, device_id_type=MESH).start()

    def wait(self, out_refs, send_sems, recv_sems, local_sems):
        me_lin, peers = _peers()
        n = len(self.parts)
        for k, (px, py, pc) in enumerate(peers):
            p_lin = 4 * px + 2 * py + pc
            for i in range(n):
                arrival = pltpu.make_async_remote_copy(
                    src_ref=out_refs[i].at[p_lin], dst_ref=out_refs[i].at[p_lin],
                    send_sem=send_sems.at[k * n + i], recv_sem=recv_sems.at[k * n + i],
                    device_id=(px, py, pc), device_id_type=MESH)
                arrival.wait_recv()
                arrival.wait_send()
        for i in range(n):
            pltpu.make_async_copy(out_refs[i].at[me_lin], out_refs[i].at[me_lin], local_sems.at[i]).wait()


class Scatter:
    def __init__(self, parts):
        self.parts = list(parts)
        self.rows = [p.shape[1] for p in parts]
        self.offs = [sum(self.rows[:i]) for i in range(len(self.rows))]
        self.out_shapes = [SDS((N_DEV, sum(self.rows), parts[0].shape[2]), parts[0].dtype)]
        self.scratch = [pltpu.SemaphoreType.DMA((7,)), pltpu.SemaphoreType.DMA((7,)), pltpu.SemaphoreType.DMA]

    def start(self, part_refs, out_refs, send_sems, recv_sems, local_sem):
        out_ref = out_refs[0]
        me_lin, peers = _peers()
        for i, (off, r) in enumerate(zip(self.offs, self.rows)):
            pltpu.make_async_copy(part_refs[i].at[me_lin], out_ref.at[me_lin, pl.ds(off, r)], local_sem).start()
        for k, (px, py, pc) in enumerate(peers):
            p_lin = 4 * px + 2 * py + pc
            for i, (off, r) in enumerate(zip(self.offs, self.rows)):
                pltpu.make_async_remote_copy(
                    src_ref=part_refs[i].at[p_lin], dst_ref=out_ref.at[me_lin, pl.ds(off, r)],
                    send_sem=send_sems.at[k], recv_sem=recv_sems.at[k],
                    device_id=(px, py, pc), device_id_type=MESH).start()

    def wait(self, out_refs, send_sems, recv_sems, local_sem):
        out_ref = out_refs[0]
        me_lin, peers = _peers()
        for k, (px, py, pc) in enumerate(peers):
            p_lin = 4 * px + 2 * py + pc
            whole = pltpu.make_async_remote_copy(
                src_ref=out_ref.at[p_lin], dst_ref=out_ref.at[p_lin],
                send_sem=send_sems.at[k], recv_sem=recv_sems.at[k],
                device_id=(px, py, pc), device_id_type=MESH)
            whole.wait_recv()
            whole.wait_send()
        pltpu.make_async_copy(out_ref.at[me_lin], out_ref.at[me_lin], local_sem).wait()


def scatter_partials(parts, name):
    sc = Scatter(parts)
    n = len(parts)

    def body(*refs):
        sc.start(refs[:n], refs[n:n + 1], *refs[n + 1:])
        sc.wait(refs[n:n + 1], *refs[n + 1:])

    return pl.pallas_call(
        body, name=name, out_shape=sc.out_shapes[0],
        in_specs=[pl.BlockSpec(memory_space=pl.ANY)] * n, out_specs=pl.BlockSpec(memory_space=pl.ANY),
        scratch_shapes=sc.scratch, compiler_params=_cp(),
    )(*parts)


def _call(body, name, grid, in_specs, out_specs, out_shape, args, scratch=(), sem=None, carry=None):
    in_specs, out_specs, out_shape, scratch = list(in_specs), list(out_specs), list(out_shape), list(scratch)
    if carry is None:
        outs = pl.pallas_call(body, name=name, grid=grid, in_specs=in_specs, out_specs=out_specs, out_shape=out_shape,
                              scratch_shapes=scratch, compiler_params=_cp(sem))(*args)
        return list(outs), None
    n_in, n_out, n_scr, n_c, n_co = len(in_specs), len(out_specs), len(scratch), len(carry.parts), len(carry.out_shapes)
    last = [g - 1 for g in grid]

    def carried(*refs):
        ins, c_ins = refs[:n_in], refs[n_in:n_in + n_c]
        o0 = n_in + n_c
        outs, c_out = refs[o0:o0 + n_out], refs[o0 + n_out:o0 + n_out + n_co]
        s0 = o0 + n_out + n_co
        scr, c_scr = refs[s0:s0 + n_scr], refs[s0 + n_scr:]
        ids = [pl.program_id(a) for a in range(len(grid))]
        is_first = functools.reduce(jnp.logical_and, [i == 0 for i in ids])
        is_last = functools.reduce(jnp.logical_and, [i == l for i, l in zip(ids, last)])

        @pl.when(is_first)
        def _():
            carry.start(c_ins, c_out, *c_scr)

        body(*ins, *outs, *scr)

        @pl.when(is_last)
        def _():
            carry.wait(c_out, *c_scr)

    hbm = pl.BlockSpec(memory_space=pl.ANY)
    outs = pl.pallas_call(
        carried, name=name + "_carry", grid=grid, in_specs=in_specs + [hbm] * n_c, out_specs=out_specs + [hbm] * n_co,
        out_shape=out_shape + carry.out_shapes, scratch_shapes=scratch + carry.scratch,
        compiler_params=_cp(sem if sem is not None else ("arbitrary",) * len(grid)),
    )(*args, *carry.parts)
    return list(outs[:n_out]), list(outs[n_out:])


def sum_slots(slots, name, out_dtype=F32):
    _, R, C = slots.shape
    tr = _tile(R, 512, 16)

    def body(s_ref, o_ref):
        acc = s_ref[0].astype(F32)
        for s in range(1, N_DEV):
            acc = acc + s_ref[s].astype(F32)
        o_ref[...] = acc.astype(out_dtype)

    return pl.pallas_call(
        body, name=name, grid=(R // tr,),
        in_specs=[pl.BlockSpec((N_DEV, tr, C), lambda i: (0, i, 0))],
        out_specs=pl.BlockSpec((tr, C), lambda i: (i, 0)),
        out_shape=SDS((R, C), out_dtype), compiler_params=_cp(),
    )(slots)


def add_ln(x, mix, g, b, alpha, carry=None):
    T, D = x.shape
    tm = _tile(T, 512, 16)

    def body(x_ref, m_ref, g_ref, b_ref, a_ref, y_ref, yb_ref):
        a = alpha * x_ref[...] + m_ref[...]
        mu = jnp.mean(a, axis=-1, keepdims=True)
        xc = a - mu
        var = jnp.mean(xc * xc, axis=-1, keepdims=True)
        y = xc * lax.rsqrt(var + LN_EPS) * g_ref[...] + b_ref[...]
        a_ref[...] = a
        y_ref[...] = y
        yb_ref[...] = y.astype(BF16)

    row = pl.BlockSpec((tm, D), lambda i: (i, 0))
    vec = pl.BlockSpec((1, D), lambda i: (0, 0))
    outs, landed = _call(body, "add_ln", (T // tm,), [row, row, vec, vec], [row, row, row],
                         [SDS((T, D), F32), SDS((T, D), F32), SDS((T, D), BF16)],
                         (x, mix, g.reshape(1, D), b.reshape(1, D)), carry=carry)
    return (*outs, landed)


def _ln_bwd_tile(dy, a, gamma):
    mu = jnp.mean(a, axis=-1, keepdims=True)
    xc = a - mu
    var = jnp.mean(xc * xc, axis=-1, keepdims=True)
    r = lax.rsqrt(var + LN_EPS)
    xh = xc * r
    dxh = dy * gamma
    m1 = jnp.mean(dxh, axis=-1, keepdims=True)
    m2 = jnp.mean(dxh * xh, axis=-1, keepdims=True)
    da = r * (dxh - m1 - xh * m2)
    return da, jnp.sum(dy * xh, axis=0, keepdims=True), jnp.sum(dy, axis=0, keepdims=True)


def ln_bwd(dy, a, g):
    T, D = a.shape
    tm = _tile(T, 512, 16)

    def body(dy_ref, a_ref, g_ref, da_ref, dab_ref, dg_ref, db_ref):
        @pl.when(pl.program_id(0) == 0)
        def _():
            dg_ref[...] = jnp.zeros_like(dg_ref)
            db_ref[...] = jnp.zeros_like(db_ref)

        da, sg, sb = _ln_bwd_tile(dy_ref[...], a_ref[...], g_ref[...])
        da_ref[...] = da
        dab_ref[...] = da.astype(BF16)
        dg_ref[...] += sg
        db_ref[...] += sb

    row = pl.BlockSpec((tm, D), lambda i: (i, 0))
    vec = pl.BlockSpec((1, D), lambda i: (0, 0))
    return pl.pallas_call(
        body, name="ln_bwd", grid=(T // tm,),
        in_specs=[row, row, vec], out_specs=[row, row, vec, vec],
        out_shape=[SDS((T, D), F32), SDS((T, D), BF16), SDS((1, D), F32), SDS((1, D), F32)],
        compiler_params=_cp(("arbitrary",)),
    )(dy, a, g.reshape(1, D))


def loss_ln_bwd(y, tgt, a, g):
    T, D = y.shape
    tm = _tile(T, 512, 16)

    def body(y_ref, t_ref, a_ref, g_ref, da_ref, dab_ref, dg_ref, db_ref, sq_ref):
        @pl.when(pl.program_id(0) == 0)
        def _():
            dg_ref[...] = jnp.zeros_like(dg_ref)
            db_ref[...] = jnp.zeros_like(db_ref)
            sq_ref[...] = jnp.zeros_like(sq_ref)

        e = y_ref[...] - t_ref[...]
        da, sg, sb = _ln_bwd_tile(e / float(D), a_ref[...], g_ref[...])
        da_ref[...] = da
        dab_ref[...] = da.astype(BF16)
        dg_ref[...] += sg
        db_ref[...] += sb
        sq_ref[...] += jnp.sum(e * e, axis=0, keepdims=True)

    row = pl.BlockSpec((tm, D), lambda i: (i, 0))
    vec = pl.BlockSpec((1, D), lambda i: (0, 0))
    return pl.pallas_call(
        body, name="loss_ln_bwd", grid=(T // tm,),
        in_specs=[row, row, row, vec], out_specs=[row, row, vec, vec, vec],
        out_shape=[SDS((T, D), F32), SDS((T, D), BF16), SDS((1, D), F32), SDS((1, D), F32), SDS((1, D), F32)],
        compiler_params=_cp(("arbitrary",)),
    )(y, tgt, a, g.reshape(1, D))


def matmul_ln(a, w, dims, res, g, b, alpha, name, carry=None):
    if dims == TN:
        K, T = a.shape
    else:
        T, K = a.shape
    D = w.shape[1]
    tm = _tile(T, 512, 128 if dims == TN else 16)

    def body(a_ref, w_ref, r_ref, g_ref, b_ref, p_ref, y_ref, yb_ref):
        pre = alpha * r_ref[...] + _dot(a_ref[...], w_ref[...], dims)
        mu = jnp.mean(pre, axis=-1, keepdims=True)
        xc = pre - mu
        var = jnp.mean(xc * xc, axis=-1, keepdims=True)
        y = xc * lax.rsqrt(var + LN_EPS) * g_ref[...] + b_ref[...]
        p_ref[...] = pre
        y_ref[...] = y
        yb_ref[...] = y.astype(BF16)

    a_spec = pl.BlockSpec((K, tm), lambda i: (0, i)) if dims == TN else pl.BlockSpec((tm, K), lambda i: (i, 0))
    row = pl.BlockSpec((tm, D), lambda i: (i, 0))
    vec = pl.BlockSpec((1, D), lambda i: (0, 0))
    outs, landed = _call(
        body, name, (T // tm,), [a_spec, pl.BlockSpec(w.shape, lambda i: (0, 0)), row, vec, vec], [row, row, row],
        [SDS((T, D), F32), SDS((T, D), F32), SDS((T, D), BF16)], (a, w, res, g.reshape(1, D), b.reshape(1, D)), carry=carry)
    return (*outs, landed)


def dx_from_T(aTs, w, res, alpha, name, tm_target, ln=None):
    T = aTs[0].shape[1]
    N = w.shape[1]
    ks = [a.shape[0] for a in aTs]
    n = len(aTs)
    tm = _tile(T, tm_target, 128)

    def body(*refs):
        a_refs, w_ref, r_ref = refs[:n], refs[n], refs[n + 1]
        acc = alpha * r_ref[...]
        off = 0
        for a_ref, k in zip(a_refs, ks):
            acc = acc + _dot(a_ref[...], w_ref[off:off + k, :], TN)
            off += k
        if ln is None:
            refs[n + 2][...] = acc
            return
        ln_a, ln_g, da_ref, dab_ref, dgm_ref, dbt_ref = refs[n + 2:]

        @pl.when(pl.program_id(0) == 0)
        def _():
            dgm_ref[...] = jnp.zeros_like(dgm_ref)
            dbt_ref[...] = jnp.zeros_like(dbt_ref)

        da, sg, sb = _ln_bwd_tile(acc, ln_a[...], ln_g[...])
        da_ref[...] = da
        dab_ref[...] = da.astype(BF16)
        dgm_ref[...] += sg
        dbt_ref[...] += sb

    row = pl.BlockSpec((tm, N), lambda i: (i, 0))
    vec = pl.BlockSpec((1, N), lambda i: (0, 0))
    in_specs = [pl.BlockSpec((k, tm), lambda i: (0, i)) for k in ks] + [pl.BlockSpec(w.shape, lambda i: (0, 0)), row]
    args = list(aTs) + [w, res]
    if ln is None:
        return pl.pallas_call(body, name=name, grid=(T // tm,), in_specs=in_specs, out_specs=row,
                              out_shape=SDS((T, N), F32), compiler_params=_cp())(*args)
    return pl.pallas_call(
        body, name=name + "_ln", grid=(T // tm,), in_specs=in_specs + [row, vec], out_specs=[row, row, vec, vec],
        out_shape=[SDS((T, N), F32), SDS((T, N), BF16), SDS((1, N), F32), SDS((1, N), F32)],
        compiler_params=_cp(("arbitrary",)),
    )(*args, ln[0], ln[1].reshape(1, N))


def matmul_to_T(w, a, name):
    M, K = w.shape
    T = a.shape[0]
    tt = _tile(T, 512, 128)

    def body(w_ref, a_ref, o_ref):
        o_ref[...] = _dot(w_ref[...], a_ref[...], NT).astype(BF16)

    return pl.pallas_call(
        body, name=name, grid=(T // tt,),
        in_specs=[pl.BlockSpec((M, K), lambda i: (0, 0)), pl.BlockSpec((tt, K), lambda i: (i, 0))],
        out_specs=pl.BlockSpec((M, tt), lambda i: (0, i)),
        out_shape=SDS((M, T), BF16), compiler_params=_cp(),
    )(w, a)


def wgrad_rows(a, b, name, carry=None):
    T, M = a.shape
    N = b.shape[1]
    tt = _tile(T, 512, 16)
    tmm = _tile(M, 1536, 128)
    nt = T // tt

    def body(a_ref, b_ref, o_ref, acc_ref):
        t = pl.program_id(1)

        @pl.when(t == 0)
        def _():
            acc_ref[...] = jnp.zeros_like(acc_ref)

        acc_ref[...] += _dot(a_ref[...], b_ref[...], TN)

        @pl.when(t == nt - 1)
        def _():
            o_ref[...] = acc_ref[...].astype(BF16)

    outs, landed = _call(
        body, name, (M // tmm, nt),
        [pl.BlockSpec((tt, tmm), lambda i, t: (t, i)), pl.BlockSpec((tt, N), lambda i, t: (t, 0))],
        [pl.BlockSpec((tmm, N), lambda i, t: (i, 0))], [SDS((M, N), BF16)], (a, b),
        scratch=[pltpu.VMEM((tmm, N), F32)], sem=("arbitrary", "arbitrary"), carry=carry)
    return outs[0], (landed[0] if landed else None)


def wgrad_T(aT, bT3, g, name):
    M, T = aT.shape
    N = bT3.shape[1]
    tt = _tile(T, 1024, 128)
    nt = T // tt

    def body(a_ref, b_ref, o_ref, acc_ref):
        t = pl.program_id(0)

        @pl.when(t == 0)
        def _():
            acc_ref[...] = jnp.zeros_like(acc_ref)

        acc_ref[...] += _dot(a_ref[...], b_ref[0], NT)

        @pl.when(t == nt - 1)
        def _():
            o_ref[...] = acc_ref[...].astype(BF16)

    return pl.pallas_call(
        body, name=name, grid=(nt,),
        in_specs=[pl.BlockSpec((M, tt), lambda t: (0, t)), pl.BlockSpec((1, N, tt), lambda t: (g, 0, t))],
        out_specs=pl.BlockSpec((M, N), lambda t: (0, 0)),
        out_shape=SDS((M, N), BF16), scratch_shapes=[pltpu.VMEM((M, N), F32)],
        compiler_params=_cp(("arbitrary",)),
    )(aT, bT3)


def wgrad_mixed(aT, b, name):
    M, T = aT.shape
    N = b.shape[1]
    tt = _tile(T, 1024, 128)
    nt = T // tt

    def body(a_ref, b_ref, o_ref, acc_ref):
        t = pl.program_id(0)

        @pl.when(t == 0)
        def _():
            acc_ref[...] = jnp.zeros_like(acc_ref)

        acc_ref[...] += _dot(a_ref[...], b_ref[...], NN)

        @pl.when(t == nt - 1)
        def _():
            o_ref[...] = acc_ref[...].astype(BF16)

    return pl.pallas_call(
        body, name=name, grid=(nt,),
        in_specs=[pl.BlockSpec((M, tt), lambda t: (0, t)), pl.BlockSpec((tt, N), lambda t: (t, 0))],
        out_specs=pl.BlockSpec((M, N), lambda t: (0, 0)),
        out_shape=SDS((M, N), BF16), scratch_shapes=[pltpu.VMEM((M, N), F32)],
        compiler_params=_cp(("arbitrary",)),
    )(aT, b)


def _shift_down(x, k, rows):
    return jnp.where(rows >= k, pltpu.roll(x, k, 0), 0.0)


def _shift_up(x, k, rows):
    n = x.shape[0]
    return jnp.where(rows < n - k, pltpu.roll(x, n - k, 0), 0.0)


def _pick(g, vals):
    out = vals[-1]
    for k in range(len(vals) - 2, -1, -1):
        out = jnp.where(g == k, vals[k], out)
    return out


def pool_fwd(x, pw, scale, B, S, carry=None):
    T, D = x.shape
    G = len(POOL_WINDOWS)
    Cg = D // G

    def body(x_ref, w_ref, s_ref, mix_ref, pooled_ref):
        g = pl.program_id(1)
        xv = x_ref[...]
        rows = lax.broadcasted_iota(jnp.int32, xv.shape, 0)
        sums, cur, k = [], xv, 1
        for _ in POOL_WINDOWS:
            cur = cur + _shift_down(cur, k, rows)
            sums.append(cur)
            k *= 2
        win = 2 * lax.shift_left(jnp.int32(1), g)
        total = _pick(g, sums)
        count = jnp.minimum(rows + 1, win).astype(F32)
        pooled = total / count - xv
        pb = pooled.astype(BF16)
        pooled_ref[...] = pb
        mix_ref[...] = _dot(pb, w_ref[0]) * s_ref[...]

    blk = pl.BlockSpec((S, Cg), lambda b, g: (b, g))
    outs, landed = _call(
        body, "pool_fwd", (B, G),
        [blk, pl.BlockSpec((1, Cg, Cg), lambda b, g: (g, 0, 0)), pl.BlockSpec((1, Cg), lambda b, g: (0, g))],
        [blk, blk], [SDS((T, D), F32), SDS((T, D), BF16)], (x, pw, scale), carry=carry)
    return (*outs, landed)


def pool_bwd(dmix, pooled, pw, scale, alpha, B, S):
    T, D = dmix.shape
    G = len(POOL_WINDOWS)
    Cg = D // G

    def body(d_ref, p_ref, w_ref, s_ref, dx_ref, ds_ref, dw_ref):
        g = pl.program_id(1)
        dm = d_ref[...]
        pb = p_ref[...]
        w = w_ref[0]
        ypre = _dot(pb, w)
        ds_ref[0] = jnp.sum(dm * ypre, axis=0, keepdims=True)
        dy = (dm * s_ref[...]).astype(BF16)
        dpool = _dot(dy, w, NT)
        dw_ref[0, 0] = _dot(pb, dy, TN)
        rows = lax.broadcasted_iota(jnp.int32, dm.shape, 0)
        win = 2 * lax.shift_left(jnp.int32(1), g)
        count = jnp.minimum(rows + 1, win).astype(F32)
        cur, k, sums = dpool / count, 1, []
        for _ in POOL_WINDOWS:
            cur = cur + _shift_up(cur, k, rows)
            sums.append(cur)
            k *= 2
        dx_ref[...] = alpha * dm + _pick(g, sums) - dpool

    blk = pl.BlockSpec((S, Cg), lambda b, g: (b, g))
    return pl.pallas_call(
        body, name="pool_bwd", grid=(B, G),
        in_specs=[blk, blk, pl.BlockSpec((1, Cg, Cg), lambda b, g: (g, 0, 0)), pl.BlockSpec((1, Cg), lambda b, g: (0, g))],
        out_specs=[blk, pl.BlockSpec((1, 1, Cg), lambda b, g: (b, 0, g)),
                   pl.BlockSpec((1, 1, Cg, Cg), lambda b, g: (b, g, 0, 0))],
        out_shape=[SDS((T, D), F32), SDS((B, 1, D), F32), SDS((B, G, Cg, Cg), F32)], compiler_params=_cp(),
    )(dmix, pooled, pw, scale)


_GELU_K = math.sqrt(2.0 / math.pi)
_GELU_C = 0.044715
FFN_ROWS = 512
FFN_HALO = 16


def ffn_up(hb, wgT, wuT, cw, cb, B, S, carry=None):
    T, D = hb.shape
    Fd = wgT.shape[0]
    fn = _tile(Fd, 256, 128)

    nc = S // _tile(S, FFN_ROWS, FFN_HALO)
    rc = S // nc

    def body(h_ref, wg_ref, wu_ref, cw_ref, cb_ref, g_ref, ge_ref, ud_ref, hh_ref):
        wg, wu, cw, cb = wg_ref[...], wu_ref[...], cw_ref[...], cb_ref[...]
        halo = jnp.zeros((FFN_HALO, fn), F32)
        for ci in range(nc):
            rows = slice(ci * rc, (ci + 1) * rc)
            h = h_ref[rows, :]
            g = _dot(h, wg, NT)
            u = _dot(h, wu, NT)
            gext = jnp.concatenate([halo, g], axis=0)
            halo = g[rc - FFN_HALO:, :]
            c = cb + cw[0:1] * pltpu.roll(gext, 2, 0)[FFN_HALO:, :] + cw[1:2] * pltpu.roll(gext, 1, 0)[FFN_HALO:, :] + cw[2:3] * g
            c2 = c * c
            th = jnp.tanh(c * (_GELU_K + (_GELU_K * _GELU_C) * c2))
            cdf = 0.5 * th + 0.5
            ge = c * cdf
            dgelu = cdf + (c * (1.0 - th * th)) * (0.5 * _GELU_K + (1.5 * _GELU_K * _GELU_C) * c2)
            g_ref[rows, :] = g.astype(BF16)
            ge_ref[rows, :] = ge.astype(BF16)
            ud_ref[rows, :] = (u * dgelu).astype(BF16)
            hh_ref[rows, :] = (ge * u).astype(BF16)

    hspec = pl.BlockSpec((S, D), lambda b, j: (b, 0))
    wspec = pl.BlockSpec((fn, D), lambda b, j: (j, 0))
    ospec = pl.BlockSpec((S, fn), lambda b, j: (b, j))
    outs, landed = _call(
        body, "ffn_up", (B, Fd // fn),
        [hspec, wspec, wspec, pl.BlockSpec((3, fn), lambda b, j: (0, j)), pl.BlockSpec((1, fn), lambda b, j: (0, j))],
        [ospec] * 4, [SDS((T, Fd), BF16)] * 4, (hb, wgT, wuT, cw, cb), carry=carry)
    return (*outs, landed)


def ffn_mid_bwd(dfb, wd, g, ge, ud, cw, B, S, carry=None):
    T, D = dfb.shape
    Fd = wd.shape[0]
    fn = _tile(Fd, 256, 128)

    def body(df_ref, wd_ref, g_ref, ge_ref, ud_ref, cw_ref, dg_ref, du_ref, dcb_ref, dcw_ref):
        dhh = _dot(df_ref[...], wd_ref[...], NT)
        gv = g_ref[...].astype(F32)
        cw = cw_ref[...]
        rows = lax.broadcasted_iota(jnp.int32, gv.shape, 0)
        g1 = _shift_down(gv, 1, rows)
        g2 = _shift_down(gv, 2, rows)
        du_ref[...] = (dhh * ge_ref[...].astype(F32)).astype(BF16)
        dc = dhh * ud_ref[...].astype(F32)
        dcb_ref[0] = jnp.sum(dc, axis=0, keepdims=True)
        dcw_ref[0] = jnp.concatenate(
            [jnp.sum(dc * g2, axis=0, keepdims=True), jnp.sum(dc * g1, axis=0, keepdims=True),
             jnp.sum(dc * gv, axis=0, keepdims=True)], axis=0)
        dg = cw[2:3] * dc + cw[1:2] * _shift_up(dc, 1, rows) + cw[0:1] * _shift_up(dc, 2, rows)
        dg_ref[...] = dg.astype(BF16)

    tspec = pl.BlockSpec((S, fn), lambda b, j: (b, j))
    outs, landed = _call(
        body, "ffn_mid_bwd", (B, Fd // fn),
        [pl.BlockSpec((S, D), lambda b, j: (b, 0)), pl.BlockSpec((fn, D), lambda b, j: (j, 0)), tspec, tspec, tspec,
         pl.BlockSpec((3, fn), lambda b, j: (0, j))],
        [tspec, tspec, pl.BlockSpec((1, 1, fn), lambda b, j: (b, 0, j)), pl.BlockSpec((1, 3, fn), lambda b, j: (b, 0, j))],
        [SDS((T, Fd), BF16), SDS((T, Fd), BF16), SDS((B, 1, Fd), F32), SDS((B, 3, Fd), F32)],
        (dfb, wd, g, ge, ud, cw), carry=carry)
    return (*outs, landed[0] if landed else None)


def ffn_dx_ln(dg, du, wgT, wuT, res, alpha, a, gamma, carry=None):
    T, Fd = dg.shape
    D = wgT.shape[1]
    tm = _tile(T, 256, 16)

    def body(dg_ref, du_ref, wg_ref, wu_ref, r_ref, a_ref, g_ref, da_ref, dab_ref, dgm_ref, dbt_ref):
        @pl.when(pl.program_id(0) == 0)
        def _():
            dgm_ref[...] = jnp.zeros_like(dgm_ref)
            dbt_ref[...] = jnp.zeros_like(dbt_ref)

        dh = alpha * r_ref[...] + _dot(dg_ref[...], wg_ref[...]) + _dot(du_ref[...], wu_ref[...])
        da, sg, sb = _ln_bwd_tile(dh, a_ref[...], g_ref[...])
        da_ref[...] = da
        dab_ref[...] = da.astype(BF16)
        dgm_ref[...] += sg
        dbt_ref[...] += sb

    a_spec = pl.BlockSpec((tm, Fd), lambda i: (i, 0))
    w_spec = pl.BlockSpec((Fd, D), lambda i: (0, 0))
    row = pl.BlockSpec((tm, D), lambda i: (i, 0))
    vec = pl.BlockSpec((1, D), lambda i: (0, 0))
    outs, landed = _call(body, "ffn_dx_ln", (T // tm,), [a_spec, a_spec, w_spec, w_spec, row, row, vec], [row, row, vec, vec],
                         [SDS((T, D), F32), SDS((T, D), BF16), SDS((1, D), F32), SDS((1, D), F32)],
                         (dg, du, wgT, wuT, res, a, gamma.reshape(1, D)), sem=("arbitrary",), carry=carry)
    return (*outs, landed[0] if landed else None)


def _partner_all(x):
    n = x.shape[0]
    r = lax.broadcasted_iota(jnp.int32, x.shape, 0)
    return jnp.where((r % HEAD_DIM) < HEAD_DIM // 2, pltpu.roll(x, n - HEAD_DIM // 2, 0), pltpu.roll(x, HEAD_DIM // 2, 0))


def proj_T(w, xT3, cosT, sinT, blk_off, rope, scale, name, carry=None):
    G, K, T = xT3.shape
    S = cosT.shape[2]
    Dout = K
    tt = _tile(S, 512, 128)
    H = Dout // HEAD_DIM
    nS = S // tt

    def body(w_ref, x_ref, c_ref, s_ref, o_ref):
        acc = _dot(w_ref[...], x_ref[0])
        if rope:
            cos = jnp.tile(c_ref[0], (H, 1))
            sin = jnp.tile(s_ref[0], (H, 1))
            acc = acc * cos + _partner_all(acc) * sin
        if scale != 1.0:
            acc = acc * scale
        o_ref[0] = acc.astype(BF16)

    tab = pl.BlockSpec((1, HEAD_DIM, tt), lambda g, j: (g, 0, j % nS))
    outs, landed = _call(
        body, name, (G, T // tt),
        [pl.BlockSpec((Dout, K), lambda g, j: (g + blk_off, 0)), pl.BlockSpec((1, K, tt), lambda g, j: (g, 0, j)), tab, tab],
        [pl.BlockSpec((1, Dout, tt), lambda g, j: (g, 0, j))], [SDS((G, Dout, T), BF16)], (w, xT3, cosT, sinT), carry=carry)
    return outs[0], landed


def _attn_bias():
    kj = lax.broadcasted_iota(jnp.int32, (2 * BLK, BLK), 0)
    qi = lax.broadcasted_iota(jnp.int32, (2 * BLK, BLK), 1)
    ok = ((kj >= BLK) & (kj - BLK <= qi)) | ((kj < BLK) & (kj >= qi))
    return jnp.where(ok, 0.0, NEG).astype(F32)


def _has_prev(g, S):
    nb = S // (DILATIONS[g] * BLK)
    return [(n % nb) != 0 for n in range(S // BLK)]


def _win(ref, n, hp):
    lo = (n - 1) * BLK if hp else n * BLK
    return ref[0, :, lo:(n + 1) * BLK]


def attn_fwd(qT3, kT3, vT3, bias, g, B, S):
    _, D, T = qT3.shape
    H = D // HEAD_DIM
    nblk = S // BLK
    hp = _has_prev(g, S)

    def body(q_ref, k_ref, v_ref, b_ref, o_ref, l_ref, s_scr, p_scr, rl_scr):
        for n in range(nblk):
            lo = 0 if hp[n] else BLK
            s_scr[n, lo:, :] = _dot(_win(k_ref, n, hp[n]), q_ref[0, :, n * BLK:(n + 1) * BLK], TN)
        for n in range(nblk):
            lo = 0 if hp[n] else BLK
            sT = s_scr[n, lo:, :] + b_ref[lo:, :]
            m = jnp.max(sT, axis=0, keepdims=True)
            p = jnp.exp(sT - m)
            l = jnp.sum(p, axis=0, keepdims=True)
            p_scr[n, lo:, :] = p.astype(BF16)
            rl_scr[n:n + 1, :] = 1.0 / l
            l_ref[0, :, n * BLK:(n + 1) * BLK] = m + jnp.log(l)
        for n in range(nblk):
            lo = 0 if hp[n] else BLK
            o_ref[:, n * BLK:(n + 1) * BLK] = _dot(_win(v_ref, n, hp[n]), p_scr[n, lo:, :]) * rl_scr[n:n + 1, :]

    spec = pl.BlockSpec((1, HEAD_DIM, S), lambda b, h: (g, h, b))
    return pl.pallas_call(
        body, name=f"attn_fwd_g{g}", grid=(B, H),
        in_specs=[spec, spec, spec, pl.BlockSpec((2 * BLK, BLK), lambda b, h: (0, 0))],
        out_specs=[pl.BlockSpec((HEAD_DIM, S), lambda b, h: (h, b)), pl.BlockSpec((1, 1, S), lambda b, h: (h, 0, b))],
        out_shape=[SDS((D, T), F32), SDS((H, 1, T), F32)],
        scratch_shapes=[pltpu.VMEM((nblk, 2 * BLK, BLK), F32), pltpu.VMEM((nblk, 2 * BLK, BLK), BF16),
                        pltpu.VMEM((nblk, BLK), F32)],
        compiler_params=_cp(),
    )(qT3, kT3, vT3, bias)


def attn_combine(oTs, lses):
    G = len(oTs)
    D, T = oTs[0].shape
    H = D // HEAD_DIM
    tn = _tile(T, 2048, 128)

    def body(*refs):
        o_refs, l_refs = refs[:G], refs[G:2 * G]
        ob_ref, of_ref, lt_ref = refs[2 * G:]
        ls = [r[0] for r in l_refs]
        m = functools.reduce(jnp.maximum, ls)
        es = [jnp.exp(v - m) for v in ls]
        z = functools.reduce(lambda a, b: a + b, es)
        o = (es[0] / z) * o_refs[0][...]
        for i in range(1, G):
            o = o + (es[i] / z) * o_refs[i][...]
        ob_ref[...] = o.astype(BF16)
        of_ref[...] = o
        lt_ref[0] = m + jnp.log(z)

    ospec = pl.BlockSpec((HEAD_DIM, tn), lambda h, j: (h, j))
    lspec = pl.BlockSpec((1, 1, tn), lambda h, j: (h, 0, j))
    return pl.pallas_call(
        body, name="attn_combine", grid=(H, T // tn),
        in_specs=[ospec] * G + [lspec] * G, out_specs=[ospec, ospec, lspec],
        out_shape=[SDS((D, T), BF16), SDS((D, T), F32), SDS((H, 1, T), F32)], compiler_params=_cp(),
    )(*oTs, *lses)


def attn_delta(doT, oT):
    D, T = doT.shape
    H = D // HEAD_DIM
    tn = _tile(T, 2048, 128)

    def body(d_ref, o_ref, r_ref):
        r_ref[0] = jnp.sum(d_ref[...].astype(F32) * o_ref[...], axis=0, keepdims=True)

    spec = pl.BlockSpec((HEAD_DIM, tn), lambda h, j: (h, j))
    return pl.pallas_call(
        body, name="attn_delta", grid=(H, T // tn), in_specs=[spec, spec],
        out_specs=pl.BlockSpec((1, 1, tn), lambda h, j: (h, 0, j)),
        out_shape=SDS((H, 1, T), F32), compiler_params=_cp(),
    )(doT, oT)


def attn_bwd(qT3, kT3, vT3, doT, lse, delta, cosT, sinT, bias, g, q_scale, B, S, dk_prev=None, dv_prev=None):
    _, D, T = qT3.shape
    H = D // HEAD_DIM
    nblk = S // BLK
    half = HEAD_DIM // 2
    hp = _has_prev(g, S)
    acc_in = dk_prev is not None
    kv_dtype = BF16 if acc_in else F32

    def body(*refs):
        q_ref, k_ref, v_ref, do_ref, l_ref, d_ref, c_ref, s_ref, b_ref = refs[:9]
        rest = refs[9:]
        if acc_in:
            dkp_ref, dvp_ref = rest[:2]
            rest = rest[2:]
        dq_ref, dk_ref, dv_ref, s_scr, dp_scr, p_scr, ds_scr = rest
        for n in range(nblk):
            lo = 0 if hp[n] else BLK
            blk = slice(n * BLK, (n + 1) * BLK)
            s_scr[n, lo:, :] = _dot(_win(k_ref, n, hp[n]), q_ref[0, :, blk], TN)
            dp_scr[n, lo:, :] = _dot(_win(v_ref, n, hp[n]), do_ref[:, blk], TN)
        for n in range(nblk):
            lo = 0 if hp[n] else BLK
            blk = slice(n * BLK, (n + 1) * BLK)
            pT = jnp.exp(s_scr[n, lo:, :] + b_ref[lo:, :] - l_ref[0, :, blk])
            p_scr[n, lo:, :] = pT.astype(BF16)
            ds_scr[n, lo:, :] = (pT * (dp_scr[n, lo:, :] - d_ref[0, :, blk])).astype(BF16)
        for j in range(nblk):
            blk = slice(j * BLK, (j + 1) * BLK)
            if j + 1 < nblk and hp[j + 1]:
                two = slice(j * BLK, (j + 2) * BLK)
                pj = jnp.concatenate([p_scr[j, BLK:, :], p_scr[j + 1, :BLK, :]], axis=1)
                dsj = jnp.concatenate([ds_scr[j, BLK:, :], ds_scr[j + 1, :BLK, :]], axis=1)
                dv = _dot(do_ref[:, two], pj, NT)
                dk = _dot(q_ref[0, :, two], dsj, NT)
            else:
                dv = _dot(do_ref[:, blk], p_scr[j, BLK:, :], NT)
                dk = _dot(q_ref[0, :, blk], ds_scr[j, BLK:, :], NT)
            dk = dk * c_ref[0, :, blk] - pltpu.roll(dk, half, 0) * s_ref[0, :, blk]
            if acc_in:
                dk = dk + dkp_ref[:, blk]
                dv = dv + dvp_ref[:, blk]
            dk_ref[:, blk] = dk.astype(kv_dtype)
            dv_ref[:, blk] = dv.astype(kv_dtype)
            lo = 0 if hp[j] else BLK
            dq = _dot(_win(k_ref, j, hp[j]), ds_scr[j, lo:, :])
            dq = dq * c_ref[0, :, blk] - pltpu.roll(dq, half, 0) * s_ref[0, :, blk]
            dq_ref[:, blk] = (dq * q_scale).astype(BF16)

    spec3 = pl.BlockSpec((1, HEAD_DIM, S), lambda b, h: (g, h, b))
    spec = pl.BlockSpec((HEAD_DIM, S), lambda b, h: (h, b))
    sspec = pl.BlockSpec((1, 1, S), lambda b, h: (h, 0, b))
    tab = pl.BlockSpec((1, HEAD_DIM, S), lambda b, h: (g, 0, 0))
    in_specs = [spec3, spec3, spec3, spec, sspec, sspec, tab, tab, pl.BlockSpec((2 * BLK, BLK), lambda b, h: (0, 0))]
    args = [qT3, kT3, vT3, doT, lse, delta, cosT, sinT, bias]
    if acc_in:
        in_specs += [spec, spec]
        args += [dk_prev, dv_prev]
    return pl.pallas_call(
        body, name=f"attn_bwd_g{g}" + ("_acc" if acc_in else ""), grid=(B, H),
        in_specs=in_specs, out_specs=[spec, spec, spec],
        out_shape=[SDS((D, T), BF16), SDS((D, T), kv_dtype), SDS((D, T), kv_dtype)],
        scratch_shapes=[pltpu.VMEM((nblk, 2 * BLK, BLK), F32), pltpu.VMEM((nblk, 2 * BLK, BLK), F32),
                        pltpu.VMEM((nblk, 2 * BLK, BLK), BF16), pltpu.VMEM((nblk, 2 * BLK, BLK), BF16)],
        compiler_params=_cp(),
    )(*args)


def adamw(w, g, m, v, name):
    R, C = w.shape
    tr = _tile(R, 512, 8)

    def body(w_ref, g_ref, m_ref, v_ref, d_ref, nm_ref, nv_ref):
        gv = g_ref[...]
        nm = ADAM_B1 * m_ref[...] + (1.0 - ADAM_B1) * gv
        nv = ADAM_B2 * v_ref[...] + (1.0 - ADAM_B2) * (gv * gv)
        m_hat = nm / (1.0 - ADAM_B1 ** ADAM_STEP)
        v_hat = nv / (1.0 - ADAM_B2 ** ADAM_STEP)
        d_ref[...] = -ADAM_LR * (m_hat / (jnp.sqrt(v_hat) + ADAM_EPS) + ADAM_WD * w_ref[...])
        nm_ref[...] = nm
        nv_ref[...] = nv

    spec = pl.BlockSpec((tr, C), lambda i: (i, 0))
    return pl.pallas_call(
        body, name=name, grid=(R // tr,), in_specs=[spec] * 4, out_specs=[spec] * 3,
        out_shape=[SDS((R, C), F32)] * 3, compiler_params=_cp(),
    )(w, g, m, v)


def _perm(a, B, S, d):
    if d == 1:
        return a
    lead = a.shape[:-1]
    return a.reshape(*lead, B, S // d, d).swapaxes(-1, -2).reshape(*lead, B * S)


def _unperm(a, B, S, d):
    if d == 1:
        return a
    lead = a.shape[:-1]
    return a.reshape(*lead, B, d, S // d).swapaxes(-1, -2).reshape(*lead, B * S)


def _perm3(a, B, S):
    return jnp.stack([_perm(a, B, S, d) for d in DILATIONS])


def _xT3(xb, B, S):
    D = xb.shape[1]
    outs = []
    for d in DILATIONS:
        outs.append(xb.reshape(B, S // d, d, D).transpose(3, 0, 2, 1).reshape(D, B * S))
    return jnp.stack(outs)


def _rope_tables(S):
    half = HEAD_DIM // 2
    inv_freq = ROPE_THETA ** (-jnp.arange(0, HEAD_DIM, 2, dtype=F32) / HEAD_DIM)
    ang = jnp.arange(S, dtype=F32)[:, None] * inv_freq[None, :]
    cos = jnp.concatenate([jnp.cos(ang), jnp.cos(ang)], axis=1).T
    sin = jnp.concatenate([-jnp.sin(ang), jnp.sin(ang)], axis=1).T
    return _perm3(cos, 1, S), _perm3(sin, 1, S)


def kernel(x, pool_w, pool_scale, w_q, w_kv, w_o, ffn_w_gate, ffn_w_up, ffn_conv_w, ffn_conv_b, ffn_w_down, ln1_g, ln1_b, ln2_g, ln2_b, loss_target, m_pool_w, m_pool_scale, m_w_q, m_w_kv, m_w_o, m_ffn_w_gate, m_ffn_w_up, m_ffn_conv_w, m_ffn_conv_b, m_ffn_w_down, m_ln1_g, m_ln1_b, m_ln2_g, m_ln2_b, v_pool_w, v_pool_scale, v_w_q, v_w_kv, v_w_o, v_ffn_w_gate, v_ffn_w_up, v_ffn_conv_w, v_ffn_conv_b, v_ffn_w_down, v_ln1_g, v_ln1_b, v_ln2_g, v_ln2_b):
    B, S, D = x.shape
    T = B * S
    depth = ln1_g.shape[0]
    nA, nB = pool_w.shape[0], w_q.shape[0]
    Fs = ffn_w_down.shape[1]
    Fd = Fs * N_DEV
    H = D // HEAD_DIM
    G = len(DILATIONS)
    PG = len(POOL_WINDOWS)
    Cg = D // PG
    alpha = (2.0 * depth) ** 0.25
    me = 4 * lax.axis_index("x") + 2 * lax.axis_index("y") + lax.axis_index("c")

    qs, kvs, os_ = w_q.shape[2], w_kv.shape[1], w_o.shape[1]
    pool_rows = pool_w.size // D
    local = {("pool",): pool_w.reshape(pool_rows, D).astype(BF16), ("wkv",): w_kv.T.astype(BF16)}
    for j in range(nB):
        local[("wq", j)] = w_q[j].T.astype(BF16)
        local[("wo", j)] = w_o[j].astype(BF16)
    for i in range(depth):
        local[("wg", i)] = ffn_w_gate[i].T.astype(BF16)
        local[("wu", i)] = ffn_w_up[i].T.astype(BF16)
        local[("wd", i)] = ffn_w_down[i].astype(BF16)
    ffn = lambda i: [("wg", i), ("wu", i), ("wd", i)]
    queue = [[("pool",)]]
    if depth == 4 and nA == 2 and nB == 2:
        queue += [[("wg", 0)], [("wu", 0)], [("wd", 0), ("wg", 1), ("wu", 1)], [("wd", 1)],
                  [("wo", 0)], [("wo", 1)], [("wkv",), ("wq", 0)], [("wu", 2)], [("wd", 2)], [("wq", 1)],
                  [("wg", 2)], ffn(3)]
    gathered = {}

    def land(keys, arrs):
        for k, a in zip(keys or (), arrs or ()):
            gathered[k] = a.reshape(-1, D)

    def next_gather():
        if not queue:
            return None, None
        keys = queue.pop(0)
        if not keys:
            return None, None
        return keys, Gather([local[k] for k in keys])

    def weight(key):
        if key not in gathered:
            keys = [key]
            for bi, batch in enumerate(queue):
                if key in batch:
                    keys = queue.pop(bi)
                    break
            blk = all_gather_blocks(jnp.concatenate([local[k] for k in keys], axis=0), "gather_" + "_".join(map(str, key)), in_vmem=False)
            off = 0
            for k in keys:
                r = local[k].shape[0]
                gathered[k] = blk[:, off:off + r].reshape(-1, D)
                off += r
        return gathered[key]

    PW = weight(("pool",)).reshape(N_DEV, nA, PG, Cg // N_DEV, Cg).transpose(1, 2, 0, 3, 4).reshape(nA, PG, Cg, Cg)

    sm_cols = 128
    sm_local = jnp.concatenate([ffn_conv_w.reshape(-1), pool_scale.reshape(-1)])
    sm_rows = -(-sm_local.size // sm_cols)
    sm_rows_p = -(-sm_rows // 8) * 8
    sm_local = jnp.pad(sm_local, (0, sm_rows_p * sm_cols - sm_local.size)).reshape(sm_rows_p, sm_cols)
    sm = all_gather_blocks(sm_local, "gather_small", in_vmem=True).reshape(N_DEV, -1)
    ncw = ffn_conv_w.size
    conv_w_full = sm[:, :ncw].reshape(N_DEV, depth, 3, Fs).transpose(1, 2, 0, 3).reshape(depth, 3, Fd)
    pool_scale_full = sm[:, ncw:ncw + pool_scale.size].reshape(N_DEV, nA, D // N_DEV).transpose(1, 0, 2).reshape(nA, 1, D)

    cosT, sinT = _rope_tables(S)
    bias = _attn_bias()

    xs = x.reshape(T, D)
    saved = []
    cur, curb = xs, None
    kT = vT = x1T3 = None
    for i in range(depth):
        sv = {}
        if i < nA:
            keys, cr = next_gather()
            mix, pooled, got = pool_fwd(cur, PW[i], pool_scale_full[i], B, S, carry=cr)
            land(keys, got)
            sv["pooled"] = pooled
            keys, cr = next_gather()
            a1, h, hb, got = add_ln(cur, mix, ln1_g[i], ln1_b[i], alpha, carry=cr)
            land(keys, got)
        else:
            j = i - nA
            xT3 = x1T3 if j == 0 else _xT3(curb, B, S)
            keys, cr = next_gather()
            qT, got = proj_T(weight(("wq", j)), xT3, cosT, sinT, 0, True, HEAD_DIM ** -0.5, "q_proj", carry=cr)
            land(keys, got)
            oTs, lses = [], []
            for gi, d in enumerate(DILATIONS):
                o_g, lse_g = attn_fwd(qT, kT, vT, bias, gi, B, S)
                oTs.append(_unperm(o_g, B, S, d))
                lses.append(_unperm(lse_g, B, S, d))
            oTb, oTf, lse_tot = attn_combine(oTs, lses)
            a1, h, hb, _ = matmul_ln(oTb, weight(("wo", j)), TN, cur, ln1_g[i], ln1_b[i], alpha, "o_proj_ln")
            sv.update(xT3=xT3, qT=qT, oTb=oTb, oTf=oTf, lse_tot=lse_tot)
        wg_i, wu_i = weight(("wg", i)), weight(("wu", i))
        keys, cr = next_gather()
        g, ge, ud, hh, got = ffn_up(hb, wg_i, wu_i, conv_w_full[i], ffn_conv_b[i].reshape(1, Fd), B, S, carry=cr)
        land(keys, got)
        wd_i = weight(("wd", i))
        keys, cr = next_gather()
        a2, cur, curb, got = matmul_ln(hh, wd_i, NN, h, ln2_g[i], ln2_b[i], alpha, "ffn_down_ln", carry=cr)
        land(keys, got)
        sv.update(a1=a1, hb=hb, g=g, ge=ge, ud=ud, hh=hh, a2=a2)
        saved.append(sv)
        if i == nA - 1:
            x1T3 = _xT3(curb, B, S)
            wkv = weight(("wkv",))
            keys, cr = next_gather()
            kT, got = proj_T(wkv, x1T3, cosT, sinT, 0, True, 1.0, "k_proj", carry=cr)
            land(keys, got)
            keys, cr = next_gather()
            vT, got = proj_T(wkv, x1T3, cosT, sinT, G, False, 1.0, "v_proj", carry=cr)
            land(keys, got)


    small = {k: [None] * depth for k in ("ln1_g", "ln1_b", "ln2_g", "ln2_b", "conv_b", "conv_w")}
    dscale = [None] * nA
    dpw = [None] * nA
    dk_acc, dv_acc = [None] * G, [None] * G

    def blocks(a, rows):
        return a.reshape(N_DEV, rows, D)

    pending, landed = [], {}

    def next_carry():
        if not pending:
            return None, None
        key, parts = pending.pop(0)
        return key, Scatter(parts)

    dcur = sq = ln2_done = None
    for i in reversed(range(depth)):
        sv = saved[i]
        if i == depth - 1:
            db2, db2b, small["ln2_g"][i], small["ln2_b"][i], sq = loss_ln_bwd(cur, loss_target.reshape(T, D), sv["a2"], ln2_g[i])
        elif ln2_done is not None:
            db2, db2b, small["ln2_g"][i], small["ln2_b"][i] = ln2_done
            ln2_done = None
        else:
            db2, db2b, small["ln2_g"][i], small["ln2_b"][i] = ln_bwd(dcur, sv["a2"], ln2_g[i])
        key, cr = next_carry()
        dg_, du_, dcb, dcw, got = ffn_mid_bwd(db2b, gathered[("wd", i)], sv["g"], sv["ge"], sv["ud"], conv_w_full[i], B, S, carry=cr)
        if cr is not None:
            landed[key] = got
        small["conv_b"][i] = jnp.sum(dcb, axis=0)
        small["conv_w"][i] = jnp.sum(dcw, axis=0)
        key, cr = next_carry()
        dwd, got = wgrad_rows(sv["hh"], db2b, "wgrad_down", carry=cr)
        if cr is not None:
            landed[key] = got
        dwg, landed[("down", i)] = wgrad_rows(dg_, sv["hb"], "wgrad_gate", carry=Scatter([blocks(dwd, Fs)]))
        dwu, landed[("gate", i)] = wgrad_rows(du_, sv["hb"], "wgrad_up", carry=Scatter([blocks(dwg, Fs)]))
        da1, da1b, small["ln1_g"][i], small["ln1_b"][i], landed[("up", i)] = ffn_dx_ln(
            dg_, du_, gathered[("wg", i)], gathered[("wu", i)], db2, alpha, sv["a1"], ln1_g[i], carry=Scatter([blocks(dwu, Fs)]))
        if i < nA:
            dcur, dsp, dpwp = pool_bwd(da1, sv["pooled"], PW[i], pool_scale_full[i], alpha, B, S)
            dscale[i] = jnp.sum(dsp, axis=0)
            dpw[i] = jnp.sum(dpwp, axis=0)
        else:
            j = i - nA
            doT = matmul_to_T(gathered[("wo", j)], da1b, "o_proj_bwd")
            dwo = wgrad_mixed(sv["oTb"], da1b, "wgrad_o")
            delta = attn_delta(doT, sv["oTf"])
            dq_tok, dwq = [], []
            for gi, d in enumerate(DILATIONS):
                dq_g, dk_acc[gi], dv_acc[gi] = attn_bwd(
                    sv["qT"], kT, vT, _perm(doT, B, S, d), _perm(sv["lse_tot"], B, S, d), _perm(delta, B, S, d),
                    cosT, sinT, bias, gi, HEAD_DIM ** -0.5, B, S, dk_prev=dk_acc[gi], dv_prev=dv_acc[gi])
                dwq.append(wgrad_T(dq_g, sv["xT3"], gi, "wgrad_q"))
                dq_tok.append(_unperm(dq_g, B, S, d))
            dwq = jnp.concatenate(dwq, axis=0)
            below = (saved[i - 1]["a2"], ln2_g[i - 1]) if i > 0 else None
            if j > 0:
                last = dx_from_T(dq_tok, gathered[("wq", j)], da1, alpha, "q_proj_bwd", 512, ln=below)
            else:
                dcur = dx_from_T(dq_tok, gathered[("wq", j)], da1, alpha, "q_proj_bwd", 512)
                dkv = [a.astype(BF16) for a in dk_acc + dv_acc]
                dwkv = jnp.concatenate([wgrad_T(a, x1T3, gi % G, "wgrad_kv") for gi, a in enumerate(dkv)], axis=0)
                dkv_tok = [_unperm(a, B, S, DILATIONS[gi % G]) for gi, a in enumerate(dkv)]
                last = dx_from_T(dkv_tok, gathered[("wkv",)], dcur, 1.0, "kv_proj_bwd", 256, ln=below)
                pending.append((("kv",), [blocks(dwkv, kvs)]))
            if below is None:
                dcur = last
            else:
                ln2_done = last
            pending.append((("attn", j), [blocks(dwq, qs), blocks(dwo, os_)]))
    grad_x = dcur.reshape(B, S, D)

    dpw_all = jnp.stack(dpw).reshape(nA, PG, N_DEV, Cg // N_DEV, Cg).transpose(2, 0, 1, 3, 4).reshape(N_DEV, pool_rows, D)
    tail_keys = [k for k, _ in pending] + [("pool",)]
    tail_parts = [parts for _, parts in pending] + [[dpw_all.astype(BF16)]]
    tail_rows = [sum(p.shape[1] for p in parts) for parts in tail_parts]
    tail = scatter_partials([p for parts in tail_parts for p in parts], "scatter_tail")
    for t, key in enumerate(tail_keys):
        lo = sum(tail_rows[:t])
        landed[key] = tail[:, lo:lo + tail_rows[t]]

    def reduced(key):
        return sum_slots(landed[key], "sum_" + "_".join(str(k) for k in key))

    g_attn = [reduced(("attn", j)) for j in range(nB)]
    g_w_q = jnp.swapaxes(jnp.stack([a[:qs] for a in g_attn]), 1, 2)
    g_w_o = jnp.stack([a[qs:] for a in g_attn])
    g_w_kv = reduced(("kv",)).T
    g_gate = jnp.swapaxes(jnp.stack([reduced(("gate", i)) for i in range(depth)]), 1, 2)
    g_up = jnp.swapaxes(jnp.stack([reduced(("up", i)) for i in range(depth)]), 1, 2)
    g_down = jnp.stack([reduced(("down", i)) for i in range(depth)])
    g_pool_w = reduced(("pool",)).reshape(pool_w.shape)

    def rows_of(a):
        a = a.reshape(-1)
        n = -(-a.size // D) * D
        return jnp.pad(a, (0, n - a.size)).reshape(-1, D)

    sm_parts = [rows_of(jnp.concatenate(small[k], axis=0)) for k in ("ln1_g", "ln1_b", "ln2_g", "ln2_b")]
    sm_parts += [rows_of(jnp.stack(small["conv_b"])), rows_of(jnp.stack(small["conv_w"])), rows_of(jnp.stack(dscale)), sq]
    sm_sizes = [p.shape[0] for p in sm_parts]
    sm_all = jnp.concatenate(sm_parts, axis=0)
    pad_rows = -(-sm_all.shape[0] // 8) * 8 - sm_all.shape[0]
    sm_all = jnp.pad(sm_all, ((0, pad_rows), (0, 0)))
    sm_sum = sum_slots(all_gather_blocks(sm_all, "gather_small_grads", in_vmem=True), "sum_small_grads")
    sm_offs = [sum(sm_sizes[:i]) for i in range(len(sm_sizes))]

    def sm_take(i, shape):
        n = math.prod(shape)
        return sm_sum[sm_offs[i]:sm_offs[i] + sm_sizes[i]].reshape(-1)[:n].reshape(shape)

    g_ln1_g, g_ln1_b = sm_take(0, (depth, D)), sm_take(1, (depth, D))
    g_ln2_g, g_ln2_b = sm_take(2, (depth, D)), sm_take(3, (depth, D))
    g_conv_b = sm_take(4, (depth, Fd))
    g_conv_w = lax.dynamic_slice_in_dim(sm_take(5, (depth, 3, Fd)), me * Fs, Fs, axis=2)
    g_pool_scale = lax.dynamic_slice_in_dim(sm_take(6, (nA, D)), me * (D // N_DEV), D // N_DEV, axis=1)
    loss = (0.5 / D) * jnp.sum(sm_take(7, (D,)))

    def v2(a):
        return a.reshape(-1, a.shape[-1])

    names = ["pool_w", "pool_scale", "w_q", "w_kv", "w_o", "ffn_w_gate", "ffn_w_up", "ffn_conv_w", "ffn_conv_b",
             "ffn_w_down", "ln1_g", "ln1_b", "ln2_g", "ln2_b"]
    ws = [pool_w, pool_scale, w_q, w_kv, w_o, ffn_w_gate, ffn_w_up, ffn_conv_w, ffn_conv_b, ffn_w_down, ln1_g, ln1_b, ln2_g, ln2_b]
    ms = [m_pool_w, m_pool_scale, m_w_q, m_w_kv, m_w_o, m_ffn_w_gate, m_ffn_w_up, m_ffn_conv_w, m_ffn_conv_b, m_ffn_w_down, m_ln1_g, m_ln1_b, m_ln2_g, m_ln2_b]
    vs = [v_pool_w, v_pool_scale, v_w_q, v_w_kv, v_w_o, v_ffn_w_gate, v_ffn_w_up, v_ffn_conv_w, v_ffn_conv_b, v_ffn_w_down, v_ln1_g, v_ln1_b, v_ln2_g, v_ln2_b]
    gs = [g_pool_w, g_pool_scale, g_w_q, g_w_kv, g_w_o, g_gate, g_up, g_conv_w, g_conv_b, g_down, g_ln1_g, g_ln1_b, g_ln2_g, g_ln2_b]
    deltas, new_ms, new_vs = [], [], []
    for nm, w, gr, m_, v_ in zip(names, ws, gs, ms, vs):
        d_, nm_, nv_ = adamw(v2(w), v2(gr), v2(m_), v2(v_), "adamw_" + nm)
        deltas.append(d_.reshape(w.shape))
        new_ms.append(nm_.reshape(w.shape))
        new_vs.append(nv_.reshape(w.shape))

    return (loss, grad_x, *gs, *deltas, *new_ms, *new_vs)
```

```python
import functools
import math

import jax
import jax.numpy as jnp
from jax import lax
from jax.experimental import pallas as pl
from jax.experimental.pallas import tpu as pltpu

F32 = jnp.float32
BF16 = jnp.bfloat16
SDS = jax.ShapeDtypeStruct
MESH = pl.DeviceIdType.MESH

N_DEV = 8
HEAD_DIM = 64
BLK = 128
DILATIONS = (1, 4, 16)
POOL_WINDOWS = (2, 4, 8, 16)
ROPE_THETA = 10000.0
LN_EPS = 1e-5
NEG = -1e30
V7X_VMEM_LIMIT = 56 * 1024 * 1024

ADAM_LR, ADAM_B1, ADAM_B2, ADAM_EPS, ADAM_WD, ADAM_STEP = 0.001, 0.9, 0.999, 1e-08, 0.01, 10

NN = (((1,), (0,)), ((), ()))
NT = (((1,), (1,)), ((), ()))
TN = (((0,), (0,)), ((), ()))


def _cp(sem=None):
    kw = dict(vmem_limit_bytes=V7X_VMEM_LIMIT)
    if sem is not None:
        kw["dimension_semantics"] = sem
    return pltpu.CompilerParams(**kw)


def _dot(a, b, dims=NN):
    return lax.dot_general(a, b, dims, preferred_element_type=F32)


def _tile(n, target, mult):
    best = None
    for t in range(mult, min(n, target) + 1, mult):
        if n % t == 0:
            best = t
    return best if best is not None else n


def _mesh_pos():
    return lax.axis_index("x"), lax.axis_index("y"), lax.axis_index("c")


def all_gather_blocks(xl, name, in_vmem):
    R, C = xl.shape
    space = pltpu.VMEM if in_vmem else pl.ANY

    def body(x_ref, out_ref, send_sems, recv_sems, local_sem):
        x, y, c = _mesh_pos()
        me, sibling = (x, y, c), (x, y, 1 - c)
        chips = [(1 - x, y), (x, 1 - y), (1 - x, 1 - y)]

        def slot(px, py, pc):
            return out_ref.at[4 * px + 2 * py + pc]

        def copy(k, block, to, src=None):
            return pltpu.make_async_remote_copy(
                src_ref=slot(*block) if src is None else src, dst_ref=slot(*block),
                send_sem=send_sems.at[k], recv_sem=recv_sems.at[k], device_id=to, device_id_type=MESH)

        mine = pltpu.make_async_copy(x_ref, slot(*me), local_sem)
        mine.start()
        first = [copy(0, me, sibling, src=x_ref)]
        first += [copy(1 + j, me, (*chip, c), src=x_ref) for j, chip in enumerate(chips)]
        for cp in first:
            cp.start()
        passed = [copy(4 + j, (*chip, c), sibling) for j, chip in enumerate(chips)]
        for j, chip in enumerate(chips):
            copy(1 + j, (*chip, c), me).wait_recv()
            passed[j].start()
        copy(0, sibling, me).wait_recv()
        for j, chip in enumerate(chips):
            copy(4 + j, (*chip, 1 - c), me).wait_recv()
        for cp in first + passed:
            cp.wait_send()
        mine.wait()

    return pl.pallas_call(
        body, name=name,
        out_shape=SDS((N_DEV, R, C), xl.dtype),
        in_specs=[pl.BlockSpec(memory_space=space)],
        out_specs=pl.BlockSpec(memory_space=space),
        scratch_shapes=[pltpu.SemaphoreType.DMA((7,)), pltpu.SemaphoreType.DMA((7,)), pltpu.SemaphoreType.DMA],
        compiler_params=_cp(),
    )(xl)


def _peers():
    x, y, c = _mesh_pos()
    peers = []
    for r in range(1, N_DEV):
        peers.append((1 - x if (r & 4) else x, 1 - y if (r & 2) else y, 1 - c if (r & 1) else c))
    return 4 * x + 2 * y + c, peers


class Gather:
    def __init__(self, parts):
        self.parts = list(parts)
        n = len(self.parts)
        self.out_shapes = [SDS((N_DEV,) + p.shape, p.dtype) for p in self.parts]
        self.scratch = [pltpu.SemaphoreType.DMA((7 * n,)), pltpu.SemaphoreType.DMA((7 * n,)), pltpu.SemaphoreType.DMA((n,))]

    def start(self, part_refs, out_refs, send_sems, recv_sems, local_sems):
        me_lin, peers = _peers()
        n = len(self.parts)
        for i in range(n):
            pltpu.make_async_copy(part_refs[i], out_refs[i].at[me_lin], local_sems.at[i]).start()
        for k, peer in enumerate(peers):
            for i in range(n):
                pltpu.make_async_remote_copy(
                    src_ref=part_refs[i], dst_ref=out_refs[i].at[me_lin],
                    send_sem=send_sems.at[k * n + i], recv_sem=recv_sems.at[k * n + i],
                    device_id=peer, device_id_type=MESH).start()

    def wait(self, out_refs, send_sems, recv_sems, local_sems):
        me_lin, peers = _peers()
        n = len(self.parts)
        for k, (px, py, pc) in enumerate(peers):
            p_lin = 4 * px + 2 * py + pc
            for i in range(n):
                arrival = pltpu.make_async_remote_copy(
                    src_ref=out_refs[i].at[p_lin], dst_ref=out_refs[i].at[p_lin],
                    send_sem=send_sems.at[k * n + i], recv_sem=recv_sems.at[k * n + i],
                    device_id=(px, py, pc), device_id_type=MESH)
                arrival.wait_recv()
                arrival.wait_send()
        for i in range(n):
            pltpu.make_async_copy(out_refs[i].at[me_lin], out_refs[i].at[me_lin], local_sems.at[i]).wait()


class Scatter:
    def __init__(self, parts):
        self.parts = list(parts)
        self.rows = [p.shape[1] for p in parts]
        self.offs = [sum(self.rows[:i]) for i in range(len(self.rows))]
        self.out_shapes = [SDS((N_DEV, sum(self.rows), parts[0].shape[2]), parts[0].dtype)]
        self.scratch = [pltpu.SemaphoreType.DMA((7,)), pltpu.SemaphoreType.DMA((7,)), pltpu.SemaphoreType.DMA]

    def start(self, part_refs, out_refs, send_sems, recv_sems, local_sem):
        out_ref = out_refs[0]
        me_lin, peers = _peers()
        for i, (off, r) in enumerate(zip(self.offs, self.rows)):
            pltpu.make_async_copy(part_refs[i].at[me_lin], out_ref.at[me_lin, pl.ds(off, r)], local_sem).start()
        for k, (px, py, pc) in enumerate(peers):
            p_lin = 4 * px + 2 * py + pc
            for i, (off, r) in enumerate(zip(self.offs, self.rows)):
                pltpu.make_async_remote_copy(
                    src_ref=part_refs[i].at[p_lin], dst_ref=out_ref.at[me_lin, pl.ds(off, r)],
                    send_sem=send_sems.at[k], recv_sem=recv_sems.at[k],
                    device_id=(px, py, pc), device_id_type=MESH).start()

    def wait(self, out_refs, send_sems, recv_sems, local_sem):
        out_ref = out_refs[0]
        me_lin, peers = _peers()
        for k, (px, py, pc) in enumerate(peers):
            p_lin = 4 * px + 2 * py + pc
            whole = pltpu.make_async_remote_copy(
                src_ref=out_ref.at[p_lin], dst_ref=out_ref.at[p_lin],
                send_sem=send_sems.at[k], recv_sem=recv_sems.at[k],
                device_id=(px, py, pc), device_id_type=MESH)
            whole.wait_recv()
            whole.wait_send()
        pltpu.make_async_copy(out_ref.at[me_lin], out_ref.at[me_lin], local_sem).wait()


def scatter_partials(parts, name):
    sc = Scatter(parts)
    n = len(parts)

    def body(*refs):
        sc.start(refs[:n], refs[n:n + 1], *refs[n + 1:])
        sc.wait(refs[n:n + 1], *refs[n + 1:])

    return pl.pallas_call(
        body, name=name, out_shape=sc.out_shapes[0],
        in_specs=[pl.BlockSpec(memory_space=pl.ANY)] * n, out_specs=pl.BlockSpec(memory_space=pl.ANY),
        scratch_shapes=sc.scratch, compiler_params=_cp(),
    )(*parts)


def _call(body, name, grid, in_specs, out_specs, out_shape, args, scratch=(), sem=None, carry=None):
    in_specs, out_specs, out_shape, scratch = list(in_specs), list(out_specs), list(out_shape), list(scratch)
    if carry is None:
        outs = pl.pallas_call(body, name=name, grid=grid, in_specs=in_specs, out_specs=out_specs, out_shape=out_shape,
                              scratch_shapes=scratch, compiler_params=_cp(sem))(*args)
        return list(outs), None
    n_in, n_out, n_scr, n_c, n_co = len(in_specs), len(out_specs), len(scratch), len(carry.parts), len(carry.out_shapes)
    last = [g - 1 for g in grid]

    def carried(*refs):
        ins, c_ins = refs[:n_in], refs[n_in:n_in + n_c]
        o0 = n_in + n_c
        outs, c_out = refs[o0:o0 + n_out], refs[o0 + n_out:o0 + n_out + n_co]
        s0 = o0 + n_out + n_co
        scr, c_scr = refs[s0:s0 + n_scr], refs[s0 + n_scr:]
        ids = [pl.program_id(a) for a in range(len(grid))]
        is_first = functools.reduce(jnp.logical_and, [i == 0 for i in ids])
        is_last = functools.reduce(jnp.logical_and, [i == l for i, l in zip(ids, last)])

        @pl.when(is_first)
        def _():
            carry.start(c_ins, c_out, *c_scr)

        body(*ins, *outs, *scr)

        @pl.when(is_last)
        def _():
            carry.wait(c_out, *c_scr)

    hbm = pl.BlockSpec(memory_space=pl.ANY)
    outs = pl.pallas_call(
        carried, name=name + "_carry", grid=grid, in_specs=in_specs + [hbm] * n_c, out_specs=out_specs + [hbm] * n_co,
        out_shape=out_shape + carry.out_shapes, scratch_shapes=scratch + carry.scratch,
        compiler_params=_cp(sem if sem is not None else ("arbitrary",) * len(grid)),
    )(*args, *carry.parts)
    return list(outs[:n_out]), list(outs[n_out:])


def sum_slots(slots, name, out_dtype=F32):
    _, R, C = slots.shape
    tr = _tile(R, 512, 16)

    def body(s_ref, o_ref):
        acc = s_ref[0].astype(F32)
        for s in range(1, N_DEV):
            acc = acc + s_ref[s].astype(F32)
        o_ref[...] = acc.astype(out_dtype)

    return pl.pallas_call(
        body, name=name, grid=(R // tr,),
        in_specs=[pl.BlockSpec((N_DEV, tr, C), lambda i: (0, i, 0))],
        out_specs=pl.BlockSpec((tr, C), lambda i: (i, 0)),
        out_shape=SDS((R, C), out_dtype), compiler_params=_cp(),
    )(slots)


def add_ln(x, mix, g, b, alpha, carry=None):
    T, D = x.shape
    tm = _tile(T, 512, 16)

    def body(x_ref, m_ref, g_ref, b_ref, a_ref, y_ref, yb_ref):
        a = alpha * x_ref[...] + m_ref[...]
        mu = jnp.mean(a, axis=-1, keepdims=True)
        xc = a - mu
        var = jnp.mean(xc * xc, axis=-1, keepdims=True)
        y = xc * lax.rsqrt(var + LN_EPS) * g_ref[...] + b_ref[...]
        a_ref[...] = a
        y_ref[...] = y
        yb_ref[...] = y.astype(BF16)

    row = pl.BlockSpec((tm, D), lambda i: (i, 0))
    vec = pl.BlockSpec((1, D), lambda i: (0, 0))
    outs, landed = _call(body, "add_ln", (T // tm,), [row, row, vec, vec], [row, row, row],
                         [SDS((T, D), F32), SDS((T, D), F32), SDS((T, D), BF16)],
                         (x, mix, g.reshape(1, D), b.reshape(1, D)), carry=carry)
    return (*outs, landed)


def _ln_bwd_tile(dy, a, gamma):
    mu = jnp.mean(a, axis=-1, keepdims=True)
    xc = a - mu
    var = jnp.mean(xc * xc, axis=-1, keepdims=True)
    r = lax.rsqrt(var + LN_EPS)
    xh = xc * r
    dxh = dy * gamma
    m1 = jnp.mean(dxh, axis=-1, keepdims=True)
    m2 = jnp.mean(dxh * xh, axis=-1, keepdims=True)
    da = r * (dxh - m1 - xh * m2)
    return da, jnp.sum(dy * xh, axis=0, keepdims=True), jnp.sum(dy, axis=0, keepdims=True)


def ln_bwd(dy, a, g):
    T, D = a.shape
    tm = _tile(T, 512, 16)

    def body(dy_ref, a_ref, g_ref, da_ref, dab_ref, dg_ref, db_ref):
        @pl.when(pl.program_id(0) == 0)
        def _():
            dg_ref[...] = jnp.zeros_like(dg_ref)
            db_ref[...] = jnp.zeros_like(db_ref)

        da, sg, sb = _ln_bwd_tile(dy_ref[...], a_ref[...], g_ref[...])
        da_ref[...] = da
        dab_ref[...] = da.astype(BF16)
        dg_ref[...] += sg
        db_ref[...] += sb

    row = pl.BlockSpec((tm, D), lambda i: (i, 0))
    vec = pl.BlockSpec((1, D), lambda i: (0, 0))
    return pl.pallas_call(
        body, name="ln_bwd", grid=(T // tm,),
        in_specs=[row, row, vec], out_specs=[row, row, vec, vec],
        out_shape=[SDS((T, D), F32), SDS((T, D), BF16), SDS((1, D), F32), SDS((1, D), F32)],
        compiler_params=_cp(("arbitrary",)),
    )(dy, a, g.reshape(1, D))


def loss_ln_bwd(y, tgt, a, g):
    T, D = y.shape
    tm = _tile(T, 512, 16)

    def body(y_ref, t_ref, a_ref, g_ref, da_ref, dab_ref, dg_ref, db_ref, sq_ref):
        @pl.when(pl.program_id(0) == 0)
        def _():
            dg_ref[...] = jnp.zeros_like(dg_ref)
            db_ref[...] = jnp.zeros_like(db_ref)
            sq_ref[...] = jnp.zeros_like(sq_ref)

        e = y_ref[...] - t_ref[...]
        da, sg, sb = _ln_bwd_tile(e / float(D), a_ref[...], g_ref[...])
        da_ref[...] = da
        dab_ref[...] = da.astype(BF16)
        dg_ref[...] += sg
        db_ref[...] += sb
        sq_ref[...] += jnp.sum(e * e, axis=0, keepdims=True)

    row = pl.BlockSpec((tm, D), lambda i: (i, 0))
    vec = pl.BlockSpec((1, D), lambda i: (0, 0))
    return pl.pallas_call(
        body, name="loss_ln_bwd", grid=(T // tm,),
        in_specs=[row, row, row, vec], out_specs=[row, row, vec, vec, vec],
        out_shape=[SDS((T, D), F32), SDS((T, D), BF16), SDS((1, D), F32), SDS((1, D), F32), SDS((1, D), F32)],
        compiler_params=_cp(("arbitrary",)),
    )(y, tgt, a, g.reshape(1, D))


def matmul_ln(a, w, dims, res, g, b, alpha, name, carry=None):
    if dims == TN:
        K, T = a.shape
    else:
        T, K = a.shape
    D = w.shape[1]
    tm = _tile(T, 512, 128 if dims == TN else 16)

    def body(a_ref, w_ref, r_ref, g_ref, b_ref, p_ref, y_ref, yb_ref):
        pre = alpha * r_ref[...] + _dot(a_ref[...], w_ref[...], dims)
        mu = jnp.mean(pre, axis=-1, keepdims=True)
        xc = pre - mu
        var = jnp.mean(xc * xc, axis=-1, keepdims=True)
        y = xc * lax.rsqrt(var + LN_EPS) * g_ref[...] + b_ref[...]
        p_ref[...] = pre
        y_ref[...] = y
        yb_ref[...] = y.astype(BF16)

    a_spec = pl.BlockSpec((K, tm), lambda i: (0, i)) if dims == TN else pl.BlockSpec((tm, K), lambda i: (i, 0))
    row = pl.BlockSpec((tm, D), lambda i: (i, 0))
    vec = pl.BlockSpec((1, D), lambda i: (0, 0))
    outs, landed = _call(
        body, name, (T // tm,), [a_spec, pl.BlockSpec(w.shape, lambda i: (0, 0)), row, vec, vec], [row, row, row],
        [SDS((T, D), F32), SDS((T, D), F32), SDS((T, D), BF16)], (a, w, res, g.reshape(1, D), b.reshape(1, D)), carry=carry)
    return (*outs, landed)


def dx_from_T(aTs, w, res, alpha, name, tm_target, ln=None):
    T = aTs[0].shape[1]
    N = w.shape[1]
    ks = [a.shape[0] for a in aTs]
    n = len(aTs)
    tm = _tile(T, tm_target, 128)

    def body(*refs):
        a_refs, w_ref, r_ref = refs[:n], refs[n], refs[n + 1]
        acc = alpha * r_ref[...]
        off = 0
        for a_ref, k in zip(a_refs, ks):
            acc = acc + _dot(a_ref[...], w_ref[off:off + k, :], TN)
            off += k
        if ln is None:
            refs[n + 2][...] = acc
            return
        ln_a, ln_g, da_ref, dab_ref, dgm_ref, dbt_ref = refs[n + 2:]

        @pl.when(pl.program_id(0) == 0)
        def _():
            dgm_ref[...] = jnp.zeros_like(dgm_ref)
            dbt_ref[...] = jnp.zeros_like(dbt_ref)

        da, sg, sb = _ln_bwd_tile(acc, ln_a[...], ln_g[...])
        da_ref[...] = da
        dab_ref[...] = da.astype(BF16)
        dgm_ref[...] += sg
        dbt_ref[...] += sb

    row = pl.BlockSpec((tm, N), lambda i: (i, 0))
    vec = pl.BlockSpec((1, N), lambda i: (0, 0))
    in_specs = [pl.BlockSpec((k, tm), lambda i: (0, i)) for k in ks] + [pl.BlockSpec(w.shape, lambda i: (0, 0)), row]
    args = list(aTs) + [w, res]
    if ln is None:
        return pl.pallas_call(body, name=name, grid=(T // tm,), in_specs=in_specs, out_specs=row,
                              out_shape=SDS((T, N), F32), compiler_params=_cp())(*args)
    return pl.pallas_call(
        body, name=name + "_ln", grid=(T // tm,), in_specs=in_specs + [row, vec], out_specs=[row, row, vec, vec],
        out_shape=[SDS((T, N), F32), SDS((T, N), BF16), SDS((1, N), F32), SDS((1, N), F32)],
        compiler_params=_cp(("arbitrary",)),
    )(*args, ln[0], ln[1].reshape(1, N))


def matmul_to_T(w, a, name):
    M, K = w.shape
    T = a.shape[0]
    tt = _tile(T, 512, 128)

    def body(w_ref, a_ref, o_ref):
        o_ref[...] = _dot(w_ref[...], a_ref[...], NT).astype(BF16)

    return pl.pallas_call(
        body, name=name, grid=(T // tt,),
        in_specs=[pl.BlockSpec((M, K), lambda i: (0, 0)), pl.BlockSpec((tt, K), lambda i: (i, 0))],
        out_specs=pl.BlockSpec((M, tt), lambda i: (0, i)),
        out_shape=SDS((M, T), BF16), compiler_params=_cp(),
    )(w, a)


def wgrad_rows(a, b, name, carry=None):
    T, M = a.shape
    N = b.shape[1]
    tt = _tile(T, 512, 16)
    tmm = _tile(M, 1536, 128)
    nt = T // tt

    def body(a_ref, b_ref, o_ref, acc_ref):
        t = pl.program_id(1)

        @pl.when(t == 0)
        def _():
            acc_ref[...] = jnp.zeros_like(acc_ref)

        acc_ref[...] += _dot(a_ref[...], b_ref[...], TN)

        @pl.when(t == nt - 1)
        def _():
            o_ref[...] = acc_ref[...].astype(BF16)

    outs, landed = _call(
        body, name, (M // tmm, nt),
        [pl.BlockSpec((tt, tmm), lambda i, t: (t, i)), pl.BlockSpec((tt, N), lambda i, t: (t, 0))],
        [pl.BlockSpec((tmm, N), lambda i, t: (i, 0))], [SDS((M, N), BF16)], (a, b),
        scratch=[pltpu.VMEM((tmm, N), F32)], sem=("arbitrary", "arbitrary"), carry=carry)
    return outs[0], (landed[0] if landed else None)


def wgrad_T(aT, bT3, g, name):
    M, T = aT.shape
    N = bT3.shape[1]
    tt = _tile(T, 1024, 128)
    nt = T // tt

    def body(a_ref, b_ref, o_ref, acc_ref):
        t = pl.program_id(0)

        @pl.when(t == 0)
        def _():
            acc_ref[...] = jnp.zeros_like(acc_ref)

        acc_ref[...] += _dot(a_ref[...], b_ref[0], NT)

        @pl.when(t == nt - 1)
        def _():
            o_ref[...] = acc_ref[...].astype(BF16)

    return pl.pallas_call(
        body, name=name, grid=(nt,),
        in_specs=[pl.BlockSpec((M, tt), lambda t: (0, t)), pl.BlockSpec((1, N, tt), lambda t: (g, 0, t))],
        out_specs=pl.BlockSpec((M, N), lambda t: (0, 0)),
        out_shape=SDS((M, N), BF16), scratch_shapes=[pltpu.VMEM((M, N), F32)],
        compiler_params=_cp(("arbitrary",)),
    )(aT, bT3)


def wgrad_mixed(aT, b, name):
    M, T = aT.shape
    N = b.shape[1]
    tt = _tile(T, 1024, 128)
    nt = T // tt

    def body(a_ref, b_ref, o_ref, acc_ref):
        t = pl.program_id(0)

        @pl.when(t == 0)
        def _():
            acc_ref[...] = jnp.zeros_like(acc_ref)

        acc_ref[...] += _dot(a_ref[...], b_ref[...], NN)

        @pl.when(t == nt - 1)
        def _():
            o_ref[...] = acc_ref[...].astype(BF16)

    return pl.pallas_call(
        body, name=name, grid=(nt,),
        in_specs=[pl.BlockSpec((M, tt), lambda t: (0, t)), pl.BlockSpec((tt, N), lambda t: (t, 0))],
        out_specs=pl.BlockSpec((M, N), lambda t: (0, 0)),
        out_shape=SDS((M, N), BF16), scratch_shapes=[pltpu.VMEM((M, N), F32)],
        compiler_params=_cp(("arbitrary",)),
    )(aT, b)


def _shift_down(x, k, rows):
    return jnp.where(rows >= k, pltpu.roll(x, k, 0), 0.0)


def _shift_up(x, k, rows):
    n = x.shape[0]
    return jnp.where(rows < n - k, pltpu.roll(x, n - k, 0), 0.0)


def _pick(g, vals):
    out = vals[-1]
    for k in range(len(vals) - 2, -1, -1):
        out = jnp.where(g == k, vals[k], out)
    return out


def pool_fwd(x, pw, scale, B, S, carry=None):
    T, D = x.shape
    G = len(POOL_WINDOWS)
    Cg = D // G

    def body(x_ref, w_ref, s_ref, mix_ref, pooled_ref):
        g = pl.program_id(1)
        xv = x_ref[...]
        rows = lax.broadcasted_iota(jnp.int32, xv.shape, 0)
        sums, cur, k = [], xv, 1
        for _ in POOL_WINDOWS:
            cur = cur + _shift_down(cur, k, rows)
            sums.append(cur)
            k *= 2
        win = 2 * lax.shift_left(jnp.int32(1), g)
        total = _pick(g, sums)
        count = jnp.minimum(rows + 1, win).astype(F32)
        pooled = total / count - xv
        pb = pooled.astype(BF16)
        pooled_ref[...] = pb
        mix_ref[...] = _dot(pb, w_ref[0]) * s_ref[...]

    blk = pl.BlockSpec((S, Cg), lambda b, g: (b, g))
    outs, landed = _call(
        body, "pool_fwd", (B, G),
        [blk, pl.BlockSpec((1, Cg, Cg), lambda b, g: (g, 0, 0)), pl.BlockSpec((1, Cg), lambda b, g: (0, g))],
        [blk, blk], [SDS((T, D), F32), SDS((T, D), BF16)], (x, pw, scale), carry=carry)
    return (*outs, landed)


def pool_bwd(dmix, pooled, pw, scale, alpha, B, S):
    T, D = dmix.shape
    G = len(POOL_WINDOWS)
    Cg = D // G

    def body(d_ref, p_ref, w_ref, s_ref, dx_ref, ds_ref, dw_ref):
        g = pl.program_id(1)
        dm = d_ref[...]
        pb = p_ref[...]
        w = w_ref[0]
        ypre = _dot(pb, w)
        ds_ref[0] = jnp.sum(dm * ypre, axis=0, keepdims=True)
        dy = (dm * s_ref[...]).astype(BF16)
        dpool = _dot(dy, w, NT)
        dw_ref[0, 0] = _dot(pb, dy, TN)
        rows = lax.broadcasted_iota(jnp.int32, dm.shape, 0)
        win = 2 * lax.shift_left(jnp.int32(1), g)
        count = jnp.minimum(rows + 1, win).astype(F32)
        cur, k, sums = dpool / count, 1, []
        for _ in POOL_WINDOWS:
            cur = cur + _shift_up(cur, k, rows)
            sums.append(cur)
            k *= 2
        dx_ref[...] = alpha * dm + _pick(g, sums) - dpool

    blk = pl.BlockSpec((S, Cg), lambda b, g: (b, g))
    return pl.pallas_call(
        body, name="pool_bwd", grid=(B, G),
        in_specs=[blk, blk, pl.BlockSpec((1, Cg, Cg), lambda b, g: (g, 0, 0)), pl.BlockSpec((1, Cg), lambda b, g: (0, g))],
        out_specs=[blk, pl.BlockSpec((1, 1, Cg), lambda b, g: (b, 0, g)),
                   pl.BlockSpec((1, 1, Cg, Cg), lambda b, g: (b, g, 0, 0))],
        out_shape=[SDS((T, D), F32), SDS((B, 1, D), F32), SDS((B, G, Cg, Cg), F32)], compiler_params=_cp(),
    )(dmix, pooled, pw, scale)


_GELU_K = math.sqrt(2.0 / math.pi)
_GELU_C = 0.044715
FFN_ROWS = 512
FFN_HALO = 16


def ffn_up(hb, wgT, wuT, cw, cb, B, S, carry=None):
    T, D = hb.shape
    Fd = wgT.shape[0]
    fn = _tile(Fd, 256, 128)

    nc = S // _tile(S, FFN_ROWS, FFN_HALO)
    rc = S // nc

    def body(h_ref, wg_ref, wu_ref, cw_ref, cb_ref, g_ref, ge_ref, ud_ref, hh_ref):
        wg, wu, cw, cb = wg_ref[...], wu_ref[...], cw_ref[...], cb_ref[...]
        halo = jnp.zeros((FFN_HALO, fn), F32)
        for ci in range(nc):
            rows = slice(ci * rc, (ci + 1) * rc)
            h = h_ref[rows, :]
            g = _dot(h, wg, NT)
            u = _dot(h, wu, NT)
            gext = jnp.concatenate([halo, g], axis=0)
            halo = g[rc - FFN_HALO:, :]
            c = cb + cw[0:1] * pltpu.roll(gext, 2, 0)[FFN_HALO:, :] + cw[1:2] * pltpu.roll(gext, 1, 0)[FFN_HALO:, :] + cw[2:3] * g
            c2 = c * c
            th = jnp.tanh(c * (_GELU_K + (_GELU_K * _GELU_C) * c2))
            cdf = 0.5 * th + 0.5
            ge = c * cdf
            dgelu = cdf + (c * (1.0 - th * th)) * (0.5 * _GELU_K + (1.5 * _GELU_K * _GELU_C) * c2)
            g_ref[rows, :] = g.astype(BF16)
            ge_ref[rows, :] = ge.astype(BF16)
            ud_ref[rows, :] = (u * dgelu).astype(BF16)
            hh_ref[rows, :] = (ge * u).astype(BF16)

    hspec = pl.BlockSpec((S, D), lambda b, j: (b, 0))
    wspec = pl.BlockSpec((fn, D), lambda b, j: (j, 0))
    ospec = pl.BlockSpec((S, fn), lambda b, j: (b, j))
    outs, landed = _call(
        body, "ffn_up", (B, Fd // fn),
        [hspec, wspec, wspec, pl.BlockSpec((3, fn), lambda b, j: (0, j)), pl.BlockSpec((1, fn), lambda b, j: (0, j))],
        [ospec] * 4, [SDS((T, Fd), BF16)] * 4, (hb, wgT, wuT, cw, cb), carry=carry)
    return (*outs, landed)


def ffn_mid_bwd(dfb, wd, g, ge, ud, cw, B, S, carry=None):
    T, D = dfb.shape
    Fd = wd.shape[0]
    fn = _tile(Fd, 256, 128)

    def body(df_ref, wd_ref, g_ref, ge_ref, ud_ref, cw_ref, dg_ref, du_ref, dcb_ref, dcw_ref):
        dhh = _dot(df_ref[...], wd_ref[...], NT)
        gv = g_ref[...].astype(F32)
        cw = cw_ref[...]
        rows = lax.broadcasted_iota(jnp.int32, gv.shape, 0)
        g1 = _shift_down(gv, 1, rows)
        g2 = _shift_down(gv, 2, rows)
        du_ref[...] = (dhh * ge_ref[...].astype(F32)).astype(BF16)
        dc = dhh * ud_ref[...].astype(F32)
        dcb_ref[0] = jnp.sum(dc, axis=0, keepdims=True)
        dcw_ref[0] = jnp.concatenate(
            [jnp.sum(dc * g2, axis=0, keepdims=True), jnp.sum(dc * g1, axis=0, keepdims=True),
             jnp.sum(dc * gv, axis=0, keepdims=True)], axis=0)
        dg = cw[2:3] * dc + cw[1:2] * _shift_up(dc, 1, rows) + cw[0:1] * _shift_up(dc, 2, rows)
        dg_ref[...] = dg.astype(BF16)

    tspec = pl.BlockSpec((S, fn), lambda b, j: (b, j))
    outs, landed = _call(
        body, "ffn_mid_bwd", (B, Fd // fn),
        [pl.BlockSpec((S, D), lambda b, j: (b, 0)), pl.BlockSpec((fn, D), lambda b, j: (j, 0)), tspec, tspec, tspec,
         pl.BlockSpec((3, fn), lambda b, j: (0, j))],
        [tspec, tspec, pl.BlockSpec((1, 1, fn), lambda b, j: (b, 0, j)), pl.BlockSpec((1, 3, fn), lambda b, j: (b, 0, j))],
        [SDS((T, Fd), BF16), SDS((T, Fd), BF16), SDS((B, 1, Fd), F32), SDS((B, 3, Fd), F32)],
        (dfb, wd, g, ge, ud, cw), carry=carry)
    return (*outs, landed[0] if landed else None)


def ffn_dx_ln(dg, du, wgT, wuT, res, alpha, a, gamma, carry=None):
    T, Fd = dg.shape
    D = wgT.shape[1]
    tm = _tile(T, 256, 16)

    def body(dg_ref, du_ref, wg_ref, wu_ref, r_ref, a_ref, g_ref, da_ref, dab_ref, dgm_ref, dbt_ref):
        @pl.when(pl.program_id(0) == 0)
        def _():
            dgm_ref[...] = jnp.zeros_like(dgm_ref)
            dbt_ref[...] = jnp.zeros_like(dbt_ref)

        dh = alpha * r_ref[...] + _dot(dg_ref[...], wg_ref[...]) + _dot(du_ref[...], wu_ref[...])
        da, sg, sb = _ln_bwd_tile(dh, a_ref[...], g_ref[...])
        da_ref[...] = da
        dab_ref[...] = da.astype(BF16)
        dgm_ref[...] += sg
        dbt_ref[...] += sb

    a_spec = pl.BlockSpec((tm, Fd), lambda i: (i, 0))
    w_spec = pl.BlockSpec((Fd, D), lambda i: (0, 0))
    row = pl.BlockSpec((tm, D), lambda i: (i, 0))
    vec = pl.BlockSpec((1, D), lambda i: (0, 0))
    outs, landed = _call(body, "ffn_dx_ln", (T // tm,), [a_spec, a_spec, w_spec, w_spec, row, row, vec], [row, row, vec, vec],
                         [SDS((T, D), F32), SDS((T, D), BF16), SDS((1, D), F32), SDS((1, D), F32)],
                         (dg, du, wgT, wuT, res, a, gamma.reshape(1, D)), sem=("arbitrary",), carry=carry)
    return (*outs, landed[0] if landed else None)


def _partner_all(x):
    n = x.shape[0]
    r = lax.broadcasted_iota(jnp.int32, x.shape, 0)
    return jnp.where((r % HEAD_DIM) < HEAD_DIM // 2, pltpu.roll(x, n - HEAD_DIM // 2, 0), pltpu.roll(x, HEAD_DIM // 2, 0))


def proj_T(w, xT3, cosT, sinT, blk_off, rope, scale, name, carry=None):
    G, K, T = xT3.shape
    S = cosT.shape[2]
    Dout = K
    tt = _tile(S, 1024, 128)
    H = Dout // HEAD_DIM
    nS = S // tt

    def body(w_ref, x_ref, c_ref, s_ref, o_ref):
        acc = _dot(w_ref[...], x_ref[0])
        if rope:
            cos = jnp.tile(c_ref[0], (H, 1))
            sin = jnp.tile(s_ref[0], (H, 1))
            acc = acc * cos + _partner_all(acc) * sin
        if scale != 1.0:
            acc = acc * scale
        o_ref[0] = acc.astype(BF16)

    tab = pl.BlockSpec((1, HEAD_DIM, tt), lambda g, j: (g, 0, j % nS))
    outs, landed = _call(
        body, name, (G, T // tt),
        [pl.BlockSpec((Dout, K), lambda g, j: (g + blk_off, 0)), pl.BlockSpec((1, K, tt), lambda g, j: (g, 0, j)), tab, tab],
        [pl.BlockSpec((1, Dout, tt), lambda g, j: (g, 0, j))], [SDS((G, Dout, T), BF16)], (w, xT3, cosT, sinT), carry=carry)
    return outs[0], landed


def _attn_bias():
    kj = lax.broadcasted_iota(jnp.int32, (2 * BLK, BLK), 0)
    qi = lax.broadcasted_iota(jnp.int32, (2 * BLK, BLK), 1)
    ok = ((kj >= BLK) & (kj - BLK <= qi)) | ((kj < BLK) & (kj >= qi))
    return jnp.where(ok, 0.0, NEG).astype(F32)


def _has_prev(g, S):
    nb = S // (DILATIONS[g] * BLK)
    return [(n % nb) != 0 for n in range(S // BLK)]


def _win(ref, n, hp):
    lo = (n - 1) * BLK if hp else n * BLK
    return ref[0, :, lo:(n + 1) * BLK]


def attn_fwd(qT3, kT3, vT3, bias, g, B, S):
    _, D, T = qT3.shape
    H = D // HEAD_DIM
    nblk = S // BLK
    hp = _has_prev(g, S)

    def body(q_ref, k_ref, v_ref, b_ref, o_ref, l_ref, s_scr, p_scr, rl_scr):
        for n in range(nblk):
            lo = 0 if hp[n] else BLK
            s_scr[n, lo:, :] = _dot(_win(k_ref, n, hp[n]), q_ref[0, :, n * BLK:(n + 1) * BLK], TN)
        for n in range(nblk):
            lo = 0 if hp[n] else BLK
            sT = s_scr[n, lo:, :] + b_ref[lo:, :]
            m = jnp.max(sT, axis=0, keepdims=True)
            p = jnp.exp(sT - m)
            l = jnp.sum(p, axis=0, keepdims=True)
            p_scr[n, lo:, :] = p.astype(BF16)
            rl_scr[n:n + 1, :] = 1.0 / l
            l_ref[0, :, n * BLK:(n + 1) * BLK] = m + jnp.log(l)
        for n in range(nblk):
            lo = 0 if hp[n] else BLK
            o_ref[:, n * BLK:(n + 1) * BLK] = _dot(_win(v_ref, n, hp[n]), p_scr[n, lo:, :]) * rl_scr[n:n + 1, :]

    spec = pl.BlockSpec((1, HEAD_DIM, S), lambda b, h: (g, h, b))
    return pl.pallas_call(
        body, name=f"attn_fwd_g{g}", grid=(B, H),
        in_specs=[spec, spec, spec, pl.BlockSpec((2 * BLK, BLK), lambda b, h: (0, 0))],
        out_specs=[pl.BlockSpec((HEAD_DIM, S), lambda b, h: (h, b)), pl.BlockSpec((1, 1, S), lambda b, h: (h, 0, b))],
        out_shape=[SDS((D, T), F32), SDS((H, 1, T), F32)],
        scratch_shapes=[pltpu.VMEM((nblk, 2 * BLK, BLK), F32), pltpu.VMEM((nblk, 2 * BLK, BLK), BF16),
                        pltpu.VMEM((nblk, BLK), F32)],
        compiler_params=_cp(),
    )(qT3, kT3, vT3, bias)


def attn_combine(oTs, lses):
    G = len(oTs)
    D, T = oTs[0].shape
    H = D // HEAD_DIM
    tn = _tile(T, 2048, 128)
    hb = _tile(H, 4, 1)

    def body(*refs):
        o_refs, l_refs = refs[:G], refs[G:2 * G]
        ob_ref, of_ref, lt_ref = refs[2 * G:]
        ls = [r[...] for r in l_refs]
        m = functools.reduce(jnp.maximum, ls)
        es = [jnp.exp(v - m) for v in ls]
        z = functools.reduce(lambda a, b: a + b, es)
        o = (es[0] / z) * o_refs[0][...].reshape(hb, HEAD_DIM, tn)
        for i in range(1, G):
            o = o + (es[i] / z) * o_refs[i][...].reshape(hb, HEAD_DIM, tn)
        o = o.reshape(hb * HEAD_DIM, tn)
        ob_ref[...] = o.astype(BF16)
        of_ref[...] = o
        lt_ref[...] = m + jnp.log(z)

    ospec = pl.BlockSpec((hb * HEAD_DIM, tn), lambda h, j: (h, j))
    lspec = pl.BlockSpec((hb, 1, tn), lambda h, j: (h, 0, j))
    return pl.pallas_call(
        body, name="attn_combine", grid=(H // hb, T // tn),
        in_specs=[ospec] * G + [lspec] * G, out_specs=[ospec, ospec, lspec],
        out_shape=[SDS((D, T), BF16), SDS((D, T), F32), SDS((H, 1, T), F32)], compiler_params=_cp(),
    )(*oTs, *lses)


def attn_delta(doT, oT):
    D, T = doT.shape
    H = D // HEAD_DIM
    tn = _tile(T, 2048, 128)
    hb = _tile(H, 4, 1)

    def body(d_ref, o_ref, r_ref):
        prod = (d_ref[...].astype(F32) * o_ref[...]).reshape(hb, HEAD_DIM, tn)
        r_ref[...] = jnp.sum(prod, axis=1, keepdims=True)

    spec = pl.BlockSpec((hb * HEAD_DIM, tn), lambda h, j: (h, j))
    return pl.pallas_call(
        body, name="attn_delta", grid=(H // hb, T // tn), in_specs=[spec, spec],
        out_specs=pl.BlockSpec((hb, 1, tn), lambda h, j: (h, 0, j)),
        out_shape=SDS((H, 1, T), F32), compiler_params=_cp(),
    )(doT, oT)


def attn_bwd(qT3, kT3, vT3, doT, lse, delta, cosT, sinT, bias, g, q_scale, B, S, dk_prev=None, dv_prev=None):
    _, D, T = qT3.shape
    H = D // HEAD_DIM
    nblk = S // BLK
    half = HEAD_DIM // 2
    hp = _has_prev(g, S)
    acc_in = dk_prev is not None
    kv_dtype = BF16 if acc_in else F32

    def body(*refs):
        q_ref, k_ref, v_ref, do_ref, l_ref, d_ref, c_ref, s_ref, b_ref = refs[:9]
        rest = refs[9:]
        if acc_in:
            dkp_ref, dvp_ref = rest[:2]
            rest = rest[2:]
        dq_ref, dk_ref, dv_ref, s_scr, dp_scr, p_scr, ds_scr = rest
        for n in range(nblk):
            lo = 0 if hp[n] else BLK
            blk = slice(n * BLK, (n + 1) * BLK)
            s_scr[n, lo:, :] = _dot(_win(k_ref, n, hp[n]), q_ref[0, :, blk], TN)
            dp_scr[n, lo:, :] = _dot(_win(v_ref, n, hp[n]), do_ref[:, blk], TN)
        for n in range(nblk):
            lo = 0 if hp[n] else BLK
            blk = slice(n * BLK, (n + 1) * BLK)
            pT = jnp.exp(s_scr[n, lo:, :] + b_ref[lo:, :] - l_ref[0, :, blk])
            p_scr[n, lo:, :] = pT.astype(BF16)
            ds_scr[n, lo:, :] = (pT * (dp_scr[n, lo:, :] - d_ref[0, :, blk])).astype(BF16)
        for j in range(nblk):
            blk = slice(j * BLK, (j + 1) * BLK)
            if j + 1 < nblk and hp[j + 1]:
                two = slice(j * BLK, (j + 2) * BLK)
                pj = jnp.concatenate([p_scr[j, BLK:, :], p_scr[j + 1, :BLK, :]], axis=1)
                dsj = jnp.concatenate([ds_scr[j, BLK:, :], ds_scr[j + 1, :BLK, :]], axis=1)
                dv = _dot(do_ref[:, two], pj, NT)
                dk = _dot(q_ref[0, :, two], dsj, NT)
            else:
                dv = _dot(do_ref[:, blk], p_scr[j, BLK:, :], NT)
                dk = _dot(q_ref[0, :, blk], ds_scr[j, BLK:, :], NT)
            dk = dk * c_ref[0, :, blk] - pltpu.roll(dk, half, 0) * s_ref[0, :, blk]
            if acc_in:
                dk = dk + dkp_ref[:, blk]
                dv = dv + dvp_ref[:, blk]
            dk_ref[:, blk] = dk.astype(kv_dtype)
            dv_ref[:, blk] = dv.astype(kv_dtype)
            lo = 0 if hp[j] else BLK
            dq = _dot(_win(k_ref, j, hp[j]), ds_scr[j, lo:, :])
            dq = dq * c_ref[0, :, blk] - pltpu.roll(dq, half, 0) * s_ref[0, :, blk]
            dq_ref[:, blk] = (dq * q_scale).astype(BF16)

    spec3 = pl.BlockSpec((1, HEAD_DIM, S), lambda b, h: (g, h, b))
    spec = pl.BlockSpec((HEAD_DIM, S), lambda b, h: (h, b))
    sspec = pl.BlockSpec((1, 1, S), lambda b, h: (h, 0, b))
    tab = pl.BlockSpec((1, HEAD_DIM, S), lambda b, h: (g, 0, 0))
    in_specs = [spec3, spec3, spec3, spec, sspec, sspec, tab, tab, pl.BlockSpec((2 * BLK, BLK), lambda b, h: (0, 0))]
    args = [qT3, kT3, vT3, doT, lse, delta, cosT, sinT, bias]
    if acc_in:
        in_specs += [spec, spec]
        args += [dk_prev, dv_prev]
    return pl.pallas_call(
        body, name=f"attn_bwd_g{g}" + ("_acc" if acc_in else ""), grid=(B, H),
        in_specs=in_specs, out_specs=[spec, spec, spec],
        out_shape=[SDS((D, T), BF16), SDS((D, T), kv_dtype), SDS((D, T), kv_dtype)],
        scratch_shapes=[pltpu.VMEM((nblk, 2 * BLK, BLK), F32), pltpu.VMEM((nblk, 2 * BLK, BLK), F32),
                        pltpu.VMEM((nblk, 2 * BLK, BLK), BF16), pltpu.VMEM((nblk, 2 * BLK, BLK), BF16)],
        compiler_params=_cp(),
    )(*args)


def adamw(w, g, m, v, name):
    R, C = w.shape
    tr = _tile(R, 512, 8)

    def body(w_ref, g_ref, m_ref, v_ref, d_ref, nm_ref, nv_ref):
        gv = g_ref[...]
        nm = ADAM_B1 * m_ref[...] + (1.0 - ADAM_B1) * gv
        nv = ADAM_B2 * v_ref[...] + (1.0 - ADAM_B2) * (gv * gv)
        m_hat = nm / (1.0 - ADAM_B1 ** ADAM_STEP)
        v_hat = nv / (1.0 - ADAM_B2 ** ADAM_STEP)
        d_ref[...] = -ADAM_LR * (m_hat / (jnp.sqrt(v_hat) + ADAM_EPS) + ADAM_WD * w_ref[...])
        nm_ref[...] = nm
        nv_ref[...] = nv

    spec = pl.BlockSpec((tr, C), lambda i: (i, 0))
    return pl.pallas_call(
        body, name=name, grid=(R // tr,), in_specs=[spec] * 4, out_specs=[spec] * 3,
        out_shape=[SDS((R, C), F32)] * 3, compiler_params=_cp(),
    )(w, g, m, v)


def _perm(a, B, S, d):
    if d == 1:
        return a
    lead = a.shape[:-1]
    return a.reshape(*lead, B, S // d, d).swapaxes(-1, -2).reshape(*lead, B * S)


def _unperm(a, B, S, d):
    if d == 1:
        return a
    lead = a.shape[:-1]
    return a.reshape(*lead, B, d, S // d).swapaxes(-1, -2).reshape(*lead, B * S)


def _perm3(a, B, S):
    return jnp.stack([_perm(a, B, S, d) for d in DILATIONS])


def _xT3(xb, B, S):
    D = xb.shape[1]
    outs = []
    for d in DILATIONS:
        outs.append(xb.reshape(B, S // d, d, D).transpose(3, 0, 2, 1).reshape(D, B * S))
    return jnp.stack(outs)


def _rope_tables(S):
    half = HEAD_DIM // 2
    inv_freq = ROPE_THETA ** (-jnp.arange(0, HEAD_DIM, 2, dtype=F32) / HEAD_DIM)
    ang = jnp.arange(S, dtype=F32)[:, None] * inv_freq[None, :]
    cos = jnp.concatenate([jnp.cos(ang), jnp.cos(ang)], axis=1).T
    sin = jnp.concatenate([-jnp.sin(ang), jnp.sin(ang)], axis=1).T
    return _perm3(cos, 1, S), _perm3(sin, 1, S)


def kernel(x, pool_w, pool_scale, w_q, w_kv, w_o, ffn_w_gate, ffn_w_up, ffn_conv_w, ffn_conv_b, ffn_w_down, ln1_g, ln1_b, ln2_g, ln2_b, loss_target, m_pool_w, m_pool_scale, m_w_q, m_w_kv, m_w_o, m_ffn_w_gate, m_ffn_w_up, m_ffn_conv_w, m_ffn_conv_b, m_ffn_w_down, m_ln1_g, m_ln1_b, m_ln2_g, m_ln2_b, v_pool_w, v_pool_scale, v_w_q, v_w_kv, v_w_o, v_ffn_w_gate, v_ffn_w_up, v_ffn_conv_w, v_ffn_conv_b, v_ffn_w_down, v_ln1_g, v_ln1_b, v_ln2_g, v_ln2_b):
    B, S, D = x.shape
    T = B * S
    depth = ln1_g.shape[0]
    nA, nB = pool_w.shape[0], w_q.shape[0]
    Fs = ffn_w_down.shape[1]
    Fd = Fs * N_DEV
    H = D // HEAD_DIM
    G = len(DILATIONS)
    PG = len(POOL_WINDOWS)
    Cg = D // PG
    alpha = (2.0 * depth) ** 0.25
    me = 4 * lax.axis_index("x") + 2 * lax.axis_index("y") + lax.axis_index("c")

    qs, kvs, os_ = w_q.shape[2], w_kv.shape[1], w_o.shape[1]
    pool_rows = pool_w.size // D
    local = {("pool",): pool_w.reshape(pool_rows, D).astype(BF16), ("wkv",): w_kv.T.astype(BF16)}
    for j in range(nB):
        local[("wq", j)] = w_q[j].T.astype(BF16)
        local[("wo", j)] = w_o[j].astype(BF16)
    for i in range(depth):
        local[("wg", i)] = ffn_w_gate[i].T.astype(BF16)
        local[("wu", i)] = ffn_w_up[i].T.astype(BF16)
        local[("wd", i)] = ffn_w_down[i].astype(BF16)
    ffn = lambda i: [("wg", i), ("wu", i), ("wd", i)]
    queue = [[("pool",)]]
    if depth == 4 and nA == 2 and nB == 2:
        queue += [[("wg", 0)], [("wu", 0)], [("wd", 0), ("wg", 1), ("wu", 1)], [("wd", 1)],
                  [("wo", 0)], [("wo", 1)], [("wkv",), ("wq", 0)], [("wu", 2)], [("wd", 2)], [("wq", 1)],
                  [("wg", 2)], ffn(3)]
    gathered = {}

    def land(keys, arrs):
        for k, a in zip(keys or (), arrs or ()):
            gathered[k] = a.reshape(-1, D)

    def next_gather():
        if not queue:
            return None, None
        keys = queue.pop(0)
        if not keys:
            return None, None
        return keys, Gather([local[k] for k in keys])

    def weight(key):
        if key not in gathered:
            keys = [key]
            for bi, batch in enumerate(queue):
                if key in batch:
                    keys = queue.pop(bi)
                    break
            blk = all_gather_blocks(jnp.concatenate([local[k] for k in keys], axis=0), "gather_" + "_".join(map(str, key)), in_vmem=False)
            off = 0
            for k in keys:
                r = local[k].shape[0]
                gathered[k] = blk[:, off:off + r].reshape(-1, D)
                off += r
        return gathered[key]

    PW = weight(("pool",)).reshape(N_DEV, nA, PG, Cg // N_DEV, Cg).transpose(1, 2, 0, 3, 4).reshape(nA, PG, Cg, Cg)

    sm_cols = 128
    sm_local = jnp.concatenate([ffn_conv_w.reshape(-1), pool_scale.reshape(-1)])
    sm_rows = -(-sm_local.size // sm_cols)
    sm_rows_p = -(-sm_rows // 8) * 8
    sm_local = jnp.pad(sm_local, (0, sm_rows_p * sm_cols - sm_local.size)).reshape(sm_rows_p, sm_cols)
    sm = all_gather_blocks(sm_local, "gather_small", in_vmem=True).reshape(N_DEV, -1)
    ncw = ffn_conv_w.size
    conv_w_full = sm[:, :ncw].reshape(N_DEV, depth, 3, Fs).transpose(1, 2, 0, 3).reshape(depth, 3, Fd)
    pool_scale_full = sm[:, ncw:ncw + pool_scale.size].reshape(N_DEV, nA, D // N_DEV).transpose(1, 0, 2).reshape(nA, 1, D)

    cosT, sinT = _rope_tables(S)
    bias = _attn_bias()

    xs = x.reshape(T, D)
    saved = []
    cur, curb = xs, None
    kT = vT = x1T3 = None
    for i in range(depth):
        sv = {}
        if i < nA:
            keys, cr = next_gather()
            mix, pooled, got = pool_fwd(cur, PW[i], pool_scale_full[i], B, S, carry=cr)
            land(keys, got)
            sv["pooled"] = pooled
            keys, cr = next_gather()
            a1, h, hb, got = add_ln(cur, mix, ln1_g[i], ln1_b[i], alpha, carry=cr)
            land(keys, got)
        else:
            j = i - nA
            xT3 = x1T3 if j == 0 else _xT3(curb, B, S)
            keys, cr = next_gather()
            qT, got = proj_T(weight(("wq", j)), xT3, cosT, sinT, 0, True, HEAD_DIM ** -0.5, "q_proj", carry=cr)
            land(keys, got)
            oTs, lses = [], []
            for gi, d in enumerate(DILATIONS):
                o_g, lse_g = attn_fwd(qT, kT, vT, bias, gi, B, S)
                oTs.append(_unperm(o_g, B, S, d))
                lses.append(_unperm(lse_g, B, S, d))
            oTb, oTf, lse_tot = attn_combine(oTs, lses)
            a1, h, hb, _ = matmul_ln(oTb, weight(("wo", j)), TN, cur, ln1_g[i], ln1_b[i], alpha, "o_proj_ln")
            sv.update(xT3=xT3, qT=qT, oTb=oTb, oTf=oTf, lse_tot=lse_tot)
        wg_i, wu_i = weight(("wg", i)), weight(("wu", i))
        keys, cr = next_gather()
        g, ge, ud, hh, got = ffn_up(hb, wg_i, wu_i, conv_w_full[i], ffn_conv_b[i].reshape(1, Fd), B, S, carry=cr)
        land(keys, got)
        wd_i = weight(("wd", i))
        keys, cr = next_gather()
        a2, cur, curb, got = matmul_ln(hh, wd_i, NN, h, ln2_g[i], ln2_b[i], alpha, "ffn_down_ln", carry=cr)
        land(keys, got)
        sv.update(a1=a1, hb=hb, g=g, ge=ge, ud=ud, hh=hh, a2=a2)
        saved.append(sv)
        if i == nA - 1:
            x1T3 = _xT3(curb, B, S)
            wkv = weight(("wkv",))
            keys, cr = next_gather()
            kT, got = proj_T(wkv, x1T3, cosT, sinT, 0, True, 1.0, "k_proj", carry=cr)
            land(keys, got)
            keys, cr = next_gather()
            vT, got = proj_T(wkv, x1T3, cosT, sinT, G, False, 1.0, "v_proj", carry=cr)
            land(keys, got)


    small = {k: [None] * depth for k in ("ln1_g", "ln1_b", "ln2_g", "ln2_b", "conv_b", "conv_w")}
    dscale = [None] * nA
    dpw = [None] * nA
    dk_acc, dv_acc = [None] * G, [None] * G

    def blocks(a, rows):
        return a.reshape(N_DEV, rows, D)

    pending, landed = [], {}

    def next_carry():
        if not pending:
            return None, None
        key, parts = pending.pop(0)
        return key, Scatter(parts)

    dcur = sq = ln2_done = None
    for i in reversed(range(depth)):
        sv = saved[i]
        if i == depth - 1:
            db2, db2b, small["ln2_g"][i], small["ln2_b"][i], sq = loss_ln_bwd(cur, loss_target.reshape(T, D), sv["a2"], ln2_g[i])
        elif ln2_done is not None:
            db2, db2b, small["ln2_g"][i], small["ln2_b"][i] = ln2_done
            ln2_done = None
        else:
            db2, db2b, small["ln2_g"][i], small["ln2_b"][i] = ln_bwd(dcur, sv["a2"], ln2_g[i])
        key, cr = next_carry()
        dg_, du_, dcb, dcw, got = ffn_mid_bwd(db2b, gathered[("wd", i)], sv["g"], sv["ge"], sv["ud"], conv_w_full[i], B, S, carry=cr)
        if cr is not None:
            landed[key] = got
        small["conv_b"][i] = jnp.sum(dcb, axis=0)
        small["conv_w"][i] = jnp.sum(dcw, axis=0)
        key, cr = next_carry()
        dwd, got = wgrad_rows(sv["hh"], db2b, "wgrad_down", carry=cr)
        if cr is not None:
            landed[key] = got
        dwg, landed[("down", i)] = wgrad_rows(dg_, sv["hb"], "wgrad_gate", carry=Scatter([blocks(dwd, Fs)]))
        dwu, landed[("gate", i)] = wgrad_rows(du_, sv["hb"], "wgrad_up", carry=Scatter([blocks(dwg, Fs)]))
        da1, da1b, small["ln1_g"][i], small["ln1_b"][i], landed[("up", i)] = ffn_dx_ln(
            dg_, du_, gathered[("wg", i)], gathered[("wu", i)], db2, alpha, sv["a1"], ln1_g[i], carry=Scatter([blocks(dwu, Fs)]))
        if i < nA:
            dcur, dsp, dpwp = pool_bwd(da1, sv["pooled"], PW[i], pool_scale_full[i], alpha, B, S)
            dscale[i] = jnp.sum(dsp, axis=0)
            dpw[i] = jnp.sum(dpwp, axis=0)
        else:
            j = i - nA
            doT = matmul_to_T(gathered[("wo", j)], da1b, "o_proj_bwd")
            dwo = wgrad_mixed(sv["oTb"], da1b, "wgrad_o")
            delta = attn_delta(doT, sv["oTf"])
            dq_tok, dwq = [], []
            for gi, d in enumerate(DILATIONS):
                dq_g, dk_acc[gi], dv_acc[gi] = attn_bwd(
                    sv["qT"], kT, vT, _perm(doT, B, S, d), _perm(sv["lse_tot"], B, S, d), _perm(delta, B, S, d),
                    cosT, sinT, bias, gi, HEAD_DIM ** -0.5, B, S, dk_prev=dk_acc[gi], dv_prev=dv_acc[gi])
                dwq.append(wgrad_T(dq_g, sv["xT3"], gi, "wgrad_q"))
                dq_tok.append(_unperm(dq_g, B, S, d))
            dwq = jnp.concatenate(dwq, axis=0)
            below = (saved[i - 1]["a2"], ln2_g[i - 1]) if i > 0 else None
            if j > 0:
                last = dx_from_T(dq_tok, gathered[("wq", j)], da1, alpha, "q_proj_bwd", 512, ln=below)
            else:
                dcur = dx_from_T(dq_tok, gathered[("wq", j)], da1, alpha, "q_proj_bwd", 512)
                dkv = [a.astype(BF16) for a in dk_acc + dv_acc]
                dwkv = jnp.concatenate([wgrad_T(a, x1T3, gi % G, "wgrad_kv") for gi, a in enumerate(dkv)], axis=0)
                dkv_tok = [_unperm(a, B, S, DILATIONS[gi % G]) for gi, a in enumerate(dkv)]
                last = dx_from_T(dkv_tok, gathered[("wkv",)], dcur, 1.0, "kv_proj_bwd", 256, ln=below)
                pending.append((("kv",), [blocks(dwkv, kvs)]))
            if below is None:
                dcur = last
            else:
                ln2_done = last
            pending.append((("attn", j), [blocks(dwq, qs), blocks(dwo, os_)]))
    grad_x = dcur.reshape(B, S, D)

    dpw_all = jnp.stack(dpw).reshape(nA, PG, N_DEV, Cg // N_DEV, Cg).transpose(2, 0, 1, 3, 4).reshape(N_DEV, pool_rows, D)
    tail_keys = [k for k, _ in pending] + [("pool",)]
    tail_parts = [parts for _, parts in pending] + [[dpw_all.astype(BF16)]]
    tail_rows = [sum(p.shape[1] for p in parts) for parts in tail_parts]
    tail = scatter_partials([p for parts in tail_parts for p in parts], "scatter_tail")
    for t, key in enumerate(tail_keys):
        lo = sum(tail_rows[:t])
        landed[key] = tail[:, lo:lo + tail_rows[t]]

    def reduced(key):
        return sum_slots(landed[key], "sum_" + "_".join(str(k) for k in key))

    g_attn = [reduced(("attn", j)) for j in range(nB)]
    g_w_q = jnp.swapaxes(jnp.stack([a[:qs] for a in g_attn]), 1, 2)
    g_w_o = jnp.stack([a[qs:] for a in g_attn])
    g_w_kv = reduced(("kv",)).T
    g_gate = jnp.swapaxes(jnp.stack([reduced(("gate", i)) for i in range(depth)]), 1, 2)
    g_up = jnp.swapaxes(jnp.stack([reduced(("up", i)) for i in range(depth)]), 1, 2)
    g_down = jnp.stack([reduced(("down", i)) for i in range(depth)])
    g_pool_w = reduced(("pool",)).reshape(pool_w.shape)

    def rows_of(a):
        a = a.reshape(-1)
        n = -(-a.size // D) * D
        return jnp.pad(a, (0, n - a.size)).reshape(-1, D)

    sm_parts = [rows_of(jnp.concatenate(small[k], axis=0)) for k in ("ln1_g", "ln1_b", "ln2_g", "ln2_b")]
    sm_parts += [rows_of(jnp.stack(small["conv_b"])), rows_of(jnp.stack(small["conv_w"])), rows_of(jnp.stack(dscale)), sq]
    sm_sizes = [p.shape[0] for p in sm_parts]
    sm_all = jnp.concatenate(sm_parts, axis=0)
    pad_rows = -(-sm_all.shape[0] // 8) * 8 - sm_all.shape[0]
    sm_all = jnp.pad(sm_all, ((0, pad_rows), (0, 0)))
    sm_sum = sum_slots(all_gather_blocks(sm_all, "gather_small_grads", in_vmem=True), "sum_small_grads")
    sm_offs = [sum(sm_sizes[:i]) for i in range(len(sm_sizes))]

    def sm_take(i, shape):
        n = math.prod(shape)
        return sm_sum[sm_offs[i]:sm_offs[i] + sm_sizes[i]].reshape(-1)[:n].reshape(shape)

    g_ln1_g, g_ln1_b = sm_take(0, (depth, D)), sm_take(1, (depth, D))
    g_ln2_g, g_ln2_b = sm_take(2, (depth, D)), sm_take(3, (depth, D))
    g_conv_b = sm_take(4, (depth, Fd))
    g_conv_w = lax.dynamic_slice_in_dim(sm_take(5, (depth, 3, Fd)), me * Fs, Fs, axis=2)
    g_pool_scale = lax.dynamic_slice_in_dim(sm_take(6, (nA, D)), me * (D // N_DEV), D // N_DEV, axis=1)
    loss = (0.5 / D) * jnp.sum(sm_take(7, (D,)))

    def v2(a):
        return a.reshape(-1, a.shape[-1])

    names = ["pool_w", "pool_scale", "w_q", "w_kv", "w_o", "ffn_w_gate", "ffn_w_up", "ffn_conv_w", "ffn_conv_b",
             "ffn_w_down", "ln1_g", "ln1_b", "ln2_g", "ln2_b"]
    ws = [pool_w, pool_scale, w_q, w_kv, w_o, ffn_w_gate, ffn_w_up, ffn_conv_w, ffn_conv_b, ffn_w_down, ln1_g, ln1_b, ln2_g, ln2_b]
    ms = [m_pool_w, m_pool_scale, m_w_q, m_w_kv, m_w_o, m_ffn_w_gate, m_ffn_w_up, m_ffn_conv_w, m_ffn_conv_b, m_ffn_w_down, m_ln1_g, m_ln1_b, m_ln2_g, m_ln2_b]
    vs = [v_pool_w, v_pool_scale, v_w_q, v_w_kv, v_w_o, v_ffn_w_gate, v_ffn_w_up, v_ffn_conv_w, v_ffn_conv_b, v_ffn_w_down, v_ln1_g, v_ln1_b, v_ln2_g, v_ln2_b]
    gs = [g_pool_w, g_pool_scale, g_w_q, g_w_kv, g_w_o, g_gate, g_up, g_conv_w, g_conv_b, g_down, g_ln1_g, g_ln1_b, g_ln2_g, g_ln2_b]
    deltas, new_ms, new_vs = [], [], []
    for nm, w, gr, m_, v_ in zip(names, ws, gs, ms, vs):
        d_, nm_, nv_ = adamw(v2(w), v2(gr), v2(m_), v2(v_), "adamw_" + nm)
        deltas.append(d_.reshape(w.shape))
        new_ms.append(nm_.reshape(w.shape))
        new_vs.append(nv_.reshape(w.shape))

    return (loss, grad_x, *gs, *deltas, *new_ms, *new_vs)
```

```python
import functools
import math

import jax
import jax.numpy as jnp
from jax import lax
from jax.experimental import pallas as pl
from jax.experimental.pallas import tpu as pltpu

F32 = jnp.float32
BF16 = jnp.bfloat16
SDS = jax.ShapeDtypeStruct
MESH = pl.DeviceIdType.MESH

N_DEV = 8
HEAD_DIM = 64
BLK = 128
DILATIONS = (1, 4, 16)
POOL_WINDOWS = (2, 4, 8, 16)
ROPE_THETA = 10000.0
LN_EPS = 1e-5
NEG = -1e30
V7X_VMEM_LIMIT = 56 * 1024 * 1024

ADAM_LR, ADAM_B1, ADAM_B2, ADAM_EPS, ADAM_WD, ADAM_STEP = 0.001, 0.9, 0.999, 1e-08, 0.01, 10

NN = (((1,), (0,)), ((), ()))
NT = (((1,), (1,)), ((), ()))
TN = (((0,), (0,)), ((), ()))


def _cp(sem=None):
    kw = dict(vmem_limit_bytes=V7X_VMEM_LIMIT)
    if sem is not None:
        kw["dimension_semantics"] = sem
    return pltpu.CompilerParams(**kw)


def _dot(a, b, dims=NN):
    return lax.dot_general(a, b, dims, preferred_element_type=F32)


def _tile(n, target, mult):
    best = None
    for t in range(mult, min(n, target) + 1, mult):
        if n % t == 0:
            best = t
    return best if best is not None else n


def _mesh_pos():
    return lax.axis_index("x"), lax.axis_index("y"), lax.axis_index("c")


def all_gather_blocks(xl, name, in_vmem):
    R, C = xl.shape
    space = pltpu.VMEM if in_vmem else pl.ANY

    def body(x_ref, out_ref, send_sems, recv_sems, local_sem):
        x, y, c = _mesh_pos()
        me, sibling = (x, y, c), (x, y, 1 - c)
        chips = [(1 - x, y), (x, 1 - y), (1 - x, 1 - y)]

        def slot(px, py, pc):
            return out_ref.at[4 * px + 2 * py + pc]

        def copy(k, block, to, src=None):
            return pltpu.make_async_remote_copy(
                src_ref=slot(*block) if src is None else src, dst_ref=slot(*block),
                send_sem=send_sems.at[k], recv_sem=recv_sems.at[k], device_id=to, device_id_type=MESH)

        mine = pltpu.make_async_copy(x_ref, slot(*me), local_sem)
        mine.start()
        first = [copy(0, me, sibling, src=x_ref)]
        first += [copy(1 + j, me, (*chip, c), src=x_ref) for j, chip in enumerate(chips)]
        for cp in first:
            cp.start()
        passed = [copy(4 + j, (*chip, c), sibling) for j, chip in enumerate(chips)]
        for j, chip in enumerate(chips):
            copy(1 + j, (*chip, c), me).wait_recv()
            passed[j].start()
        copy(0, sibling, me).wait_recv()
        for j, chip in enumerate(chips):
            copy(4 + j, (*chip, 1 - c), me).wait_recv()
        for cp in first + passed:
            cp.wait_send()
        mine.wait()

    return pl.pallas_call(
        body, name=name,
        out_shape=SDS((N_DEV, R, C), xl.dtype),
        in_specs=[pl.BlockSpec(memory_space=space)],
        out_specs=pl.BlockSpec(memory_space=space),
        scratch_shapes=[pltpu.SemaphoreType.DMA((7,)), pltpu.SemaphoreType.DMA((7,)), pltpu.SemaphoreType.DMA],
        compiler_params=_cp(),
    )(xl)


def _peers():
    x, y, c = _mesh_pos()
    peers = []
    for r in range(1, N_DEV):
        peers.append((1 - x if (r & 4) else x, 1 - y if (r & 2) else y, 1 - c if (r & 1) else c))
    return 4 * x + 2 * y + c, peers


class Gather:
    def __init__(self, parts):
        self.parts = list(parts)
        n = len(self.parts)
        self.out_shapes = [SDS((N_DEV,) + p.shape, p.dtype) for p in self.parts]
        self.scratch = [pltpu.SemaphoreType.DMA((7 * n,)), pltpu.SemaphoreType.DMA((7 * n,)), pltpu.SemaphoreType.DMA((n,))]

    def start(self, part_refs, out_refs, send_sems, recv_sems, local_sems):
        me_lin, peers = _peers()
        n = len(self.parts)
        for i in range(n):
            pltpu.make_async_copy(part_refs[i], out_refs[i].at[me_lin], local_sems.at[i]).start()
        for k, peer in enumerate(peers):
            for i in range(n):
                pltpu.make_async_remote_copy(
                    src_ref=part_refs[i], dst_ref=out_refs[i].at[me_lin],
                    send_sem=send_sems.at[k * n + i], recv_sem=recv_sems.at[k * n + i],
                    device_id=peer, device_id_type=MESH).start()

    def wait(self, out_refs, send_sems, recv_sems, local_sems):
        me_lin, peers = _peers()
        n = len(self.parts)
        for k, (px, py, pc) in enumerate(peers):
            p_lin = 4 * px + 2 * py + pc
            for i in range(n):
                arrival = pltpu.make_async_remote_copy(
                    src_ref=out_refs[i].at[p_lin], dst_ref=out_refs[i].at[p_lin],
                    send_sem=send_sems.at[k * n + i], recv_sem=recv_sems.at[k * n + i],
                    device_id=(px, py, pc), device_id_type=MESH)
                arrival.wait_recv()
                arrival.wait_send()
        for i in range(n):
            pltpu.make_async_copy(out_refs[i].at[me_lin], out_refs[i].at[me_lin], local_sems.at[i]).wait()


class Scatter:
    def __init__(self, parts):
        self.parts = list(parts)
        self.rows = [p.shape[1] for p in parts]
        self.offs = [sum(self.rows[:i]) for i in range(len(self.rows))]
        self.out_shapes = [SDS((N_DEV, sum(self.rows), parts[0].shape[2]), parts[0].dtype)]
        self.scratch = [pltpu.SemaphoreType.DMA((7,)), pltpu.SemaphoreType.DMA((7,)), pltpu.SemaphoreType.DMA]

    def start(self, part_refs, out_refs, send_sems, recv_sems, local_sem):
        out_ref = out_refs[0]
        me_lin, peers = _peers()
        for i, (off, r) in enumerate(zip(self.offs, self.rows)):
            pltpu.make_async_copy(part_refs[i].at[me_lin], out_ref.at[me_lin, pl.ds(off, r)], local_sem).start()
        for k, (px, py, pc) in enumerate(peers):
            p_lin = 4 * px + 2 * py + pc
            for i, (off, r) in enumerate(zip(self.offs, self.rows)):
                pltpu.make_async_remote_copy(
                    src_ref=part_refs[i].at[p_lin], dst_ref=out_ref.at[me_lin, pl.ds(off, r)],
                    send_sem=send_sems.at[k], recv_sem=recv_sems.at[k],
                    device_id=(px, py, pc), device_id_type=MESH).start()

    def wait(self, out_refs, send_sems, recv_sems, local_sem):
        out_ref = out_refs[0]
        me_lin, peers = _peers()
        for k, (px, py, pc) in enumerate(peers):
            p_lin = 4 * px + 2 * py + pc
            whole = pltpu.make_async_remote_copy(
                src_ref=out_ref.at[p_lin], dst_ref=out_ref.at[p_lin],
                send_sem=send_sems.at[k], recv_sem=recv_sems.at[k],
                device_id=(px, py, pc), device_id_type=MESH)
            whole.wait_recv()
            whole.wait_send()
        pltpu.make_async_copy(out_ref.at[me_lin], out_ref.at[me_lin], local_sem).wait()


def scatter_partials(parts, name):
    sc = Scatter(parts)
    n = len(parts)

    def body(*refs):
        sc.start(refs[:n], refs[n:n + 1], *refs[n + 1:])
        sc.wait(refs[n:n + 1], *refs[n + 1:])

    return pl.pallas_call(
        body, name=name, out_shape=sc.out_shapes[0],
        in_specs=[pl.BlockSpec(memory_space=pl.ANY)] * n, out_specs=pl.BlockSpec(memory_space=pl.ANY),
        scratch_shapes=sc.scratch, compiler_params=_cp(),
    )(*parts)


def _call(body, name, grid, in_specs, out_specs, out_shape, args, scratch=(), sem=None, carry=None):
    in_specs, out_specs, out_shape, scratch = list(in_specs), list(out_specs), list(out_shape), list(scratch)
    if carry is None:
        outs = pl.pallas_call(body, name=name, grid=grid, in_specs=in_specs, out_specs=out_specs, out_shape=out_shape,
                              scratch_shapes=scratch, compiler_params=_cp(sem))(*args)
        return list(outs), None
    n_in, n_out, n_scr, n_c, n_co = len(in_specs), len(out_specs), len(scratch), len(carry.parts), len(carry.out_shapes)
    last = [g - 1 for g in grid]

    def carried(*refs):
        ins, c_ins = refs[:n_in], refs[n_in:n_in + n_c]
        o0 = n_in + n_c
        outs, c_out = refs[o0:o0 + n_out], refs[o0 + n_out:o0 + n_out + n_co]
        s0 = o0 + n_out + n_co
        scr, c_scr = refs[s0:s0 + n_scr], refs[s0 + n_scr:]
        ids = [pl.program_id(a) for a in range(len(grid))]
        is_first = functools.reduce(jnp.logical_and, [i == 0 for i in ids])
        is_last = functools.reduce(jnp.logical_and, [i == l for i, l in zip(ids, last)])

        @pl.when(is_first)
        def _():
            carry.start(c_ins, c_out, *c_scr)

        body(*ins, *outs, *scr)

        @pl.when(is_last)
        def _():
            carry.wait(c_out, *c_scr)

    hbm = pl.BlockSpec(memory_space=pl.ANY)
    outs = pl.pallas_call(
        carried, name=name + "_carry", grid=grid, in_specs=in_specs + [hbm] * n_c, out_specs=out_specs + [hbm] * n_co,
        out_shape=out_shape + carry.out_shapes, scratch_shapes=scratch + carry.scratch,
        compiler_params=_cp(sem if sem is not None else ("arbitrary",) * len(grid)),
    )(*args, *carry.parts)
    return list(outs[:n_out]), list(outs[n_out:])


def sum_slots(slots, name, out_dtype=F32):
    _, R, C = slots.shape
    tr = _tile(R, 512, 16)

    def body(s_ref, o_ref):
        acc = s_ref[0].astype(F32)
        for s in range(1, N_DEV):
            acc = acc + s_ref[s].astype(F32)
        o_ref[...] = acc.astype(out_dtype)

    return pl.pallas_call(
        body, name=name, grid=(R // tr,),
        in_specs=[pl.BlockSpec((N_DEV, tr, C), lambda i: (0, i, 0))],
        out_specs=pl.BlockSpec((tr, C), lambda i: (i, 0)),
        out_shape=SDS((R, C), out_dtype), compiler_params=_cp(),
    )(slots)


def add_ln(x, mix, g, b, alpha, carry=None):
    T, D = x.shape
    tm = _tile(T, 512, 16)

    def body(x_ref, m_ref, g_ref, b_ref, a_ref, y_ref, yb_ref):
        a = alpha * x_ref[...] + m_ref[...]
        mu = jnp.mean(a, axis=-1, keepdims=True)
        xc = a - mu
        var = jnp.mean(xc * xc, axis=-1, keepdims=True)
        y = xc * lax.rsqrt(var + LN_EPS) * g_ref[...] + b_ref[...]
        a_ref[...] = a
        y_ref[...] = y
        yb_ref[...] = y.astype(BF16)

    row = pl.BlockSpec((tm, D), lambda i: (i, 0))
    vec = pl.BlockSpec((1, D), lambda i: (0, 0))
    outs, landed = _call(body, "add_ln", (T // tm,), [row, row, vec, vec], [row, row, row],
                         [SDS((T, D), F32), SDS((T, D), F32), SDS((T, D), BF16)],
                         (x, mix, g.reshape(1, D), b.reshape(1, D)), carry=carry)
    return (*outs, landed)


def _ln_bwd_tile(dy, a, gamma):
    mu = jnp.mean(a, axis=-1, keepdims=True)
    xc = a - mu
    var = jnp.mean(xc * xc, axis=-1, keepdims=True)
    r = lax.rsqrt(var + LN_EPS)
    xh = xc * r
    dxh = dy * gamma
    m1 = jnp.mean(dxh, axis=-1, keepdims=True)
    m2 = jnp.mean(dxh * xh, axis=-1, keepdims=True)
    da = r * (dxh - m1 - xh * m2)
    return da, jnp.sum(dy * xh, axis=0, keepdims=True), jnp.sum(dy, axis=0, keepdims=True)


def ln_bwd(dy, a, g):
    T, D = a.shape
    tm = _tile(T, 512, 16)

    def body(dy_ref, a_ref, g_ref, da_ref, dab_ref, dg_ref, db_ref):
        @pl.when(pl.program_id(0) == 0)
        def _():
            dg_ref[...] = jnp.zeros_like(dg_ref)
            db_ref[...] = jnp.zeros_like(db_ref)

        da, sg, sb = _ln_bwd_tile(dy_ref[...], a_ref[...], g_ref[...])
        da_ref[...] = da
        dab_ref[...] = da.astype(BF16)
        dg_ref[...] += sg
        db_ref[...] += sb

    row = pl.BlockSpec((tm, D), lambda i: (i, 0))
    vec = pl.BlockSpec((1, D), lambda i: (0, 0))
    return pl.pallas_call(
        body, name="ln_bwd", grid=(T // tm,),
        in_specs=[row, row, vec], out_specs=[row, row, vec, vec],
        out_shape=[SDS((T, D), F32), SDS((T, D), BF16), SDS((1, D), F32), SDS((1, D), F32)],
        compiler_params=_cp(("arbitrary",)),
    )(dy, a, g.reshape(1, D))


def loss_ln_bwd(y, tgt, a, g):
    T, D = y.shape
    tm = _tile(T, 512, 16)

    def body(y_ref, t_ref, a_ref, g_ref, da_ref, dab_ref, dg_ref, db_ref, sq_ref):
        @pl.when(pl.program_id(0) == 0)
        def _():
            dg_ref[...] = jnp.zeros_like(dg_ref)
            db_ref[...] = jnp.zeros_like(db_ref)
            sq_ref[...] = jnp.zeros_like(sq_ref)

        e = y_ref[...] - t_ref[...]
        da, sg, sb = _ln_bwd_tile(e / float(D), a_ref[...], g_ref[...])
        da_ref[...] = da
        dab_ref[...] = da.astype(BF16)
        dg_ref[...] += sg
        db_ref[...] += sb
        sq_ref[...] += jnp.sum(e * e, axis=0, keepdims=True)

    row = pl.BlockSpec((tm, D), lambda i: (i, 0))
    vec = pl.BlockSpec((1, D), lambda i: (0, 0))
    return pl.pallas_call(
        body, name="loss_ln_bwd", grid=(T // tm,),
        in_specs=[row, row, row, vec], out_specs=[row, row, vec, vec, vec],
        out_shape=[SDS((T, D), F32), SDS((T, D), BF16), SDS((1, D), F32), SDS((1, D), F32), SDS((1, D), F32)],
        compiler_params=_cp(("arbitrary",)),
    )(y, tgt, a, g.reshape(1, D))


def matmul_ln(a, w, dims, res, g, b, alpha, name, carry=None):
    if dims == TN:
        K, T = a.shape
    else:
        T, K = a.shape
    D = w.shape[1]
    tm = _tile(T, 512, 128 if dims == TN else 16)

    def body(a_ref, w_ref, r_ref, g_ref, b_ref, p_ref, y_ref, yb_ref):
        pre = alpha * r_ref[...] + _dot(a_ref[...], w_ref[...], dims)
        mu = jnp.mean(pre, axis=-1, keepdims=True)
        xc = pre - mu
        var = jnp.mean(xc * xc, axis=-1, keepdims=True)
        y = xc * lax.rsqrt(var + LN_EPS) * g_ref[...] + b_ref[...]
        p_ref[...] = pre
        y_ref[...] = y
        yb_ref[...] = y.astype(BF16)

    a_spec = pl.BlockSpec((K, tm), lambda i: (0, i)) if dims == TN else pl.BlockSpec((tm, K), lambda i: (i, 0))
    row = pl.BlockSpec((tm, D), lambda i: (i, 0))
    vec = pl.BlockSpec((1, D), lambda i: (0, 0))
    outs, landed = _call(
        body, name, (T // tm,), [a_spec, pl.BlockSpec(w.shape, lambda i: (0, 0)), row, vec, vec], [row, row, row],
        [SDS((T, D), F32), SDS((T, D), F32), SDS((T, D), BF16)], (a, w, res, g.reshape(1, D), b.reshape(1, D)), carry=carry)
    return (*outs, landed)


def dx_from_T(aTs, w, res, alpha, name, tm_target, ln=None):
    T = aTs[0].shape[1]
    N = w.shape[1]
    ks = [a.shape[0] for a in aTs]
    n = len(aTs)
    tm = _tile(T, tm_target, 128)

    def body(*refs):
        a_refs, w_ref, r_ref = refs[:n], refs[n], refs[n + 1]
        acc = alpha * r_ref[...]
        off = 0
        for a_ref, k in zip(a_refs, ks):
            acc = acc + _dot(a_ref[...], w_ref[off:off + k, :], TN)
            off += k
        if ln is None:
            refs[n + 2][...] = acc
            return
        ln_a, ln_g, da_ref, dab_ref, dgm_ref, dbt_ref = refs[n + 2:]

        @pl.when(pl.program_id(0) == 0)
        def _():
            dgm_ref[...] = jnp.zeros_like(dgm_ref)
            dbt_ref[...] = jnp.zeros_like(dbt_ref)

        da, sg, sb = _ln_bwd_tile(acc, ln_a[...], ln_g[...])
        da_ref[...] = da
        dab_ref[...] = da.astype(BF16)
        dgm_ref[...] += sg
        dbt_ref[...] += sb

    row = pl.BlockSpec((tm, N), lambda i: (i, 0))
    vec = pl.BlockSpec((1, N), lambda i: (0, 0))
    in_specs = [pl.BlockSpec((k, tm), lambda i: (0, i)) for k in ks] + [pl.BlockSpec(w.shape, lambda i: (0, 0)), row]
    args = list(aTs) + [w, res]
    if ln is None:
        return pl.pallas_call(body, name=name, grid=(T // tm,), in_specs=in_specs, out_specs=row,
                              out_shape=SDS((T, N), F32), compiler_params=_cp())(*args)
    return pl.pallas_call(
        body, name=name + "_ln", grid=(T // tm,), in_specs=in_specs + [row, vec], out_specs=[row, row, vec, vec],
        out_shape=[SDS((T, N), F32), SDS((T, N), BF16), SDS((1, N), F32), SDS((1, N), F32)],
        compiler_params=_cp(("arbitrary",)),
    )(*args, ln[0], ln[1].reshape(1, N))


def matmul_to_T(w, a, name):
    M, K = w.shape
    T = a.shape[0]
    tt = _tile(T, 512, 128)

    def body(w_ref, a_ref, o_ref):
        o_ref[...] = _dot(w_ref[...], a_ref[...], NT).astype(BF16)

    return pl.pallas_call(
        body, name=name, grid=(T // tt,),
        in_specs=[pl.BlockSpec((M, K), lambda i: (0, 0)), pl.BlockSpec((tt, K), lambda i: (i, 0))],
        out_specs=pl.BlockSpec((M, tt), lambda i: (0, i)),
        out_shape=SDS((M, T), BF16), compiler_params=_cp(),
    )(w, a)


def wgrad_rows(a, b, name, carry=None):
    T, M = a.shape
    N = b.shape[1]
    tt = _tile(T, 1024, 16)
    tmm = _tile(M, 1536, 128)
    nt = T // tt

    def body(a_ref, b_ref, o_ref, acc_ref):
        t = pl.program_id(1)

        @pl.when(t == 0)
        def _():
            acc_ref[...] = jnp.zeros_like(acc_ref)

        acc_ref[...] += _dot(a_ref[...], b_ref[...], TN)

        @pl.when(t == nt - 1)
        def _():
            o_ref[...] = acc_ref[...].astype(BF16)

    outs, landed = _call(
        body, name, (M // tmm, nt),
        [pl.BlockSpec((tt, tmm), lambda i, t: (t, i)), pl.BlockSpec((tt, N), lambda i, t: (t, 0))],
        [pl.BlockSpec((tmm, N), lambda i, t: (i, 0))], [SDS((M, N), BF16)], (a, b),
        scratch=[pltpu.VMEM((tmm, N), F32)], sem=("arbitrary", "arbitrary"), carry=carry)
    return outs[0], (landed[0] if landed else None)


def wgrad_T(aT, bT3, g, name):
    M, T = aT.shape
    N = bT3.shape[1]
    tt = _tile(T, 2048, 128)
    nt = T // tt

    def body(a_ref, b_ref, o_ref, acc_ref):
        t = pl.program_id(0)

        @pl.when(t == 0)
        def _():
            acc_ref[...] = jnp.zeros_like(acc_ref)

        acc_ref[...] += _dot(a_ref[...], b_ref[0], NT)

        @pl.when(t == nt - 1)
        def _():
            o_ref[...] = acc_ref[...].astype(BF16)

    return pl.pallas_call(
        body, name=name, grid=(nt,),
        in_specs=[pl.BlockSpec((M, tt), lambda t: (0, t)), pl.BlockSpec((1, N, tt), lambda t: (g, 0, t))],
        out_specs=pl.BlockSpec((M, N), lambda t: (0, 0)),
        out_shape=SDS((M, N), BF16), scratch_shapes=[pltpu.VMEM((M, N), F32)],
        compiler_params=_cp(("arbitrary",)),
    )(aT, bT3)


def wgrad_mixed(aT, b, name):
    M, T = aT.shape
    N = b.shape[1]
    tt = _tile(T, 2048, 128)
    nt = T // tt

    def body(a_ref, b_ref, o_ref, acc_ref):
        t = pl.program_id(0)

        @pl.when(t == 0)
        def _():
            acc_ref[...] = jnp.zeros_like(acc_ref)

        acc_ref[...] += _dot(a_ref[...], b_ref[...], NN)

        @pl.when(t == nt - 1)
        def _():
            o_ref[...] = acc_ref[...].astype(BF16)

    return pl.pallas_call(
        body, name=name, grid=(nt,),
        in_specs=[pl.BlockSpec((M, tt), lambda t: (0, t)), pl.BlockSpec((tt, N), lambda t: (t, 0))],
        out_specs=pl.BlockSpec((M, N), lambda t: (0, 0)),
        out_shape=SDS((M, N), BF16), scratch_shapes=[pltpu.VMEM((M, N), F32)],
        compiler_params=_cp(("arbitrary",)),
    )(aT, b)


def _shift_down(x, k, rows):
    return jnp.where(rows >= k, pltpu.roll(x, k, 0), 0.0)


def _shift_up(x, k, rows):
    n = x.shape[0]
    return jnp.where(rows < n - k, pltpu.roll(x, n - k, 0), 0.0)


def _pick(g, vals):
    out = vals[-1]
    for k in range(len(vals) - 2, -1, -1):
        out = jnp.where(g == k, vals[k], out)
    return out


def pool_fwd(x, pw, scale, B, S, carry=None):
    T, D = x.shape
    G = len(POOL_WINDOWS)
    Cg = D // G

    def body(x_ref, w_ref, s_ref, mix_ref, pooled_ref):
        g = pl.program_id(1)
        xv = x_ref[...]
        rows = lax.broadcasted_iota(jnp.int32, xv.shape, 0)
        sums, cur, k = [], xv, 1
        for _ in POOL_WINDOWS:
            cur = cur + _shift_down(cur, k, rows)
            sums.append(cur)
            k *= 2
        win = 2 * lax.shift_left(jnp.int32(1), g)
        total = _pick(g, sums)
        count = jnp.minimum(rows + 1, win).astype(F32)
        pooled = total / count - xv
        pb = pooled.astype(BF16)
        pooled_ref[...] = pb
        mix_ref[...] = _dot(pb, w_ref[0]) * s_ref[...]

    blk = pl.BlockSpec((S, Cg), lambda b, g: (b, g))
    outs, landed = _call(
        body, "pool_fwd", (B, G),
        [blk, pl.BlockSpec((1, Cg, Cg), lambda b, g: (g, 0, 0)), pl.BlockSpec((1, Cg), lambda b, g: (0, g))],
        [blk, blk], [SDS((T, D), F32), SDS((T, D), BF16)], (x, pw, scale), carry=carry)
    return (*outs, landed)


def pool_bwd(dmix, pooled, pw, scale, alpha, B, S):
    T, D = dmix.shape
    G = len(POOL_WINDOWS)
    Cg = D // G

    def body(d_ref, p_ref, w_ref, s_ref, dx_ref, ds_ref, dw_ref):
        g = pl.program_id(1)
        dm = d_ref[...]
        pb = p_ref[...]
        w = w_ref[0]
        ypre = _dot(pb, w)
        ds_ref[0] = jnp.sum(dm * ypre, axis=0, keepdims=True)
        dy = (dm * s_ref[...]).astype(BF16)
        dpool = _dot(dy, w, NT)
        dw_ref[0, 0] = _dot(pb, dy, TN)
        rows = lax.broadcasted_iota(jnp.int32, dm.shape, 0)
        win = 2 * lax.shift_left(jnp.int32(1), g)
        count = jnp.minimum(rows + 1, win).astype(F32)
        cur, k, sums = dpool / count, 1, []
        for _ in POOL_WINDOWS:
            cur = cur + _shift_up(cur, k, rows)
            sums.append(cur)
            k *= 2
        dx_ref[...] = alpha * dm + _pick(g, sums) - dpool

    blk = pl.BlockSpec((S, Cg), lambda b, g: (b, g))
    return pl.pallas_call(
        body, name="pool_bwd", grid=(B, G),
        in_specs=[blk, blk, pl.BlockSpec((1, Cg, Cg), lambda b, g: (g, 0, 0)), pl.BlockSpec((1, Cg), lambda b, g: (0, g))],
        out_specs=[blk, pl.BlockSpec((1, 1, Cg), lambda b, g: (b, 0, g)),
                   pl.BlockSpec((1, 1, Cg, Cg), lambda b, g: (b, g, 0, 0))],
        out_shape=[SDS((T, D), F32), SDS((B, 1, D), F32), SDS((B, G, Cg, Cg), F32)], compiler_params=_cp(),
    )(dmix, pooled, pw, scale)


_GELU_K = math.sqrt(2.0 / math.pi)
_GELU_C = 0.044715
FFN_ROWS = 512
FFN_HALO = 16


def ffn_up(hb, wgT, wuT, cw, cb, B, S, carry=None):
    T, D = hb.shape
    Fd = wgT.shape[0]
    fn = _tile(Fd, 256, 128)

    nc = S // _tile(S, FFN_ROWS, FFN_HALO)
    rc = S // nc

    def body(h_ref, wg_ref, wu_ref, cw_ref, cb_ref, g_ref, ge_ref, ud_ref, hh_ref):
        wg, wu, cw, cb = wg_ref[...], wu_ref[...], cw_ref[...], cb_ref[...]
        halo = jnp.zeros((FFN_HALO, fn), F32)
        for ci in range(nc):
            rows = slice(ci * rc, (ci + 1) * rc)
            h = h_ref[rows, :]
            g = _dot(h, wg, NT)
            u = _dot(h, wu, NT)
            gext = jnp.concatenate([halo, g], axis=0)
            halo = g[rc - FFN_HALO:, :]
            c = cb + cw[0:1] * pltpu.roll(gext, 2, 0)[FFN_HALO:, :] + cw[1:2] * pltpu.roll(gext, 1, 0)[FFN_HALO:, :] + cw[2:3] * g
            c2 = c * c
            th = jnp.tanh(c * (_GELU_K + (_GELU_K * _GELU_C) * c2))
            cdf = 0.5 * th + 0.5
            ge = c * cdf
            dgelu = cdf + (c * (1.0 - th * th)) * (0.5 * _GELU_K + (1.5 * _GELU_K * _GELU_C) * c2)
            g_ref[rows, :] = g.astype(BF16)
            ge_ref[rows, :] = ge.astype(BF16)
            ud_ref[rows, :] = (u * dgelu).astype(BF16)
            hh_ref[rows, :] = (ge * u).astype(BF16)

    hspec = pl.BlockSpec((S, D), lambda b, j: (b, 0))
    wspec = pl.BlockSpec((fn, D), lambda b, j: (j, 0))
    ospec = pl.BlockSpec((S, fn), lambda b, j: (b, j))
    outs, landed = _call(
        body, "ffn_up", (B, Fd // fn),
        [hspec, wspec, wspec, pl.BlockSpec((3, fn), lambda b, j: (0, j)), pl.BlockSpec((1, fn), lambda b, j: (0, j))],
        [ospec] * 4, [SDS((T, Fd), BF16)] * 4, (hb, wgT, wuT, cw, cb), carry=carry)
    return (*outs, landed)


def ffn_mid_bwd(dfb, wd, g, ge, ud, cw, B, S, carry=None):
    T, D = dfb.shape
    Fd = wd.shape[0]
    fn = _tile(Fd, 256, 128)

    def body(df_ref, wd_ref, g_ref, ge_ref, ud_ref, cw_ref, dg_ref, du_ref, dcb_ref, dcw_ref):
        dhh = _dot(df_ref[...], wd_ref[...], NT)
        gv = g_ref[...].astype(F32)
        cw = cw_ref[...]
        rows = lax.broadcasted_iota(jnp.int32, gv.shape, 0)
        g1 = _shift_down(gv, 1, rows)
        g2 = _shift_down(gv, 2, rows)
        du_ref[...] = (dhh * ge_ref[...].astype(F32)).astype(BF16)
        dc = dhh * ud_ref[...].astype(F32)
        dcb_ref[0] = jnp.sum(dc, axis=0, keepdims=True)
        dcw_ref[0] = jnp.concatenate(
            [jnp.sum(dc * g2, axis=0, keepdims=True), jnp.sum(dc * g1, axis=0, keepdims=True),
             jnp.sum(dc * gv, axis=0, keepdims=True)], axis=0)
        dg = cw[2:3] * dc + cw[1:2] * _shift_up(dc, 1, rows) + cw[0:1] * _shift_up(dc, 2, rows)
        dg_ref[...] = dg.astype(BF16)

    tspec = pl.BlockSpec((S, fn), lambda b, j: (b, j))
    outs, landed = _call(
        body, "ffn_mid_bwd", (B, Fd // fn),
        [pl.BlockSpec((S, D), lambda b, j: (b, 0)), pl.BlockSpec((fn, D), lambda b, j: (j, 0)), tspec, tspec, tspec,
         pl.BlockSpec((3, fn), lambda b, j: (0, j))],
        [tspec, tspec, pl.BlockSpec((1, 1, fn), lambda b, j: (b, 0, j)), pl.BlockSpec((1, 3, fn), lambda b, j: (b, 0, j))],
        [SDS((T, Fd), BF16), SDS((T, Fd), BF16), SDS((B, 1, Fd), F32), SDS((B, 3, Fd), F32)],
        (dfb, wd, g, ge, ud, cw), carry=carry)
    return (*outs, landed[0] if landed else None)


def ffn_dx_ln(dg, du, wgT, wuT, res, alpha, a, gamma, carry=None):
    T, Fd = dg.shape
    D = wgT.shape[1]
    tm = _tile(T, 256, 16)

    def body(dg_ref, du_ref, wg_ref, wu_ref, r_ref, a_ref, g_ref, da_ref, dab_ref, dgm_ref, dbt_ref):
        @pl.when(pl.program_id(0) == 0)
        def _():
            dgm_ref[...] = jnp.zeros_like(dgm_ref)
            dbt_ref[...] = jnp.zeros_like(dbt_ref)

        dh = alpha * r_ref[...] + _dot(dg_ref[...], wg_ref[...]) + _dot(du_ref[...], wu_ref[...])
        da, sg, sb = _ln_bwd_tile(dh, a_ref[...], g_ref[...])
        da_ref[...] = da
        dab_ref[...] = da.astype(BF16)
        dgm_ref[...] += sg
        dbt_ref[...] += sb

    a_spec = pl.BlockSpec((tm, Fd), lambda i: (i, 0))
    w_spec = pl.BlockSpec((Fd, D), lambda i: (0, 0))
    row = pl.BlockSpec((tm, D), lambda i: (i, 0))
    vec = pl.BlockSpec((1, D), lambda i: (0, 0))
    outs, landed = _call(body, "ffn_dx_ln", (T // tm,), [a_spec, a_spec, w_spec, w_spec, row, row, vec], [row, row, vec, vec],
                         [SDS((T, D), F32), SDS((T, D), BF16), SDS((1, D), F32), SDS((1, D), F32)],
                         (dg, du, wgT, wuT, res, a, gamma.reshape(1, D)), sem=("arbitrary",), carry=carry)
    return (*outs, landed[0] if landed else None)


def _partner_all(x):
    n = x.shape[0]
    r = lax.broadcasted_iota(jnp.int32, x.shape, 0)
    return jnp.where((r % HEAD_DIM) < HEAD_DIM // 2, pltpu.roll(x, n - HEAD_DIM // 2, 0), pltpu.roll(x, HEAD_DIM // 2, 0))


def proj_T(w, xT3, cosT, sinT, blk_off, rope, scale, name, carry=None):
    G, K, T = xT3.shape
    S = cosT.shape[2]
    Dout = K
    tt = _tile(S, 1024, 128)
    H = Dout // HEAD_DIM
    nS = S // tt

    def body(w_ref, x_ref, c_ref, s_ref, o_ref):
        acc = _dot(w_ref[...], x_ref[0])
        if rope:
            cos = jnp.tile(c_ref[0], (H, 1))
            sin = jnp.tile(s_ref[0], (H, 1))
            acc = acc * cos + _partner_all(acc) * sin
        if scale != 1.0:
            acc = acc * scale
        o_ref[0] = acc.astype(BF16)

    tab = pl.BlockSpec((1, HEAD_DIM, tt), lambda g, j: (g, 0, j % nS))
    outs, landed = _call(
        body, name, (G, T // tt),
        [pl.BlockSpec((Dout, K), lambda g, j: (g + blk_off, 0)), pl.BlockSpec((1, K, tt), lambda g, j: (g, 0, j)), tab, tab],
        [pl.BlockSpec((1, Dout, tt), lambda g, j: (g, 0, j))], [SDS((G, Dout, T), BF16)], (w, xT3, cosT, sinT), carry=carry)
    return outs[0], landed


def _attn_bias():
    kj = lax.broadcasted_iota(jnp.int32, (2 * BLK, BLK), 0)
    qi = lax.broadcasted_iota(jnp.int32, (2 * BLK, BLK), 1)
    ok = ((kj >= BLK) & (kj - BLK <= qi)) | ((kj < BLK) & (kj >= qi))
    return jnp.where(ok, 0.0, NEG).astype(F32)


def _has_prev(g, S):
    nb = S // (DILATIONS[g] * BLK)
    return [(n % nb) != 0 for n in range(S // BLK)]


def _win(ref, n, hp):
    lo = (n - 1) * BLK if hp else n * BLK
    return ref[0, :, lo:(n + 1) * BLK]


def attn_fwd(qT3, kT3, vT3, bias, g, B, S):
    _, D, T = qT3.shape
    H = D // HEAD_DIM
    nblk = S // BLK
    hp = _has_prev(g, S)

    def body(q_ref, k_ref, v_ref, b_ref, o_ref, l_ref, s_scr, p_scr, rl_scr):
        for n in range(nblk):
            lo = 0 if hp[n] else BLK
            s_scr[n, lo:, :] = _dot(_win(k_ref, n, hp[n]), q_ref[0, :, n * BLK:(n + 1) * BLK], TN)
        for n in range(nblk):
            lo = 0 if hp[n] else BLK
            sT = s_scr[n, lo:, :] + b_ref[lo:, :]
            m = jnp.max(sT, axis=0, keepdims=True)
            p = jnp.exp(sT - m)
            l = jnp.sum(p, axis=0, keepdims=True)
            p_scr[n, lo:, :] = p.astype(BF16)
            rl_scr[n:n + 1, :] = 1.0 / l
            l_ref[0, :, n * BLK:(n + 1) * BLK] = m + jnp.log(l)
        for n in range(nblk):
            lo = 0 if hp[n] else BLK
            o_ref[:, n * BLK:(n + 1) * BLK] = _dot(_win(v_ref, n, hp[n]), p_scr[n, lo:, :]) * rl_scr[n:n + 1, :]

    spec = pl.BlockSpec((1, HEAD_DIM, S), lambda b, h: (g, h, b))
    return pl.pallas_call(
        body, name=f"attn_fwd_g{g}", grid=(B, H),
        in_specs=[spec, spec, spec, pl.BlockSpec((2 * BLK, BLK), lambda b, h: (0, 0))],
        out_specs=[pl.BlockSpec((HEAD_DIM, S), lambda b, h: (h, b)), pl.BlockSpec((1, 1, S), lambda b, h: (h, 0, b))],
        out_shape=[SDS((D, T), F32), SDS((H, 1, T), F32)],
        scratch_shapes=[pltpu.VMEM((nblk, 2 * BLK, BLK), F32), pltpu.VMEM((nblk, 2 * BLK, BLK), BF16),
                        pltpu.VMEM((nblk, BLK), F32)],
        compiler_params=_cp(),
    )(qT3, kT3, vT3, bias)


def attn_combine(oTs, lses):
    G = len(oTs)
    D, T = oTs[0].shape
    H = D // HEAD_DIM
    tn = _tile(T, 2048, 128)
    hb = _tile(H, 4, 1)

    def body(*refs):
        o_refs, l_refs = refs[:G], refs[G:2 * G]
        ob_ref, of_ref, lt_ref = refs[2 * G:]
        ls = [r[...] for r in l_refs]
        m = functools.reduce(jnp.maximum, ls)
        es = [jnp.exp(v - m) for v in ls]
        z = functools.reduce(lambda a, b: a + b, es)
        o = (es[0] / z) * o_refs[0][...].reshape(hb, HEAD_DIM, tn)
        for i in range(1, G):
            o = o + (es[i] / z) * o_refs[i][...].reshape(hb, HEAD_DIM, tn)
        o = o.reshape(hb * HEAD_DIM, tn)
        ob_ref[...] = o.astype(BF16)
        of_ref[...] = o
        lt_ref[...] = m + jnp.log(z)

    ospec = pl.BlockSpec((hb * HEAD_DIM, tn), lambda h, j: (h, j))
    lspec = pl.BlockSpec((hb, 1, tn), lambda h, j: (h, 0, j))
    return pl.pallas_call(
        body, name="attn_combine", grid=(H // hb, T // tn),
        in_specs=[ospec] * G + [lspec] * G, out_specs=[ospec, ospec, lspec],
        out_shape=[SDS((D, T), BF16), SDS((D, T), F32), SDS((H, 1, T), F32)], compiler_params=_cp(),
    )(*oTs, *lses)


def attn_delta(doT, oT):
    D, T = doT.shape
    H = D // HEAD_DIM
    tn = _tile(T, 2048, 128)
    hb = _tile(H, 4, 1)

    def body(d_ref, o_ref, r_ref):
        prod = (d_ref[...].astype(F32) * o_ref[...]).reshape(hb, HEAD_DIM, tn)
        r_ref[...] = jnp.sum(prod, axis=1, keepdims=True)

    spec = pl.BlockSpec((hb * HEAD_DIM, tn), lambda h, j: (h, j))
    return pl.pallas_call(
        body, name="attn_delta", grid=(H // hb, T // tn), in_specs=[spec, spec],
        out_specs=pl.BlockSpec((hb, 1, tn), lambda h, j: (h, 0, j)),
        out_shape=SDS((H, 1, T), F32), compiler_params=_cp(),
    )(doT, oT)


def attn_bwd(qT3, kT3, vT3, doT, lse, delta, cosT, sinT, bias, g, q_scale, B, S, dk_prev=None, dv_prev=None):
    _, D, T = qT3.shape
    H = D // HEAD_DIM
    nblk = S // BLK
    half = HEAD_DIM // 2
    hp = _has_prev(g, S)
    acc_in = dk_prev is not None
    kv_dtype = BF16 if acc_in else F32

    def body(*refs):
        q_ref, k_ref, v_ref, do_ref, l_ref, d_ref, c_ref, s_ref, b_ref = refs[:9]
        rest = refs[9:]
        if acc_in:
            dkp_ref, dvp_ref = rest[:2]
            rest = rest[2:]
        dq_ref, dk_ref, dv_ref, s_scr, dp_scr, p_scr, ds_scr = rest
        for n in range(nblk):
            lo = 0 if hp[n] else BLK
            blk = slice(n * BLK, (n + 1) * BLK)
            s_scr[n, lo:, :] = _dot(_win(k_ref, n, hp[n]), q_ref[0, :, blk], TN)
            dp_scr[n, lo:, :] = _dot(_win(v_ref, n, hp[n]), do_ref[:, blk], TN)
        for n in range(nblk):
            lo = 0 if hp[n] else BLK
            blk = slice(n * BLK, (n + 1) * BLK)
            pT = jnp.exp(s_scr[n, lo:, :] + b_ref[lo:, :] - l_ref[0, :, blk])
            p_scr[n, lo:, :] = pT.astype(BF16)
            ds_scr[n, lo:, :] = (pT * (dp_scr[n, lo:, :] - d_ref[0, :, blk])).astype(BF16)
        for j in range(nblk):
            blk = slice(j * BLK, (j + 1) * BLK)
            if j + 1 < nblk and hp[j + 1]:
                two = slice(j * BLK, (j + 2) * BLK)
                pj = jnp.concatenate([p_scr[j, BLK:, :], p_scr[j + 1, :BLK, :]], axis=1)
                dsj = jnp.concatenate([ds_scr[j, BLK:, :], ds_scr[j + 1, :BLK, :]], axis=1)
                dv = _dot(do_ref[:, two], pj, NT)
                dk = _dot(q_ref[0, :, two], dsj, NT)
            else:
                dv = _dot(do_ref[:, blk], p_scr[j, BLK:, :], NT)
                dk = _dot(q_ref[0, :, blk], ds_scr[j, BLK:, :], NT)
            dk = dk * c_ref[0, :, blk] - pltpu.roll(dk, half, 0) * s_ref[0, :, blk]
            if acc_in:
                dk = dk + dkp_ref[:, blk]
                dv = dv + dvp_ref[:, blk]
            dk_ref[:, blk] = dk.astype(kv_dtype)
            dv_ref[:, blk] = dv.astype(kv_dtype)
            lo = 0 if hp[j] else BLK
            dq = _dot(_win(k_ref, j, hp[j]), ds_scr[j, lo:, :])
            dq = dq * c_ref[0, :, blk] - pltpu.roll(dq, half, 0) * s_ref[0, :, blk]
            dq_ref[:, blk] = (dq * q_scale).astype(BF16)

    spec3 = pl.BlockSpec((1, HEAD_DIM, S), lambda b, h: (g, h, b))
    spec = pl.BlockSpec((HEAD_DIM, S), lambda b, h: (h, b))
    sspec = pl.BlockSpec((1, 1, S), lambda b, h: (h, 0, b))
    tab = pl.BlockSpec((1, HEAD_DIM, S), lambda b, h: (g, 0, 0))
    in_specs = [spec3, spec3, spec3, spec, sspec, sspec, tab, tab, pl.BlockSpec((2 * BLK, BLK), lambda b, h: (0, 0))]
    args = [qT3, kT3, vT3, doT, lse, delta, cosT, sinT, bias]
    if acc_in:
        in_specs += [spec, spec]
        args += [dk_prev, dv_prev]
    return pl.pallas_call(
        body, name=f"attn_bwd_g{g}" + ("_acc" if acc_in else ""), grid=(B, H),
        in_specs=in_specs, out_specs=[spec, spec, spec],
        out_shape=[SDS((D, T), BF16), SDS((D, T), kv_dtype), SDS((D, T), kv_dtype)],
        scratch_shapes=[pltpu.VMEM((nblk, 2 * BLK, BLK), F32), pltpu.VMEM((nblk, 2 * BLK, BLK), F32),
                        pltpu.VMEM((nblk, 2 * BLK, BLK), BF16), pltpu.VMEM((nblk, 2 * BLK, BLK), BF16)],
        compiler_params=_cp(),
    )(*args)


def adamw(w, g, m, v, name):
    R, C = w.shape
    tr = _tile(R, 512, 8)

    def body(w_ref, g_ref, m_ref, v_ref, d_ref, nm_ref, nv_ref):
        gv = g_ref[...]
        nm = ADAM_B1 * m_ref[...] + (1.0 - ADAM_B1) * gv
        nv = ADAM_B2 * v_ref[...] + (1.0 - ADAM_B2) * (gv * gv)
        m_hat = nm / (1.0 - ADAM_B1 ** ADAM_STEP)
        v_hat = nv / (1.0 - ADAM_B2 ** ADAM_STEP)
        d_ref[...] = -ADAM_LR * (m_hat / (jnp.sqrt(v_hat) + ADAM_EPS) + ADAM_WD * w_ref[...])
        nm_ref[...] = nm
        nv_ref[...] = nv

    spec = pl.BlockSpec((tr, C), lambda i: (i, 0))
    return pl.pallas_call(
        body, name=name, grid=(R // tr,), in_specs=[spec] * 4, out_specs=[spec] * 3,
        out_shape=[SDS((R, C), F32)] * 3, compiler_params=_cp(),
    )(w, g, m, v)


def _perm(a, B, S, d):
    if d == 1:
        return a
    lead = a.shape[:-1]
    return a.reshape(*lead, B, S // d, d).swapaxes(-1, -2).reshape(*lead, B * S)


def _unperm(a, B, S, d):
    if d == 1:
        return a
    lead = a.shape[:-1]
    return a.reshape(*lead, B, d, S // d).swapaxes(-1, -2).reshape(*lead, B * S)


def _perm3(a, B, S):
    return jnp.stack([_perm(a, B, S, d) for d in DILATIONS])


def _xT3(xb, B, S):
    D = xb.shape[1]
    outs = []
    for d in DILATIONS:
        outs.append(xb.reshape(B, S // d, d, D).transpose(3, 0, 2, 1).reshape(D, B * S))
    return jnp.stack(outs)


def _rope_tables(S):
    half = HEAD_DIM // 2
    inv_freq = ROPE_THETA ** (-jnp.arange(0, HEAD_DIM, 2, dtype=F32) / HEAD_DIM)
    ang = jnp.arange(S, dtype=F32)[:, None] * inv_freq[None, :]
    cos = jnp.concatenate([jnp.cos(ang), jnp.cos(ang)], axis=1).T
    sin = jnp.concatenate([-jnp.sin(ang), jnp.sin(ang)], axis=1).T
    return _perm3(cos, 1, S), _perm3(sin, 1, S)


def kernel(x, pool_w, pool_scale, w_q, w_kv, w_o, ffn_w_gate, ffn_w_up, ffn_conv_w, ffn_conv_b, ffn_w_down, ln1_g, ln1_b, ln2_g, ln2_b, loss_target, m_pool_w, m_pool_scale, m_w_q, m_w_kv, m_w_o, m_ffn_w_gate, m_ffn_w_up, m_ffn_conv_w, m_ffn_conv_b, m_ffn_w_down, m_ln1_g, m_ln1_b, m_ln2_g, m_ln2_b, v_pool_w, v_pool_scale, v_w_q, v_w_kv, v_w_o, v_ffn_w_gate, v_ffn_w_up, v_ffn_conv_w, v_ffn_conv_b, v_ffn_w_down, v_ln1_g, v_ln1_b, v_ln2_g, v_ln2_b):
    B, S, D = x.shape
    T = B * S
    depth = ln1_g.shape[0]
    nA, nB = pool_w.shape[0], w_q.shape[0]
    Fs = ffn_w_down.shape[1]
    Fd = Fs * N_DEV
    H = D // HEAD_DIM
    G = len(DILATIONS)
    PG = len(POOL_WINDOWS)
    Cg = D // PG
    alpha = (2.0 * depth) ** 0.25
    me = 4 * lax.axis_index("x") + 2 * lax.axis_index("y") + lax.axis_index("c")

    qs, kvs, os_ = w_q.shape[2], w_kv.shape[1], w_o.shape[1]
    pool_rows = pool_w.size // D
    local = {("pool",): pool_w.reshape(pool_rows, D).astype(BF16), ("wkv",): w_kv.T.astype(BF16)}
    for j in range(nB):
        local[("wq", j)] = w_q[j].T.astype(BF16)
        local[("wo", j)] = w_o[j].astype(BF16)
    for i in range(depth):
        local[("wg", i)] = ffn_w_gate[i].T.astype(BF16)
        local[("wu", i)] = ffn_w_up[i].T.astype(BF16)
        local[("wd", i)] = ffn_w_down[i].astype(BF16)
    ffn = lambda i: [("wg", i), ("wu", i), ("wd", i)]
    queue = [[("pool",)]]
    if depth == 4 and nA == 2 and nB == 2:
        queue += [[("wg", 0)], [("wu", 0)], [("wd", 0), ("wg", 1), ("wu", 1)], [("wd", 1)],
                  [("wo", 0)], [("wo", 1)], [("wkv",), ("wq", 0)], [("wu", 2)], [("wd", 2)], [("wq", 1)],
                  [("wg", 2)], ffn(3)]
    gathered = {}

    def land(keys, arrs):
        for k, a in zip(keys or (), arrs or ()):
            gathered[k] = a.reshape(-1, D)

    def next_gather():
        if not queue:
            return None, None
        keys = queue.pop(0)
        if not keys:
            return None, None
        return keys, Gather([local[k] for k in keys])

    def weight(key):
        if key not in gathered:
            keys = [key]
            for bi, batch in enumerate(queue):
                if key in batch:
                    keys = queue.pop(bi)
                    break
            blk = all_gather_blocks(jnp.concatenate([local[k] for k in keys], axis=0), "gather_" + "_".join(map(str, key)), in_vmem=False)
            off = 0
            for k in keys:
                r = local[k].shape[0]
                gathered[k] = blk[:, off:off + r].reshape(-1, D)
                off += r
        return gathered[key]

    PW = weight(("pool",)).reshape(N_DEV, nA, PG, Cg // N_DEV, Cg).transpose(1, 2, 0, 3, 4).reshape(nA, PG, Cg, Cg)

    sm_cols = 128
    sm_local = jnp.concatenate([ffn_conv_w.reshape(-1), pool_scale.reshape(-1)])
    sm_rows = -(-sm_local.size // sm_cols)
    sm_rows_p = -(-sm_rows // 8) * 8
    sm_local = jnp.pad(sm_local, (0, sm_rows_p * sm_cols - sm_local.size)).reshape(sm_rows_p, sm_cols)
    sm = all_gather_blocks(sm_local, "gather_small", in_vmem=True).reshape(N_DEV, -1)
    ncw = ffn_conv_w.size
    conv_w_full = sm[:, :ncw].reshape(N_DEV, depth, 3, Fs).transpose(1, 2, 0, 3).reshape(depth, 3, Fd)
    pool_scale_full = sm[:, ncw:ncw + pool_scale.size].reshape(N_DEV, nA, D // N_DEV).transpose(1, 0, 2).reshape(nA, 1, D)

    cosT, sinT = _rope_tables(S)
    bias = _attn_bias()

    xs = x.reshape(T, D)
    saved = []
    cur, curb = xs, None
    kT = vT = x1T3 = None
    for i in range(depth):
        sv = {}
        if i < nA:
            keys, cr = next_gather()
            mix, pooled, got = pool_fwd(cur, PW[i], pool_scale_full[i], B, S, carry=cr)
            land(keys, got)
            sv["pooled"] = pooled
            keys, cr = next_gather()
            a1, h, hb, got = add_ln(cur, mix, ln1_g[i], ln1_b[i], alpha, carry=cr)
            land(keys, got)
        else:
            j = i - nA
            xT3 = x1T3 if j == 0 else _xT3(curb, B, S)
            keys, cr = next_gather()
            qT, got = proj_T(weight(("wq", j)), xT3, cosT, sinT, 0, True, HEAD_DIM ** -0.5, "q_proj", carry=cr)
            land(keys, got)
            oTs, lses = [], []
            for gi, d in enumerate(DILATIONS):
                o_g, lse_g = attn_fwd(qT, kT, vT, bias, gi, B, S)
                oTs.append(_unperm(o_g, B, S, d))
                lses.append(_unperm(lse_g, B, S, d))
            oTb, oTf, lse_tot = attn_combine(oTs, lses)
            a1, h, hb, _ = matmul_ln(oTb, weight(("wo", j)), TN, cur, ln1_g[i], ln1_b[i], alpha, "o_proj_ln")
            sv.update(xT3=xT3, qT=qT, oTb=oTb, oTf=oTf, lse_tot=lse_tot)
        wg_i, wu_i = weight(("wg", i)), weight(("wu", i))
        keys, cr = next_gather()
        g, ge, ud, hh, got = ffn_up(hb, wg_i, wu_i, conv_w_full[i], ffn_conv_b[i].reshape(1, Fd), B, S, carry=cr)
        land(keys, got)
        wd_i = weight(("wd", i))
        keys, cr = next_gather()
        a2, cur, curb, got = matmul_ln(hh, wd_i, NN, h, ln2_g[i], ln2_b[i], alpha, "ffn_down_ln", carry=cr)
        land(keys, got)
        sv.update(a1=a1, hb=hb, g=g, ge=ge, ud=ud, hh=hh, a2=a2)
        saved.append(sv)
        if i == nA - 1:
            x1T3 = _xT3(curb, B, S)
            wkv = weight(("wkv",))
            keys, cr = next_gather()
            kT, got = proj_T(wkv, x1T3, cosT, sinT, 0, True, 1.0, "k_proj", carry=cr)
            land(keys, got)
            keys, cr = next_gather()
            vT, got = proj_T(wkv, x1T3, cosT, sinT, G, False, 1.0, "v_proj", carry=cr)
            land(keys, got)


    small = {k: [None] * depth for k in ("ln1_g", "ln1_b", "ln2_g", "ln2_b", "conv_b", "conv_w")}
    dscale = [None] * nA
    dpw = [None] * nA
    dk_acc, dv_acc = [None] * G, [None] * G

    def blocks(a, rows):
        return a.reshape(N_DEV, rows, D)

    pending, landed = [], {}

    def next_carry():
        if not pending:
            return None, None
        key, parts = pending.pop(0)
        return key, Scatter(parts)

    dcur = sq = ln2_done = None
    for i in reversed(range(depth)):
        sv = saved[i]
        if i == depth - 1:
            db2, db2b, small["ln2_g"][i], small["ln2_b"][i], sq = loss_ln_bwd(cur, loss_target.reshape(T, D), sv["a2"], ln2_g[i])
        elif ln2_done is not None:
            db2, db2b, small["ln2_g"][i], small["ln2_b"][i] = ln2_done
            ln2_done = None
        else:
            db2, db2b, small["ln2_g"][i], small["ln2_b"][i] = ln_bwd(dcur, sv["a2"], ln2_g[i])
        key, cr = next_carry()
        dg_, du_, dcb, dcw, got = ffn_mid_bwd(db2b, gathered[("wd", i)], sv["g"], sv["ge"], sv["ud"], conv_w_full[i], B, S, carry=cr)
        if cr is not None:
            landed[key] = got
        small["conv_b"][i] = jnp.sum(dcb, axis=0)
        small["conv_w"][i] = jnp.sum(dcw, axis=0)
        key, cr = next_carry()
        dwd, got = wgrad_rows(sv["hh"], db2b, "wgrad_down", carry=cr)
        if cr is not None:
            landed[key] = got
        dwg, landed[("down", i)] = wgrad_rows(dg_, sv["hb"], "wgrad_gate", carry=Scatter([blocks(dwd, Fs)]))
        dwu, landed[("gate", i)] = wgrad_rows(du_, sv["hb"], "wgrad_up", carry=Scatter([blocks(dwg, Fs)]))
        da1, da1b, small["ln1_g"][i], small["ln1_b"][i], landed[("up", i)] = ffn_dx_ln(
            dg_, du_, gathered[("wg", i)], gathered[("wu", i)], db2, alpha, sv["a1"], ln1_g[i], carry=Scatter([blocks(dwu, Fs)]))
        if i < nA:
            dcur, dsp, dpwp = pool_bwd(da1, sv["pooled"], PW[i], pool_scale_full[i], alpha, B, S)
            dscale[i] = jnp.sum(dsp, axis=0)
            dpw[i] = jnp.sum(dpwp, axis=0)
        else:
            j = i - nA
            doT = matmul_to_T(gathered[("wo", j)], da1b, "o_proj_bwd")
            dwo = wgrad_mixed(sv["oTb"], da1b, "wgrad_o")
            delta = attn_delta(doT, sv["oTf"])
            dq_tok, dwq = [], []
            for gi, d in enumerate(DILATIONS):
                dq_g, dk_acc[gi], dv_acc[gi] = attn_bwd(
                    sv["qT"], kT, vT, _perm(doT, B, S, d), _perm(sv["lse_tot"], B, S, d), _perm(delta, B, S, d),
                    cosT, sinT, bias, gi, HEAD_DIM ** -0.5, B, S, dk_prev=dk_acc[gi], dv_prev=dv_acc[gi])
                dwq.append(wgrad_T(dq_g, sv["xT3"], gi, "wgrad_q"))
                dq_tok.append(_unperm(dq_g, B, S, d))
            dwq = jnp.concatenate(dwq, axis=0)
            below = (saved[i - 1]["a2"], ln2_g[i - 1]) if i > 0 else None
            if j > 0:
                last = dx_from_T(dq_tok, gathered[("wq", j)], da1, alpha, "q_proj_bwd", 512, ln=below)
            else:
                dcur = dx_from_T(dq_tok, gathered[("wq", j)], da1, alpha, "q_proj_bwd", 512)
                dkv = [a.astype(BF16) for a in dk_acc + dv_acc]
                dwkv = jnp.concatenate([wgrad_T(a, x1T3, gi % G, "wgrad_kv") for gi, a in enumerate(dkv)], axis=0)
                dkv_tok = [_unperm(a, B, S, DILATIONS[gi % G]) for gi, a in enumerate(dkv)]
                last = dx_from_T(dkv_tok, gathered[("wkv",)], dcur, 1.0, "kv_proj_bwd", 256, ln=below)
                pending.append((("kv",), [blocks(dwkv, kvs)]))
            if below is None:
                dcur = last
            else:
                ln2_done = last
            pending.append((("attn", j), [blocks(dwq, qs), blocks(dwo, os_)]))
    grad_x = dcur.reshape(B, S, D)

    dpw_all = jnp.stack(dpw).reshape(nA, PG, N_DEV, Cg // N_DEV, Cg).transpose(2, 0, 1, 3, 4).reshape(N_DEV, pool_rows, D)
    tail_keys = [k for k, _ in pending] + [("pool",)]
    tail_parts = [parts for _, parts in pending] + [[dpw_all.astype(BF16)]]
    tail_rows = [sum(p.shape[1] for p in parts) for parts in tail_parts]
    tail = scatter_partials([p for parts in tail_parts for p in parts], "scatter_tail")
    for t, key in enumerate(tail_keys):
        lo = sum(tail_rows[:t])
        landed[key] = tail[:, lo:lo + tail_rows[t]]

    def reduced(key):
        return sum_slots(landed[key], "sum_" + "_".join(str(k) for k in key))

    g_attn = [reduced(("attn", j)) for j in range(nB)]
    g_w_q = jnp.swapaxes(jnp.stack([a[:qs] for a in g_attn]), 1, 2)
    g_w_o = jnp.stack([a[qs:] for a in g_attn])
    g_w_kv = reduced(("kv",)).T
    g_gate = jnp.swapaxes(jnp.stack([reduced(("gate", i)) for i in range(depth)]), 1, 2)
    g_up = jnp.swapaxes(jnp.stack([reduced(("up", i)) for i in range(depth)]), 1, 2)
    g_down = jnp.stack([reduced(("down", i)) for i in range(depth)])
    g_pool_w = reduced(("pool",)).reshape(pool_w.shape)

    def rows_of(a):
        a = a.reshape(-1)
        n = -(-a.size // D) * D
        return jnp.pad(a, (0, n - a.size)).reshape(-1, D)

    sm_parts = [rows_of(jnp.concatenate(small[k], axis=0)) for k in ("ln1_g", "ln1_b", "ln2_g", "ln2_b")]
    sm_parts += [rows_of(jnp.stack(small["conv_b"])), rows_of(jnp.stack(small["conv_w"])), rows_of(jnp.stack(dscale)), sq]
    sm_sizes = [p.shape[0] for p in sm_parts]
    sm_all = jnp.concatenate(sm_parts, axis=0)
    pad_rows = -(-sm_all.shape[0] // 8) * 8 - sm_all.shape[0]
    sm_all = jnp.pad(sm_all, ((0, pad_rows), (0, 0)))
    sm_sum = sum_slots(all_gather_blocks(sm_all, "gather_small_grads", in_vmem=True), "sum_small_grads")
    sm_offs = [sum(sm_sizes[:i]) for i in range(len(sm_sizes))]

    def sm_take(i, shape):
        n = math.prod(shape)
        return sm_sum[sm_offs[i]:sm_offs[i] + sm_sizes[i]].reshape(-1)[:n].reshape(shape)

    g_ln1_g, g_ln1_b = sm_take(0, (depth, D)), sm_take(1, (depth, D))
    g_ln2_g, g_ln2_b = sm_take(2, (depth, D)), sm_take(3, (depth, D))
    g_conv_b = sm_take(4, (depth, Fd))
    g_conv_w = lax.dynamic_slice_in_dim(sm_take(5, (depth, 3, Fd)), me * Fs, Fs, axis=2)
    g_pool_scale = lax.dynamic_slice_in_dim(sm_take(6, (nA, D)), me * (D // N_DEV), D // N_DEV, axis=1)
    loss = (0.5 / D) * jnp.sum(sm_take(7, (D,)))

    def v2(a):
        return a.reshape(-1, a.shape[-1])

    names = ["pool_w", "pool_scale", "w_q", "w_kv", "w_o", "ffn_w_gate", "ffn_w_up", "ffn_conv_w", "ffn_conv_b",
             "ffn_w_down", "ln1_g", "ln1_b", "ln2_g", "ln2_b"]
    ws = [pool_w, pool_scale, w_q, w_kv, w_o, ffn_w_gate, ffn_w_up, ffn_conv_w, ffn_conv_b, ffn_w_down, ln1_g, ln1_b, ln2_g, ln2_b]
    ms = [m_pool_w, m_pool_scale, m_w_q, m_w_kv, m_w_o, m_ffn_w_gate, m_ffn_w_up, m_ffn_conv_w, m_ffn_conv_b, m_ffn_w_down, m_ln1_g, m_ln1_b, m_ln2_g, m_ln2_b]
    vs = [v_pool_w, v_pool_scale, v_w_q, v_w_kv, v_w_o, v_ffn_w_gate, v_ffn_w_up, v_ffn_conv_w, v_ffn_conv_b, v_ffn_w_down, v_ln1_g, v_ln1_b, v_ln2_g, v_ln2_b]
    gs = [g_pool_w, g_pool_scale, g_w_q, g_w_kv, g_w_o, g_gate, g_up, g_conv_w, g_conv_b, g_down, g_ln1_g, g_ln1_b, g_ln2_g, g_ln2_b]
    deltas, new_ms, new_vs = [], [], []
    for nm, w, gr, m_, v_ in zip(names, ws, gs, ms, vs):
        d_, nm_, nv_ = adamw(v2(w), v2(gr), v2(m_), v2(v_), "adamw_" + nm)
        deltas.append(d_.reshape(w.shape))
        new_ms.append(nm_.reshape(w.shape))
        new_vs.append(nv_.reshape(w.shape))

    return (loss, grad_x, *gs, *deltas, *new_ms, *new_vs)
```

```python
import functools
import math

import jax
import jax.numpy as jnp
from jax import lax
from jax.experimental import pallas as pl
from jax.experimental.pallas import tpu as pltpu

F32 = jnp.float32
BF16 = jnp.bfloat16
SDS = jax.ShapeDtypeStruct
MESH = pl.DeviceIdType.MESH

N_DEV = 8
HEAD_DIM = 64
BLK = 128
DILATIONS = (1, 4, 16)
POOL_WINDOWS = (2, 4, 8, 16)
ROPE_THETA = 10000.0
LN_EPS = 1e-5
NEG = -1e30
V7X_VMEM_LIMIT = 56 * 1024 * 1024

ADAM_LR, ADAM_B1, ADAM_B2, ADAM_EPS, ADAM_WD, ADAM_STEP = 0.001, 0.9, 0.999, 1e-08, 0.01, 10

NN = (((1,), (0,)), ((), ()))
NT = (((1,), (1,)), ((), ()))
TN = (((0,), (0,)), ((), ()))


def _cp(sem=None):
    kw = dict(vmem_limit_bytes=V7X_VMEM_LIMIT)
    if sem is not None:
        kw["dimension_semantics"] = sem
    return pltpu.CompilerParams(**kw)


def _dot(a, b, dims=NN):
    return lax.dot_general(a, b, dims, preferred_element_type=F32)


def _tile(n, target, mult):
    best = None
    for t in range(mult, min(n, target) + 1, mult):
        if n % t == 0:
            best = t
    return best if best is not None else n


def _mesh_pos():
    return lax.axis_index("x"), lax.axis_index("y"), lax.axis_index("c")


def all_gather_blocks(xl, name, in_vmem):
    R, C = xl.shape
    space = pltpu.VMEM if in_vmem else pl.ANY

    def body(x_ref, out_ref, send_sems, recv_sems, local_sem):
        x, y, c = _mesh_pos()
        me, sibling = (x, y, c), (x, y, 1 - c)
        chips = [(1 - x, y), (x, 1 - y), (1 - x, 1 - y)]

        def slot(px, py, pc):
            return out_ref.at[4 * px + 2 * py + pc]

        def copy(k, block, to, src=None):
            return pltpu.make_async_remote_copy(
                src_ref=slot(*block) if src is None else src, dst_ref=slot(*block),
                send_sem=send_sems.at[k], recv_sem=recv_sems.at[k], device_id=to, device_id_type=MESH)

        mine = pltpu.make_async_copy(x_ref, slot(*me), local_sem)
        mine.start()
        first = [copy(0, me, sibling, src=x_ref)]
        first += [copy(1 + j, me, (*chip, c), src=x_ref) for j, chip in enumerate(chips)]
        for cp in first:
            cp.start()
        passed = [copy(4 + j, (*chip, c), sibling) for j, chip in enumerate(chips)]
        for j, chip in enumerate(chips):
            copy(1 + j, (*chip, c), me).wait_recv()
            passed[j].start()
        copy(0, sibling, me).wait_recv()
        for j, chip in enumerate(chips):
            copy(4 + j, (*chip, 1 - c), me).wait_recv()
        for cp in first + passed:
            cp.wait_send()
        mine.wait()

    return pl.pallas_call(
        body, name=name,
        out_shape=SDS((N_DEV, R, C), xl.dtype),
        in_specs=[pl.BlockSpec(memory_space=space)],
        out_specs=pl.BlockSpec(memory_space=space),
        scratch_shapes=[pltpu.SemaphoreType.DMA((7,)), pltpu.SemaphoreType.DMA((7,)), pltpu.SemaphoreType.DMA],
        compiler_params=_cp(),
    )(xl)


def _peers():
    x, y, c = _mesh_pos()
    peers = []
    for r in range(1, N_DEV):
        peers.append((1 - x if (r & 4) else x, 1 - y if (r & 2) else y, 1 - c if (r & 1) else c))
    return 4 * x + 2 * y + c, peers


class Gather:
    def __init__(self, parts):
        self.parts = list(parts)
        n = len(self.parts)
        self.out_shapes = [SDS((N_DEV,) + p.shape, p.dtype) for p in self.parts]
        self.scratch = [pltpu.SemaphoreType.DMA((7 * n,)), pltpu.SemaphoreType.DMA((7 * n,)), pltpu.SemaphoreType.DMA((n,))]

    def start(self, part_refs, out_refs, send_sems, recv_sems, local_sems):
        me_lin, peers = _peers()
        n = len(self.parts)
        for i in range(n):
            pltpu.make_async_copy(part_refs[i], out_refs[i].at[me_lin], local_sems.at[i]).start()
        for k, peer in enumerate(peers):
            for i in range(n):
                pltpu.make_async_remote_copy(
                    src_ref=part_refs[i], dst_ref=out_refs[i].at[me_lin],
                    send_sem=send_sems.at[k * n + i], recv_sem=recv_sems.at[k * n + i],
                    device_id=peer, device_id_type=MESH).start()

    def wait(self, out_refs, send_sems, recv_sems, local_sems):
        me_lin, peers = _peers()
        n = len(self.parts)
        for k, (px, py, pc) in enumerate(peers):
            p_lin = 4 * px + 2 * py + pc
            for i in range(n):
                arrival = pltpu.make_async_remote_copy(
                    src_ref=out_refs[i].at[p_lin], dst_ref=out_refs[i].at[p_lin],
                    send_sem=send_sems.at[k * n + i], recv_sem=recv_sems.at[k * n + i],
                    device_id=(px, py, pc), device_id_type=MESH)
                arrival.wait_recv()
                arrival.wait_send()
        for i in range(n):
            pltpu.make_async_copy(out_refs[i].at[me_lin], out_refs[i].at[me_lin], local_sems.at[i]).wait()


class Scatter:
    def __init__(self, parts):
        self.parts = list(parts)
        self.rows = [p.shape[1] for p in parts]
        self.offs = [sum(self.rows[:i]) for i in range(len(self.rows))]
        self.out_shapes = [SDS((N_DEV, sum(self.rows), parts[0].shape[2]), parts[0].dtype)]
        self.scratch = [pltpu.SemaphoreType.DMA((7,)), pltpu.SemaphoreType.DMA((7,)), pltpu.SemaphoreType.DMA]

    def start(self, part_refs, out_refs, send_sems, recv_sems, local_sem):
        out_ref = out_refs[0]
        me_lin, peers = _peers()
        for i, (off, r) in enumerate(zip(self.offs, self.rows)):
            pltpu.make_async_copy(part_refs[i].at[me_lin], out_ref.at[me_lin, pl.ds(off, r)], local_sem).start()
        for k, (px, py, pc) in enumerate(peers):
            p_lin = 4 * px + 2 * py + pc
            for i, (off, r) in enumerate(zip(self.offs, self.rows)):
                pltpu.make_async_remote_copy(
                    src_ref=part_refs[i].at[p_lin], dst_ref=out_ref.at[me_lin, pl.ds(off, r)],
                    send_sem=send_sems.at[k], recv_sem=recv_sems.at[k],
                    device_id=(px, py, pc), device_id_type=MESH).start()

    def wait(self, out_refs, send_sems, recv_sems, local_sem):
        out_ref = out_refs[0]
        me_lin, peers = _peers()
        for k, (px, py, pc) in enumerate(peers):
            p_lin = 4 * px + 2 * py + pc
            whole = pltpu.make_async_remote_copy(
                src_ref=out_ref.at[p_lin], dst_ref=out_ref.at[p_lin],
                send_sem=send_sems.at[k], recv_sem=recv_sems.at[k],
                device_id=(px, py, pc), device_id_type=MESH)
            whole.wait_recv()
            whole.wait_send()
        pltpu.make_async_copy(out_ref.at[me_lin], out_ref.at[me_lin], local_sem).wait()


def scatter_partials(parts, name):
    sc = Scatter(parts)
    n = len(parts)

    def body(*refs):
        sc.start(refs[:n], refs[n:n + 1], *refs[n + 1:])
        sc.wait(refs[n:n + 1], *refs[n + 1:])

    return pl.pallas_call(
        body, name=name, out_shape=sc.out_shapes[0],
        in_specs=[pl.BlockSpec(memory_space=pl.ANY)] * n, out_specs=pl.BlockSpec(memory_space=pl.ANY),
        scratch_shapes=sc.scratch, compiler_params=_cp(),
    )(*parts)


def _call(body, name, grid, in_specs, out_specs, out_shape, args, scratch=(), sem=None, carry=None):
    in_specs, out_specs, out_shape, scratch = list(in_specs), list(out_specs), list(out_shape), list(scratch)
    if carry is None:
        outs = pl.pallas_call(body, name=name, grid=grid, in_specs=in_specs, out_specs=out_specs, out_shape=out_shape,
                              scratch_shapes=scratch, compiler_params=_cp(sem))(*args)
        return list(outs), None
    n_in, n_out, n_scr, n_c, n_co = len(in_specs), len(out_specs), len(scratch), len(carry.parts), len(carry.out_shapes)
    last = [g - 1 for g in grid]

    def carried(*refs):
        ins, c_ins = refs[:n_in], refs[n_in:n_in + n_c]
        o0 = n_in + n_c
        outs, c_out = refs[o0:o0 + n_out], refs[o0 + n_out:o0 + n_out + n_co]
        s0 = o0 + n_out + n_co
        scr, c_scr = refs[s0:s0 + n_scr], refs[s0 + n_scr:]
        ids = [pl.program_id(a) for a in range(len(grid))]
        is_first = functools.reduce(jnp.logical_and, [i == 0 for i in ids])
        is_last = functools.reduce(jnp.logical_and, [i == l for i, l in zip(ids, last)])

        @pl.when(is_first)
        def _():
            carry.start(c_ins, c_out, *c_scr)

        body(*ins, *outs, *scr)

        @pl.when(is_last)
        def _():
            carry.wait(c_out, *c_scr)

    hbm = pl.BlockSpec(memory_space=pl.ANY)
    outs = pl.pallas_call(
        carried, name=name + "_carry", grid=grid, in_specs=in_specs + [hbm] * n_c, out_specs=out_specs + [hbm] * n_co,
        out_shape=out_shape + carry.out_shapes, scratch_shapes=scratch + carry.scratch,
        compiler_params=_cp(sem if sem is not None else ("arbitrary",) * len(grid)),
    )(*args, *carry.parts)
    return list(outs[:n_out]), list(outs[n_out:])


def sum_slots(slots, name, out_dtype=F32):
    _, R, C = slots.shape
    tr = _tile(R, 512, 16)

    def body(s_ref, o_ref):
        acc = s_ref[0].astype(F32)
        for s in range(1, N_DEV):
            acc = acc + s_ref[s].astype(F32)
        o_ref[...] = acc.astype(out_dtype)

    return pl.pallas_call(
        body, name=name, grid=(R // tr,),
        in_specs=[pl.BlockSpec((N_DEV, tr, C), lambda i: (0, i, 0))],
        out_specs=pl.BlockSpec((tr, C), lambda i: (i, 0)),
        out_shape=SDS((R, C), out_dtype), compiler_params=_cp(),
    )(slots)


def add_ln(x, mix, g, b, alpha, carry=None):
    T, D = x.shape
    tm = _tile(T, 512, 16)

    def body(x_ref, m_ref, g_ref, b_ref, a_ref, y_ref, yb_ref):
        a = alpha * x_ref[...] + m_ref[...]
        mu = jnp.mean(a, axis=-1, keepdims=True)
        xc = a - mu
        var = jnp.mean(xc * xc, axis=-1, keepdims=True)
        y = xc * lax.rsqrt(var + LN_EPS) * g_ref[...] + b_ref[...]
        a_ref[...] = a
        y_ref[...] = y
        yb_ref[...] = y.astype(BF16)

    row = pl.BlockSpec((tm, D), lambda i: (i, 0))
    vec = pl.BlockSpec((1, D), lambda i: (0, 0))
    outs, landed = _call(body, "add_ln", (T // tm,), [row, row, vec, vec], [row, row, row],
                         [SDS((T, D), F32), SDS((T, D), F32), SDS((T, D), BF16)],
                         (x, mix, g.reshape(1, D), b.reshape(1, D)), carry=carry)
    return (*outs, landed)


def _ln_bwd_tile(dy, a, gamma):
    mu = jnp.mean(a, axis=-1, keepdims=True)
    xc = a - mu
    var = jnp.mean(xc * xc, axis=-1, keepdims=True)
    r = lax.rsqrt(var + LN_EPS)
    xh = xc * r
    dxh = dy * gamma
    m1 = jnp.mean(dxh, axis=-1, keepdims=True)
    m2 = jnp.mean(dxh * xh, axis=-1, keepdims=True)
    da = r * (dxh - m1 - xh * m2)
    return da, jnp.sum(dy * xh, axis=0, keepdims=True), jnp.sum(dy, axis=0, keepdims=True)


def ln_bwd(dy, a, g):
    T, D = a.shape
    tm = _tile(T, 512, 16)

    def body(dy_ref, a_ref, g_ref, da_ref, dab_ref, dg_ref, db_ref):
        @pl.when(pl.program_id(0) == 0)
        def _():
            dg_ref[...] = jnp.zeros_like(dg_ref)
            db_ref[...] = jnp.zeros_like(db_ref)

        da, sg, sb = _ln_bwd_tile(dy_ref[...], a_ref[...], g_ref[...])
        da_ref[...] = da
        dab_ref[...] = da.astype(BF16)
        dg_ref[...] += sg
        db_ref[...] += sb

    row = pl.BlockSpec((tm, D), lambda i: (i, 0))
    vec = pl.BlockSpec((1, D), lambda i: (0, 0))
    return pl.pallas_call(
        body, name="ln_bwd", grid=(T // tm,),
        in_specs=[row, row, vec], out_specs=[row, row, vec, vec],
        out_shape=[SDS((T, D), F32), SDS((T, D), BF16), SDS((1, D), F32), SDS((1, D), F32)],
        compiler_params=_cp(("arbitrary",)),
    )(dy, a, g.reshape(1, D))


def loss_ln_bwd(y, tgt, a, g):
    T, D = y.shape
    tm = _tile(T, 512, 16)

    def body(y_ref, t_ref, a_ref, g_ref, da_ref, dab_ref, dg_ref, db_ref, sq_ref):
        @pl.when(pl.program_id(0) == 0)
        def _():
            dg_ref[...] = jnp.zeros_like(dg_ref)
            db_ref[...] = jnp.zeros_like(db_ref)
            sq_ref[...] = jnp.zeros_like(sq_ref)

        e = y_ref[...] - t_ref[...]
        da, sg, sb = _ln_bwd_tile(e / float(D), a_ref[...], g_ref[...])
        da_ref[...] = da
        dab_ref[...] = da.astype(BF16)
        dg_ref[...] += sg
        db_ref[...] += sb
        sq_ref[...] += jnp.sum(e * e, axis=0, keepdims=True)

    row = pl.BlockSpec((tm, D), lambda i: (i, 0))
    vec = pl.BlockSpec((1, D), lambda i: (0, 0))
    return pl.pallas_call(
        body, name="loss_ln_bwd", grid=(T // tm,),
        in_specs=[row, row, row, vec], out_specs=[row, row, vec, vec, vec],
        out_shape=[SDS((T, D), F32), SDS((T, D), BF16), SDS((1, D), F32), SDS((1, D), F32), SDS((1, D), F32)],
        compiler_params=_cp(("arbitrary",)),
    )(y, tgt, a, g.reshape(1, D))


def matmul_ln(a, w, dims, res, g, b, alpha, name, carry=None):
    if dims == TN:
        K, T = a.shape
    else:
        T, K = a.shape
    D = w.shape[1]
    tm = _tile(T, 512, 128 if dims == TN else 16)

    def body(a_ref, w_ref, r_ref, g_ref, b_ref, p_ref, y_ref, yb_ref):
        pre = alpha * r_ref[...] + _dot(a_ref[...], w_ref[...], dims)
        mu = jnp.mean(pre, axis=-1, keepdims=True)
        xc = pre - mu
        var = jnp.mean(xc * xc, axis=-1, keepdims=True)
        y = xc * lax.rsqrt(var + LN_EPS) * g_ref[...] + b_ref[...]
        p_ref[...] = pre
        y_ref[...] = y
        yb_ref[...] = y.astype(BF16)

    a_spec = pl.BlockSpec((K, tm), lambda i: (0, i)) if dims == TN else pl.BlockSpec((tm, K), lambda i: (i, 0))
    row = pl.BlockSpec((tm, D), lambda i: (i, 0))
    vec = pl.BlockSpec((1, D), lambda i: (0, 0))
    outs, landed = _call(
        body, name, (T // tm,), [a_spec, pl.BlockSpec(w.shape, lambda i: (0, 0)), row, vec, vec], [row, row, row],
        [SDS((T, D), F32), SDS((T, D), F32), SDS((T, D), BF16)], (a, w, res, g.reshape(1, D), b.reshape(1, D)), carry=carry)
    return (*outs, landed)


def dx_from_T(aTs, w, res, alpha, name, tm_target, ln=None):
    T = aTs[0].shape[1]
    N = w.shape[1]
    ks = [a.shape[0] for a in aTs]
    n = len(aTs)
    tm = _tile(T, tm_target, 128)

    def body(*refs):
        a_refs, w_ref, r_ref = refs[:n], refs[n], refs[n + 1]
        acc = alpha * r_ref[...]
        off = 0
        for a_ref, k in zip(a_refs, ks):
            acc = acc + _dot(a_ref[...], w_ref[off:off + k, :], TN)
            off += k
        if ln is None:
            refs[n + 2][...] = acc
            return
        ln_a, ln_g, da_ref, dab_ref, dgm_ref, dbt_ref = refs[n + 2:]

        @pl.when(pl.program_id(0) == 0)
        def _():
            dgm_ref[...] = jnp.zeros_like(dgm_ref)
            dbt_ref[...] = jnp.zeros_like(dbt_ref)

        da, sg, sb = _ln_bwd_tile(acc, ln_a[...], ln_g[...])
        da_ref[...] = da
        dab_ref[...] = da.astype(BF16)
        dgm_ref[...] += sg
        dbt_ref[...] += sb

    row = pl.BlockSpec((tm, N), lambda i: (i, 0))
    vec = pl.BlockSpec((1, N), lambda i: (0, 0))
    in_specs = [pl.BlockSpec((k, tm), lambda i: (0, i)) for k in ks] + [pl.BlockSpec(w.shape, lambda i: (0, 0)), row]
    args = list(aTs) + [w, res]
    if ln is None:
        return pl.pallas_call(body, name=name, grid=(T // tm,), in_specs=in_specs, out_specs=row,
                              out_shape=SDS((T, N), F32), compiler_params=_cp())(*args)
    return pl.pallas_call(
        body, name=name + "_ln", grid=(T // tm,), in_specs=in_specs + [row, vec], out_specs=[row, row, vec, vec],
        out_shape=[SDS((T, N), F32), SDS((T, N), BF16), SDS((1, N), F32), SDS((1, N), F32)],
        compiler_params=_cp(("arbitrary",)),
    )(*args, ln[0], ln[1].reshape(1, N))


def matmul_to_T(w, a, name):
    M, K = w.shape
    T = a.shape[0]
    tt = _tile(T, 512, 128)

    def body(w_ref, a_ref, o_ref):
        o_ref[...] = _dot(w_ref[...], a_ref[...], NT).astype(BF16)

    return pl.pallas_call(
        body, name=name, grid=(T // tt,),
        in_specs=[pl.BlockSpec((M, K), lambda i: (0, 0)), pl.BlockSpec((tt, K), lambda i: (i, 0))],
        out_specs=pl.BlockSpec((M, tt), lambda i: (0, i)),
        out_shape=SDS((M, T), BF16), compiler_params=_cp(),
    )(w, a)


def wgrad_rows(a, b, name, carry=None):
    T, M = a.shape
    N = b.shape[1]
    tt = _tile(T, 1024, 16)
    tmm = _tile(M, 1536, 128)
    nt = T // tt

    def body(a_ref, b_ref, o_ref, acc_ref):
        t = pl.program_id(1)

        @pl.when(t == 0)
        def _():
            acc_ref[...] = jnp.zeros_like(acc_ref)

        acc_ref[...] += _dot(a_ref[...], b_ref[...], TN)

        @pl.when(t == nt - 1)
        def _():
            o_ref[...] = acc_ref[...].astype(BF16)

    outs, landed = _call(
        body, name, (M // tmm, nt),
        [pl.BlockSpec((tt, tmm), lambda i, t: (t, i)), pl.BlockSpec((tt, N), lambda i, t: (t, 0))],
        [pl.BlockSpec((tmm, N), lambda i, t: (i, 0))], [SDS((M, N), BF16)], (a, b),
        scratch=[pltpu.VMEM((tmm, N), F32)], sem=("arbitrary", "arbitrary"), carry=carry)
    return outs[0], (landed[0] if landed else None)


def wgrad_T(aT, bT3, g, name):
    M, T = aT.shape
    N = bT3.shape[1]
    tt = _tile(T, 2048, 128)
    nt = T // tt

    def body(a_ref, b_ref, o_ref, acc_ref):
        t = pl.program_id(0)

        @pl.when(t == 0)
        def _():
            acc_ref[...] = jnp.zeros_like(acc_ref)

        acc_ref[...] += _dot(a_ref[...], b_ref[0], NT)

        @pl.when(t == nt - 1)
        def _():
            o_ref[...] = acc_ref[...].astype(BF16)

    return pl.pallas_call(
        body, name=name, grid=(nt,),
        in_specs=[pl.BlockSpec((M, tt), lambda t: (0, t)), pl.BlockSpec((1, N, tt), lambda t: (g, 0, t))],
        out_specs=pl.BlockSpec((M, N), lambda t: (0, 0)),
        out_shape=SDS((M, N), BF16), scratch_shapes=[pltpu.VMEM((M, N), F32)],
        compiler_params=_cp(("arbitrary",)),
    )(aT, bT3)


def wgrad_mixed(aT, b, name):
    M, T = aT.shape
    N = b.shape[1]
    tt = _tile(T, 2048, 128)
    nt = T // tt

    def body(a_ref, b_ref, o_ref, acc_ref):
        t = pl.program_id(0)

        @pl.when(t == 0)
        def _():
            acc_ref[...] = jnp.zeros_like(acc_ref)

        acc_ref[...] += _dot(a_ref[...], b_ref[...], NN)

        @pl.when(t == nt - 1)
        def _():
            o_ref[...] = acc_ref[...].astype(BF16)

    return pl.pallas_call(
        body, name=name, grid=(nt,),
        in_specs=[pl.BlockSpec((M, tt), lambda t: (0, t)), pl.BlockSpec((tt, N), lambda t: (t, 0))],
        out_specs=pl.BlockSpec((M, N), lambda t: (0, 0)),
        out_shape=SDS((M, N), BF16), scratch_shapes=[pltpu.VMEM((M, N), F32)],
        compiler_params=_cp(("arbitrary",)),
    )(aT, b)


def _shift_down(x, k, rows):
    return jnp.where(rows >= k, pltpu.roll(x, k, 0), 0.0)


def _shift_up(x, k, rows):
    n = x.shape[0]
    return jnp.where(rows < n - k, pltpu.roll(x, n - k, 0), 0.0)


def _pick(g, vals):
    out = vals[-1]
    for k in range(len(vals) - 2, -1, -1):
        out = jnp.where(g == k, vals[k], out)
    return out


def pool_fwd(x, pw, scale, B, S, carry=None):
    T, D = x.shape
    G = len(POOL_WINDOWS)
    Cg = D // G

    def body(x_ref, w_ref, s_ref, mix_ref, pooled_ref):
        g = pl.program_id(1)
        xv = x_ref[...]
        rows = lax.broadcasted_iota(jnp.int32, xv.shape, 0)
        sums, cur, k = [], xv, 1
        for _ in POOL_WINDOWS:
            cur = cur + _shift_down(cur, k, rows)
            sums.append(cur)
            k *= 2
        win = 2 * lax.shift_left(jnp.int32(1), g)
        total = _pick(g, sums)
        count = jnp.minimum(rows + 1, win).astype(F32)
        pooled = total / count - xv
        pb = pooled.astype(BF16)
        pooled_ref[...] = pb
        mix_ref[...] = _dot(pb, w_ref[0]) * s_ref[...]

    blk = pl.BlockSpec((S, Cg), lambda b, g: (b, g))
    outs, landed = _call(
        body, "pool_fwd", (B, G),
        [blk, pl.BlockSpec((1, Cg, Cg), lambda b, g: (g, 0, 0)), pl.BlockSpec((1, Cg), lambda b, g: (0, g))],
        [blk, blk], [SDS((T, D), F32), SDS((T, D), BF16)], (x, pw, scale), carry=carry)
    return (*outs, landed)


def pool_bwd(dmix, pooled, pw, scale, alpha, B, S):
    T, D = dmix.shape
    G = len(POOL_WINDOWS)
    Cg = D // G

    def body(d_ref, p_ref, w_ref, s_ref, dx_ref, ds_ref, dw_ref):
        g = pl.program_id(1)
        dm = d_ref[...]
        pb = p_ref[...]
        w = w_ref[0]
        ypre = _dot(pb, w)
        ds_ref[0] = jnp.sum(dm * ypre, axis=0, keepdims=True)
        dy = (dm * s_ref[...]).astype(BF16)
        dpool = _dot(dy, w, NT)
        dw_ref[0, 0] = _dot(pb, dy, TN)
        rows = lax.broadcasted_iota(jnp.int32, dm.shape, 0)
        win = 2 * lax.shift_left(jnp.int32(1), g)
        count = jnp.minimum(rows + 1, win).astype(F32)
        cur, k, sums = dpool / count, 1, []
        for _ in POOL_WINDOWS:
            cur = cur + _shift_up(cur, k, rows)
            sums.append(cur)
            k *= 2
        dx_ref[...] = alpha * dm + _pick(g, sums) - dpool

    blk = pl.BlockSpec((S, Cg), lambda b, g: (b, g))
    return pl.pallas_call(
        body, name="pool_bwd", grid=(B, G),
        in_specs=[blk, blk, pl.BlockSpec((1, Cg, Cg), lambda b, g: (g, 0, 0)), pl.BlockSpec((1, Cg), lambda b, g: (0, g))],
        out_specs=[blk, pl.BlockSpec((1, 1, Cg), lambda b, g: (b, 0, g)),
                   pl.BlockSpec((1, 1, Cg, Cg), lambda b, g: (b, g, 0, 0))],
        out_shape=[SDS((T, D), F32), SDS((B, 1, D), F32), SDS((B, G, Cg, Cg), F32)], compiler_params=_cp(),
    )(dmix, pooled, pw, scale)


_GELU_K = math.sqrt(2.0 / math.pi)
_GELU_C = 0.044715
FFN_ROWS = 512
FFN_HALO = 16


def ffn_up(hb, wgT, wuT, cw, cb, B, S, carry=None):
    T, D = hb.shape
    Fd = wgT.shape[0]
    fn = _tile(Fd, 256, 128)

    nc = S // _tile(S, FFN_ROWS, FFN_HALO)
    rc = S // nc

    def body(h_ref, wg_ref, wu_ref, cw_ref, cb_ref, g_ref, ge_ref, ud_ref, hh_ref):
        wg, wu, cw, cb = wg_ref[...], wu_ref[...], cw_ref[...], cb_ref[...]
        halo = jnp.zeros((FFN_HALO, fn), F32)
        for ci in range(nc):
            rows = slice(ci * rc, (ci + 1) * rc)
            h = h_ref[rows, :]
            g = _dot(h, wg, NT)
            u = _dot(h, wu, NT)
            gext = jnp.concatenate([halo, g], axis=0)
            halo = g[rc - FFN_HALO:, :]
            c = cb + cw[0:1] * pltpu.roll(gext, 2, 0)[FFN_HALO:, :] + cw[1:2] * pltpu.roll(gext, 1, 0)[FFN_HALO:, :] + cw[2:3] * g
            c2 = c * c
            th = jnp.tanh(c * (_GELU_K + (_GELU_K * _GELU_C) * c2))
            cdf = 0.5 * th + 0.5
            ge = c * cdf
            dgelu = cdf + (c * (1.0 - th * th)) * (0.5 * _GELU_K + (1.5 * _GELU_K * _GELU_C) * c2)
            g_ref[rows, :] = g.astype(BF16)
            ge_ref[rows, :] = ge.astype(BF16)
            ud_ref[rows, :] = (u * dgelu).astype(BF16)
            hh_ref[rows, :] = (ge * u).astype(BF16)

    hspec = pl.BlockSpec((S, D), lambda b, j: (b, 0))
    wspec = pl.BlockSpec((fn, D), lambda b, j: (j, 0))
    ospec = pl.BlockSpec((S, fn), lambda b, j: (b, j))
    outs, landed = _call(
        body, "ffn_up", (B, Fd // fn),
        [hspec, wspec, wspec, pl.BlockSpec((3, fn), lambda b, j: (0, j)), pl.BlockSpec((1, fn), lambda b, j: (0, j))],
        [ospec] * 4, [SDS((T, Fd), BF16)] * 4, (hb, wgT, wuT, cw, cb), carry=carry)
    return (*outs, landed)


def ffn_mid_bwd(dfb, wd, g, ge, ud, cw, B, S, carry=None):
    T, D = dfb.shape
    Fd = wd.shape[0]
    fn = _tile(Fd, 256, 128)

    def body(df_ref, wd_ref, g_ref, ge_ref, ud_ref, cw_ref, dg_ref, du_ref, dcb_ref, dcw_ref):
        dhh = _dot(df_ref[...], wd_ref[...], NT)
        gv = g_ref[...].astype(F32)
        cw = cw_ref[...]
        rows = lax.broadcasted_iota(jnp.int32, gv.shape, 0)
        g1 = _shift_down(gv, 1, rows)
        g2 = _shift_down(gv, 2, rows)
        du_ref[...] = (dhh * ge_ref[...].astype(F32)).astype(BF16)
        dc = dhh * ud_ref[...].astype(F32)
        dcb_ref[0] = jnp.sum(dc, axis=0, keepdims=True)
        dcw_ref[0] = jnp.concatenate(
            [jnp.sum(dc * g2, axis=0, keepdims=True), jnp.sum(dc * g1, axis=0, keepdims=True),
             jnp.sum(dc * gv, axis=0, keepdims=True)], axis=0)
        dg = cw[2:3] * dc + cw[1:2] * _shift_up(dc, 1, rows) + cw[0:1] * _shift_up(dc, 2, rows)
        dg_ref[...] = dg.astype(BF16)

    tspec = pl.BlockSpec((S, fn), lambda b, j: (b, j))
    outs, landed = _call(
        body, "ffn_mid_bwd", (B, Fd // fn),
        [pl.BlockSpec((S, D), lambda b, j: (b, 0)), pl.BlockSpec((fn, D), lambda b, j: (j, 0)), tspec, tspec, tspec,
         pl.BlockSpec((3, fn), lambda b, j: (0, j))],
        [tspec, tspec, pl.BlockSpec((1, 1, fn), lambda b, j: (b, 0, j)), pl.BlockSpec((1, 3, fn), lambda b, j: (b, 0, j))],
        [SDS((T, Fd), BF16), SDS((T, Fd), BF16), SDS((B, 1, Fd), F32), SDS((B, 3, Fd), F32)],
        (dfb, wd, g, ge, ud, cw), carry=carry)
    return (*outs, landed[0] if landed else None)


def ffn_dx_ln(dg, du, wgT, wuT, res, alpha, a, gamma, carry=None):
    T, Fd = dg.shape
    D = wgT.shape[1]
    tm = _tile(T, 256, 16)

    def body(dg_ref, du_ref, wg_ref, wu_ref, r_ref, a_ref, g_ref, da_ref, dab_ref, dgm_ref, dbt_ref):
        @pl.when(pl.program_id(0) == 0)
        def _():
            dgm_ref[...] = jnp.zeros_like(dgm_ref)
            dbt_ref[...] = jnp.zeros_like(dbt_ref)

        dh = alpha * r_ref[...] + _dot(dg_ref[...], wg_ref[...]) + _dot(du_ref[...], wu_ref[...])
        da, sg, sb = _ln_bwd_tile(dh, a_ref[...], g_ref[...])
        da_ref[...] = da
        dab_ref[...] = da.astype(BF16)
        dgm_ref[...] += sg
        dbt_ref[...] += sb

    a_spec = pl.BlockSpec((tm, Fd), lambda i: (i, 0))
    w_spec = pl.BlockSpec((Fd, D), lambda i: (0, 0))
    row = pl.BlockSpec((tm, D), lambda i: (i, 0))
    vec = pl.BlockSpec((1, D), lambda i: (0, 0))
    outs, landed = _call(body, "ffn_dx_ln", (T // tm,), [a_spec, a_spec, w_spec, w_spec, row, row, vec], [row, row, vec, vec],
                         [SDS((T, D), F32), SDS((T, D), BF16), SDS((1, D), F32), SDS((1, D), F32)],
                         (dg, du, wgT, wuT, res, a, gamma.reshape(1, D)), sem=("arbitrary",), carry=carry)
    return (*outs, landed[0] if landed else None)


def _partner_all(x):
    n = x.shape[0]
    r = lax.broadcasted_iota(jnp.int32, x.shape, 0)
    return jnp.where((r % HEAD_DIM) < HEAD_DIM // 2, pltpu.roll(x, n - HEAD_DIM // 2, 0), pltpu.roll(x, HEAD_DIM // 2, 0))


def proj_T(w, xT3, cosT, sinT, blk_off, rope, scale, name, carry=None):
    G, K, T = xT3.shape
    S = cosT.shape[2]
    Dout = K
    tt = _tile(S, 1024, 128)
    H = Dout // HEAD_DIM
    nS = S // tt

    def body(w_ref, x_ref, c_ref, s_ref, o_ref):
        acc = _dot(w_ref[...], x_ref[0])
        if rope:
            cos = jnp.tile(c_ref[0], (H, 1))
            sin = jnp.tile(s_ref[0], (H, 1))
            acc = acc * cos + _partner_all(acc) * sin
        if scale != 1.0:
            acc = acc * scale
        o_ref[0] = acc.astype(BF16)

    tab = pl.BlockSpec((1, HEAD_DIM, tt), lambda g, j: (g, 0, j % nS))
    outs, landed = _call(
        body, name, (G, T // tt),
        [pl.BlockSpec((Dout, K), lambda g, j: (g + blk_off, 0)), pl.BlockSpec((1, K, tt), lambda g, j: (g, 0, j)), tab, tab],
        [pl.BlockSpec((1, Dout, tt), lambda g, j: (g, 0, j))], [SDS((G, Dout, T), BF16)], (w, xT3, cosT, sinT), carry=carry)
    return outs[0], landed


def _attn_bias():
    kj = lax.broadcasted_iota(jnp.int32, (2 * BLK, BLK), 0)
    qi = lax.broadcasted_iota(jnp.int32, (2 * BLK, BLK), 1)
    ok = ((kj >= BLK) & (kj - BLK <= qi)) | ((kj < BLK) & (kj >= qi))
    return jnp.where(ok, 0.0, NEG).astype(F32)


def _has_prev(g, S):
    nb = S // (DILATIONS[g] * BLK)
    return [(n % nb) != 0 for n in range(S // BLK)]


def _win(ref, n, hp):
    lo = (n - 1) * BLK if hp else n * BLK
    return ref[0, :, lo:(n + 1) * BLK]


def attn_fwd(qT3, kT3, vT3, bias, g, B, S):
    _, D, T = qT3.shape
    H = D // HEAD_DIM
    nblk = S // BLK
    hp = _has_prev(g, S)

    def body(q_ref, k_ref, v_ref, b_ref, o_ref, l_ref, s_scr, p_scr, rl_scr):
        for n in range(nblk):
            lo = 0 if hp[n] else BLK
            s_scr[n, lo:, :] = _dot(_win(k_ref, n, hp[n]), q_ref[0, :, n * BLK:(n + 1) * BLK], TN)
        for n in range(nblk):
            lo = 0 if hp[n] else BLK
            sT = s_scr[n, lo:, :] + b_ref[lo:, :]
            m = jnp.max(sT, axis=0, keepdims=True)
            p = jnp.exp(sT - m)
            l = jnp.sum(p, axis=0, keepdims=True)
            p_scr[n, lo:, :] = p.astype(BF16)
            rl_scr[n:n + 1, :] = 1.0 / l
            l_ref[0, :, n * BLK:(n + 1) * BLK] = m + jnp.log(l)
        for n in range(nblk):
            lo = 0 if hp[n] else BLK
            o_ref[:, n * BLK:(n + 1) * BLK] = (_dot(_win(v_ref, n, hp[n]), p_scr[n, lo:, :]) * rl_scr[n:n + 1, :]).astype(BF16)

    spec = pl.BlockSpec((1, HEAD_DIM, S), lambda b, h: (g, h, b))
    return pl.pallas_call(
        body, name=f"attn_fwd_g{g}", grid=(B, H),
        in_specs=[spec, spec, spec, pl.BlockSpec((2 * BLK, BLK), lambda b, h: (0, 0))],
        out_specs=[pl.BlockSpec((HEAD_DIM, S), lambda b, h: (h, b)), pl.BlockSpec((1, 1, S), lambda b, h: (h, 0, b))],
        out_shape=[SDS((D, T), BF16), SDS((H, 1, T), F32)],
        scratch_shapes=[pltpu.VMEM((nblk, 2 * BLK, BLK), F32), pltpu.VMEM((nblk, 2 * BLK, BLK), BF16),
                        pltpu.VMEM((nblk, BLK), F32)],
        compiler_params=_cp(),
    )(qT3, kT3, vT3, bias)


def attn_combine(oTs, lses):
    G = len(oTs)
    D, T = oTs[0].shape
    H = D // HEAD_DIM
    tn = _tile(T, 2048, 128)
    hb = _tile(H, 4, 1)

    def body(*refs):
        o_refs, l_refs = refs[:G], refs[G:2 * G]
        ob_ref, of_ref, lt_ref = refs[2 * G:]
        ls = [r[...] for r in l_refs]
        m = functools.reduce(jnp.maximum, ls)
        es = [jnp.exp(v - m) for v in ls]
        z = functools.reduce(lambda a, b: a + b, es)
        o = (es[0] / z) * o_refs[0][...].astype(F32).reshape(hb, HEAD_DIM, tn)
        for i in range(1, G):
            o = o + (es[i] / z) * o_refs[i][...].astype(F32).reshape(hb, HEAD_DIM, tn)
        o = o.reshape(hb * HEAD_DIM, tn)
        ob_ref[...] = o.astype(BF16)
        of_ref[...] = o
        lt_ref[...] = m + jnp.log(z)

    ospec = pl.BlockSpec((hb * HEAD_DIM, tn), lambda h, j: (h, j))
    lspec = pl.BlockSpec((hb, 1, tn), lambda h, j: (h, 0, j))
    return pl.pallas_call(
        body, name="attn_combine", grid=(H // hb, T // tn),
        in_specs=[ospec] * G + [lspec] * G, out_specs=[ospec, ospec, lspec],
        out_shape=[SDS((D, T), BF16), SDS((D, T), F32), SDS((H, 1, T), F32)], compiler_params=_cp(),
    )(*oTs, *lses)


def attn_delta(doT, oT):
    D, T = doT.shape
    H = D // HEAD_DIM
    tn = _tile(T, 2048, 128)
    hb = _tile(H, 4, 1)

    def body(d_ref, o_ref, r_ref):
        prod = (d_ref[...].astype(F32) * o_ref[...]).reshape(hb, HEAD_DIM, tn)
        r_ref[...] = jnp.sum(prod, axis=1, keepdims=True)

    spec = pl.BlockSpec((hb * HEAD_DIM, tn), lambda h, j: (h, j))
    return pl.pallas_call(
        body, name="attn_delta", grid=(H // hb, T // tn), in_specs=[spec, spec],
        out_specs=pl.BlockSpec((hb, 1, tn), lambda h, j: (h, 0, j)),
        out_shape=SDS((H, 1, T), F32), compiler_params=_cp(),
    )(doT, oT)


def attn_bwd(qT3, kT3, vT3, doT, lse, delta, cosT, sinT, bias, g, q_scale, B, S, dk_prev=None, dv_prev=None):
    _, D, T = qT3.shape
    H = D // HEAD_DIM
    nblk = S // BLK
    half = HEAD_DIM // 2
    hp = _has_prev(g, S)
    acc_in = dk_prev is not None
    kv_dtype = BF16 if acc_in else F32

    def body(*refs):
        q_ref, k_ref, v_ref, do_ref, l_ref, d_ref, c_ref, s_ref, b_ref = refs[:9]
        rest = refs[9:]
        if acc_in:
            dkp_ref, dvp_ref = rest[:2]
            rest = rest[2:]
        dq_ref, dk_ref, dv_ref, s_scr, dp_scr, p_scr, ds_scr = rest
        for n in range(nblk):
            lo = 0 if hp[n] else BLK
            blk = slice(n * BLK, (n + 1) * BLK)
            s_scr[n, lo:, :] = _dot(_win(k_ref, n, hp[n]), q_ref[0, :, blk], TN)
            dp_scr[n, lo:, :] = _dot(_win(v_ref, n, hp[n]), do_ref[:, blk], TN)
        for n in range(nblk):
            lo = 0 if hp[n] else BLK
            blk = slice(n * BLK, (n + 1) * BLK)
            pT = jnp.exp(s_scr[n, lo:, :] + b_ref[lo:, :] - l_ref[0, :, blk])
            p_scr[n, lo:, :] = pT.astype(BF16)
            ds_scr[n, lo:, :] = (pT * (dp_scr[n, lo:, :] - d_ref[0, :, blk])).astype(BF16)
        for j in range(nblk):
            blk = slice(j * BLK, (j + 1) * BLK)
            if j + 1 < nblk and hp[j + 1]:
                two = slice(j * BLK, (j + 2) * BLK)
                pj = jnp.concatenate([p_scr[j, BLK:, :], p_scr[j + 1, :BLK, :]], axis=1)
                dsj = jnp.concatenate([ds_scr[j, BLK:, :], ds_scr[j + 1, :BLK, :]], axis=1)
                dv = _dot(do_ref[:, two], pj, NT)
                dk = _dot(q_ref[0, :, two], dsj, NT)
            else:
                dv = _dot(do_ref[:, blk], p_scr[j, BLK:, :], NT)
                dk = _dot(q_ref[0, :, blk], ds_scr[j, BLK:, :], NT)
            dk = dk * c_ref[0, :, blk] - pltpu.roll(dk, half, 0) * s_ref[0, :, blk]
            if acc_in:
                dk = dk + dkp_ref[:, blk]
                dv = dv + dvp_ref[:, blk]
            dk_ref[:, blk] = dk.astype(kv_dtype)
            dv_ref[:, blk] = dv.astype(kv_dtype)
            lo = 0 if hp[j] else BLK
            dq = _dot(_win(k_ref, j, hp[j]), ds_scr[j, lo:, :])
            dq = dq * c_ref[0, :, blk] - pltpu.roll(dq, half, 0) * s_ref[0, :, blk]
            dq_ref[:, blk] = (dq * q_scale).astype(BF16)

    spec3 = pl.BlockSpec((1, HEAD_DIM, S), lambda b, h: (g, h, b))
    spec = pl.BlockSpec((HEAD_DIM, S), lambda b, h: (h, b))
    sspec = pl.BlockSpec((1, 1, S), lambda b, h: (h, 0, b))
    tab = pl.BlockSpec((1, HEAD_DIM, S), lambda b, h: (g, 0, 0))
    in_specs = [spec3, spec3, spec3, spec, sspec, sspec, tab, tab, pl.BlockSpec((2 * BLK, BLK), lambda b, h: (0, 0))]
    args = [qT3, kT3, vT3, doT, lse, delta, cosT, sinT, bias]
    if acc_in:
        in_specs += [spec, spec]
        args += [dk_prev, dv_prev]
    return pl.pallas_call(
        body, name=f"attn_bwd_g{g}" + ("_acc" if acc_in else ""), grid=(B, H),
        in_specs=in_specs, out_specs=[spec, spec, spec],
        out_shape=[SDS((D, T), BF16), SDS((D, T), kv_dtype), SDS((D, T), kv_dtype)],
        scratch_shapes=[pltpu.VMEM((nblk, 2 * BLK, BLK), F32), pltpu.VMEM((nblk, 2 * BLK, BLK), F32),
                        pltpu.VMEM((nblk, 2 * BLK, BLK), BF16), pltpu.VMEM((nblk, 2 * BLK, BLK), BF16)],
        compiler_params=_cp(),
    )(*args)


def adamw(w, g, m, v, name):
    R, C = w.shape
    tr = _tile(R, 512, 8)

    def body(w_ref, g_ref, m_ref, v_ref, d_ref, nm_ref, nv_ref):
        gv = g_ref[...]
        nm = ADAM_B1 * m_ref[...] + (1.0 - ADAM_B1) * gv
        nv = ADAM_B2 * v_ref[...] + (1.0 - ADAM_B2) * (gv * gv)
        m_hat = nm / (1.0 - ADAM_B1 ** ADAM_STEP)
        v_hat = nv / (1.0 - ADAM_B2 ** ADAM_STEP)
        d_ref[...] = -ADAM_LR * (m_hat / (jnp.sqrt(v_hat) + ADAM_EPS) + ADAM_WD * w_ref[...])
        nm_ref[...] = nm
        nv_ref[...] = nv

    spec = pl.BlockSpec((tr, C), lambda i: (i, 0))
    return pl.pallas_call(
        body, name=name, grid=(R // tr,), in_specs=[spec] * 4, out_specs=[spec] * 3,
        out_shape=[SDS((R, C), F32)] * 3, compiler_params=_cp(),
    )(w, g, m, v)


def _perm(a, B, S, d):
    if d == 1:
        return a
    lead = a.shape[:-1]
    return a.reshape(*lead, B, S // d, d).swapaxes(-1, -2).reshape(*lead, B * S)


def _unperm(a, B, S, d):
    if d == 1:
        return a
    lead = a.shape[:-1]
    return a.reshape(*lead, B, d, S // d).swapaxes(-1, -2).reshape(*lead, B * S)


def _perm3(a, B, S):
    return jnp.stack([_perm(a, B, S, d) for d in DILATIONS])


def _xT3(xb, B, S):
    D = xb.shape[1]
    outs = []
    for d in DILATIONS:
        outs.append(xb.reshape(B, S // d, d, D).transpose(3, 0, 2, 1).reshape(D, B * S))
    return jnp.stack(outs)


def _rope_tables(S):
    half = HEAD_DIM // 2
    inv_freq = ROPE_THETA ** (-jnp.arange(0, HEAD_DIM, 2, dtype=F32) / HEAD_DIM)
    ang = jnp.arange(S, dtype=F32)[:, None] * inv_freq[None, :]
    cos = jnp.concatenate([jnp.cos(ang), jnp.cos(ang)], axis=1).T
    sin = jnp.concatenate([-jnp.sin(ang), jnp.sin(ang)], axis=1).T
    return _perm3(cos, 1, S), _perm3(sin, 1, S)


def kernel(x, pool_w, pool_scale, w_q, w_kv, w_o, ffn_w_gate, ffn_w_up, ffn_conv_w, ffn_conv_b, ffn_w_down, ln1_g, ln1_b, ln2_g, ln2_b, loss_target, m_pool_w, m_pool_scale, m_w_q, m_w_kv, m_w_o, m_ffn_w_gate, m_ffn_w_up, m_ffn_conv_w, m_ffn_conv_b, m_ffn_w_down, m_ln1_g, m_ln1_b, m_ln2_g, m_ln2_b, v_pool_w, v_pool_scale, v_w_q, v_w_kv, v_w_o, v_ffn_w_gate, v_ffn_w_up, v_ffn_conv_w, v_ffn_conv_b, v_ffn_w_down, v_ln1_g, v_ln1_b, v_ln2_g, v_ln2_b):
    B, S, D = x.shape
    T = B * S
    depth = ln1_g.shape[0]
    nA, nB = pool_w.shape[0], w_q.shape[0]
    Fs = ffn_w_down.shape[1]
    Fd = Fs * N_DEV
    H = D // HEAD_DIM
    G = len(DILATIONS)
    PG = len(POOL_WINDOWS)
    Cg = D // PG
    alpha = (2.0 * depth) ** 0.25
    me = 4 * lax.axis_index("x") + 2 * lax.axis_index("y") + lax.axis_index("c")

    qs, kvs, os_ = w_q.shape[2], w_kv.shape[1], w_o.shape[1]
    pool_rows = pool_w.size // D
    local = {("pool",): pool_w.reshape(pool_rows, D).astype(BF16), ("wkv",): w_kv.T.astype(BF16)}
    for j in range(nB):
        local[("wq", j)] = w_q[j].T.astype(BF16)
        local[("wo", j)] = w_o[j].astype(BF16)
    for i in range(depth):
        local[("wg", i)] = ffn_w_gate[i].T.astype(BF16)
        local[("wu", i)] = ffn_w_up[i].T.astype(BF16)
        local[("wd", i)] = ffn_w_down[i].astype(BF16)
    ffn = lambda i: [("wg", i), ("wu", i), ("wd", i)]
    queue = [[("pool",)]]
    if depth == 4 and nA == 2 and nB == 2:
        queue += [[("wg", 0)], [("wu", 0)], [("wd", 0), ("wg", 1), ("wu", 1)], [("wd", 1)],
                  [("wo", 0)], [("wo", 1)], [("wkv",), ("wq", 0)], [("wu", 2)], [("wd", 2)], [("wq", 1)],
                  [("wg", 2)], ffn(3)]
    gathered = {}

    def land(keys, arrs):
        for k, a in zip(keys or (), arrs or ()):
            gathered[k] = a.reshape(-1, D)

    def next_gather():
        if not queue:
            return None, None
        keys = queue.pop(0)
        if not keys:
            return None, None
        return keys, Gather([local[k] for k in keys])

    def weight(key):
        if key not in gathered:
            keys = [key]
            for bi, batch in enumerate(queue):
                if key in batch:
                    keys = queue.pop(bi)
                    break
            blk = all_gather_blocks(jnp.concatenate([local[k] for k in keys], axis=0), "gather_" + "_".join(map(str, key)), in_vmem=False)
            off = 0
            for k in keys:
                r = local[k].shape[0]
                gathered[k] = blk[:, off:off + r].reshape(-1, D)
                off += r
        return gathered[key]

    PW = weight(("pool",)).reshape(N_DEV, nA, PG, Cg // N_DEV, Cg).transpose(1, 2, 0, 3, 4).reshape(nA, PG, Cg, Cg)

    sm_cols = 128
    sm_local = jnp.concatenate([ffn_conv_w.reshape(-1), pool_scale.reshape(-1)])
    sm_rows = -(-sm_local.size // sm_cols)
    sm_rows_p = -(-sm_rows // 8) * 8
    sm_local = jnp.pad(sm_local, (0, sm_rows_p * sm_cols - sm_local.size)).reshape(sm_rows_p, sm_cols)
    sm = all_gather_blocks(sm_local, "gather_small", in_vmem=True).reshape(N_DEV, -1)
    ncw = ffn_conv_w.size
    conv_w_full = sm[:, :ncw].reshape(N_DEV, depth, 3, Fs).transpose(1, 2, 0, 3).reshape(depth, 3, Fd)
    pool_scale_full = sm[:, ncw:ncw + pool_scale.size].reshape(N_DEV, nA, D // N_DEV).transpose(1, 0, 2).reshape(nA, 1, D)

    cosT, sinT = _rope_tables(S)
    bias = _attn_bias()

    xs = x.reshape(T, D)
    saved = []
    cur, curb = xs, None
    kT = vT = x1T3 = None
    for i in range(depth):
        sv = {}
        if i < nA:
            keys, cr = next_gather()
            mix, pooled, got = pool_fwd(cur, PW[i], pool_scale_full[i], B, S, carry=cr)
            land(keys, got)
            sv["pooled"] = pooled
            keys, cr = next_gather()
            a1, h, hb, got = add_ln(cur, mix, ln1_g[i], ln1_b[i], alpha, carry=cr)
            land(keys, got)
        else:
            j = i - nA
            xT3 = x1T3 if j == 0 else _xT3(curb, B, S)
            keys, cr = next_gather()
            qT, got = proj_T(weight(("wq", j)), xT3, cosT, sinT, 0, True, HEAD_DIM ** -0.5, "q_proj", carry=cr)
            land(keys, got)
            oTs, lses = [], []
            for gi, d in enumerate(DILATIONS):
                o_g, lse_g = attn_fwd(qT, kT, vT, bias, gi, B, S)
                oTs.append(_unperm(o_g, B, S, d))
                lses.append(_unperm(lse_g, B, S, d))
            oTb, oTf, lse_tot = attn_combine(oTs, lses)
            a1, h, hb, _ = matmul_ln(oTb, weight(("wo", j)), TN, cur, ln1_g[i], ln1_b[i], alpha, "o_proj_ln")
            sv.update(xT3=xT3, qT=qT, oTb=oTb, oTf=oTf, lse_tot=lse_tot)
        wg_i, wu_i = weight(("wg", i)), weight(("wu", i))
        keys, cr = next_gather()
        g, ge, ud, hh, got = ffn_up(hb, wg_i, wu_i, conv_w_full[i], ffn_conv_b[i].reshape(1, Fd), B, S, carry=cr)
        land(keys, got)
        wd_i = weight(("wd", i))
        keys, cr = next_gather()
        a2, cur, curb, got = matmul_ln(hh, wd_i, NN, h, ln2_g[i], ln2_b[i], alpha, "ffn_down_ln", carry=cr)
        land(keys, got)
        sv.update(a1=a1, hb=hb, g=g, ge=ge, ud=ud, hh=hh, a2=a2)
        saved.append(sv)
        if i == nA - 1:
            x1T3 = _xT3(curb, B, S)
            wkv = weight(("wkv",))
            keys, cr = next_gather()
            kT, got = proj_T(wkv, x1T3, cosT, sinT, 0, True, 1.0, "k_proj", carry=cr)
            land(keys, got)
            keys, cr = next_gather()
            vT, got = proj_T(wkv, x1T3, cosT, sinT, G, False, 1.0, "v_proj", carry=cr)
            land(keys, got)


    small = {k: [None] * depth for k in ("ln1_g", "ln1_b", "ln2_g", "ln2_b", "conv_b", "conv_w")}
    dscale = [None] * nA
    dpw = [None] * nA
    dk_acc, dv_acc = [None] * G, [None] * G

    def blocks(a, rows):
        return a.reshape(N_DEV, rows, D)

    pending, landed = [], {}

    def next_carry():
        if not pending:
            return None, None
        key, parts = pending.pop(0)
        return key, Scatter(parts)

    dcur = sq = ln2_done = None
    for i in reversed(range(depth)):
        sv = saved[i]
        if i == depth - 1:
            db2, db2b, small["ln2_g"][i], small["ln2_b"][i], sq = loss_ln_bwd(cur, loss_target.reshape(T, D), sv["a2"], ln2_g[i])
        elif ln2_done is not None:
            db2, db2b, small["ln2_g"][i], small["ln2_b"][i] = ln2_done
            ln2_done = None
        else:
            db2, db2b, small["ln2_g"][i], small["ln2_b"][i] = ln_bwd(dcur, sv["a2"], ln2_g[i])
        key, cr = next_carry()
        dg_, du_, dcb, dcw, got = ffn_mid_bwd(db2b, gathered[("wd", i)], sv["g"], sv["ge"], sv["ud"], conv_w_full[i], B, S, carry=cr)
        if cr is not None:
            landed[key] = got
        small["conv_b"][i] = jnp.sum(dcb, axis=0)
        small["conv_w"][i] = jnp.sum(dcw, axis=0)
        key, cr = next_carry()
        dwd, got = wgrad_rows(sv["hh"], db2b, "wgrad_down", carry=cr)
        if cr is not None:
            landed[key] = got
        dwg, landed[("down", i)] = wgrad_rows(dg_, sv["hb"], "wgrad_gate", carry=Scatter([blocks(dwd, Fs)]))
        dwu, landed[("gate", i)] = wgrad_rows(du_, sv["hb"], "wgrad_up", carry=Scatter([blocks(dwg, Fs)]))
        da1, da1b, small["ln1_g"][i], small["ln1_b"][i], landed[("up", i)] = ffn_dx_ln(
            dg_, du_, gathered[("wg", i)], gathered[("wu", i)], db2, alpha, sv["a1"], ln1_g[i], carry=Scatter([blocks(dwu, Fs)]))
        if i < nA:
            dcur, dsp, dpwp = pool_bwd(da1, sv["pooled"], PW[i], pool_scale_full[i], alpha, B, S)
            dscale[i] = jnp.sum(dsp, axis=0)
            dpw[i] = jnp.sum(dpwp, axis=0)
        else:
            j = i - nA
            doT = matmul_to_T(gathered[("wo", j)], da1b, "o_proj_bwd")
            dwo = wgrad_mixed(sv["oTb"], da1b, "wgrad_o")
            delta = attn_delta(doT, sv["oTf"])
            dq_tok, dwq = [], []
            for gi, d in enumerate(DILATIONS):
                dq_g, dk_acc[gi], dv_acc[gi] = attn_bwd(
                    sv["qT"], kT, vT, _perm(doT, B, S, d), _perm(sv["lse_tot"], B, S, d), _perm(delta, B, S, d),
                    cosT, sinT, bias, gi, HEAD_DIM ** -0.5, B, S, dk_prev=dk_acc[gi], dv_prev=dv_acc[gi])
                dwq.append(wgrad_T(dq_g, sv["xT3"], gi, "wgrad_q"))
                dq_tok.append(_unperm(dq_g, B, S, d))
            dwq = jnp.concatenate(dwq, axis=0)
            below = (saved[i - 1]["a2"], ln2_g[i - 1]) if i > 0 else None
            if j > 0:
                last = dx_from_T(dq_tok, gathered[("wq", j)], da1, alpha, "q_proj_bwd", 512, ln=below)
            else:
                dcur = dx_from_T(dq_tok, gathered[("wq", j)], da1, alpha, "q_proj_bwd", 512)
                dkv = [a.astype(BF16) for a in dk_acc + dv_acc]
                dwkv = jnp.concatenate([wgrad_T(a, x1T3, gi % G, "wgrad_kv") for gi, a in enumerate(dkv)], axis=0)
                dkv_tok = [_unperm(a, B, S, DILATIONS[gi % G]) for gi, a in enumerate(dkv)]
                last = dx_from_T(dkv_tok, gathered[("wkv",)], dcur, 1.0, "kv_proj_bwd", 256, ln=below)
                pending.append((("kv",), [blocks(dwkv, kvs)]))
            if below is None:
                dcur = last
            else:
                ln2_done = last
            pending.append((("attn", j), [blocks(dwq, qs), blocks(dwo, os_)]))
    grad_x = dcur.reshape(B, S, D)

    dpw_all = jnp.stack(dpw).reshape(nA, PG, N_DEV, Cg // N_DEV, Cg).transpose(2, 0, 1, 3, 4).reshape(N_DEV, pool_rows, D)
    tail_keys = [k for k, _ in pending] + [("pool",)]
    tail_parts = [parts for _, parts in pending] + [[dpw_all.astype(BF16)]]
    tail_rows = [sum(p.shape[1] for p in parts) for parts in tail_parts]
    tail = scatter_partials([p for parts in tail_parts for p in parts], "scatter_tail")
    for t, key in enumerate(tail_keys):
        lo = sum(tail_rows[:t])
        landed[key] = tail[:, lo:lo + tail_rows[t]]

    def reduced(key):
        return sum_slots(landed[key], "sum_" + "_".join(str(k) for k in key))

    g_attn = [reduced(("attn", j)) for j in range(nB)]
    g_w_q = jnp.swapaxes(jnp.stack([a[:qs] for a in g_attn]), 1, 2)
    g_w_o = jnp.stack([a[qs:] for a in g_attn])
    g_w_kv = reduced(("kv",)).T
    g_gate = jnp.swapaxes(jnp.stack([reduced(("gate", i)) for i in range(depth)]), 1, 2)
    g_up = jnp.swapaxes(jnp.stack([reduced(("up", i)) for i in range(depth)]), 1, 2)
    g_down = jnp.stack([reduced(("down", i)) for i in range(depth)])
    g_pool_w = reduced(("pool",)).reshape(pool_w.shape)

    def rows_of(a):
        a = a.reshape(-1)
        n = -(-a.size // D) * D
        return jnp.pad(a, (0, n - a.size)).reshape(-1, D)

    sm_parts = [rows_of(jnp.concatenate(small[k], axis=0)) for k in ("ln1_g", "ln1_b", "ln2_g", "ln2_b")]
    sm_parts += [rows_of(jnp.stack(small["conv_b"])), rows_of(jnp.stack(small["conv_w"])), rows_of(jnp.stack(dscale)), sq]
    sm_sizes = [p.shape[0] for p in sm_parts]
    sm_all = jnp.concatenate(sm_parts, axis=0)
    pad_rows = -(-sm_all.shape[0] // 8) * 8 - sm_all.shape[0]
    sm_all = jnp.pad(sm_all, ((0, pad_rows), (0, 0)))
    sm_sum = sum_slots(all_gather_blocks(sm_all, "gather_small_grads", in_vmem=True), "sum_small_grads")
    sm_offs = [sum(sm_sizes[:i]) for i in range(len(sm_sizes))]

    def sm_take(i, shape):
        n = math.prod(shape)
        return sm_sum[sm_offs[i]:sm_offs[i] + sm_sizes[i]].reshape(-1)[:n].reshape(shape)

    g_ln1_g, g_ln1_b = sm_take(0, (depth, D)), sm_take(1, (depth, D))
    g_ln2_g, g_ln2_b = sm_take(2, (depth, D)), sm_take(3, (depth, D))
    g_conv_b = sm_take(4, (depth, Fd))
    g_conv_w = lax.dynamic_slice_in_dim(sm_take(5, (depth, 3, Fd)), me * Fs, Fs, axis=2)
    g_pool_scale = lax.dynamic_slice_in_dim(sm_take(6, (nA, D)), me * (D // N_DEV), D // N_DEV, axis=1)
    loss = (0.5 / D) * jnp.sum(sm_take(7, (D,)))

    def v2(a):
        return a.reshape(-1, a.shape[-1])

    names = ["pool_w", "pool_scale", "w_q", "w_kv", "w_o", "ffn_w_gate", "ffn_w_up", "ffn_conv_w", "ffn_conv_b",
             "ffn_w_down", "ln1_g", "ln1_b", "ln2_g", "ln2_b"]
    ws = [pool_w, pool_scale, w_q, w_kv, w_o, ffn_w_gate, ffn_w_up, ffn_conv_w, ffn_conv_b, ffn_w_down, ln1_g, ln1_b, ln2_g, ln2_b]
    ms = [m_pool_w, m_pool_scale, m_w_q, m_w_kv, m_w_o, m_ffn_w_gate, m_ffn_w_up, m_ffn_conv_w, m_ffn_conv_b, m_ffn_w_down, m_ln1_g, m_ln1_b, m_ln2_g, m_ln2_b]
    vs = [v_pool_w, v_pool_scale, v_w_q, v_w_kv, v_w_o, v_ffn_w_gate, v_ffn_w_up, v_ffn_conv_w, v_ffn_conv_b, v_ffn_w_down, v_ln1_g, v_ln1_b, v_ln2_g, v_ln2_b]
    gs = [g_pool_w, g_pool_scale, g_w_q, g_w_kv, g_w_o, g_gate, g_up, g_conv_w, g_conv_b, g_down, g_ln1_g, g_ln1_b, g_ln2_g, g_ln2_b]
    deltas, new_ms, new_vs = [], [], []
    for nm, w, gr, m_, v_ in zip(names, ws, gs, ms, vs):
        d_, nm_, nv_ = adamw(v2(w), v2(gr), v2(m_), v2(v_), "adamw_" + nm)
        deltas.append(d_.reshape(w.shape))
        new_ms.append(nm_.reshape(w.shape))
        new_vs.append(nv_.reshape(w.shape))

    return (loss, grad_x, *gs, *deltas, *new_ms, *new_vs)
```

```python
import functools
import math

import jax
import jax.numpy as jnp
from jax import lax
from jax.experimental import pallas as pl
from jax.experimental.pallas import tpu as pltpu

F32 = jnp.float32
BF16 = jnp.bfloat16
SDS = jax.ShapeDtypeStruct
MESH = pl.DeviceIdType.MESH

N_DEV = 8
HEAD_DIM = 64
BLK = 128
DILATIONS = (1, 4, 16)
POOL_WINDOWS = (2, 4, 8, 16)
ROPE_THETA = 10000.0
LN_EPS = 1e-5
NEG = -1e30
V7X_VMEM_LIMIT = 56 * 1024 * 1024

ADAM_LR, ADAM_B1, ADAM_B2, ADAM_EPS, ADAM_WD, ADAM_STEP = 0.001, 0.9, 0.999, 1e-08, 0.01, 10

NN = (((1,), (0,)), ((), ()))
NT = (((1,), (1,)), ((), ()))
TN = (((0,), (0,)), ((), ()))


def _cp(sem=None):
    kw = dict(vmem_limit_bytes=V7X_VMEM_LIMIT)
    if sem is not None:
        kw["dimension_semantics"] = sem
    return pltpu.CompilerParams(**kw)


def _dot(a, b, dims=NN):
    return lax.dot_general(a, b, dims, preferred_element_type=F32)


def _tile(n, target, mult):
    best = None
    for t in range(mult, min(n, target) + 1, mult):
        if n % t == 0:
            best = t
    return best if best is not None else n


def _mesh_pos():
    return lax.axis_index("x"), lax.axis_index("y"), lax.axis_index("c")


def all_gather_blocks(xl, name, in_vmem):
    R, C = xl.shape
    space = pltpu.VMEM if in_vmem else pl.ANY

    def body(x_ref, out_ref, send_sems, recv_sems, local_sem):
        x, y, c = _mesh_pos()
        me, sibling = (x, y, c), (x, y, 1 - c)
        chips = [(1 - x, y), (x, 1 - y), (1 - x, 1 - y)]

        def slot(px, py, pc):
            return out_ref.at[4 * px + 2 * py + pc]

        def copy(k, block, to, src=None):
            return pltpu.make_async_remote_copy(
                src_ref=slot(*block) if src is None else src, dst_ref=slot(*block),
                send_sem=send_sems.at[k], recv_sem=recv_sems.at[k], device_id=to, device_id_type=MESH)

        mine = pltpu.make_async_copy(x_ref, slot(*me), local_sem)
        mine.start()
        first = [copy(0, me, sibling, src=x_ref)]
        first += [copy(1 + j, me, (*chip, c), src=x_ref) for j, chip in enumerate(chips)]
        for cp in first:
            cp.start()
        passed = [copy(4 + j, (*chip, c), sibling) for j, chip in enumerate(chips)]
        for j, chip in enumerate(chips):
            copy(1 + j, (*chip, c), me).wait_recv()
            passed[j].start()
        copy(0, sibling, me).wait_recv()
        for j, chip in enumerate(chips):
            copy(4 + j, (*chip, 1 - c), me).wait_recv()
        for cp in first + passed:
            cp.wait_send()
        mine.wait()

    return pl.pallas_call(
        body, name=name,
        out_shape=SDS((N_DEV, R, C), xl.dtype),
        in_specs=[pl.BlockSpec(memory_space=space)],
        out_specs=pl.BlockSpec(memory_space=space),
        scratch_shapes=[pltpu.SemaphoreType.DMA((7,)), pltpu.SemaphoreType.DMA((7,)), pltpu.SemaphoreType.DMA],
        compiler_params=_cp(),
    )(xl)


def _peers():
    x, y, c = _mesh_pos()
    peers = []
    for r in range(1, N_DEV):
        peers.append((1 - x if (r & 4) else x, 1 - y if (r & 2) else y, 1 - c if (r & 1) else c))
    return 4 * x + 2 * y + c, peers


class Gather:
    def __init__(self, parts):
        self.parts = list(parts)
        n = len(self.parts)
        self.out_shapes = [SDS((N_DEV,) + p.shape, p.dtype) for p in self.parts]
        self.scratch = [pltpu.SemaphoreType.DMA((7 * n,)), pltpu.SemaphoreType.DMA((7 * n,)), pltpu.SemaphoreType.DMA((n,))]

    def start(self, part_refs, out_refs, send_sems, recv_sems, local_sems):
        me_lin, peers = _peers()
        n = len(self.parts)
        for i in range(n):
            pltpu.make_async_copy(part_refs[i], out_refs[i].at[me_lin], local_sems.at[i]).start()
        for k, peer in enumerate(peers):
            for i in range(n):
                pltpu.make_async_remote_copy(
                    src_ref=part_refs[i], dst_ref=out_refs[i].at[me_lin],
                    send_sem=send_sems.at[k * n + i], recv_sem=recv_sems.at[k * n + i],
                    device_id=peer, device_id_type=MESH).start()

    def wait(self, out_refs, send_sems, recv_sems, local_sems):
        me_lin, peers = _peers()
        n = len(self.parts)
        for k, (px, py, pc) in enumerate(peers):
            p_lin = 4 * px + 2 * py + pc
            for i in range(n):
                arrival = pltpu.make_async_remote_copy(
                    src_ref=out_refs[i].at[p_lin], dst_ref=out_refs[i].at[p_lin],
                    send_sem=send_sems.at[k * n + i], recv_sem=recv_sems.at[k * n + i],
                    device_id=(px, py, pc), device_id_type=MESH)
                arrival.wait_recv()
                arrival.wait_send()
        for i in range(n):
            pltpu.make_async_copy(out_refs[i].at[me_lin], out_refs[i].at[me_lin], local_sems.at[i]).wait()


class Scatter:
    def __init__(self, parts):
        self.parts = list(parts)
        self.rows = [p.shape[1] for p in parts]
        self.offs = [sum(self.rows[:i]) for i in range(len(self.rows))]
        self.out_shapes = [SDS((N_DEV, sum(self.rows), parts[0].shape[2]), parts[0].dtype)]
        self.scratch = [pltpu.SemaphoreType.DMA((7,)), pltpu.SemaphoreType.DMA((7,)), pltpu.SemaphoreType.DMA]

    def start(self, part_refs, out_refs, send_sems, recv_sems, local_sem):
        out_ref = out_refs[0]
        me_lin, peers = _peers()
        for i, (off, r) in enumerate(zip(self.offs, self.rows)):
            pltpu.make_async_copy(part_refs[i].at[me_lin], out_ref.at[me_lin, pl.ds(off, r)], local_sem).start()
        for k, (px, py, pc) in enumerate(peers):
            p_lin = 4 * px + 2 * py + pc
            for i, (off, r) in enumerate(zip(self.offs, self.rows)):
                pltpu.make_async_remote_copy(
                    src_ref=part_refs[i].at[p_lin], dst_ref=out_ref.at[me_lin, pl.ds(off, r)],
                    send_sem=send_sems.at[k], recv_sem=recv_sems.at[k],
                    device_id=(px, py, pc), device_id_type=MESH).start()

    def wait(self, out_refs, send_sems, recv_sems, local_sem):
        out_ref = out_refs[0]
        me_lin, peers = _peers()
        for k, (px, py, pc) in enumerate(peers):
            p_lin = 4 * px + 2 * py + pc
            whole = pltpu.make_async_remote_copy(
                src_ref=out_ref.at[p_lin], dst_ref=out_ref.at[p_lin],
                send_sem=send_sems.at[k], recv_sem=recv_sems.at[k],
                device_id=(px, py, pc), device_id_type=MESH)
            whole.wait_recv()
            whole.wait_send()
        pltpu.make_async_copy(out_ref.at[me_lin], out_ref.at[me_lin], local_sem).wait()


def scatter_partials(parts, name):
    sc = Scatter(parts)
    n = len(parts)

    def body(*refs):
        sc.start(refs[:n], refs[n:n + 1], *refs[n + 1:])
        sc.wait(refs[n:n + 1], *refs[n + 1:])

    return pl.pallas_call(
        body, name=name, out_shape=sc.out_shapes[0],
        in_specs=[pl.BlockSpec(memory_space=pl.ANY)] * n, out_specs=pl.BlockSpec(memory_space=pl.ANY),
        scratch_shapes=sc.scratch, compiler_params=_cp(),
    )(*parts)


def _call(body, name, grid, in_specs, out_specs, out_shape, args, scratch=(), sem=None, carry=None):
    in_specs, out_specs, out_shape, scratch = list(in_specs), list(out_specs), list(out_shape), list(scratch)
    if carry is None:
        outs = pl.pallas_call(body, name=name, grid=grid, in_specs=in_specs, out_specs=out_specs, out_shape=out_shape,
                              scratch_shapes=scratch, compiler_params=_cp(sem))(*args)
        return list(outs), None
    n_in, n_out, n_scr, n_c, n_co = len(in_specs), len(out_specs), len(scratch), len(carry.parts), len(carry.out_shapes)
    last = [g - 1 for g in grid]

    def carried(*refs):
        ins, c_ins = refs[:n_in], refs[n_in:n_in + n_c]
        o0 = n_in + n_c
        outs, c_out = refs[o0:o0 + n_out], refs[o0 + n_out:o0 + n_out + n_co]
        s0 = o0 + n_out + n_co
        scr, c_scr = refs[s0:s0 + n_scr], refs[s0 + n_scr:]
        ids = [pl.program_id(a) for a in range(len(grid))]
        is_first = functools.reduce(jnp.logical_and, [i == 0 for i in ids])
        is_last = functools.reduce(jnp.logical_and, [i == l for i, l in zip(ids, last)])

        @pl.when(is_first)
        def _():
            carry.start(c_ins, c_out, *c_scr)

        body(*ins, *outs, *scr)

        @pl.when(is_last)
        def _():
            carry.wait(c_out, *c_scr)

    hbm = pl.BlockSpec(memory_space=pl.ANY)
    outs = pl.pallas_call(
        carried, name=name + "_carry", grid=grid, in_specs=in_specs + [hbm] * n_c, out_specs=out_specs + [hbm] * n_co,
        out_shape=out_shape + carry.out_shapes, scratch_shapes=scratch + carry.scratch,
        compiler_params=_cp(sem if sem is not None else ("arbitrary",) * len(grid)),
    )(*args, *carry.parts)
    return list(outs[:n_out]), list(outs[n_out:])


def sum_slots(slots, name, out_dtype=F32):
    _, R, C = slots.shape
    tr = _tile(R, 512, 16)

    def body(s_ref, o_ref):
        acc = s_ref[0].astype(F32)
        for s in range(1, N_DEV):
            acc = acc + s_ref[s].astype(F32)
        o_ref[...] = acc.astype(out_dtype)

    return pl.pallas_call(
        body, name=name, grid=(R // tr,),
        in_specs=[pl.BlockSpec((N_DEV, tr, C), lambda i: (0, i, 0))],
        out_specs=pl.BlockSpec((tr, C), lambda i: (i, 0)),
        out_shape=SDS((R, C), out_dtype), compiler_params=_cp(),
    )(slots)


def add_ln(x, mix, g, b, alpha, carry=None):
    T, D = x.shape
    tm = _tile(T, 512, 16)

    def body(x_ref, m_ref, g_ref, b_ref, a_ref, y_ref, yb_ref):
        a = alpha * x_ref[...] + m_ref[...]
        mu = jnp.mean(a, axis=-1, keepdims=True)
        xc = a - mu
        var = jnp.mean(xc * xc, axis=-1, keepdims=True)
        y = xc * lax.rsqrt(var + LN_EPS) * g_ref[...] + b_ref[...]
        a_ref[...] = a
        y_ref[...] = y
        yb_ref[...] = y.astype(BF16)

    row = pl.BlockSpec((tm, D), lambda i: (i, 0))
    vec = pl.BlockSpec((1, D), lambda i: (0, 0))
    outs, landed = _call(body, "add_ln", (T // tm,), [row, row, vec, vec], [row, row, row],
                         [SDS((T, D), F32), SDS((T, D), F32), SDS((T, D), BF16)],
                         (x, mix, g.reshape(1, D), b.reshape(1, D)), carry=carry)
    return (*outs, landed)


def _ln_bwd_tile(dy, a, gamma):
    mu = jnp.mean(a, axis=-1, keepdims=True)
    xc = a - mu
    var = jnp.mean(xc * xc, axis=-1, keepdims=True)
    r = lax.rsqrt(var + LN_EPS)
    xh = xc * r
    dxh = dy * gamma
    m1 = jnp.mean(dxh, axis=-1, keepdims=True)
    m2 = jnp.mean(dxh * xh, axis=-1, keepdims=True)
    da = r * (dxh - m1 - xh * m2)
    return da, jnp.sum(dy * xh, axis=0, keepdims=True), jnp.sum(dy, axis=0, keepdims=True)


def ln_bwd(dy, a, g):
    T, D = a.shape
    tm = _tile(T, 512, 16)

    def body(dy_ref, a_ref, g_ref, da_ref, dab_ref, dg_ref, db_ref):
        @pl.when(pl.program_id(0) == 0)
        def _():
            dg_ref[...] = jnp.zeros_like(dg_ref)
            db_ref[...] = jnp.zeros_like(db_ref)

        da, sg, sb = _ln_bwd_tile(dy_ref[...], a_ref[...], g_ref[...])
        da_ref[...] = da
        dab_ref[...] = da.astype(BF16)
        dg_ref[...] += sg
        db_ref[...] += sb

    row = pl.BlockSpec((tm, D), lambda i: (i, 0))
    vec = pl.BlockSpec((1, D), lambda i: (0, 0))
    return pl.pallas_call(
        body, name="ln_bwd", grid=(T // tm,),
        in_specs=[row, row, vec], out_specs=[row, row, vec, vec],
        out_shape=[SDS((T, D), F32), SDS((T, D), BF16), SDS((1, D), F32), SDS((1, D), F32)],
        compiler_params=_cp(("arbitrary",)),
    )(dy, a, g.reshape(1, D))


def loss_ln_bwd(y, tgt, a, g):
    T, D = y.shape
    tm = _tile(T, 512, 16)

    def body(y_ref, t_ref, a_ref, g_ref, da_ref, dab_ref, dg_ref, db_ref, sq_ref):
        @pl.when(pl.program_id(0) == 0)
        def _():
            dg_ref[...] = jnp.zeros_like(dg_ref)
            db_ref[...] = jnp.zeros_like(db_ref)
            sq_ref[...] = jnp.zeros_like(sq_ref)

        e = y_ref[...] - t_ref[...]
        da, sg, sb = _ln_bwd_tile(e / float(D), a_ref[...], g_ref[...])
        da_ref[...] = da
        dab_ref[...] = da.astype(BF16)
        dg_ref[...] += sg
        db_ref[...] += sb
        sq_ref[...] += jnp.sum(e * e, axis=0, keepdims=True)

    row = pl.BlockSpec((tm, D), lambda i: (i, 0))
    vec = pl.BlockSpec((1, D), lambda i: (0, 0))
    return pl.pallas_call(
        body, name="loss_ln_bwd", grid=(T // tm,),
        in_specs=[row, row, row, vec], out_specs=[row, row, vec, vec, vec],
        out_shape=[SDS((T, D), F32), SDS((T, D), BF16), SDS((1, D), F32), SDS((1, D), F32), SDS((1, D), F32)],
        compiler_params=_cp(("arbitrary",)),
    )(y, tgt, a, g.reshape(1, D))


def matmul_ln(a, w, dims, res, g, b, alpha, name, carry=None):
    if dims == TN:
        K, T = a.shape
    else:
        T, K = a.shape
    D = w.shape[1]
    tm = _tile(T, 512, 128 if dims == TN else 16)

    def body(a_ref, w_ref, r_ref, g_ref, b_ref, p_ref, y_ref, yb_ref):
        pre = alpha * r_ref[...] + _dot(a_ref[...], w_ref[...], dims)
        mu = jnp.mean(pre, axis=-1, keepdims=True)
        xc = pre - mu
        var = jnp.mean(xc * xc, axis=-1, keepdims=True)
        y = xc * lax.rsqrt(var + LN_EPS) * g_ref[...] + b_ref[...]
        p_ref[...] = pre
        y_ref[...] = y
        yb_ref[...] = y.astype(BF16)

    a_spec = pl.BlockSpec((K, tm), lambda i: (0, i)) if dims == TN else pl.BlockSpec((tm, K), lambda i: (i, 0))
    row = pl.BlockSpec((tm, D), lambda i: (i, 0))
    vec = pl.BlockSpec((1, D), lambda i: (0, 0))
    outs, landed = _call(
        body, name, (T // tm,), [a_spec, pl.BlockSpec(w.shape, lambda i: (0, 0)), row, vec, vec], [row, row, row],
        [SDS((T, D), F32), SDS((T, D), F32), SDS((T, D), BF16)], (a, w, res, g.reshape(1, D), b.reshape(1, D)), carry=carry)
    return (*outs, landed)


def dx_from_T(aTs, w, res, alpha, name, tm_target, ln=None):
    T = aTs[0].shape[1]
    N = w.shape[1]
    ks = [a.shape[0] for a in aTs]
    n = len(aTs)
    tm = _tile(T, tm_target, 128)

    def body(*refs):
        a_refs, w_ref, r_ref = refs[:n], refs[n], refs[n + 1]
        acc = alpha * r_ref[...]
        off = 0
        for a_ref, k in zip(a_refs, ks):
            acc = acc + _dot(a_ref[...], w_ref[off:off + k, :], TN)
            off += k
        if ln is None:
            refs[n + 2][...] = acc
            return
        ln_a, ln_g, da_ref, dab_ref, dgm_ref, dbt_ref = refs[n + 2:]

        @pl.when(pl.program_id(0) == 0)
        def _():
            dgm_ref[...] = jnp.zeros_like(dgm_ref)
            dbt_ref[...] = jnp.zeros_like(dbt_ref)

        da, sg, sb = _ln_bwd_tile(acc, ln_a[...], ln_g[...])
        da_ref[...] = da
        dab_ref[...] = da.astype(BF16)
        dgm_ref[...] += sg
        dbt_ref[...] += sb

    row = pl.BlockSpec((tm, N), lambda i: (i, 0))
    vec = pl.BlockSpec((1, N), lambda i: (0, 0))
    in_specs = [pl.BlockSpec((k, tm), lambda i: (0, i)) for k in ks] + [pl.BlockSpec(w.shape, lambda i: (0, 0)), row]
    args = list(aTs) + [w, res]
    if ln is None:
        return pl.pallas_call(body, name=name, grid=(T // tm,), in_specs=in_specs, out_specs=row,
                              out_shape=SDS((T, N), F32), compiler_params=_cp())(*args)
    return pl.pallas_call(
        body, name=name + "_ln", grid=(T // tm,), in_specs=in_specs + [row, vec], out_specs=[row, row, vec, vec],
        out_shape=[SDS((T, N), F32), SDS((T, N), BF16), SDS((1, N), F32), SDS((1, N), F32)],
        compiler_params=_cp(("arbitrary",)),
    )(*args, ln[0], ln[1].reshape(1, N))


def matmul_to_T(w, a, name):
    M, K = w.shape
    T = a.shape[0]
    tt = _tile(T, 512, 128)

    def body(w_ref, a_ref, o_ref):
        o_ref[...] = _dot(w_ref[...], a_ref[...], NT).astype(BF16)

    return pl.pallas_call(
        body, name=name, grid=(T // tt,),
        in_specs=[pl.BlockSpec((M, K), lambda i: (0, 0)), pl.BlockSpec((tt, K), lambda i: (i, 0))],
        out_specs=pl.BlockSpec((M, tt), lambda i: (0, i)),
        out_shape=SDS((M, T), BF16), compiler_params=_cp(),
    )(w, a)


def wgrad_rows(a, b, name, carry=None):
    T, M = a.shape
    N = b.shape[1]
    tt = _tile(T, 1024, 16)
    tmm = _tile(M, 1536, 128)
    nt = T // tt

    def body(a_ref, b_ref, o_ref, acc_ref):
        t = pl.program_id(1)

        @pl.when(t == 0)
        def _():
            acc_ref[...] = jnp.zeros_like(acc_ref)

        acc_ref[...] += _dot(a_ref[...], b_ref[...], TN)

        @pl.when(t == nt - 1)
        def _():
            o_ref[...] = acc_ref[...].astype(BF16)

    outs, landed = _call(
        body, name, (M // tmm, nt),
        [pl.BlockSpec((tt, tmm), lambda i, t: (t, i)), pl.BlockSpec((tt, N), lambda i, t: (t, 0))],
        [pl.BlockSpec((tmm, N), lambda i, t: (i, 0))], [SDS((M, N), BF16)], (a, b),
        scratch=[pltpu.VMEM((tmm, N), F32)], sem=("arbitrary", "arbitrary"), carry=carry)
    return outs[0], (landed[0] if landed else None)


def wgrad_T(aT, bT3, g, name):
    M, T = aT.shape
    N = bT3.shape[1]
    tt = _tile(T, 2048, 128)
    nt = T // tt

    def body(a_ref, b_ref, o_ref, acc_ref):
        t = pl.program_id(0)

        @pl.when(t == 0)
        def _():
            acc_ref[...] = jnp.zeros_like(acc_ref)

        acc_ref[...] += _dot(a_ref[...], b_ref[0], NT)

        @pl.when(t == nt - 1)
        def _():
            o_ref[...] = acc_ref[...].astype(BF16)

    return pl.pallas_call(
        body, name=name, grid=(nt,),
        in_specs=[pl.BlockSpec((M, tt), lambda t: (0, t)), pl.BlockSpec((1, N, tt), lambda t: (g, 0, t))],
        out_specs=pl.BlockSpec((M, N), lambda t: (0, 0)),
        out_shape=SDS((M, N), BF16), scratch_shapes=[pltpu.VMEM((M, N), F32)],
        compiler_params=_cp(("arbitrary",)),
    )(aT, bT3)


def wgrad_mixed(aT, b, name):
    M, T = aT.shape
    N = b.shape[1]
    tt = _tile(T, 2048, 128)
    nt = T // tt

    def body(a_ref, b_ref, o_ref, acc_ref):
        t = pl.program_id(0)

        @pl.when(t == 0)
        def _():
            acc_ref[...] = jnp.zeros_like(acc_ref)

        acc_ref[...] += _dot(a_ref[...], b_ref[...], NN)

        @pl.when(t == nt - 1)
        def _():
            o_ref[...] = acc_ref[...].astype(BF16)

    return pl.pallas_call(
        body, name=name, grid=(nt,),
        in_specs=[pl.BlockSpec((M, tt), lambda t: (0, t)), pl.BlockSpec((tt, N), lambda t: (t, 0))],
        out_specs=pl.BlockSpec((M, N), lambda t: (0, 0)),
        out_shape=SDS((M, N), BF16), scratch_shapes=[pltpu.VMEM((M, N), F32)],
        compiler_params=_cp(("arbitrary",)),
    )(aT, b)


def _shift_down(x, k, rows):
    return jnp.where(rows >= k, pltpu.roll(x, k, 0), 0.0)


def _shift_up(x, k, rows):
    n = x.shape[0]
    return jnp.where(rows < n - k, pltpu.roll(x, n - k, 0), 0.0)


def _pick(g, vals):
    out = vals[-1]
    for k in range(len(vals) - 2, -1, -1):
        out = jnp.where(g == k, vals[k], out)
    return out


def pool_fwd(x, pw, scale, B, S, carry=None):
    T, D = x.shape
    G = len(POOL_WINDOWS)
    Cg = D // G

    def body(x_ref, w_ref, s_ref, mix_ref, pooled_ref):
        g = pl.program_id(1)
        xv = x_ref[...]
        rows = lax.broadcasted_iota(jnp.int32, xv.shape, 0)
        sums, cur, k = [], xv, 1
        for _ in POOL_WINDOWS:
            cur = cur + _shift_down(cur, k, rows)
            sums.append(cur)
            k *= 2
        win = 2 * lax.shift_left(jnp.int32(1), g)
        total = _pick(g, sums)
        count = jnp.minimum(rows + 1, win).astype(F32)
        pooled = total / count - xv
        pb = pooled.astype(BF16)
        pooled_ref[...] = pb
        mix_ref[...] = _dot(pb, w_ref[0]) * s_ref[...]

    blk = pl.BlockSpec((S, Cg), lambda b, g: (b, g))
    outs, landed = _call(
        body, "pool_fwd", (B, G),
        [blk, pl.BlockSpec((1, Cg, Cg), lambda b, g: (g, 0, 0)), pl.BlockSpec((1, Cg), lambda b, g: (0, g))],
        [blk, blk], [SDS((T, D), F32), SDS((T, D), BF16)], (x, pw, scale), carry=carry)
    return (*outs, landed)


def pool_bwd(dmix, pooled, pw, scale, alpha, B, S):
    T, D = dmix.shape
    G = len(POOL_WINDOWS)
    Cg = D // G

    def body(d_ref, p_ref, w_ref, s_ref, dx_ref, ds_ref, dw_ref):
        g = pl.program_id(1)
        dm = d_ref[...]
        pb = p_ref[...]
        w = w_ref[0]
        ypre = _dot(pb, w)
        ds_ref[0] = jnp.sum(dm * ypre, axis=0, keepdims=True)
        dy = (dm * s_ref[...]).astype(BF16)
        dpool = _dot(dy, w, NT)
        dw_ref[0, 0] = _dot(pb, dy, TN)
        rows = lax.broadcasted_iota(jnp.int32, dm.shape, 0)
        win = 2 * lax.shift_left(jnp.int32(1), g)
        count = jnp.minimum(rows + 1, win).astype(F32)
        cur, k, sums = dpool / count, 1, []
        for _ in POOL_WINDOWS:
            cur = cur + _shift_up(cur, k, rows)
            sums.append(cur)
            k *= 2
        dx_ref[...] = alpha * dm + _pick(g, sums) - dpool

    blk = pl.BlockSpec((S, Cg), lambda b, g: (b, g))
    return pl.pallas_call(
        body, name="pool_bwd", grid=(B, G),
        in_specs=[blk, blk, pl.BlockSpec((1, Cg, Cg), lambda b, g: (g, 0, 0)), pl.BlockSpec((1, Cg), lambda b, g: (0, g))],
        out_specs=[blk, pl.BlockSpec((1, 1, Cg), lambda b, g: (b, 0, g)),
                   pl.BlockSpec((1, 1, Cg, Cg), lambda b, g: (b, g, 0, 0))],
        out_shape=[SDS((T, D), F32), SDS((B, 1, D), F32), SDS((B, G, Cg, Cg), F32)], compiler_params=_cp(),
    )(dmix, pooled, pw, scale)


_GELU_K = math.sqrt(2.0 / math.pi)
_GELU_C = 0.044715
FFN_ROWS = 512
FFN_HALO = 16


def ffn_up(hb, wgT, wuT, cw, cb, B, S, carry=None):
    T, D = hb.shape
    Fd = wgT.shape[0]
    fn = _tile(Fd, 256, 128)

    nc = S // _tile(S, FFN_ROWS, FFN_HALO)
    rc = S // nc

    def body(h_ref, wg_ref, wu_ref, cw_ref, cb_ref, g_ref, ge_ref, ud_ref, hh_ref):
        wg, wu, cw, cb = wg_ref[...], wu_ref[...], cw_ref[...], cb_ref[...]
        halo = jnp.zeros((FFN_HALO, fn), F32)
        for ci in range(nc):
            rows = slice(ci * rc, (ci + 1) * rc)
            h = h_ref[rows, :]
            g = _dot(h, wg, NT)
            u = _dot(h, wu, NT)
            gext = jnp.concatenate([halo, g], axis=0)
            halo = g[rc - FFN_HALO:, :]
            c = cb + cw[0:1] * pltpu.roll(gext, 2, 0)[FFN_HALO:, :] + cw[1:2] * pltpu.roll(gext, 1, 0)[FFN_HALO:, :] + cw[2:3] * g
            c2 = c * c
            th = jnp.tanh(c * (_GELU_K + (_GELU_K * _GELU_C) * c2))
            cdf = 0.5 * th + 0.5
            ge = c * cdf
            dgelu = cdf + (c * (1.0 - th * th)) * (0.5 * _GELU_K + (1.5 * _GELU_K * _GELU_C) * c2)
            g_ref[rows, :] = g.astype(BF16)
            ge_ref[rows, :] = ge.astype(BF16)
            ud_ref[rows, :] = (u * dgelu).astype(BF16)
            hh_ref[rows, :] = (ge * u).astype(BF16)

    hspec = pl.BlockSpec((S, D), lambda b, j: (b, 0))
    wspec = pl.BlockSpec((fn, D), lambda b, j: (j, 0))
    ospec = pl.BlockSpec((S, fn), lambda b, j: (b, j))
    outs, landed = _call(
        body, "ffn_up", (B, Fd // fn),
        [hspec, wspec, wspec, pl.BlockSpec((3, fn), lambda b, j: (0, j)), pl.BlockSpec((1, fn), lambda b, j: (0, j))],
        [ospec] * 4, [SDS((T, Fd), BF16)] * 4, (hb, wgT, wuT, cw, cb), carry=carry)
    return (*outs, landed)


def ffn_mid_bwd(dfb, wd, g, ge, ud, cw, B, S, carry=None):
    T, D = dfb.shape
    Fd = wd.shape[0]
    fn = _tile(Fd, 256, 128)

    def body(df_ref, wd_ref, g_ref, ge_ref, ud_ref, cw_ref, dg_ref, du_ref, dcb_ref, dcw_ref):
        dhh = _dot(df_ref[...], wd_ref[...], NT)
        gv = g_ref[...].astype(F32)
        cw = cw_ref[...]
        rows = lax.broadcasted_iota(jnp.int32, gv.shape, 0)
        g1 = _shift_down(gv, 1, rows)
        g2 = _shift_down(gv, 2, rows)
        du_ref[...] = (dhh * ge_ref[...].astype(F32)).astype(BF16)
        dc = dhh * ud_ref[...].astype(F32)
        dcb_ref[0] = jnp.sum(dc, axis=0, keepdims=True)
        dcw_ref[0] = jnp.concatenate(
            [jnp.sum(dc * g2, axis=0, keepdims=True), jnp.sum(dc * g1, axis=0, keepdims=True),
             jnp.sum(dc * gv, axis=0, keepdims=True)], axis=0)
        dg = cw[2:3] * dc + cw[1:2] * _shift_up(dc, 1, rows) + cw[0:1] * _shift_up(dc, 2, rows)
        dg_ref[...] = dg.astype(BF16)

    tspec = pl.BlockSpec((S, fn), lambda b, j: (b, j))
    outs, landed = _call(
        body, "ffn_mid_bwd", (B, Fd // fn),
        [pl.BlockSpec((S, D), lambda b, j: (b, 0)), pl.BlockSpec((fn, D), lambda b, j: (j, 0)), tspec, tspec, tspec,
         pl.BlockSpec((3, fn), lambda b, j: (0, j))],
        [tspec, tspec, pl.BlockSpec((1, 1, fn), lambda b, j: (b, 0, j)), pl.BlockSpec((1, 3, fn), lambda b, j: (b, 0, j))],
        [SDS((T, Fd), BF16), SDS((T, Fd), BF16), SDS((B, 1, Fd), F32), SDS((B, 3, Fd), F32)],
        (dfb, wd, g, ge, ud, cw), carry=carry)
    return (*outs, landed[0] if landed else None)


def ffn_dx_ln(dg, du, wgT, wuT, res, alpha, a, gamma, carry=None):
    T, Fd = dg.shape
    D = wgT.shape[1]
    tm = _tile(T, 256, 16)

    def body(dg_ref, du_ref, wg_ref, wu_ref, r_ref, a_ref, g_ref, da_ref, dab_ref, dgm_ref, dbt_ref):
        @pl.when(pl.program_id(0) == 0)
        def _():
            dgm_ref[...] = jnp.zeros_like(dgm_ref)
            dbt_ref[...] = jnp.zeros_like(dbt_ref)

        dh = alpha * r_ref[...] + _dot(dg_ref[...], wg_ref[...]) + _dot(du_ref[...], wu_ref[...])
        da, sg, sb = _ln_bwd_tile(dh, a_ref[...], g_ref[...])
        da_ref[...] = da
        dab_ref[...] = da.astype(BF16)
        dgm_ref[...] += sg
        dbt_ref[...] += sb

    a_spec = pl.BlockSpec((tm, Fd), lambda i: (i, 0))
    w_spec = pl.BlockSpec((Fd, D), lambda i: (0, 0))
    row = pl.BlockSpec((tm, D), lambda i: (i, 0))
    vec = pl.BlockSpec((1, D), lambda i: (0, 0))
    outs, landed = _call(body, "ffn_dx_ln", (T // tm,), [a_spec, a_spec, w_spec, w_spec, row, row, vec], [row, row, vec, vec],
                         [SDS((T, D), F32), SDS((T, D), BF16), SDS((1, D), F32), SDS((1, D), F32)],
                         (dg, du, wgT, wuT, res, a, gamma.reshape(1, D)), sem=("arbitrary",), carry=carry)
    return (*outs, landed[0] if landed else None)


def _partner_all(x):
    n = x.shape[0]
    r = lax.broadcasted_iota(jnp.int32, x.shape, 0)
    return jnp.where((r % HEAD_DIM) < HEAD_DIM // 2, pltpu.roll(x, n - HEAD_DIM // 2, 0), pltpu.roll(x, HEAD_DIM // 2, 0))


def proj_T(w, xT3, cosT, sinT, blk_off, rope, scale, name, carry=None):
    G, K, T = xT3.shape
    S = cosT.shape[2]
    Dout = K
    tt = _tile(S, 1024, 128)
    H = Dout // HEAD_DIM
    nS = S // tt

    def body(w_ref, x_ref, c_ref, s_ref, o_ref):
        acc = _dot(w_ref[...], x_ref[0])
        if rope:
            cos = jnp.tile(c_ref[0], (H, 1))
            sin = jnp.tile(s_ref[0], (H, 1))
            acc = acc * cos + _partner_all(acc) * sin
        if scale != 1.0:
            acc = acc * scale
        o_ref[0] = acc.astype(BF16)

    tab = pl.BlockSpec((1, HEAD_DIM, tt), lambda g, j: (g, 0, j % nS))
    outs, landed = _call(
        body, name, (G, T // tt),
        [pl.BlockSpec((Dout, K), lambda g, j: (g + blk_off, 0)), pl.BlockSpec((1, K, tt), lambda g, j: (g, 0, j)), tab, tab],
        [pl.BlockSpec((1, Dout, tt), lambda g, j: (g, 0, j))], [SDS((G, Dout, T), BF16)], (w, xT3, cosT, sinT), carry=carry)
    return outs[0], landed


def _attn_bias():
    kj = lax.broadcasted_iota(jnp.int32, (2 * BLK, BLK), 0)
    qi = lax.broadcasted_iota(jnp.int32, (2 * BLK, BLK), 1)
    ok = ((kj >= BLK) & (kj - BLK <= qi)) | ((kj < BLK) & (kj >= qi))
    return jnp.where(ok, 0.0, NEG).astype(F32)


def _has_prev(g, S):
    nb = S // (DILATIONS[g] * BLK)
    return [(n % nb) != 0 for n in range(S // BLK)]


def _win(ref, n, hp):
    lo = (n - 1) * BLK if hp else n * BLK
    return ref[0, :, lo:(n + 1) * BLK]


def attn_fwd(qT3, kT3, vT3, bias, g, B, S):
    _, D, T = qT3.shape
    H = D // HEAD_DIM
    nblk = S // BLK
    hp = _has_prev(g, S)

    def body(q_ref, k_ref, v_ref, b_ref, o_ref, l_ref, s_scr, p_scr, rl_scr):
        for n in range(nblk):
            lo = 0 if hp[n] else BLK
            s_scr[n, lo:, :] = _dot(_win(k_ref, n, hp[n]), q_ref[0, :, n * BLK:(n + 1) * BLK], TN)
        for n in range(nblk):
            lo = 0 if hp[n] else BLK
            sT = s_scr[n, lo:, :] + b_ref[lo:, :]
            m = jnp.max(sT, axis=0, keepdims=True)
            p = jnp.exp(sT - m)
            l = jnp.sum(p, axis=0, keepdims=True)
            p_scr[n, lo:, :] = p.astype(BF16)
            rl_scr[n:n + 1, :] = 1.0 / l
            l_ref[0, :, n * BLK:(n + 1) * BLK] = m + jnp.log(l)
        for n in range(nblk):
            lo = 0 if hp[n] else BLK
            o_ref[:, n * BLK:(n + 1) * BLK] = (_dot(_win(v_ref, n, hp[n]), p_scr[n, lo:, :]) * rl_scr[n:n + 1, :]).astype(BF16)

    spec = pl.BlockSpec((1, HEAD_DIM, S), lambda b, h: (g, h, b))
    return pl.pallas_call(
        body, name=f"attn_fwd_g{g}", grid=(B, H),
        in_specs=[spec, spec, spec, pl.BlockSpec((2 * BLK, BLK), lambda b, h: (0, 0))],
        out_specs=[pl.BlockSpec((HEAD_DIM, S), lambda b, h: (h, b)), pl.BlockSpec((1, 1, S), lambda b, h: (h, 0, b))],
        out_shape=[SDS((D, T), BF16), SDS((H, 1, T), F32)],
        scratch_shapes=[pltpu.VMEM((nblk, 2 * BLK, BLK), F32), pltpu.VMEM((nblk, 2 * BLK, BLK), BF16),
                        pltpu.VMEM((nblk, BLK), F32)],
        compiler_params=_cp(),
    )(qT3, kT3, vT3, bias)


def attn_combine(oTs, lses):
    G = len(oTs)
    D, T = oTs[0].shape
    H = D // HEAD_DIM
    tn = _tile(T, 2048, 128)
    hb = _tile(H, 4, 1)

    def body(*refs):
        o_refs, l_refs = refs[:G], refs[G:2 * G]
        ob_ref, of_ref, lt_ref = refs[2 * G:]
        ls = [r[...] for r in l_refs]
        m = functools.reduce(jnp.maximum, ls)
        es = [jnp.exp(v - m) for v in ls]
        z = functools.reduce(lambda a, b: a + b, es)
        o = (es[0] / z) * o_refs[0][...].astype(F32).reshape(hb, HEAD_DIM, tn)
        for i in range(1, G):
            o = o + (es[i] / z) * o_refs[i][...].astype(F32).reshape(hb, HEAD_DIM, tn)
        o = o.reshape(hb * HEAD_DIM, tn)
        ob_ref[...] = o.astype(BF16)
        of_ref[...] = o
        lt_ref[...] = m + jnp.log(z)

    ospec = pl.BlockSpec((hb * HEAD_DIM, tn), lambda h, j: (h, j))
    lspec = pl.BlockSpec((hb, 1, tn), lambda h, j: (h, 0, j))
    return pl.pallas_call(
        body, name="attn_combine", grid=(H // hb, T // tn),
        in_specs=[ospec] * G + [lspec] * G, out_specs=[ospec, ospec, lspec],
        out_shape=[SDS((D, T), BF16), SDS((D, T), F32), SDS((H, 1, T), F32)], compiler_params=_cp(),
    )(*oTs, *lses)


def attn_delta(doT, oT):
    D, T = doT.shape
    H = D // HEAD_DIM
    tn = _tile(T, 2048, 128)
    hb = _tile(H, 4, 1)

    def body(d_ref, o_ref, r_ref):
        prod = (d_ref[...].astype(F32) * o_ref[...]).reshape(hb, HEAD_DIM, tn)
        r_ref[...] = jnp.sum(prod, axis=1, keepdims=True)

    spec = pl.BlockSpec((hb * HEAD_DIM, tn), lambda h, j: (h, j))
    return pl.pallas_call(
        body, name="attn_delta", grid=(H // hb, T // tn), in_specs=[spec, spec],
        out_specs=pl.BlockSpec((hb, 1, tn), lambda h, j: (h, 0, j)),
        out_shape=SDS((H, 1, T), F32), compiler_params=_cp(),
    )(doT, oT)


def attn_bwd(qT3, kT3, vT3, doT, lse, delta, cosT, sinT, bias, g, q_scale, B, S, dk_prev=None, dv_prev=None):
    _, D, T = qT3.shape
    H = D // HEAD_DIM
    nblk = S // BLK
    half = HEAD_DIM // 2
    hp = _has_prev(g, S)
    acc_in = dk_prev is not None
    kv_dtype = BF16

    def body(*refs):
        q_ref, k_ref, v_ref, do_ref, l_ref, d_ref, c_ref, s_ref, b_ref = refs[:9]
        rest = refs[9:]
        if acc_in:
            dkp_ref, dvp_ref = rest[:2]
            rest = rest[2:]
        dq_ref, dk_ref, dv_ref, s_scr, dp_scr, p_scr, ds_scr = rest
        for n in range(nblk):
            lo = 0 if hp[n] else BLK
            blk = slice(n * BLK, (n + 1) * BLK)
            s_scr[n, lo:, :] = _dot(_win(k_ref, n, hp[n]), q_ref[0, :, blk], TN)
            dp_scr[n, lo:, :] = _dot(_win(v_ref, n, hp[n]), do_ref[:, blk], TN)
        for n in range(nblk):
            lo = 0 if hp[n] else BLK
            blk = slice(n * BLK, (n + 1) * BLK)
            pT = jnp.exp(s_scr[n, lo:, :] + b_ref[lo:, :] - l_ref[0, :, blk])
            p_scr[n, lo:, :] = pT.astype(BF16)
            ds_scr[n, lo:, :] = (pT * (dp_scr[n, lo:, :] - d_ref[0, :, blk])).astype(BF16)
        for j in range(nblk):
            blk = slice(j * BLK, (j + 1) * BLK)
            if j + 1 < nblk and hp[j + 1]:
                two = slice(j * BLK, (j + 2) * BLK)
                pj = jnp.concatenate([p_scr[j, BLK:, :], p_scr[j + 1, :BLK, :]], axis=1)
                dsj = jnp.concatenate([ds_scr[j, BLK:, :], ds_scr[j + 1, :BLK, :]], axis=1)
                dv = _dot(do_ref[:, two], pj, NT)
                dk = _dot(q_ref[0, :, two], dsj, NT)
            else:
                dv = _dot(do_ref[:, blk], p_scr[j, BLK:, :], NT)
                dk = _dot(q_ref[0, :, blk], ds_scr[j, BLK:, :], NT)
            dk = dk * c_ref[0, :, blk] - pltpu.roll(dk, half, 0) * s_ref[0, :, blk]
            if acc_in:
                dk = dk + dkp_ref[:, blk]
                dv = dv + dvp_ref[:, blk]
            dk_ref[:, blk] = dk.astype(kv_dtype)
            dv_ref[:, blk] = dv.astype(kv_dtype)
            lo = 0 if hp[j] else BLK
            dq = _dot(_win(k_ref, j, hp[j]), ds_scr[j, lo:, :])
            dq = dq * c_ref[0, :, blk] - pltpu.roll(dq, half, 0) * s_ref[0, :, blk]
            dq_ref[:, blk] = (dq * q_scale).astype(BF16)

    spec3 = pl.BlockSpec((1, HEAD_DIM, S), lambda b, h: (g, h, b))
    spec = pl.BlockSpec((HEAD_DIM, S), lambda b, h: (h, b))
    sspec = pl.BlockSpec((1, 1, S), lambda b, h: (h, 0, b))
    tab = pl.BlockSpec((1, HEAD_DIM, S), lambda b, h: (g, 0, 0))
    in_specs = [spec3, spec3, spec3, spec, sspec, sspec, tab, tab, pl.BlockSpec((2 * BLK, BLK), lambda b, h: (0, 0))]
    args = [qT3, kT3, vT3, doT, lse, delta, cosT, sinT, bias]
    if acc_in:
        in_specs += [spec, spec]
        args += [dk_prev, dv_prev]
    return pl.pallas_call(
        body, name=f"attn_bwd_g{g}" + ("_acc" if acc_in else ""), grid=(B, H),
        in_specs=in_specs, out_specs=[spec, spec, spec],
        out_shape=[SDS((D, T), BF16), SDS((D, T), kv_dtype), SDS((D, T), kv_dtype)],
        scratch_shapes=[pltpu.VMEM((nblk, 2 * BLK, BLK), F32), pltpu.VMEM((nblk, 2 * BLK, BLK), F32),
                        pltpu.VMEM((nblk, 2 * BLK, BLK), BF16), pltpu.VMEM((nblk, 2 * BLK, BLK), BF16)],
        compiler_params=_cp(),
    )(*args)


def adamw(w, g, m, v, name):
    R, C = w.shape
    tr = _tile(R, 512, 8)

    def body(w_ref, g_ref, m_ref, v_ref, d_ref, nm_ref, nv_ref):
        gv = g_ref[...]
        nm = ADAM_B1 * m_ref[...] + (1.0 - ADAM_B1) * gv
        nv = ADAM_B2 * v_ref[...] + (1.0 - ADAM_B2) * (gv * gv)
        m_hat = nm / (1.0 - ADAM_B1 ** ADAM_STEP)
        v_hat = nv / (1.0 - ADAM_B2 ** ADAM_STEP)
        d_ref[...] = -ADAM_LR * (m_hat / (jnp.sqrt(v_hat) + ADAM_EPS) + ADAM_WD * w_ref[...])
        nm_ref[...] = nm
        nv_ref[...] = nv

    spec = pl.BlockSpec((tr, C), lambda i: (i, 0))
    return pl.pallas_call(
        body, name=name, grid=(R // tr,), in_specs=[spec] * 4, out_specs=[spec] * 3,
        out_shape=[SDS((R, C), F32)] * 3, compiler_params=_cp(),
    )(w, g, m, v)


def _perm(a, B, S, d):
    if d == 1:
        return a
    lead = a.shape[:-1]
    return a.reshape(*lead, B, S // d, d).swapaxes(-1, -2).reshape(*lead, B * S)


def _unperm(a, B, S, d):
    if d == 1:
        return a
    lead = a.shape[:-1]
    return a.reshape(*lead, B, d, S // d).swapaxes(-1, -2).reshape(*lead, B * S)


def _perm3(a, B, S):
    return jnp.stack([_perm(a, B, S, d) for d in DILATIONS])


def _xT3(xb, B, S):
    D = xb.shape[1]
    outs = []
    for d in DILATIONS:
        outs.append(xb.reshape(B, S // d, d, D).transpose(3, 0, 2, 1).reshape(D, B * S))
    return jnp.stack(outs)


def _rope_tables(S):
    half = HEAD_DIM // 2
    inv_freq = ROPE_THETA ** (-jnp.arange(0, HEAD_DIM, 2, dtype=F32) / HEAD_DIM)
    ang = jnp.arange(S, dtype=F32)[:, None] * inv_freq[None, :]
    cos = jnp.concatenate([jnp.cos(ang), jnp.cos(ang)], axis=1).T
    sin = jnp.concatenate([-jnp.sin(ang), jnp.sin(ang)], axis=1).T
    return _perm3(cos, 1, S), _perm3(sin, 1, S)


def kernel(x, pool_w, pool_scale, w_q, w_kv, w_o, ffn_w_gate, ffn_w_up, ffn_conv_w, ffn_conv_b, ffn_w_down, ln1_g, ln1_b, ln2_g, ln2_b, loss_target, m_pool_w, m_pool_scale, m_w_q, m_w_kv, m_w_o, m_ffn_w_gate, m_ffn_w_up, m_ffn_conv_w, m_ffn_conv_b, m_ffn_w_down, m_ln1_g, m_ln1_b, m_ln2_g, m_ln2_b, v_pool_w, v_pool_scale, v_w_q, v_w_kv, v_w_o, v_ffn_w_gate, v_ffn_w_up, v_ffn_conv_w, v_ffn_conv_b, v_ffn_w_down, v_ln1_g, v_ln1_b, v_ln2_g, v_ln2_b):
    B, S, D = x.shape
    T = B * S
    depth = ln1_g.shape[0]
    nA, nB = pool_w.shape[0], w_q.shape[0]
    Fs = ffn_w_down.shape[1]
    Fd = Fs * N_DEV
    H = D // HEAD_DIM
    G = len(DILATIONS)
    PG = len(POOL_WINDOWS)
    Cg = D // PG
    alpha = (2.0 * depth) ** 0.25
    me = 4 * lax.axis_index("x") + 2 * lax.axis_index("y") + lax.axis_index("c")

    qs, kvs, os_ = w_q.shape[2], w_kv.shape[1], w_o.shape[1]
    pool_rows = pool_w.size // D
    local = {("pool",): pool_w.reshape(pool_rows, D).astype(BF16), ("wkv",): w_kv.T.astype(BF16)}
    for j in range(nB):
        local[("wq", j)] = w_q[j].T.astype(BF16)
        local[("wo", j)] = w_o[j].astype(BF16)
    for i in range(depth):
        local[("wg", i)] = ffn_w_gate[i].T.astype(BF16)
        local[("wu", i)] = ffn_w_up[i].T.astype(BF16)
        local[("wd", i)] = ffn_w_down[i].astype(BF16)
    ffn = lambda i: [("wg", i), ("wu", i), ("wd", i)]
    queue = [[("pool",)]]
    if depth == 4 and nA == 2 and nB == 2:
        queue += [[("wg", 0)], [("wu", 0)], [("wd", 0), ("wg", 1), ("wu", 1)], [("wd", 1)],
                  [("wo", 0)], [("wo", 1)], [("wkv",), ("wq", 0)], [("wu", 2)], [("wd", 2)], [("wq", 1)],
                  [("wg", 2)], ffn(3)]
    gathered = {}

    def land(keys, arrs):
        for k, a in zip(keys or (), arrs or ()):
            gathered[k] = a.reshape(-1, D)

    def next_gather():
        if not queue:
            return None, None
        keys = queue.pop(0)
        if not keys:
            return None, None
        return keys, Gather([local[k] for k in keys])

    def weight(key):
        if key not in gathered:
            keys = [key]
            for bi, batch in enumerate(queue):
                if key in batch:
                    keys = queue.pop(bi)
                    break
            blk = all_gather_blocks(jnp.concatenate([local[k] for k in keys], axis=0), "gather_" + "_".join(map(str, key)), in_vmem=False)
            off = 0
            for k in keys:
                r = local[k].shape[0]
                gathered[k] = blk[:, off:off + r].reshape(-1, D)
                off += r
        return gathered[key]

    PW = weight(("pool",)).reshape(N_DEV, nA, PG, Cg // N_DEV, Cg).transpose(1, 2, 0, 3, 4).reshape(nA, PG, Cg, Cg)

    sm_cols = 128
    sm_local = jnp.concatenate([ffn_conv_w.reshape(-1), pool_scale.reshape(-1)])
    sm_rows = -(-sm_local.size // sm_cols)
    sm_rows_p = -(-sm_rows // 8) * 8
    sm_local = jnp.pad(sm_local, (0, sm_rows_p * sm_cols - sm_local.size)).reshape(sm_rows_p, sm_cols)
    sm = all_gather_blocks(sm_local, "gather_small", in_vmem=True).reshape(N_DEV, -1)
    ncw = ffn_conv_w.size
    conv_w_full = sm[:, :ncw].reshape(N_DEV, depth, 3, Fs).transpose(1, 2, 0, 3).reshape(depth, 3, Fd)
    pool_scale_full = sm[:, ncw:ncw + pool_scale.size].reshape(N_DEV, nA, D // N_DEV).transpose(1, 0, 2).reshape(nA, 1, D)

    cosT, sinT = _rope_tables(S)
    bias = _attn_bias()

    xs = x.reshape(T, D)
    saved = []
    cur, curb = xs, None
    kT = vT = x1T3 = None
    for i in range(depth):
        sv = {}
        if i < nA:
            keys, cr = next_gather()
            mix, pooled, got = pool_fwd(cur, PW[i], pool_scale_full[i], B, S, carry=cr)
            land(keys, got)
            sv["pooled"] = pooled
            keys, cr = next_gather()
            a1, h, hb, got = add_ln(cur, mix, ln1_g[i], ln1_b[i], alpha, carry=cr)
            land(keys, got)
        else:
            j = i - nA
            xT3 = x1T3 if j == 0 else _xT3(curb, B, S)
            keys, cr = next_gather()
            qT, got = proj_T(weight(("wq", j)), xT3, cosT, sinT, 0, True, HEAD_DIM ** -0.5, "q_proj", carry=cr)
            land(keys, got)
            oTs, lses = [], []
            for gi, d in enumerate(DILATIONS):
                o_g, lse_g = attn_fwd(qT, kT, vT, bias, gi, B, S)
                oTs.append(_unperm(o_g, B, S, d))
                lses.append(_unperm(lse_g, B, S, d))
            oTb, oTf, lse_tot = attn_combine(oTs, lses)
            a1, h, hb, _ = matmul_ln(oTb, weight(("wo", j)), TN, cur, ln1_g[i], ln1_b[i], alpha, "o_proj_ln")
            sv.update(xT3=xT3, qT=qT, oTb=oTb, oTf=oTf, lse_tot=lse_tot)
        wg_i, wu_i = weight(("wg", i)), weight(("wu", i))
        keys, cr = next_gather()
        g, ge, ud, hh, got = ffn_up(hb, wg_i, wu_i, conv_w_full[i], ffn_conv_b[i].reshape(1, Fd), B, S, carry=cr)
        land(keys, got)
        wd_i = weight(("wd", i))
        keys, cr = next_gather()
        a2, cur, curb, got = matmul_ln(hh, wd_i, NN, h, ln2_g[i], ln2_b[i], alpha, "ffn_down_ln", carry=cr)
        land(keys, got)
        sv.update(a1=a1, hb=hb, g=g, ge=ge, ud=ud, hh=hh, a2=a2)
        saved.append(sv)
        if i == nA - 1:
            x1T3 = _xT3(curb, B, S)
            wkv = weight(("wkv",))
            keys, cr = next_gather()
            kT, got = proj_T(wkv, x1T3, cosT, sinT, 0, True, 1.0, "k_proj", carry=cr)
            land(keys, got)
            keys, cr = next_gather()
            vT, got = proj_T(wkv, x1T3, cosT, sinT, G, False, 1.0, "v_proj", carry=cr)
            land(keys, got)


    small = {k: [None] * depth for k in ("ln1_g", "ln1_b", "ln2_g", "ln2_b", "conv_b", "conv_w")}
    dscale = [None] * nA
    dpw = [None] * nA
    dk_acc, dv_acc = [None] * G, [None] * G

    def blocks(a, rows):
        return a.reshape(N_DEV, rows, D)

    pending, landed = [], {}

    def next_carry():
        if not pending:
            return None, None
        key, parts = pending.pop(0)
        return key, Scatter(parts)

    dcur = sq = ln2_done = None
    for i in reversed(range(depth)):
        sv = saved[i]
        if i == depth - 1:
            db2, db2b, small["ln2_g"][i], small["ln2_b"][i], sq = loss_ln_bwd(cur, loss_target.reshape(T, D), sv["a2"], ln2_g[i])
        elif ln2_done is not None:
            db2, db2b, small["ln2_g"][i], small["ln2_b"][i] = ln2_done
            ln2_done = None
        else:
            db2, db2b, small["ln2_g"][i], small["ln2_b"][i] = ln_bwd(dcur, sv["a2"], ln2_g[i])
        key, cr = next_carry()
        dg_, du_, dcb, dcw, got = ffn_mid_bwd(db2b, gathered[("wd", i)], sv["g"], sv["ge"], sv["ud"], conv_w_full[i], B, S, carry=cr)
        if cr is not None:
            landed[key] = got
        small["conv_b"][i] = jnp.sum(dcb, axis=0)
        small["conv_w"][i] = jnp.sum(dcw, axis=0)
        key, cr = next_carry()
        dwd, got = wgrad_rows(sv["hh"], db2b, "wgrad_down", carry=cr)
        if cr is not None:
            landed[key] = got
        dwg, landed[("down", i)] = wgrad_rows(dg_, sv["hb"], "wgrad_gate", carry=Scatter([blocks(dwd, Fs)]))
        dwu, landed[("gate", i)] = wgrad_rows(du_, sv["hb"], "wgrad_up", carry=Scatter([blocks(dwg, Fs)]))
        da1, da1b, small["ln1_g"][i], small["ln1_b"][i], landed[("up", i)] = ffn_dx_ln(
            dg_, du_, gathered[("wg", i)], gathered[("wu", i)], db2, alpha, sv["a1"], ln1_g[i], carry=Scatter([blocks(dwu, Fs)]))
        if i < nA:
            dcur, dsp, dpwp = pool_bwd(da1, sv["pooled"], PW[i], pool_scale_full[i], alpha, B, S)
            dscale[i] = jnp.sum(dsp, axis=0)
            dpw[i] = jnp.sum(dpwp, axis=0)
        else:
            j = i - nA
            doT = matmul_to_T(gathered[("wo", j)], da1b, "o_proj_bwd")
            dwo = wgrad_mixed(sv["oTb"], da1b, "wgrad_o")
            delta = attn_delta(doT, sv["oTf"])
            dq_tok, dwq = [], []
            for gi, d in enumerate(DILATIONS):
                dq_g, dk_acc[gi], dv_acc[gi] = attn_bwd(
                    sv["qT"], kT, vT, _perm(doT, B, S, d), _perm(sv["lse_tot"], B, S, d), _perm(delta, B, S, d),
                    cosT, sinT, bias, gi, HEAD_DIM ** -0.5, B, S, dk_prev=dk_acc[gi], dv_prev=dv_acc[gi])
                dwq.append(wgrad_T(dq_g, sv["xT3"], gi, "wgrad_q"))
                dq_tok.append(_unperm(dq_g, B, S, d))
            dwq = jnp.concatenate(dwq, axis=0)
            below = (saved[i - 1]["a2"], ln2_g[i - 1]) if i > 0 else None
            if j > 0:
                last = dx_from_T(dq_tok, gathered[("wq", j)], da1, alpha, "q_proj_bwd", 512, ln=below)
            else:
                dcur = dx_from_T(dq_tok, gathered[("wq", j)], da1, alpha, "q_proj_bwd", 512)
                dkv = [a.astype(BF16) for a in dk_acc + dv_acc]
                dwkv = jnp.concatenate([wgrad_T(a, x1T3, gi % G, "wgrad_kv") for gi, a in enumerate(dkv)], axis=0)
                dkv_tok = [_unperm(a, B, S, DILATIONS[gi % G]) for gi, a in enumerate(dkv)]
                last = dx_from_T(dkv_tok, gathered[("wkv",)], dcur, 1.0, "kv_proj_bwd", 256, ln=below)
                pending.append((("kv",), [blocks(dwkv, kvs)]))
            if below is None:
                dcur = last
            else:
                ln2_done = last
            pending.append((("attn", j), [blocks(dwq, qs), blocks(dwo, os_)]))
    grad_x = dcur.reshape(B, S, D)

    dpw_all = jnp.stack(dpw).reshape(nA, PG, N_DEV, Cg // N_DEV, Cg).transpose(2, 0, 1, 3, 4).reshape(N_DEV, pool_rows, D)
    tail_keys = [k for k, _ in pending] + [("pool",)]
    tail_parts = [parts for _, parts in pending] + [[dpw_all.astype(BF16)]]
    tail_rows = [sum(p.shape[1] for p in parts) for parts in tail_parts]
    tail = scatter_partials([p for parts in tail_parts for p in parts], "scatter_tail")
    for t, key in enumerate(tail_keys):
        lo = sum(tail_rows[:t])
        landed[key] = tail[:, lo:lo + tail_rows[t]]

    def reduced(key):
        return sum_slots(landed[key], "sum_" + "_".join(str(k) for k in key))

    g_attn = [reduced(("attn", j)) for j in range(nB)]
    g_w_q = jnp.swapaxes(jnp.stack([a[:qs] for a in g_attn]), 1, 2)
    g_w_o = jnp.stack([a[qs:] for a in g_attn])
    g_w_kv = reduced(("kv",)).T
    g_gate = jnp.swapaxes(jnp.stack([reduced(("gate", i)) for i in range(depth)]), 1, 2)
    g_up = jnp.swapaxes(jnp.stack([reduced(("up", i)) for i in range(depth)]), 1, 2)
    g_down = jnp.stack([reduced(("down", i)) for i in range(depth)])
    g_pool_w = reduced(("pool",)).reshape(pool_w.shape)

    def rows_of(a):
        a = a.reshape(-1)
        n = -(-a.size // D) * D
        return jnp.pad(a, (0, n - a.size)).reshape(-1, D)

    sm_parts = [rows_of(jnp.concatenate(small[k], axis=0)) for k in ("ln1_g", "ln1_b", "ln2_g", "ln2_b")]
    sm_parts += [rows_of(jnp.stack(small["conv_b"])), rows_of(jnp.stack(small["conv_w"])), rows_of(jnp.stack(dscale)), sq]
    sm_sizes = [p.shape[0] for p in sm_parts]
    sm_all = jnp.concatenate(sm_parts, axis=0)
    pad_rows = -(-sm_all.shape[0] // 8) * 8 - sm_all.shape[0]
    sm_all = jnp.pad(sm_all, ((0, pad_rows), (0, 0)))
    sm_sum = sum_slots(all_gather_blocks(sm_all, "gather_small_grads", in_vmem=True), "sum_small_grads")
    sm_offs = [sum(sm_sizes[:i]) for i in range(len(sm_sizes))]

    def sm_take(i, shape):
        n = math.prod(shape)
        return sm_sum[sm_offs[i]:sm_offs[i] + sm_sizes[i]].reshape(-1)[:n].reshape(shape)

    g_ln1_g, g_ln1_b = sm_take(0, (depth, D)), sm_take(1, (depth, D))
    g_ln2_g, g_ln2_b = sm_take(2, (depth, D)), sm_take(3, (depth, D))
    g_conv_b = sm_take(4, (depth, Fd))
    g_conv_w = lax.dynamic_slice_in_dim(sm_take(5, (depth, 3, Fd)), me * Fs, Fs, axis=2)
    g_pool_scale = lax.dynamic_slice_in_dim(sm_take(6, (nA, D)), me * (D // N_DEV), D // N_DEV, axis=1)
    loss = (0.5 / D) * jnp.sum(sm_take(7, (D,)))

    def v2(a):
        return a.reshape(-1, a.shape[-1])

    names = ["pool_w", "pool_scale", "w_q", "w_kv", "w_o", "ffn_w_gate", "ffn_w_up", "ffn_conv_w", "ffn_conv_b",
             "ffn_w_down", "ln1_g", "ln1_b", "ln2_g", "ln2_b"]
    ws = [pool_w, pool_scale, w_q, w_kv, w_o, ffn_w_gate, ffn_w_up, ffn_conv_w, ffn_conv_b, ffn_w_down, ln1_g, ln1_b, ln2_g, ln2_b]
    ms = [m_pool_w, m_pool_scale, m_w_q, m_w_kv, m_w_o, m_ffn_w_gate, m_ffn_w_up, m_ffn_conv_w, m_ffn_conv_b, m_ffn_w_down, m_ln1_g, m_ln1_b, m_ln2_g, m_ln2_b]
    vs = [v_pool_w, v_pool_scale, v_w_q, v_w_kv, v_w_o, v_ffn_w_gate, v_ffn_w_up, v_ffn_conv_w, v_ffn_conv_b, v_ffn_w_down, v_ln1_g, v_ln1_b, v_ln2_g, v_ln2_b]
    gs = [g_pool_w, g_pool_scale, g_w_q, g_w_kv, g_w_o, g_gate, g_up, g_conv_w, g_conv_b, g_down, g_ln1_g, g_ln1_b, g_ln2_g, g_ln2_b]
    deltas, new_ms, new_vs = [], [], []
    for nm, w, gr, m_, v_ in zip(names, ws, gs, ms, vs):
        d_, nm_, nv_ = adamw(v2(w), v2(gr), v2(m_), v2(v_), "adamw_" + nm)
        deltas.append(d_.reshape(w.shape))
        new_ms.append(nm_.reshape(w.shape))
        new_vs.append(nv_.reshape(w.shape))

    return (loss, grad_x, *gs, *deltas, *new_ms, *new_vs)
```
